```python
import math
import jax, jax.numpy as jnp
from jax import lax
import numpy as np

D_MODEL = 1024
BATCH = 16
SEQ = 2048
DEPTH = 1

HEAD_DIM = 64
ATTN_WIDTH = D_MODEL // 2
ATTN_HEADS = ATTN_WIDTH // HEAD_DIM
SSM_WIDTH = D_MODEL - ATTN_WIDTH
SSM_GROUP = 16
SSM_GROUPS = SSM_WIDTH // SSM_GROUP
SSM_STATE = 64
MIX_WIDTH = ATTN_WIDTH + SSM_WIDTH
IN_COLS = 3 * ATTN_WIDTH + ATTN_HEADS + SSM_WIDTH
Q_BLOCK = 128
N_EXPERT_GROUPS = 4
EXPERTS_PER_GROUP = 8
N_EXPERTS = N_EXPERT_GROUPS * EXPERTS_PER_GROUP
TOP_K_IN_GROUP = 2
D_EXPERT = D_MODEL // 2
MOE_BLOCK = 256
DEEPNORM_ALPHA = (2.0 * DEPTH) ** 0.25
DEEPNORM_BETA = (8.0 * DEPTH) ** -0.25
LN_EPS = 1e-5
RMS_EPS = 1e-6

kernel_name = 'fox_s5_hymba_hmoe_deepnorm'


def layer_norm(x, g, b):
    xf = x.astype(jnp.float32)
    mu = jnp.mean(xf, axis=-1, keepdims=True)
    var = jnp.mean(jnp.square(xf - mu), axis=-1, keepdims=True)
    return ((xf - mu) * lax.rsqrt(var + LN_EPS) * g.astype(jnp.float32) + b.astype(jnp.float32)).astype(x.dtype)


def rms_norm(x, g, out_dtype):
    xf = x.astype(jnp.float32)
    y = xf * lax.rsqrt(jnp.mean(jnp.square(xf), axis=-1, keepdims=True) + RMS_EPS)
    return (y * g.astype(jnp.float32)).astype(out_dtype)


def forgetting_attention(q, k, v, log_f):
    seq = q.shape[2]
    scale = HEAD_DIM ** -0.5
    cum = jnp.cumsum(log_f, axis=-1)
    outs = []
    for blk in range(seq // Q_BLOCK):
        s0, s1 = blk * Q_BLOCK, (blk + 1) * Q_BLOCK
        qb, kb, vb = q[:, :, s0:s1], k[:, :, :s1], v[:, :, :s1]
        scores = jnp.einsum('bhqd,bhkd->bhqk', qb, kb).astype(jnp.float32) * scale
        scores = scores + (cum[:, :, s0:s1, None] - cum[:, :, None, :s1])
        mask = (s0 + jnp.arange(Q_BLOCK))[:, None] >= jnp.arange(s1)[None, :]
        scores = jnp.where(mask, scores, -jnp.inf)
        p = jax.nn.softmax(scores, axis=-1)
        outs.append(jnp.einsum('bhqk,bhkd->bhqd', p.astype(vb.dtype), vb))
    return jnp.concatenate(outs, axis=2)


def _complex_affine_combine(e1, e2):
    a1r, a1i, b1r, b1i = e1
    a2r, a2i, b2r, b2i = e2
    ar = a2r * a1r - a2i * a1i
    ai = a2r * a1i + a2i * a1r
    br = a2r * b1r - a2i * b1i + b2r
    bi = a2r * b1i + a2i * b1r + b2i
    return ar, ai, br, bi


def s5_branch(u, lam_re, lam_im, log_step, b_re, b_im, c_re, c_im, d_skip, w_glu, b_glu):
    f32 = jnp.float32
    bsz, seq, _ = u.shape
    u = u.astype(f32).reshape(bsz, seq, SSM_GROUPS, SSM_GROUP)
    lam_re, lam_im = lam_re.astype(f32), lam_im.astype(f32)
    delta = jnp.exp(log_step.astype(f32))[:, None]
    mag = jnp.exp(lam_re * delta)
    ar = mag * jnp.cos(lam_im * delta)
    ai = mag * jnp.sin(lam_im * delta)
    den = lam_re * lam_re + lam_im * lam_im
    num_re = ar - 1.0
    coef_re = (num_re * lam_re + ai * lam_im) / den
    coef_im = (ai * lam_re - num_re * lam_im) / den
    br, bi = b_re.astype(f32), b_im.astype(f32)
    bbar_re = coef_re[..., None] * br - coef_im[..., None] * bi
    bbar_im = coef_re[..., None] * bi + coef_im[..., None] * br
    bu_re = jnp.einsum('blgh,gph->blgp', u, bbar_re)
    bu_im = jnp.einsum('blgh,gph->blgp', u, bbar_im)
    a_re = jnp.broadcast_to(ar, bu_re.shape)
    a_im = jnp.broadcast_to(ai, bu_re.shape)
    _, _, xr, xi = lax.associative_scan(_complex_affine_combine, (a_re, a_im, bu_re, bu_im), axis=1)
    y = (jnp.einsum('blgp,ghp->blgh', xr, c_re.astype(f32))
         - jnp.einsum('blgp,ghp->blgh', xi, c_im.astype(f32))
         + d_skip.astype(f32) * u)
    y = jax.nn.gelu(y.reshape(bsz, seq, SSM_WIDTH))
    return y * jax.nn.sigmoid(y @ w_glu.astype(f32) + b_glu.astype(f32))


def hybrid_mixer(x, w_in, b_fgate, lam_re, lam_im, log_step, b_re, b_im, c_re, c_im, d_skip,
                 w_glu, b_glu, attn_norm_g, ssm_norm_g, w_out):
    bsz, seq, _ = x.shape
    z = jnp.einsum('bld,dc->blc', x, w_in)
    a = ATTN_WIDTH
    def heads(t):
        return t.reshape(bsz, seq, ATTN_HEADS, HEAD_DIM).transpose(0, 2, 1, 3)
    q, k, v = heads(z[..., :a]), heads(z[..., a:2 * a]), heads(z[..., 2 * a:3 * a])
    f_logit = z[..., 3 * a:3 * a + ATTN_HEADS].astype(jnp.float32) + b_fgate.astype(jnp.float32)
    log_f = jax.nn.log_sigmoid(f_logit).transpose(0, 2, 1)
    u = z[..., 3 * a + ATTN_HEADS:]
    attn = forgetting_attention(q, k, v, log_f).transpose(0, 2, 1, 3).reshape(bsz, seq, ATTN_WIDTH)
    ssm = s5_branch(u, lam_re, lam_im, log_step, b_re, b_im, c_re, c_im, d_skip, w_glu, b_glu)
    y = jnp.concatenate([rms_norm(attn, attn_norm_g, x.dtype), rms_norm(ssm, ssm_norm_g, x.dtype)], axis=-1)
    return jnp.einsum('blc,cd->bld', y, w_out)


def hierarchical_moe(h, w_rg, b_rg, w_re, b_re, w_gate, w_up, w_down):
    bsz, seq, d = h.shape
    n_tok = bsz * seq
    ht = h.reshape(n_tok, d)
    g_prob = jax.nn.softmax((ht @ w_rg + b_rg).astype(jnp.float32), axis=-1)
    g_p, g_idx = lax.top_k(g_prob, 1)
    e_logits = (ht @ w_re + b_re).astype(jnp.float32).reshape(n_tok, N_EXPERT_GROUPS, EXPERTS_PER_GROUP)
    e_in = jnp.take_along_axis(e_logits, g_idx[:, :, None], axis=1)[:, 0]
    e_p, e_loc = lax.top_k(jax.nn.softmax(e_in, axis=-1), TOP_K_IN_GROUP)
    e_p = e_p / jnp.sum(e_p, axis=-1, keepdims=True)
    gates = (g_p * e_p).reshape(-1)
    eid = (g_idx * EXPERTS_PER_GROUP + e_loc).reshape(-1).astype(jnp.int32)
    tok = jnp.repeat(jnp.arange(n_tok, dtype=jnp.int32), TOP_K_IN_GROUP)
    n_asg = n_tok * TOP_K_IN_GROUP
    order = jnp.argsort(eid)
    eid_s, tok_s, gate_s = eid[order], tok[order], gates[order]
    counts = jnp.bincount(eid, length=N_EXPERTS).astype(jnp.int32)
    starts = jnp.cumsum(counts) - counts
    padded = ((counts + MOE_BLOCK - 1) // MOE_BLOCK) * MOE_BLOCK
    pends = jnp.cumsum(padded)
    pstarts = pends - padded
    slot = pstarts[eid_s] + (jnp.arange(n_asg, dtype=jnp.int32) - starts[eid_s])
    n_pad = ((n_asg + N_EXPERTS * (MOE_BLOCK - 1) + MOE_BLOCK - 1) // MOE_BLOCK) * MOE_BLOCK
    n_blocks = n_pad // MOE_BLOCK
    slot_tok = jnp.full((n_pad,), n_tok, jnp.int32).at[slot].set(tok_s)
    x_pad = jnp.concatenate([ht, jnp.zeros((1, d), ht.dtype)], axis=0)
    xb = x_pad[slot_tok].reshape(n_blocks, MOE_BLOCK, d)
    block_e = jnp.minimum(jnp.searchsorted(pends, jnp.arange(n_blocks, dtype=jnp.int32) * MOE_BLOCK, side='right'),
                          N_EXPERTS - 1).astype(jnp.int32)

    def expert_block(args):
        xblk, e = args
        hid = jax.nn.silu(xblk @ w_gate[e]) * (xblk @ w_up[e])
        return hid @ w_down[e]

    yb = lax.map(expert_block, (xb, block_e)).reshape(n_pad, d)
    y_asg = yb[slot] * gate_s[:, None].astype(yb.dtype)
    y = jax.ops.segment_sum(y_asg, tok_s, num_segments=n_tok)
    return y.reshape(bsz, seq, d).astype(h.dtype)


def setup_inputs(seed: int = 0) -> dict:
    key = jax.random.key(seed)
    ks = jax.random.split(key, 32)
    f32 = jnp.float32
    L, G, P, H = DEPTH, SSM_GROUPS, SSM_STATE, SSM_GROUP

    def nrm(k, shape, scale):
        return jax.random.normal(k, shape, f32) * scale

    x = nrm(ks[0], (BATCH, SEQ, D_MODEL), 1.0)
    w_in = nrm(ks[1], (L, D_MODEL, IN_COLS), D_MODEL ** -0.5)
    b_fgate = jnp.linspace(1.0, 5.0, ATTN_HEADS, dtype=f32)[None, :] + nrm(ks[2], (L, ATTN_HEADS), 0.1)
    s5_lambda_re = -0.5 + nrm(ks[3], (L, G, P), 0.01)
    s5_lambda_im = math.pi * jnp.arange(P, dtype=f32) + nrm(ks[4], (L, G, P), 0.01)
    s5_log_step = jax.random.uniform(ks[5], (L, G), f32, math.log(1e-3), math.log(1e-1))
    s5_b_re = nrm(ks[6], (L, G, P, H), (2.0 * H) ** -0.5)
    s5_b_im = nrm(ks[7], (L, G, P, H), (2.0 * H) ** -0.5)
    s5_c_re = nrm(ks[8], (L, G, H, P), (2.0 * P) ** -0.5)
    s5_c_im = nrm(ks[9], (L, G, H, P), (2.0 * P) ** -0.5)
    s5_d = nrm(ks[10], (L, G, H), 1.0)
    w_glu = nrm(ks[11], (L, SSM_WIDTH, SSM_WIDTH), SSM_WIDTH ** -0.5)
    b_glu = nrm(ks[12], (L, SSM_WIDTH), 0.01)
    attn_norm_g = 1.0 + nrm(ks[13], (L, ATTN_WIDTH), 0.02)
    ssm_norm_g = 1.0 + nrm(ks[14], (L, SSM_WIDTH), 0.02)
    w_out = nrm(ks[15], (L, MIX_WIDTH, D_MODEL), MIX_WIDTH ** -0.5 * DEEPNORM_BETA)
    ln1_g = 1.0 + nrm(ks[16], (L, D_MODEL), 0.02)
    ln1_b = nrm(ks[17], (L, D_MODEL), 0.02)
    w_router_group = nrm(ks[18], (L, D_MODEL, N_EXPERT_GROUPS), D_MODEL ** -0.5)
    b_router_group = nrm(ks[19], (L, N_EXPERT_GROUPS), 0.01)
    w_router_expert = nrm(ks[20], (L, D_MODEL, N_EXPERTS), D_MODEL ** -0.5)
    b_router_expert = nrm(ks[21], (L, N_EXPERTS), 0.01)
    w_gate = nrm(ks[22], (L, N_EXPERTS, D_MODEL, D_EXPERT), D_MODEL ** -0.5)
    w_up = nrm(ks[23], (L, N_EXPERTS, D_MODEL, D_EXPERT), D_MODEL ** -0.5)
    w_down = nrm(ks[24], (L, N_EXPERTS, D_EXPERT, D_MODEL), D_EXPERT ** -0.5 * DEEPNORM_BETA)
    ln2_g = 1.0 + nrm(ks[25], (L, D_MODEL), 0.02)
    ln2_b = nrm(ks[26], (L, D_MODEL), 0.02)
    return {'x': x, 'w_in': w_in, 'b_fgate': b_fgate,
            's5_lambda_re': s5_lambda_re, 's5_lambda_im': s5_lambda_im, 's5_log_step': s5_log_step,
            's5_b_re': s5_b_re, 's5_b_im': s5_b_im, 's5_c_re': s5_c_re, 's5_c_im': s5_c_im, 's5_d': s5_d,
            'w_glu': w_glu, 'b_glu': b_glu, 'attn_norm_g': attn_norm_g, 'ssm_norm_g': ssm_norm_g,
            'w_out': w_out, 'ln1_g': ln1_g, 'ln1_b': ln1_b,
            'w_router_group': w_router_group, 'b_router_group': b_router_group,
            'w_router_expert': w_router_expert, 'b_router_expert': b_router_expert,
            'w_gate': w_gate, 'w_up': w_up, 'w_down': w_down, 'ln2_g': ln2_g, 'ln2_b': ln2_b}


def reference(x, w_in, b_fgate, s5_lambda_re, s5_lambda_im, s5_log_step, s5_b_re, s5_b_im,
              s5_c_re, s5_c_im, s5_d, w_glu, b_glu, attn_norm_g, ssm_norm_g, w_out, ln1_g, ln1_b,
              w_router_group, b_router_group, w_router_expert, b_router_expert,
              w_gate, w_up, w_down, ln2_g, ln2_b):
    for l in range(DEPTH):
        mix = hybrid_mixer(x, w_in[l], b_fgate[l], s5_lambda_re[l], s5_lambda_im[l], s5_log_step[l],
                           s5_b_re[l], s5_b_im[l], s5_c_re[l], s5_c_im[l], s5_d[l], w_glu[l], b_glu[l],
                           attn_norm_g[l], ssm_norm_g[l], w_out[l])
        x = layer_norm(DEEPNORM_ALPHA * x + mix, ln1_g[l], ln1_b[l])
        moe = hierarchical_moe(x, w_router_group[l], b_router_group[l], w_router_expert[l],
                               b_router_expert[l], w_gate[l], w_up[l], w_down[l])
        x = layer_norm(DEEPNORM_ALPHA * x + moe, ln2_g[l], ln2_b[l])
    return x
```

```python
import functools
import math

import jax
import jax.numpy as jnp
from jax import lax
from jax.experimental import pallas as pl
from jax.experimental.pallas import tpu as pltpu

F32 = jnp.float32
BF16 = jnp.bfloat16

D_MODEL = 1024
HEAD_DIM = 64
ATTN_WIDTH = 512
ATTN_HEADS = 8
SSM_WIDTH = 512
SSM_GROUP = 16
SSM_GROUPS = 32
SSM_STATE = 64
N_EXPERT_GROUPS = 4
EXPERTS_PER_GROUP = 8
N_EXPERTS = 32
D_EXPERT = 512
MOE_BLOCK = 256
DEPTH = 1
DEEPNORM_ALPHA = (2.0 * DEPTH) ** 0.25
LN_EPS = 1e-5
RMS_EPS = 1e-6

LANES = 128
NEG_BIG = -1e30
VMEM_LIMIT = 48 * 1024 * 1024

ROW_BLOCK = 512
ATTN_BLOCK = 256
S5_TIME_BLOCK = 32
S5_SLABS = 4
SLAB_CH = SSM_WIDTH // S5_SLABS
SLAB_STATE = SSM_GROUPS // S5_SLABS * SSM_STATE


def _split3(c):
    hi = c.astype(BF16)
    r1 = c - hi.astype(F32)
    mid = r1.astype(BF16)
    lo = (r1 - mid.astype(F32)).astype(BF16)
    return hi, mid, lo


def _inproj_kernel(x_ref, w_ref, bf_ref, q_ref, k_ref, v_ref, u_ref, cum_ref, carry_ref, *, blocks_per_seq):
    i = pl.program_id(0)

    @pl.when(i % blocks_per_seq == 0)
    def _():
        carry_ref[...] = jnp.zeros_like(carry_ref)

    tm = x_ref.shape[0]
    z = jnp.dot(x_ref[...].astype(BF16), w_ref[...], preferred_element_type=F32)
    a = ATTN_WIDTH
    q_ref[...] = z[:, :a].astype(BF16)
    k_ref[...] = z[:, a:2 * a].astype(BF16)
    v_ref[...] = z[:, 2 * a:3 * a].astype(BF16)
    u_ref[...] = z[:, 3 * a:3 * a + SSM_WIDTH]
    f = z[:, 3 * a + SSM_WIDTH:] + bf_ref[...]
    logf = jnp.minimum(f, 0.0) - jnp.log1p(jnp.exp(-jnp.abs(f)))
    row = lax.broadcasted_iota(jnp.int32, (tm, tm), 0)
    col = lax.broadcasted_iota(jnp.int32, (tm, tm), 1)
    tri = (row >= col).astype(BF16)
    hi, mid, lo = _split3(logf)
    cum = (jnp.dot(tri, hi, preferred_element_type=F32)
           + jnp.dot(tri, mid, preferred_element_type=F32)
           + jnp.dot(tri, lo, preferred_element_type=F32)) + carry_ref[...]
    carry_ref[...] = cum[tm - 1:tm, :]
    cum_ref[...] = cum[:, :ATTN_HEADS]


def _inproj(x2, w_cat, bf_row, seq):
    n_tok = x2.shape[0]
    tm = ROW_BLOCK
    assert seq % tm == 0 and n_tok % tm == 0
    ncol = w_cat.shape[1]
    row_spec = lambda w: pl.BlockSpec((tm, w), lambda i: (i, 0))
    return pl.pallas_call(
        functools.partial(_inproj_kernel, blocks_per_seq=seq // tm),
        grid=(n_tok // tm,),
        in_specs=[row_spec(D_MODEL),
                  pl.BlockSpec((D_MODEL, ncol), lambda i: (0, 0)),
                  pl.BlockSpec((1, LANES), lambda i: (0, 0))],
        out_specs=[row_spec(ATTN_WIDTH), row_spec(ATTN_WIDTH), row_spec(ATTN_WIDTH),
                   row_spec(SSM_WIDTH), row_spec(ATTN_HEADS)],
        out_shape=[jax.ShapeDtypeStruct((n_tok, ATTN_WIDTH), BF16)] * 3
                  + [jax.ShapeDtypeStruct((n_tok, SSM_WIDTH), F32),
                     jax.ShapeDtypeStruct((n_tok, ATTN_HEADS), F32)],
        scratch_shapes=[pltpu.VMEM((1, LANES), F32)],
        compiler_params=pltpu.CompilerParams(dimension_semantics=("arbitrary",),
                                             vmem_limit_bytes=VMEM_LIMIT),
        name="inproj",
    )(x2, w_cat, bf_row)


def _bias_lanes_q(c):
    hi, mid, lo = (t.astype(F32) for t in _split3(c))
    lane = lax.broadcasted_iota(jnp.int32, (c.shape[0], LANES), 1)
    sel = jnp.where(lane == 0, hi, jnp.where(lane == 1, mid, jnp.where(lane == 2, lo,
                    jnp.where(lane < 6, 1.0, 0.0))))
    return sel.astype(BF16)


def _bias_lanes_k(c):
    hi, mid, lo = (t.astype(F32) for t in _split3(c))
    lane = lax.broadcasted_iota(jnp.int32, (c.shape[0], LANES), 1)
    sel = jnp.where(lane < 3, 1.0, jnp.where(lane == 3, -hi, jnp.where(lane == 4, -mid,
                    jnp.where(lane == 5, -lo, 0.0))))
    return sel.astype(BF16)


def _attn_kernel(q_ref, k_ref, v_ref, cum_ref, o_ref, extk_ref, *, tq, seq):
    qi = pl.program_id(2)
    n_tiles = seq // tq

    @pl.when(qi == 0)
    def _():
        def fill(j, carry):
            r0 = pl.multiple_of(j * tq, tq)
            c = cum_ref[pl.ds(r0, tq), :]
            for h in range(2):
                extk_ref[h, pl.ds(r0, tq), :] = _bias_lanes_k(c[:, h:h + 1])
            return carry
        lax.fori_loop(0, n_tiles, fill, 0)

    qb = q_ref[...]
    q0 = pl.multiple_of(qi * tq, tq)
    cq = cum_ref[pl.ds(q0, tq), :]
    lane = lax.broadcasted_iota(jnp.int32, (tq, LANES), 1)
    row = lax.broadcasted_iota(jnp.int32, (tq, tq), 0)
    col = lax.broadcasted_iota(jnp.int32, (tq, tq), 1)
    causal = row >= col
    outs = []
    for h in range(2):
        head_lanes = (lane < HEAD_DIM) if h == 0 else (lane >= HEAD_DIM)
        qh = jnp.where(head_lanes, qb, jnp.zeros((), BF16))
        q_aug = jnp.concatenate([qh, _bias_lanes_q(cq[:, h:h + 1])], axis=1)

        def step(j, carry, masked):
            m, l, acc = carry
            k0 = pl.multiple_of(j * tq, tq)
            k_aug = jnp.concatenate([k_ref[pl.ds(k0, tq), :], extk_ref[h, pl.ds(k0, tq), :]], axis=1)
            s = lax.dot_general(q_aug, k_aug, (((1,), (1,)), ((), ())), preferred_element_type=F32)
            if masked:
                s = jnp.where(causal, s, NEG_BIG)
            m_new = jnp.maximum(m, jnp.max(s, axis=1, keepdims=True))
            corr = jnp.exp(m - m_new)
            p = jnp.exp(s - m_new)
            l = corr * l + jnp.sum(p, axis=1, keepdims=True)
            acc = corr * acc + jnp.dot(p.astype(BF16), v_ref[pl.ds(k0, tq), :], preferred_element_type=F32)
            return m_new, l, acc

        init = (jnp.full((tq, 1), NEG_BIG, F32), jnp.zeros((tq, 1), F32), jnp.zeros((tq, LANES), F32))
        carry = lax.fori_loop(0, qi, functools.partial(step, masked=False), init)
        m, l, acc = step(qi, carry, True)
        outs.append(acc / l)
    o_ref[...] = jnp.where(lane < HEAD_DIM, outs[0], outs[1])


def _attention(q, k, v, cum4):
    bsz, seq, _ = q.shape
    tq = ATTN_BLOCK
    assert seq % tq == 0
    pairs = ATTN_HEADS // 2
    return pl.pallas_call(
        functools.partial(_attn_kernel, tq=tq, seq=seq),
        grid=(bsz, pairs, seq // tq),
        in_specs=[pl.BlockSpec((None, tq, LANES), lambda b, p, i: (b, i, p)),
                  pl.BlockSpec((None, seq, LANES), lambda b, p, i: (b, 0, p)),
                  pl.BlockSpec((None, seq, LANES), lambda b, p, i: (b, 0, p)),
                  pl.BlockSpec((None, None, seq, 2), lambda b, p, i: (b, p, 0, 0))],
        out_specs=pl.BlockSpec((None, tq, LANES), lambda b, p, i: (b, i, p)),
        out_shape=jax.ShapeDtypeStruct((bsz, seq, ATTN_WIDTH), F32),
        scratch_shapes=[pltpu.VMEM((2, seq, LANES), BF16)],
        compiler_params=pltpu.CompilerParams(dimension_semantics=("parallel", "parallel", "arbitrary"),
                                             vmem_limit_bytes=VMEM_LIMIT),
        name="fox_attention",
    )(q, k, v, cum4)


def _s5_kernel(u_ref, bw_ref, cw_ref, ar_ref, ai_ref, d_ref, wglu_ref, bglu_ref, g_ref, o_ref,
               state_ref, bu_ref, *, tc, bsz):
    @pl.when(pl.program_id(0) == 0)
    def _():
        state_ref[...] = jnp.zeros_like(state_ref)

    rows = tc * bsz
    ns = SLAB_STATE
    u = u_ref[...].reshape(rows, SSM_WIDTH)
    ub = u.astype(BF16)
    ys = []
    for i in range(S5_SLABS):
        bu_ref[...] = jnp.dot(ub[:, SLAB_CH * i:SLAB_CH * (i + 1)], bw_ref[i], preferred_element_type=F32)
        ar = jnp.broadcast_to(ar_ref[i], (bsz, ns))
        ai = jnp.broadcast_to(ai_ref[i], (bsz, ns))

        def step(t, carry):
            xr, xi = carry
            r0 = pl.multiple_of(t * bsz, bsz)
            nr = ar * xr - ai * xi + bu_ref[pl.ds(r0, bsz), 0:ns]
            ni = ar * xi + ai * xr + bu_ref[pl.ds(r0, bsz), ns:2 * ns]
            bu_ref[pl.ds(r0, bsz), 0:ns] = nr
            bu_ref[pl.ds(r0, bsz), ns:2 * ns] = ni
            return nr, ni

        xr, xi = lax.fori_loop(0, tc, step, (state_ref[i, :, 0:ns], state_ref[i, :, ns:2 * ns]))
        state_ref[i, :, 0:ns] = xr
        state_ref[i, :, ns:2 * ns] = xi
        ys.append(jnp.dot(bu_ref[...].astype(BF16), cw_ref[i], preferred_element_type=F32))
    y = jnp.concatenate(ys, axis=1) + d_ref[...] * u
    y = 0.5 * y * (1.0 + jnp.tanh(math.sqrt(2.0 / math.pi) * (y + 0.044715 * (y * y * y))))
    z = jnp.dot(y.astype(BF16), wglu_ref[...], preferred_element_type=F32) + bglu_ref[...]
    y = y * (1.0 / (1.0 + jnp.exp(-z)))
    y = y * lax.rsqrt(jnp.mean(y * y, axis=-1, keepdims=True) + RMS_EPS) * g_ref[...]
    o_ref[...] = y.astype(BF16).reshape(tc, bsz, SSM_WIDTH)


def _s5_weights(lam_re, lam_im, log_step, b_re, b_im, c_re, c_im):
    delta = jnp.exp(log_step)[:, None]
    mag = jnp.exp(lam_re * delta)
    ar = mag * jnp.cos(lam_im * delta)
    ai = mag * jnp.sin(lam_im * delta)
    den = lam_re * lam_re + lam_im * lam_im
    num_re = ar - 1.0
    coef_re = (num_re * lam_re + ai * lam_im) / den
    coef_im = (ai * lam_re - num_re * lam_im) / den
    bbar_re = coef_re[..., None] * b_re - coef_im[..., None] * b_im
    bbar_im = coef_re[..., None] * b_im + coef_im[..., None] * b_re
    gs = SSM_GROUPS // S5_SLABS
    eye = jnp.eye(gs, dtype=F32)

    def b_slab(bb):
        bb = bb.reshape(S5_SLABS, gs, SSM_STATE, SSM_GROUP)
        w = jnp.einsum('ab,sapc->sacbp', eye, bb)
        return w.reshape(S5_SLABS, gs * SSM_GROUP, gs * SSM_STATE)

    def c_slab(cc):
        cc = cc.reshape(S5_SLABS, gs, SSM_GROUP, SSM_STATE)
        w = jnp.einsum('ab,sacp->sbpac', eye, cc)
        return w.reshape(S5_SLABS, gs * SSM_STATE, gs * SSM_GROUP)

    bw = jnp.concatenate([b_slab(bbar_re), b_slab(bbar_im)], axis=2).astype(BF16)
    cw = jnp.concatenate([c_slab(c_re), -c_slab(c_im)], axis=1).astype(BF16)
    ar_s = ar.reshape(S5_SLABS, 1, SLAB_STATE)
    ai_s = ai.reshape(S5_SLABS, 1, SLAB_STATE)
    return bw, cw, ar_s, ai_s


def _s5(u_tb, bw, cw, ar_s, ai_s, d_row, w_glu, b_glu_row, g_row):
    seq, bsz, _ = u_tb.shape
    tc = S5_TIME_BLOCK
    assert seq % tc == 0 and bsz % 16 == 0
    full = lambda shape: pl.BlockSpec(shape, lambda t: (0,) * len(shape))
    return pl.pallas_call(
        functools.partial(_s5_kernel, tc=tc, bsz=bsz),
        grid=(seq // tc,),
        in_specs=[pl.BlockSpec((tc, bsz, SSM_WIDTH), lambda t: (t, 0, 0)),
                  full(bw.shape), full(cw.shape), full(ar_s.shape), full(ai_s.shape),
                  full((1, SSM_WIDTH)), full((SSM_WIDTH, SSM_WIDTH)), full((1, SSM_WIDTH)),
                  full((1, SSM_WIDTH))],
        out_specs=pl.BlockSpec((tc, bsz, SSM_WIDTH), lambda t: (t, 0, 0)),
        out_shape=jax.ShapeDtypeStruct((seq, bsz, SSM_WIDTH), BF16),
        scratch_shapes=[pltpu.VMEM((S5_SLABS, bsz, 2 * SLAB_STATE), F32),
                        pltpu.VMEM((tc * bsz, 2 * SLAB_STATE), F32)],
        compiler_params=pltpu.CompilerParams(dimension_semantics=("arbitrary",),
                                             vmem_limit_bytes=VMEM_LIMIT),
        name="s5_branch",
    )(u_tb, bw, cw, ar_s, ai_s, d_row, w_glu, b_glu_row, g_row)


def _layer_norm_rows(v, g, b):
    mu = jnp.mean(v, axis=-1, keepdims=True)
    c = v - mu
    var = jnp.mean(c * c, axis=-1, keepdims=True)
    return c * lax.rsqrt(var + LN_EPS) * g + b


def _outproj_kernel(attn_ref, ssm_ref, x_ref, wo_ref, ag_ref, g1_ref, b1_ref, wr_ref, br_ref,
                    h_ref, eid_ref, gate_ref):
    a = attn_ref[...]
    a = a * lax.rsqrt(jnp.mean(a * a, axis=-1, keepdims=True) + RMS_EPS) * ag_ref[...]
    mix = (jnp.dot(a.astype(BF16), wo_ref[0:ATTN_WIDTH, :], preferred_element_type=F32)
           + jnp.dot(ssm_ref[...], wo_ref[ATTN_WIDTH:, :], preferred_element_type=F32))
    h = _layer_norm_rows(DEEPNORM_ALPHA * x_ref[...] + mix, g1_ref[...], b1_ref[...])
    h_ref[...] = h

    logits = jnp.dot(h.astype(BF16), wr_ref[...], preferred_element_type=F32) + br_ref[...]
    tm = logits.shape[0]
    lane = lax.broadcasted_iota(jnp.int32, (tm, LANES), 1).astype(F32)
    no_lane = float(LANES)
    is_g = lane < N_EXPERT_GROUPS
    gl = jnp.where(is_g, logits, NEG_BIG)
    gmax = jnp.max(gl, axis=1, keepdims=True)
    g_idx = jnp.min(jnp.where(is_g & (gl == gmax), lane, no_lane), axis=1, keepdims=True)
    g_p = 1.0 / jnp.sum(jnp.where(is_g, jnp.exp(gl - gmax), 0.0), axis=1, keepdims=True)
    first = N_EXPERT_GROUPS + g_idx * EXPERTS_PER_GROUP
    in_grp = (lane >= first) & (lane < first + EXPERTS_PER_GROUP)
    el = jnp.where(in_grp, logits, NEG_BIG)
    m1 = jnp.max(el, axis=1, keepdims=True)
    i1 = jnp.min(jnp.where(in_grp & (el == m1), lane, no_lane), axis=1, keepdims=True)
    rest = in_grp & (lane != i1)
    el2 = jnp.where(rest, logits, NEG_BIG)
    m2 = jnp.max(el2, axis=1, keepdims=True)
    i2 = jnp.min(jnp.where(rest & (el2 == m2), lane, no_lane), axis=1, keepdims=True)
    r = jnp.exp(m2 - m1)
    p1 = 1.0 / (1.0 + r)
    p2 = r / (1.0 + r)
    k2 = lax.broadcasted_iota(jnp.int32, (tm, 2), 1)
    eid_ref[...] = (jnp.where(k2 == 0, i1, i2) - N_EXPERT_GROUPS).astype(jnp.int32)
    gate_ref[...] = g_p * jnp.where(k2 == 0, p1, p2)


def _outproj(attn2, ssm2, x2, wo, ag_row, g1_row, b1_row, wr, br_row):
    n_tok = x2.shape[0]
    tm = ROW_BLOCK
    row_spec = lambda w: pl.BlockSpec((tm, w), lambda i: (i, 0))
    full = lambda shape: pl.BlockSpec(shape, lambda i: (0,) * len(shape))
    return pl.pallas_call(
        _outproj_kernel,
        grid=(n_tok // tm,),
        in_specs=[row_spec(ATTN_WIDTH), row_spec(SSM_WIDTH), row_spec(D_MODEL),
                  full((D_MODEL, D_MODEL)), full((1, ATTN_WIDTH)), full((1, D_MODEL)), full((1, D_MODEL)),
                  full((D_MODEL, LANES)), full((1, LANES))],
        out_specs=[row_spec(D_MODEL), row_spec(2), row_spec(2)],
        out_shape=[jax.ShapeDtypeStruct((n_tok, D_MODEL), F32),
                   jax.ShapeDtypeStruct((n_tok, 2), jnp.int32),
                   jax.ShapeDtypeStruct((n_tok, 2), F32)],
        compiler_params=pltpu.CompilerParams(dimension_semantics=("parallel",),
                                             vmem_limit_bytes=VMEM_LIMIT),
        name="outproj_ln_router",
    )(attn2, ssm2, x2, wo, ag_row, g1_row, b1_row, wr, br_row)


def _dispatch_plan(eid, n_tok):
    n_asg = n_tok * 2
    eid_flat = eid.reshape(-1)
    onehot = (eid_flat[:, None] == jnp.arange(N_EXPERTS, dtype=jnp.int32)[None, :]).astype(jnp.int32)
    csum = jnp.cumsum(onehot, axis=0)
    counts = csum[-1]
    rank = jnp.sum(csum * onehot, axis=1) - 1
    padded = ((counts + MOE_BLOCK - 1) // MOE_BLOCK) * MOE_BLOCK
    pends = jnp.cumsum(padded)
    pstarts = pends - padded
    slot = pstarts[eid_flat] + rank
    n_pad = ((n_asg + N_EXPERTS * (MOE_BLOCK - 1) + MOE_BLOCK - 1) // MOE_BLOCK) * MOE_BLOCK
    n_blocks = n_pad // MOE_BLOCK
    asg = jnp.arange(n_asg, dtype=jnp.int32)
    slot_asg = jnp.zeros((n_pad,), jnp.int32).at[slot].set(asg)
    slot_tok = (slot_asg // 2).reshape(n_blocks, 1, MOE_BLOCK)
    slot_dst = ((slot_asg % 2) * n_tok + slot_asg // 2).reshape(n_blocks, 1, MOE_BLOCK)
    blk0 = jnp.arange(n_blocks, dtype=jnp.int32) * MOE_BLOCK
    block_e = jnp.minimum(jnp.searchsorted(pends, blk0, side='right'), N_EXPERTS - 1).astype(jnp.int32)
    n_valid = jnp.clip(counts[block_e] - (blk0 - pstarts[block_e]), 0, MOE_BLOCK).astype(jnp.int32)
    n_used = (pends[-1] // MOE_BLOCK).astype(jnp.int32).reshape(1)
    return block_e, n_valid, n_used, slot_tok, slot_dst


def _moe_kernel(be_ref, nv_ref, nu_ref, tok_ref, dst_ref, h_hbm, wg_ref, wu_ref, wd_ref, y_hbm,
                xbuf, ybuf, sem_in, sem_out):
    i = pl.program_id(0)

    @pl.when(i < nu_ref[0])
    def _():
        def gather_row(r, carry):
            pltpu.make_async_copy(h_hbm.at[pl.ds(tok_ref[0, r], 1)], xbuf.at[pl.ds(r, 1)], sem_in).start()
            return carry
        lax.fori_loop(0, MOE_BLOCK, gather_row, 0)
        pltpu.make_async_copy(h_hbm.at[pl.ds(0, MOE_BLOCK)], xbuf, sem_in).wait()

        xb = xbuf[...].astype(BF16)
        gt = jnp.dot(xb, wg_ref[...], preferred_element_type=F32)
        up = jnp.dot(xb, wu_ref[...], preferred_element_type=F32)
        hid = gt * (1.0 / (1.0 + jnp.exp(-gt))) * up
        ybuf[...] = jnp.dot(hid.astype(BF16), wd_ref[...], preferred_element_type=F32)

        nv = nv_ref[i]

        def row_copy(r):
            return pltpu.make_async_copy(ybuf.at[pl.ds(r, 1)], y_hbm.at[pl.ds(dst_ref[0, r], 1)], sem_out)

        def scatter_row(r, carry):
            row_copy(r).start()
            return carry
        lax.fori_loop(0, nv, scatter_row, 0)

        def wait_row(r, carry):
            row_copy(r).wait()
            return carry
        lax.fori_loop(0, nv, wait_row, 0)


def _experts(h, plan, wg, wu, wd):
    block_e, n_valid, n_used, slot_tok, slot_dst = plan
    n_tok = h.shape[0]
    n_blocks = slot_tok.shape[0]
    smem_blk = pl.BlockSpec((None, 1, MOE_BLOCK), lambda i, be, nv, nu: (i, 0, 0), memory_space=pltpu.SMEM)
    grid_spec = pltpu.PrefetchScalarGridSpec(
        num_scalar_prefetch=3,
        grid=(n_blocks,),
        in_specs=[smem_blk, smem_blk,
                  pl.BlockSpec(memory_space=pl.ANY),
                  pl.BlockSpec((None, D_MODEL, D_EXPERT), lambda i, be, nv, nu: (be[i], 0, 0)),
                  pl.BlockSpec((None, D_MODEL, D_EXPERT), lambda i, be, nv, nu: (be[i], 0, 0)),
                  pl.BlockSpec((None, D_EXPERT, D_MODEL), lambda i, be, nv, nu: (be[i], 0, 0))],
        out_specs=pl.BlockSpec(memory_space=pl.ANY),
        scratch_shapes=[pltpu.VMEM((MOE_BLOCK, D_MODEL), F32), pltpu.VMEM((MOE_BLOCK, D_MODEL), F32),
                        pltpu.SemaphoreType.DMA, pltpu.SemaphoreType.DMA],
    )
    return pl.pallas_call(
        _moe_kernel,
        grid_spec=grid_spec,
        out_shape=jax.ShapeDtypeStruct((2 * n_tok, D_MODEL), F32),
        compiler_params=pltpu.CompilerParams(dimension_semantics=("arbitrary",),
                                             vmem_limit_bytes=VMEM_LIMIT),
        name="moe_experts",
    )(block_e, n_valid, n_used, slot_tok, slot_dst, h, wg, wu, wd)


def _combine_kernel(h_ref, y0_ref, y1_ref, gate_ref, g2_ref, b2_ref, o_ref):
    gate = gate_ref[...]
    moe = gate[:, 0:1] * y0_ref[...] + gate[:, 1:2] * y1_ref[...]
    o_ref[...] = _layer_norm_rows(DEEPNORM_ALPHA * h_ref[...] + moe, g2_ref[...], b2_ref[...])


def _combine(h, y2, gate, g2_row, b2_row):
    n_tok = h.shape[0]
    tm = ROW_BLOCK
    nb = n_tok // tm
    row_spec = lambda w: pl.BlockSpec((tm, w), lambda i: (i, 0))
    full = lambda shape: pl.BlockSpec(shape, lambda i: (0,) * len(shape))
    return pl.pallas_call(
        _combine_kernel,
        grid=(nb,),
        in_specs=[row_spec(D_MODEL),
                  pl.BlockSpec((tm, D_MODEL), lambda i: (i, 0)),
                  pl.BlockSpec((tm, D_MODEL), lambda i: (i + nb, 0)),
                  row_spec(2), full((1, D_MODEL)), full((1, D_MODEL))],
        out_specs=row_spec(D_MODEL),
        out_shape=jax.ShapeDtypeStruct((n_tok, D_MODEL), F32),
        compiler_params=pltpu.CompilerParams(dimension_semantics=("parallel",),
                                             vmem_limit_bytes=VMEM_LIMIT),
        name="combine_ln",
    )(h, y2, y2, gate, g2_row, b2_row)


def kernel(x, w_in, b_fgate, s5_lambda_re, s5_lambda_im, s5_log_step, s5_b_re, s5_b_im, s5_c_re, s5_c_im, s5_d, w_glu, b_glu, attn_norm_g, ssm_norm_g, w_out, ln1_g, ln1_b, w_router_group, b_router_group, w_router_expert, b_router_expert, w_gate, w_up, w_down, ln2_g, ln2_b):
    bsz, seq, d = x.shape
    n_tok = bsz * seq
    a = ATTN_WIDTH
    for l in range(DEPTH):
        x2 = x.reshape(n_tok, d)
        w = w_in[l]
        f0 = 3 * a
        w_cat = jnp.concatenate(
            [w[:, :a] * (HEAD_DIM ** -0.5), w[:, a:f0], w[:, f0 + ATTN_HEADS:],
             jnp.pad(w[:, f0:f0 + ATTN_HEADS], ((0, 0), (0, LANES - ATTN_HEADS)))], axis=1).astype(BF16)
        bf_row = jnp.pad(b_fgate[l], (0, LANES - ATTN_HEADS)).reshape(1, LANES)
        q, k, v, u, cum = _inproj(x2, w_cat, bf_row, seq)

        cum4 = cum.reshape(bsz, seq, ATTN_HEADS // 2, 2).transpose(0, 2, 1, 3)
        attn = _attention(q.reshape(bsz, seq, a), k.reshape(bsz, seq, a), v.reshape(bsz, seq, a), cum4)

        bw, cw, ar_s, ai_s = _s5_weights(s5_lambda_re[l], s5_lambda_im[l], s5_log_step[l],
                                         s5_b_re[l], s5_b_im[l], s5_c_re[l], s5_c_im[l])
        u_tb = u.reshape(bsz, seq, SSM_WIDTH).transpose(1, 0, 2)
        ssm_tb = _s5(u_tb, bw, cw, ar_s, ai_s, s5_d[l].reshape(1, SSM_WIDTH), w_glu[l].astype(BF16),
                     b_glu[l].reshape(1, SSM_WIDTH), ssm_norm_g[l].reshape(1, SSM_WIDTH))
        ssm2 = ssm_tb.transpose(1, 0, 2).reshape(n_tok, SSM_WIDTH)

        wr = jnp.pad(jnp.concatenate([w_router_group[l], w_router_expert[l]], axis=1),
                     ((0, 0), (0, LANES - N_EXPERT_GROUPS - N_EXPERTS))).astype(BF16)
        br_row = jnp.pad(jnp.concatenate([b_router_group[l], b_router_expert[l]]),
                         (0, LANES - N_EXPERT_GROUPS - N_EXPERTS)).reshape(1, LANES)
        h, eid, gate = _outproj(attn.reshape(n_tok, a), ssm2, x2, w_out[l].astype(BF16),
                                attn_norm_g[l].reshape(1, a), ln1_g[l].reshape(1, d), ln1_b[l].reshape(1, d),
                                wr, br_row)

        plan = _dispatch_plan(eid, n_tok)
        y2 = _experts(h, plan, w_gate[l].astype(BF16), w_up[l].astype(BF16), w_down[l].astype(BF16))
        x = _combine(h, y2, gate, ln2_g[l].reshape(1, d), ln2_b[l].reshape(1, d)).reshape(bsz, seq, d)
    return x
```

```python
import functools
import math

import jax
import jax.numpy as jnp
from jax import lax
from jax.experimental import pallas as pl
from jax.experimental.pallas import tpu as pltpu

F32 = jnp.float32
BF16 = jnp.bfloat16

D_MODEL = 1024
HEAD_DIM = 64
ATTN_WIDTH = 512
ATTN_HEADS = 8
SSM_WIDTH = 512
SSM_GROUP = 16
SSM_GROUPS = 32
SSM_STATE = 64
N_EXPERT_GROUPS = 4
EXPERTS_PER_GROUP = 8
N_EXPERTS = 32
D_EXPERT = 512
MOE_BLOCK = 256
DEPTH = 1
DEEPNORM_ALPHA = (2.0 * DEPTH) ** 0.25
LN_EPS = 1e-5
RMS_EPS = 1e-6

LANES = 128
NEG_BIG = -1e30
LOG2E = math.log2(math.e)
ONES_ROWS = 16
VMEM_LIMIT = 48 * 1024 * 1024

ROW_BLOCK = 512
ATTN_BLOCK = 256
ATTN_HEADS_PER_STEP = 8
S5_TIME_BLOCK = 32
S5_SLABS = 4
SLAB_CH = SSM_WIDTH // S5_SLABS
SLAB_STATE = SSM_GROUPS // S5_SLABS * SSM_STATE


def _split3(c):
    hi = c.astype(BF16)
    r1 = c - hi.astype(F32)
    mid = r1.astype(BF16)
    lo = (r1 - mid.astype(F32)).astype(BF16)
    return hi, mid, lo


def _inproj_kernel(x_ref, w_ref, wvt_ref, bf_ref, q_ref, k_ref, vt_ref, u_ref, cum_ref, carry_ref, *,
                   blocks_per_seq):
    i = pl.program_id(0)

    @pl.when(i % blocks_per_seq == 0)
    def _():
        carry_ref[...] = jnp.zeros_like(carry_ref)

    tm = x_ref.shape[0]
    xb = x_ref[...].astype(BF16)
    z = jnp.dot(xb, w_ref[...], preferred_element_type=F32)
    a = ATTN_WIDTH
    q_ref[...] = z[:, :a].astype(BF16)
    k_ref[...] = z[:, a:2 * a].astype(BF16)
    u_ref[...] = z[:, 2 * a:2 * a + SSM_WIDTH]
    vt = lax.dot_general(wvt_ref[...], xb, (((1,), (1,)), ((), ())), preferred_element_type=F32).astype(BF16)
    for jj in range(tm // ATTN_BLOCK):
        vt_ref[jj] = vt[:, jj * ATTN_BLOCK:(jj + 1) * ATTN_BLOCK]
    f = z[:, 2 * a + SSM_WIDTH:] + bf_ref[...]
    logf = jnp.minimum(f, 0.0) - jnp.log1p(jnp.exp(-jnp.abs(f)))
    row = lax.broadcasted_iota(jnp.int32, (tm, tm), 0)
    col = lax.broadcasted_iota(jnp.int32, (tm, tm), 1)
    tri = (row >= col).astype(BF16)
    hi, mid, lo = _split3(logf)
    cum = (jnp.dot(tri, hi, preferred_element_type=F32)
           + jnp.dot(tri, mid, preferred_element_type=F32)
           + jnp.dot(tri, lo, preferred_element_type=F32)) + carry_ref[...]
    carry_ref[...] = cum[tm - 1:tm, :]
    cum_ref[...] = cum[:, :ATTN_HEADS] * LOG2E


def _inproj(x2, w_cat, wvt, bf_row, seq):
    n_tok = x2.shape[0]
    tm = ROW_BLOCK
    assert seq % tm == 0 and n_tok % tm == 0 and tm % ATTN_BLOCK == 0
    bsz = n_tok // seq
    bps = seq // tm
    ncol = w_cat.shape[1]
    vblk = tm // ATTN_BLOCK
    row_spec = lambda w: pl.BlockSpec((tm, w), lambda i: (i, 0))
    return pl.pallas_call(
        functools.partial(_inproj_kernel, blocks_per_seq=bps),
        grid=(n_tok // tm,),
        in_specs=[row_spec(D_MODEL),
                  pl.BlockSpec((D_MODEL, ncol), lambda i: (0, 0)),
                  pl.BlockSpec((ATTN_WIDTH, D_MODEL), lambda i: (0, 0)),
                  pl.BlockSpec((1, LANES), lambda i: (0, 0))],
        out_specs=[row_spec(ATTN_WIDTH), row_spec(ATTN_WIDTH),
                   pl.BlockSpec((None, vblk, ATTN_WIDTH, ATTN_BLOCK), lambda i: (i // bps, i % bps, 0, 0)),
                   row_spec(SSM_WIDTH), row_spec(ATTN_HEADS)],
        out_shape=[jax.ShapeDtypeStruct((n_tok, ATTN_WIDTH), BF16)] * 2
                  + [jax.ShapeDtypeStruct((bsz, seq // ATTN_BLOCK, ATTN_WIDTH, ATTN_BLOCK), BF16),
                     jax.ShapeDtypeStruct((n_tok, SSM_WIDTH), F32),
                     jax.ShapeDtypeStruct((n_tok, ATTN_HEADS), F32)],
        scratch_shapes=[pltpu.VMEM((1, LANES), F32)],
        compiler_params=pltpu.CompilerParams(dimension_semantics=("arbitrary",),
                                             vmem_limit_bytes=VMEM_LIMIT),
        name="inproj",
    )(x2, w_cat, wvt, bf_row)


def _bias_lanes_q(c):
    hi, mid, lo = (t.astype(F32) for t in _split3(c))
    lane = lax.broadcasted_iota(jnp.int32, (c.shape[0], LANES), 1)
    sel = jnp.where(lane == 0, hi, jnp.where(lane == 1, mid, jnp.where(lane == 2, lo,
                    jnp.where(lane < 6, 1.0, 0.0))))
    return sel.astype(BF16)


def _bias_lanes_k(c):
    hi, mid, lo = (t.astype(F32) for t in _split3(c))
    lane = lax.broadcasted_iota(jnp.int32, (c.shape[0], LANES), 1)
    sel = jnp.where(lane < 3, 1.0, jnp.where(lane == 3, -hi, jnp.where(lane == 4, -mid,
                    jnp.where(lane == 5, -lo, 0.0))))
    return sel.astype(BF16)


def _attn_kernel(q_ref, k_ref, vt_ref, cum_ref, o_ref, extk_ref, *, tq, seq, nh):
    qi = pl.program_id(2)
    n_tiles = seq // tq

    @pl.when(qi == 0)
    def _():
        def fill(j, carry):
            r0 = pl.multiple_of(j * tq, tq)
            c = cum_ref[pl.ds(r0, tq), :]
            for h in range(nh):
                extk_ref[h, pl.ds(r0, tq), :] = _bias_lanes_k(c[:, h:h + 1])
            return carry
        lax.fori_loop(0, n_tiles, fill, 0)

    q0 = pl.multiple_of(qi * tq, tq)
    cq = cum_ref[pl.ds(q0, tq), :]
    lane = lax.broadcasted_iota(jnp.int32, (tq, LANES), 1)
    q_augs = []
    for h in range(nh):
        qb = q_ref[:, LANES * (h // 2):LANES * (h // 2 + 1)]
        head_lanes = (lane < HEAD_DIM) if h % 2 == 0 else (lane >= HEAD_DIM)
        qh = jnp.where(head_lanes, qb, jnp.zeros((), BF16))
        q_augs.append(jnp.concatenate([qh, _bias_lanes_q(cq[:, h:h + 1])], axis=1))
    key_i = lax.broadcasted_iota(jnp.int32, (tq, tq), 0)
    qry_i = lax.broadcasted_iota(jnp.int32, (tq, tq), 1)
    causal = key_i <= qry_i
    ones_rows = jnp.ones((ONES_ROWS, tq), BF16)

    def step(j, carry, masked):
        k0 = pl.multiple_of(j * tq, tq)
        scores = []
        for h in range(nh):
            kj = k_ref[pl.ds(k0, tq), LANES * (h // 2):LANES * (h // 2 + 1)]
            k_aug = jnp.concatenate([kj, extk_ref[h, pl.ds(k0, tq), :]], axis=1)
            s = lax.dot_general(k_aug, q_augs[h], (((1,), (1,)), ((), ())), preferred_element_type=F32)
            scores.append(jnp.where(causal, s, NEG_BIG) if masked else s)
        stats = []
        for h in range(nh):
            m = carry[h][0]
            m_new = jnp.maximum(m, jnp.max(scores[h], axis=0, keepdims=True))
            stats.append((m_new, jnp.exp2(m - m_new), jnp.exp2(scores[h] - m_new).astype(BF16)))
        new = []
        for h in range(nh):
            m_new, corr, p = stats[h]
            v_pair = vt_ref[j, LANES * (h // 2):LANES * (h // 2 + 1), :]
            v_aug = jnp.concatenate([v_pair, ones_rows], axis=0)
            res = jnp.dot(v_aug, p, preferred_element_type=F32)
            hh = h % 2
            acc = corr * carry[h][2] + res[HEAD_DIM * hh:HEAD_DIM * (hh + 1), :]
            l = corr * carry[h][1] + res[2 * HEAD_DIM:2 * HEAD_DIM + 1, :]
            new.append((m_new, l, acc))
        return tuple(new)

    init1 = (jnp.full((1, tq), NEG_BIG, F32), jnp.zeros((1, tq), F32), jnp.zeros((HEAD_DIM, tq), F32))
    carry = lax.fori_loop(0, qi, functools.partial(step, masked=False), (init1,) * nh)
    carry = step(qi, carry, True)
    out_t = jnp.concatenate([c[2] / c[1] for c in carry], axis=0)
    o_ref[...] = out_t.T


def _attention(q, k, vt, cum_g):
    bsz, seq, _ = q.shape
    tq = ATTN_BLOCK
    nh = ATTN_HEADS_PER_STEP
    assert seq % tq == 0 and ATTN_HEADS % nh == 0 and nh % 2 == 0
    w = nh * HEAD_DIM
    return pl.pallas_call(
        functools.partial(_attn_kernel, tq=tq, seq=seq, nh=nh),
        grid=(bsz, ATTN_HEADS // nh, seq // tq),
        in_specs=[pl.BlockSpec((None, tq, w), lambda b, g, i: (b, i, g)),
                  pl.BlockSpec((None, seq, w), lambda b, g, i: (b, 0, g)),
                  pl.BlockSpec((None, seq // tq, w, tq), lambda b, g, i: (b, 0, g, 0)),
                  pl.BlockSpec((None, None, seq, nh), lambda b, g, i: (b, g, 0, 0))],
        out_specs=pl.BlockSpec((None, tq, w), lambda b, g, i: (b, i, g)),
        out_shape=jax.ShapeDtypeStruct((bsz, seq, ATTN_WIDTH), F32),
        scratch_shapes=[pltpu.VMEM((nh, seq, LANES), BF16)],
        compiler_params=pltpu.CompilerParams(dimension_semantics=("parallel", "parallel", "arbitrary"),
                                             vmem_limit_bytes=VMEM_LIMIT),
        name="fox_attention",
    )(q, k, vt, cum_g)


def _s5_kernel(u_ref, bw_ref, cw_ref, ar_ref, ai_ref, d_ref, wglu_ref, bglu_ref, g_ref, o_ref,
               state_ref, bu_ref, *, tc, bsz):
    @pl.when(pl.program_id(0) == 0)
    def _():
        state_ref[...] = jnp.zeros_like(state_ref)

    rows = tc * bsz
    ns = SLAB_STATE
    u = u_ref[...].reshape(rows, SSM_WIDTH)
    ub = u.astype(BF16)
    ys = []
    for i in range(S5_SLABS):
        bu_ref[...] = jnp.dot(ub[:, SLAB_CH * i:SLAB_CH * (i + 1)], bw_ref[i], preferred_element_type=F32)
        ar = jnp.broadcast_to(ar_ref[i], (bsz, ns))
        ai = jnp.broadcast_to(ai_ref[i], (bsz, ns))

        def step(t, carry):
            xr, xi = carry
            r0 = pl.multiple_of(t * bsz, bsz)
            nr = ar * xr - ai * xi + bu_ref[pl.ds(r0, bsz), 0:ns]
            ni = ar * xi + ai * xr + bu_ref[pl.ds(r0, bsz), ns:2 * ns]
            bu_ref[pl.ds(r0, bsz), 0:ns] = nr
            bu_ref[pl.ds(r0, bsz), ns:2 * ns] = ni
            return nr, ni

        xr, xi = lax.fori_loop(0, tc, step, (state_ref[i, :, 0:ns], state_ref[i, :, ns:2 * ns]))
        state_ref[i, :, 0:ns] = xr
        state_ref[i, :, ns:2 * ns] = xi
        ys.append(jnp.dot(bu_ref[...].astype(BF16), cw_ref[i], preferred_element_type=F32))
    y = jnp.concatenate(ys, axis=1) + d_ref[...] * u
    y = 0.5 * y * (1.0 + jnp.tanh(math.sqrt(2.0 / math.pi) * (y + 0.044715 * (y * y * y))))
    z = jnp.dot(y.astype(BF16), wglu_ref[...], preferred_element_type=F32) + bglu_ref[...]
    y = y * (1.0 / (1.0 + jnp.exp(-z)))
    y = y * lax.rsqrt(jnp.mean(y * y, axis=-1, keepdims=True) + RMS_EPS) * g_ref[...]
    o_ref[...] = y.astype(BF16).reshape(tc, bsz, SSM_WIDTH)


def _s5_weights(lam_re, lam_im, log_step, b_re, b_im, c_re, c_im):
    delta = jnp.exp(log_step)[:, None]
    mag = jnp.exp(lam_re * delta)
    ar = mag * jnp.cos(lam_im * delta)
    ai = mag * jnp.sin(lam_im * delta)
    den = lam_re * lam_re + lam_im * lam_im
    num_re = ar - 1.0
    coef_re = (num_re * lam_re + ai * lam_im) / den
    coef_im = (ai * lam_re - num_re * lam_im) / den
    bbar_re = coef_re[..., None] * b_re - coef_im[..., None] * b_im
    bbar_im = coef_re[..., None] * b_im + coef_im[..., None] * b_re
    gs = SSM_GROUPS // S5_SLABS
    eye = jnp.eye(gs, dtype=F32)

    def b_slab(bb):
        bb = bb.reshape(S5_SLABS, gs, SSM_STATE, SSM_GROUP)
        w = jnp.einsum('ab,sapc->sacbp', eye, bb)
        return w.reshape(S5_SLABS, gs * SSM_GROUP, gs * SSM_STATE)

    def c_slab(cc):
        cc = cc.reshape(S5_SLABS, gs, SSM_GROUP, SSM_STATE)
        w = jnp.einsum('ab,sacp->sbpac', eye, cc)
        return w.reshape(S5_SLABS, gs * SSM_STATE, gs * SSM_GROUP)

    bw = jnp.concatenate([b_slab(bbar_re), b_slab(bbar_im)], axis=2).astype(BF16)
    cw = jnp.concatenate([c_slab(c_re), -c_slab(c_im)], axis=1).astype(BF16)
    ar_s = ar.reshape(S5_SLABS, 1, SLAB_STATE)
    ai_s = ai.reshape(S5_SLABS, 1, SLAB_STATE)
    return bw, cw, ar_s, ai_s


def _s5(u_tb, bw, cw, ar_s, ai_s, d_row, w_glu, b_glu_row, g_row):
    seq, bsz, _ = u_tb.shape
    tc = S5_TIME_BLOCK
    assert seq % tc == 0 and bsz % 16 == 0
    full = lambda shape: pl.BlockSpec(shape, lambda t: (0,) * len(shape))
    return pl.pallas_call(
        functools.partial(_s5_kernel, tc=tc, bsz=bsz),
        grid=(seq // tc,),
        in_specs=[pl.BlockSpec((tc, bsz, SSM_WIDTH), lambda t: (t, 0, 0)),
                  full(bw.shape), full(cw.shape), full(ar_s.shape), full(ai_s.shape),
                  full((1, SSM_WIDTH)), full((SSM_WIDTH, SSM_WIDTH)), full((1, SSM_WIDTH)),
                  full((1, SSM_WIDTH))],
        out_specs=pl.BlockSpec((tc, bsz, SSM_WIDTH), lambda t: (t, 0, 0)),
        out_shape=jax.ShapeDtypeStruct((seq, bsz, SSM_WIDTH), BF16),
        scratch_shapes=[pltpu.VMEM((S5_SLABS, bsz, 2 * SLAB_STATE), F32),
                        pltpu.VMEM((tc * bsz, 2 * SLAB_STATE), F32)],
        compiler_params=pltpu.CompilerParams(dimension_semantics=("arbitrary",),
                                             vmem_limit_bytes=VMEM_LIMIT),
        name="s5_branch",
    )(u_tb, bw, cw, ar_s, ai_s, d_row, w_glu, b_glu_row, g_row)


def _layer_norm_rows(v, g, b):
    mu = jnp.mean(v, axis=-1, keepdims=True)
    c = v - mu
    var = jnp.mean(c * c, axis=-1, keepdims=True)
    return c * lax.rsqrt(var + LN_EPS) * g + b


def _outproj_kernel(attn_ref, ssm_ref, x_ref, wo_ref, ag_ref, g1_ref, b1_ref, wr_ref, br_ref,
                    h_ref, eid_ref, gate_ref):
    a = attn_ref[...]
    a = a * lax.rsqrt(jnp.mean(a * a, axis=-1, keepdims=True) + RMS_EPS) * ag_ref[...]
    mix = (jnp.dot(a.astype(BF16), wo_ref[0:ATTN_WIDTH, :], preferred_element_type=F32)
           + jnp.dot(ssm_ref[...], wo_ref[ATTN_WIDTH:, :], preferred_element_type=F32))
    h = _layer_norm_rows(DEEPNORM_ALPHA * x_ref[...] + mix, g1_ref[...], b1_ref[...])
    h_ref[...] = h

    logits = jnp.dot(h.astype(BF16), wr_ref[...], preferred_element_type=F32) + br_ref[...]
    tm = logits.shape[0]
    lane = lax.broadcasted_iota(jnp.int32, (tm, LANES), 1).astype(F32)
    no_lane = float(LANES)
    is_g = lane < N_EXPERT_GROUPS
    gl = jnp.where(is_g, logits, NEG_BIG)
    gmax = jnp.max(gl, axis=1, keepdims=True)
    g_idx = jnp.min(jnp.where(is_g & (gl == gmax), lane, no_lane), axis=1, keepdims=True)
    g_p = 1.0 / jnp.sum(jnp.where(is_g, jnp.exp(gl - gmax), 0.0), axis=1, keepdims=True)
    first = N_EXPERT_GROUPS + g_idx * EXPERTS_PER_GROUP
    in_grp = (lane >= first) & (lane < first + EXPERTS_PER_GROUP)
    el = jnp.where(in_grp, logits, NEG_BIG)
    m1 = jnp.max(el, axis=1, keepdims=True)
    i1 = jnp.min(jnp.where(in_grp & (el == m1), lane, no_lane), axis=1, keepdims=True)
    rest = in_grp & (lane != i1)
    el2 = jnp.where(rest, logits, NEG_BIG)
    m2 = jnp.max(el2, axis=1, keepdims=True)
    i2 = jnp.min(jnp.where(rest & (el2 == m2), lane, no_lane), axis=1, keepdims=True)
    r = jnp.exp(m2 - m1)
    p1 = 1.0 / (1.0 + r)
    p2 = r / (1.0 + r)
    k2 = lax.broadcasted_iota(jnp.int32, (tm, 2), 1)
    eid_ref[...] = (jnp.where(k2 == 0, i1, i2) - N_EXPERT_GROUPS).astype(jnp.int32)
    gate_ref[...] = g_p * jnp.where(k2 == 0, p1, p2)


def _outproj(attn2, ssm2, x2, wo, ag_row, g1_row, b1_row, wr, br_row):
    n_tok = x2.shape[0]
    tm = ROW_BLOCK
    row_spec = lambda w: pl.BlockSpec((tm, w), lambda i: (i, 0))
    full = lambda shape: pl.BlockSpec(shape, lambda i: (0,) * len(shape))
    return pl.pallas_call(
        _outproj_kernel,
        grid=(n_tok // tm,),
        in_specs=[row_spec(ATTN_WIDTH), row_spec(SSM_WIDTH), row_spec(D_MODEL),
                  full((D_MODEL, D_MODEL)), full((1, ATTN_WIDTH)), full((1, D_MODEL)), full((1, D_MODEL)),
                  full((D_MODEL, LANES)), full((1, LANES))],
        out_specs=[row_spec(D_MODEL), row_spec(2), row_spec(2)],
        out_shape=[jax.ShapeDtypeStruct((n_tok, D_MODEL), F32),
                   jax.ShapeDtypeStruct((n_tok, 2), jnp.int32),
                   jax.ShapeDtypeStruct((n_tok, 2), F32)],
        compiler_params=pltpu.CompilerParams(dimension_semantics=("parallel",),
                                             vmem_limit_bytes=VMEM_LIMIT),
        name="outproj_ln_router",
    )(attn2, ssm2, x2, wo, ag_row, g1_row, b1_row, wr, br_row)


def _dispatch_plan(eid, n_tok):
    n_asg = n_tok * 2
    eid_flat = eid.reshape(-1)
    onehot = (eid_flat[:, None] == jnp.arange(N_EXPERTS, dtype=jnp.int32)[None, :]).astype(jnp.int32)
    csum = jnp.cumsum(onehot, axis=0)
    counts = csum[-1]
    rank = jnp.sum(csum * onehot, axis=1) - 1
    padded = ((counts + MOE_BLOCK - 1) // MOE_BLOCK) * MOE_BLOCK
    pends = jnp.cumsum(padded)
    pstarts = pends - padded
    slot = pstarts[eid_flat] + rank
    n_pad = ((n_asg + N_EXPERTS * (MOE_BLOCK - 1) + MOE_BLOCK - 1) // MOE_BLOCK) * MOE_BLOCK
    n_blocks = n_pad // MOE_BLOCK
    asg = jnp.arange(n_asg, dtype=jnp.int32)
    slot_asg = jnp.zeros((n_pad,), jnp.int32).at[slot].set(asg)
    slot_tok = (slot_asg // 2).reshape(n_blocks, 1, MOE_BLOCK)
    slot_dst = ((slot_asg % 2) * n_tok + slot_asg // 2).reshape(n_blocks, 1, MOE_BLOCK)
    blk0 = jnp.arange(n_blocks, dtype=jnp.int32) * MOE_BLOCK
    block_e = jnp.minimum(jnp.searchsorted(pends, blk0, side='right'), N_EXPERTS - 1).astype(jnp.int32)
    n_valid = jnp.clip(counts[block_e] - (blk0 - pstarts[block_e]), 0, MOE_BLOCK).astype(jnp.int32)
    n_used = (pends[-1] // MOE_BLOCK).astype(jnp.int32).reshape(1)
    return block_e, n_valid, n_used, slot_tok, slot_dst


def _moe_kernel(be_ref, nv_ref, nu_ref, tok_ref, dst_ref, h_hbm, wg_ref, wu_ref, wd_ref, y_hbm,
                xbuf, ybuf, sem_in, sem_out):
    i = pl.program_id(0)

    @pl.when(i < nu_ref[0])
    def _():
        def gather_row(r, carry):
            pltpu.make_async_copy(h_hbm.at[pl.ds(tok_ref[0, r], 1)], xbuf.at[pl.ds(r, 1)], sem_in).start()
            return carry
        lax.fori_loop(0, MOE_BLOCK, gather_row, 0)
        pltpu.make_async_copy(h_hbm.at[pl.ds(0, MOE_BLOCK)], xbuf, sem_in).wait()

        xb = xbuf[...].astype(BF16)
        gt = jnp.dot(xb, wg_ref[...], preferred_element_type=F32)
        up = jnp.dot(xb, wu_ref[...], preferred_element_type=F32)
        hid = gt * (1.0 / (1.0 + jnp.exp(-gt))) * up
        ybuf[...] = jnp.dot(hid.astype(BF16), wd_ref[...], preferred_element_type=F32)

        nv = nv_ref[i]

        def row_copy(r):
            return pltpu.make_async_copy(ybuf.at[pl.ds(r, 1)], y_hbm.at[pl.ds(dst_ref[0, r], 1)], sem_out)

        def scatter_row(r, carry):
            row_copy(r).start()
            return carry
        lax.fori_loop(0, nv, scatter_row, 0)

        def wait_row(r, carry):
            row_copy(r).wait()
            return carry
        lax.fori_loop(0, nv, wait_row, 0)


def _experts(h, plan, wg, wu, wd):
    block_e, n_valid, n_used, slot_tok, slot_dst = plan
    n_tok = h.shape[0]
    n_blocks = slot_tok.shape[0]
    smem_blk = pl.BlockSpec((None, 1, MOE_BLOCK), lambda i, be, nv, nu: (i, 0, 0), memory_space=pltpu.SMEM)
    grid_spec = pltpu.PrefetchScalarGridSpec(
        num_scalar_prefetch=3,
        grid=(n_blocks,),
        in_specs=[smem_blk, smem_blk,
                  pl.BlockSpec(memory_space=pl.ANY),
                  pl.BlockSpec((None, D_MODEL, D_EXPERT), lambda i, be, nv, nu: (be[i], 0, 0)),
                  pl.BlockSpec((None, D_MODEL, D_EXPERT), lambda i, be, nv, nu: (be[i], 0, 0)),
                  pl.BlockSpec((None, D_EXPERT, D_MODEL), lambda i, be, nv, nu: (be[i], 0, 0))],
        out_specs=pl.BlockSpec(memory_space=pl.ANY),
        scratch_shapes=[pltpu.VMEM((MOE_BLOCK, D_MODEL), F32), pltpu.VMEM((MOE_BLOCK, D_MODEL), F32),
                        pltpu.SemaphoreType.DMA, pltpu.SemaphoreType.DMA],
    )
    return pl.pallas_call(
        _moe_kernel,
        grid_spec=grid_spec,
        out_shape=jax.ShapeDtypeStruct((2 * n_tok, D_MODEL), F32),
        compiler_params=pltpu.CompilerParams(dimension_semantics=("arbitrary",),
                                             vmem_limit_bytes=VMEM_LIMIT),
        name="moe_experts",
    )(block_e, n_valid, n_used, slot_tok, slot_dst, h, wg, wu, wd)


def _combine_kernel(h_ref, y0_ref, y1_ref, gate_ref, g2_ref, b2_ref, o_ref):
    gate = gate_ref[...]
    moe = gate[:, 0:1] * y0_ref[...] + gate[:, 1:2] * y1_ref[...]
    o_ref[...] = _layer_norm_rows(DEEPNORM_ALPHA * h_ref[...] + moe, g2_ref[...], b2_ref[...])


def _combine(h, y2, gate, g2_row, b2_row):
    n_tok = h.shape[0]
    tm = ROW_BLOCK
    nb = n_tok // tm
    row_spec = lambda w: pl.BlockSpec((tm, w), lambda i: (i, 0))
    full = lambda shape: pl.BlockSpec(shape, lambda i: (0,) * len(shape))
    return pl.pallas_call(
        _combine_kernel,
        grid=(nb,),
        in_specs=[row_spec(D_MODEL),
                  pl.BlockSpec((tm, D_MODEL), lambda i: (i, 0)),
                  pl.BlockSpec((tm, D_MODEL), lambda i: (i + nb, 0)),
                  row_spec(2), full((1, D_MODEL)), full((1, D_MODEL))],
        out_specs=row_spec(D_MODEL),
        out_shape=jax.ShapeDtypeStruct((n_tok, D_MODEL), F32),
        compiler_params=pltpu.CompilerParams(dimension_semantics=("parallel",),
                                             vmem_limit_bytes=VMEM_LIMIT),
        name="combine_ln",
    )(h, y2, y2, gate, g2_row, b2_row)


def kernel(x, w_in, b_fgate, s5_lambda_re, s5_lambda_im, s5_log_step, s5_b_re, s5_b_im, s5_c_re, s5_c_im, s5_d, w_glu, b_glu, attn_norm_g, ssm_norm_g, w_out, ln1_g, ln1_b, w_router_group, b_router_group, w_router_expert, b_router_expert, w_gate, w_up, w_down, ln2_g, ln2_b):
    bsz, seq, d = x.shape
    n_tok = bsz * seq
    a = ATTN_WIDTH
    for l in range(DEPTH):
        x2 = x.reshape(n_tok, d)
        w = w_in[l]
        f0 = 3 * a
        w_cat = jnp.concatenate(
            [w[:, :a] * (HEAD_DIM ** -0.5 * LOG2E), w[:, a:2 * a], w[:, f0 + ATTN_HEADS:],
             jnp.pad(w[:, f0:f0 + ATTN_HEADS], ((0, 0), (0, LANES - ATTN_HEADS)))], axis=1).astype(BF16)
        wvt = w[:, 2 * a:f0].T.astype(BF16)
        bf_row = jnp.pad(b_fgate[l], (0, LANES - ATTN_HEADS)).reshape(1, LANES)
        q, k, vt, u, cum = _inproj(x2, w_cat, wvt, bf_row, seq)

        cum_g = cum.reshape(bsz, seq, ATTN_HEADS // ATTN_HEADS_PER_STEP, ATTN_HEADS_PER_STEP).transpose(0, 2, 1, 3)
        attn = _attention(q.reshape(bsz, seq, a), k.reshape(bsz, seq, a), vt, cum_g)

        bw, cw, ar_s, ai_s = _s5_weights(s5_lambda_re[l], s5_lambda_im[l], s5_log_step[l],
                                         s5_b_re[l], s5_b_im[l], s5_c_re[l], s5_c_im[l])
        u_tb = u.reshape(bsz, seq, SSM_WIDTH).transpose(1, 0, 2)
        ssm_tb = _s5(u_tb, bw, cw, ar_s, ai_s, s5_d[l].reshape(1, SSM_WIDTH), w_glu[l].astype(BF16),
                     b_glu[l].reshape(1, SSM_WIDTH), ssm_norm_g[l].reshape(1, SSM_WIDTH))
        ssm2 = ssm_tb.transpose(1, 0, 2).reshape(n_tok, SSM_WIDTH)

        wr = jnp.pad(jnp.concatenate([w_router_group[l], w_router_expert[l]], axis=1),
                     ((0, 0), (0, LANES - N_EXPERT_GROUPS - N_EXPERTS))).astype(BF16)
        br_row = jnp.pad(jnp.concatenate([b_router_group[l], b_router_expert[l]]),
                         (0, LANES - N_EXPERT_GROUPS - N_EXPERTS)).reshape(1, LANES)
        h, eid, gate = _outproj(attn.reshape(n_tok, a), ssm2, x2, w_out[l].astype(BF16),
                                attn_norm_g[l].reshape(1, a), ln1_g[l].reshape(1, d), ln1_b[l].reshape(1, d),
                                wr, br_row)

        plan = _dispatch_plan(eid, n_tok)
        y2 = _experts(h, plan, w_gate[l].astype(BF16), w_up[l].astype(BF16), w_down[l].astype(BF16))
        x = _combine(h, y2, gate, ln2_g[l].reshape(1, d), ln2_b[l].reshape(1, d)).reshape(bsz, seq, d)
    return x
```

```python
import functools
import math

import jax
import jax.numpy as jnp
from jax import lax
from jax.experimental import pallas as pl
from jax.experimental.pallas import tpu as pltpu

F32 = jnp.float32
BF16 = jnp.bfloat16

D_MODEL = 1024
HEAD_DIM = 64
ATTN_WIDTH = 512
ATTN_HEADS = 8
SSM_WIDTH = 512
SSM_GROUP = 16
SSM_GROUPS = 32
SSM_STATE = 64
N_EXPERT_GROUPS = 4
EXPERTS_PER_GROUP = 8
N_EXPERTS = 32
D_EXPERT = 512
MOE_BLOCK = 256
DEPTH = 1
DEEPNORM_ALPHA = (2.0 * DEPTH) ** 0.25
LN_EPS = 1e-5
RMS_EPS = 1e-6

LANES = 128
NEG_BIG = -1e30
LOG2E = math.log2(math.e)
ONES_ROWS = 16
VMEM_LIMIT = 48 * 1024 * 1024

ROW_BLOCK = 512
ATTN_BLOCK = 256
ATTN_HEADS_PER_STEP = 8
S5_TIME_BLOCK = 32
S5_SLABS = 4
SLAB_CH = SSM_WIDTH // S5_SLABS
SLAB_STATE = SSM_GROUPS // S5_SLABS * SSM_STATE


def _split3(c):
    hi = c.astype(BF16)
    r1 = c - hi.astype(F32)
    mid = r1.astype(BF16)
    lo = (r1 - mid.astype(F32)).astype(BF16)
    return hi, mid, lo


def _inproj_kernel(x_ref, w_ref, wvt_ref, bf_ref, q_ref, k_ref, vt_ref, u_ref, cum_ref, carry_ref, *,
                   blocks_per_seq):
    i = pl.program_id(0)

    @pl.when(i % blocks_per_seq == 0)
    def _():
        carry_ref[...] = jnp.zeros_like(carry_ref)

    tm = x_ref.shape[0]
    xb = x_ref[...].astype(BF16)
    z = jnp.dot(xb, w_ref[...], preferred_element_type=F32)
    a = ATTN_WIDTH
    q_ref[...] = z[:, :a].astype(BF16)
    k_ref[...] = z[:, a:2 * a].astype(BF16)
    u_ref[...] = z[:, 2 * a:2 * a + SSM_WIDTH]
    vt = lax.dot_general(wvt_ref[...], xb, (((1,), (1,)), ((), ())), preferred_element_type=F32).astype(BF16)
    for jj in range(tm // ATTN_BLOCK):
        vt_ref[jj] = vt[:, jj * ATTN_BLOCK:(jj + 1) * ATTN_BLOCK]
    f = z[:, 2 * a + SSM_WIDTH:] + bf_ref[...]
    logf = jnp.minimum(f, 0.0) - jnp.log1p(jnp.exp(-jnp.abs(f)))
    row = lax.broadcasted_iota(jnp.int32, (tm, tm), 0)
    col = lax.broadcasted_iota(jnp.int32, (tm, tm), 1)
    tri = (row >= col).astype(BF16)
    hi, mid, lo = _split3(logf)
    cum = (jnp.dot(tri, hi, preferred_element_type=F32)
           + jnp.dot(tri, mid, preferred_element_type=F32)
           + jnp.dot(tri, lo, preferred_element_type=F32)) + carry_ref[...]
    carry_ref[...] = cum[tm - 1:tm, :]
    cum_ref[...] = cum[:, :ATTN_HEADS] * LOG2E


def _inproj(x2, w_cat, wvt, bf_row, seq):
    n_tok = x2.shape[0]
    tm = ROW_BLOCK
    assert seq % tm == 0 and n_tok % tm == 0 and tm % ATTN_BLOCK == 0
    bsz = n_tok // seq
    bps = seq // tm
    ncol = w_cat.shape[1]
    vblk = tm // ATTN_BLOCK
    row_spec = lambda w: pl.BlockSpec((tm, w), lambda i: (i, 0))
    return pl.pallas_call(
        functools.partial(_inproj_kernel, blocks_per_seq=bps),
        grid=(n_tok // tm,),
        in_specs=[row_spec(D_MODEL),
                  pl.BlockSpec((D_MODEL, ncol), lambda i: (0, 0)),
                  pl.BlockSpec((ATTN_WIDTH, D_MODEL), lambda i: (0, 0)),
                  pl.BlockSpec((1, LANES), lambda i: (0, 0))],
        out_specs=[row_spec(ATTN_WIDTH), row_spec(ATTN_WIDTH),
                   pl.BlockSpec((None, vblk, ATTN_WIDTH, ATTN_BLOCK), lambda i: (i // bps, i % bps, 0, 0)),
                   row_spec(SSM_WIDTH), row_spec(ATTN_HEADS)],
        out_shape=[jax.ShapeDtypeStruct((n_tok, ATTN_WIDTH), BF16)] * 2
                  + [jax.ShapeDtypeStruct((bsz, seq // ATTN_BLOCK, ATTN_WIDTH, ATTN_BLOCK), BF16),
                     jax.ShapeDtypeStruct((n_tok, SSM_WIDTH), F32),
                     jax.ShapeDtypeStruct((n_tok, ATTN_HEADS), F32)],
        scratch_shapes=[pltpu.VMEM((1, LANES), F32)],
        compiler_params=pltpu.CompilerParams(dimension_semantics=("arbitrary",),
                                             vmem_limit_bytes=VMEM_LIMIT),
        name="inproj",
    )(x2, w_cat, wvt, bf_row)


def _bias_lanes_q(c):
    hi, mid, lo = (t.astype(F32) for t in _split3(c))
    lane = lax.broadcasted_iota(jnp.int32, (c.shape[0], LANES), 1)
    sel = jnp.where(lane == 0, hi, jnp.where(lane == 1, mid, jnp.where(lane == 2, lo,
                    jnp.where(lane < 6, 1.0, 0.0))))
    return sel.astype(BF16)


def _bias_lanes_k(c):
    hi, mid, lo = (t.astype(F32) for t in _split3(c))
    lane = lax.broadcasted_iota(jnp.int32, (c.shape[0], LANES), 1)
    sel = jnp.where(lane < 3, 1.0, jnp.where(lane == 3, -hi, jnp.where(lane == 4, -mid,
                    jnp.where(lane == 5, -lo, 0.0))))
    return sel.astype(BF16)


def _attn_kernel(q_ref, k_ref, vt_ref, cum_ref, o_ref, extk_ref, *, tq, seq, nh):
    qi = pl.program_id(2)
    n_tiles = seq // tq

    @pl.when(qi == 0)
    def _():
        def fill(j, carry):
            r0 = pl.multiple_of(j * tq, tq)
            c = cum_ref[pl.ds(r0, tq), :]
            for h in range(nh):
                extk_ref[h, pl.ds(r0, tq), :] = _bias_lanes_k(c[:, h:h + 1])
            return carry
        lax.fori_loop(0, n_tiles, fill, 0)

    q0 = pl.multiple_of(qi * tq, tq)
    cq = cum_ref[pl.ds(q0, tq), :]
    lane = lax.broadcasted_iota(jnp.int32, (tq, LANES), 1)
    q_augs = []
    for h in range(nh):
        qb = q_ref[:, LANES * (h // 2):LANES * (h // 2 + 1)]
        head_lanes = (lane < HEAD_DIM) if h % 2 == 0 else (lane >= HEAD_DIM)
        qh = jnp.where(head_lanes, qb, jnp.zeros((), BF16))
        q_augs.append(jnp.concatenate([qh, _bias_lanes_q(cq[:, h:h + 1])], axis=1))
    key_i = lax.broadcasted_iota(jnp.int32, (tq, tq), 0)
    qry_i = lax.broadcasted_iota(jnp.int32, (tq, tq), 1)
    causal = key_i <= qry_i
    ones_rows = jnp.ones((ONES_ROWS, tq), BF16)

    def step(j, carry, masked):
        k0 = pl.multiple_of(j * tq, tq)
        scores = []
        for h in range(nh):
            kj = k_ref[pl.ds(k0, tq), LANES * (h // 2):LANES * (h // 2 + 1)]
            k_aug = jnp.concatenate([kj, extk_ref[h, pl.ds(k0, tq), :]], axis=1)
            s = lax.dot_general(k_aug, q_augs[h], (((1,), (1,)), ((), ())), preferred_element_type=F32)
            scores.append(jnp.where(causal, s, NEG_BIG) if masked else s)
        stats = []
        for h in range(nh):
            m = carry[h][0]
            m_new = jnp.maximum(m, jnp.max(scores[h], axis=0, keepdims=True))
            stats.append((m_new, jnp.exp2(m - m_new), jnp.exp2(scores[h] - m_new).astype(BF16)))
        new = []
        for h in range(nh):
            m_new, corr, p = stats[h]
            v_pair = vt_ref[j, LANES * (h // 2):LANES * (h // 2 + 1), :]
            v_aug = jnp.concatenate([v_pair, ones_rows], axis=0)
            res = jnp.dot(v_aug, p, preferred_element_type=F32)
            hh = h % 2
            acc = corr * carry[h][2] + res[HEAD_DIM * hh:HEAD_DIM * (hh + 1), :]
            l = corr * carry[h][1] + res[2 * HEAD_DIM:2 * HEAD_DIM + 1, :]
            new.append((m_new, l, acc))
        return tuple(new)

    init1 = (jnp.full((1, tq), NEG_BIG, F32), jnp.zeros((1, tq), F32), jnp.zeros((HEAD_DIM, tq), F32))
    carry = lax.fori_loop(0, qi, functools.partial(step, masked=False), (init1,) * nh)
    carry = step(qi, carry, True)
    out_t = jnp.concatenate([c[2] / c[1] for c in carry], axis=0)
    o_ref[...] = out_t.T


def _attention(q, k, vt, cum_g):
    bsz, seq, _ = q.shape
    tq = ATTN_BLOCK
    nh = ATTN_HEADS_PER_STEP
    assert seq % tq == 0 and ATTN_HEADS % nh == 0 and nh % 2 == 0
    w = nh * HEAD_DIM
    return pl.pallas_call(
        functools.partial(_attn_kernel, tq=tq, seq=seq, nh=nh),
        grid=(bsz, ATTN_HEADS // nh, seq // tq),
        in_specs=[pl.BlockSpec((None, tq, w), lambda b, g, i: (b, i, g)),
                  pl.BlockSpec((None, seq, w), lambda b, g, i: (b, 0, g)),
                  pl.BlockSpec((None, seq // tq, w, tq), lambda b, g, i: (b, 0, g, 0)),
                  pl.BlockSpec((None, None, seq, nh), lambda b, g, i: (b, g, 0, 0))],
        out_specs=pl.BlockSpec((None, tq, w), lambda b, g, i: (b, i, g)),
        out_shape=jax.ShapeDtypeStruct((bsz, seq, ATTN_WIDTH), F32),
        scratch_shapes=[pltpu.VMEM((nh, seq, LANES), BF16)],
        compiler_params=pltpu.CompilerParams(dimension_semantics=("parallel", "parallel", "arbitrary"),
                                             vmem_limit_bytes=VMEM_LIMIT),
        name="fox_attention",
    )(q, k, vt, cum_g)


def _s5_kernel(u_ref, bw_ref, cw_ref, ar_ref, ai_ref, d_ref, wglu_ref, bglu_ref, g_ref, o_ref,
               state_ref, bu_ref, *, tc, bsz):
    @pl.when(pl.program_id(0) == 0)
    def _():
        state_ref[...] = jnp.zeros_like(state_ref)

    rows = tc * bsz
    ns = SLAB_STATE
    u = u_ref[...].reshape(rows, SSM_WIDTH)
    ub = u.astype(BF16)
    ys = []
    for i in range(S5_SLABS):
        bu_ref[...] = jnp.dot(ub[:, SLAB_CH * i:SLAB_CH * (i + 1)], bw_ref[i], preferred_element_type=F32)
        ar = jnp.broadcast_to(ar_ref[i], (bsz, ns))
        ai = jnp.broadcast_to(ai_ref[i], (bsz, ns))

        def step(t, carry):
            xr, xi = carry
            r0 = pl.multiple_of(t * bsz, bsz)
            nr = ar * xr - ai * xi + bu_ref[pl.ds(r0, bsz), 0:ns]
            ni = ar * xi + ai * xr + bu_ref[pl.ds(r0, bsz), ns:2 * ns]
            bu_ref[pl.ds(r0, bsz), 0:ns] = nr
            bu_ref[pl.ds(r0, bsz), ns:2 * ns] = ni
            return nr, ni

        xr, xi = lax.fori_loop(0, tc, step, (state_ref[i, :, 0:ns], state_ref[i, :, ns:2 * ns]))
        state_ref[i, :, 0:ns] = xr
        state_ref[i, :, ns:2 * ns] = xi
        ys.append(jnp.dot(bu_ref[...].astype(BF16), cw_ref[i], preferred_element_type=F32))
    y = jnp.concatenate(ys, axis=1) + d_ref[...] * u
    y = 0.5 * y * (1.0 + jnp.tanh(math.sqrt(2.0 / math.pi) * (y + 0.044715 * (y * y * y))))
    z = jnp.dot(y.astype(BF16), wglu_ref[...], preferred_element_type=F32) + bglu_ref[...]
    y = y * (1.0 / (1.0 + jnp.exp(-z)))
    y = y * lax.rsqrt(jnp.mean(y * y, axis=-1, keepdims=True) + RMS_EPS) * g_ref[...]
    o_ref[...] = y.astype(BF16).reshape(tc, bsz, SSM_WIDTH)


def _s5_weights(lam_re, lam_im, log_step, b_re, b_im, c_re, c_im):
    delta = jnp.exp(log_step)[:, None]
    mag = jnp.exp(lam_re * delta)
    ar = mag * jnp.cos(lam_im * delta)
    ai = mag * jnp.sin(lam_im * delta)
    den = lam_re * lam_re + lam_im * lam_im
    num_re = ar - 1.0
    coef_re = (num_re * lam_re + ai * lam_im) / den
    coef_im = (ai * lam_re - num_re * lam_im) / den
    bbar_re = coef_re[..., None] * b_re - coef_im[..., None] * b_im
    bbar_im = coef_re[..., None] * b_im + coef_im[..., None] * b_re
    gs = SSM_GROUPS // S5_SLABS
    eye = jnp.eye(gs, dtype=F32)

    def b_slab(bb):
        bb = bb.reshape(S5_SLABS, gs, SSM_STATE, SSM_GROUP)
        w = jnp.einsum('ab,sapc->sacbp', eye, bb)
        return w.reshape(S5_SLABS, gs * SSM_GROUP, gs * SSM_STATE)

    def c_slab(cc):
        cc = cc.reshape(S5_SLABS, gs, SSM_GROUP, SSM_STATE)
        w = jnp.einsum('ab,sacp->sbpac', eye, cc)
        return w.reshape(S5_SLABS, gs * SSM_STATE, gs * SSM_GROUP)

    bw = jnp.concatenate([b_slab(bbar_re), b_slab(bbar_im)], axis=2).astype(BF16)
    cw = jnp.concatenate([c_slab(c_re), -c_slab(c_im)], axis=1).astype(BF16)
    ar_s = ar.reshape(S5_SLABS, 1, SLAB_STATE)
    ai_s = ai.reshape(S5_SLABS, 1, SLAB_STATE)
    return bw, cw, ar_s, ai_s


def _s5(u_tb, bw, cw, ar_s, ai_s, d_row, w_glu, b_glu_row, g_row):
    seq, bsz, _ = u_tb.shape
    tc = S5_TIME_BLOCK
    assert seq % tc == 0 and bsz % 16 == 0
    full = lambda shape: pl.BlockSpec(shape, lambda t: (0,) * len(shape))
    return pl.pallas_call(
        functools.partial(_s5_kernel, tc=tc, bsz=bsz),
        grid=(seq // tc,),
        in_specs=[pl.BlockSpec((tc, bsz, SSM_WIDTH), lambda t: (t, 0, 0)),
                  full(bw.shape), full(cw.shape), full(ar_s.shape), full(ai_s.shape),
                  full((1, SSM_WIDTH)), full((SSM_WIDTH, SSM_WIDTH)), full((1, SSM_WIDTH)),
                  full((1, SSM_WIDTH))],
        out_specs=pl.BlockSpec((tc, bsz, SSM_WIDTH), lambda t: (t, 0, 0)),
        out_shape=jax.ShapeDtypeStruct((seq, bsz, SSM_WIDTH), BF16),
        scratch_shapes=[pltpu.VMEM((S5_SLABS, bsz, 2 * SLAB_STATE), F32),
                        pltpu.VMEM((tc * bsz, 2 * SLAB_STATE), F32)],
        compiler_params=pltpu.CompilerParams(dimension_semantics=("arbitrary",),
                                             vmem_limit_bytes=VMEM_LIMIT),
        name="s5_branch",
    )(u_tb, bw, cw, ar_s, ai_s, d_row, w_glu, b_glu_row, g_row)


def _layer_norm_rows(v, g, b):
    mu = jnp.mean(v, axis=-1, keepdims=True)
    c = v - mu
    var = jnp.mean(c * c, axis=-1, keepdims=True)
    return c * lax.rsqrt(var + LN_EPS) * g + b


def _outproj_kernel(attn_ref, ssm_ref, x_ref, wo_ref, ag_ref, g1_ref, b1_ref, wr_ref, br_ref, tril_ref,
                    h_ref, eid_ref, gate_ref, rank_ref, cnt_ref):
    a = attn_ref[...]
    a = a * lax.rsqrt(jnp.mean(a * a, axis=-1, keepdims=True) + RMS_EPS) * ag_ref[...]
    mix = (jnp.dot(a.astype(BF16), wo_ref[0:ATTN_WIDTH, :], preferred_element_type=F32)
           + jnp.dot(ssm_ref[...], wo_ref[ATTN_WIDTH:, :], preferred_element_type=F32))
    h = _layer_norm_rows(DEEPNORM_ALPHA * x_ref[...] + mix, g1_ref[...], b1_ref[...])
    h_ref[...] = h

    logits = jnp.dot(h.astype(BF16), wr_ref[...], preferred_element_type=F32) + br_ref[...]
    tm = logits.shape[0]
    lane = lax.broadcasted_iota(jnp.int32, (tm, LANES), 1).astype(F32)
    no_lane = float(LANES)
    is_g = lane < N_EXPERT_GROUPS
    gl = jnp.where(is_g, logits, NEG_BIG)
    gmax = jnp.max(gl, axis=1, keepdims=True)
    g_idx = jnp.min(jnp.where(is_g & (gl == gmax), lane, no_lane), axis=1, keepdims=True)
    g_p = 1.0 / jnp.sum(jnp.where(is_g, jnp.exp(gl - gmax), 0.0), axis=1, keepdims=True)
    first = N_EXPERT_GROUPS + g_idx * EXPERTS_PER_GROUP
    in_grp = (lane >= first) & (lane < first + EXPERTS_PER_GROUP)
    el = jnp.where(in_grp, logits, NEG_BIG)
    m1 = jnp.max(el, axis=1, keepdims=True)
    i1 = jnp.min(jnp.where(in_grp & (el == m1), lane, no_lane), axis=1, keepdims=True)
    rest = in_grp & (lane != i1)
    el2 = jnp.where(rest, logits, NEG_BIG)
    m2 = jnp.max(el2, axis=1, keepdims=True)
    i2 = jnp.min(jnp.where(rest & (el2 == m2), lane, no_lane), axis=1, keepdims=True)
    r = jnp.exp(m2 - m1)
    p1 = 1.0 / (1.0 + r)
    p2 = r / (1.0 + r)
    k2 = lax.broadcasted_iota(jnp.int32, (tm, 2), 1)
    eid_ref[...] = (jnp.where(k2 == 0, i1, i2) - N_EXPERT_GROUPS).astype(jnp.int32)
    gate_ref[...] = g_p * jnp.where(k2 == 0, p1, p2)

    @pl.when(pl.program_id(0) == 0)
    def _():
        cnt_ref[...] = jnp.zeros_like(cnt_ref)
    hit1 = lane == i1
    hit2 = lane == i2
    onehot = jnp.where(hit1 | hit2, 1.0, 0.0)
    before = jnp.dot(tril_ref[...], onehot.astype(BF16), preferred_element_type=F32) + cnt_ref[...]
    r1 = jnp.sum(jnp.where(hit1, before, 0.0), axis=1, keepdims=True)
    r2 = jnp.sum(jnp.where(hit2, before, 0.0), axis=1, keepdims=True)
    rank_ref[...] = jnp.where(k2 == 0, r1, r2).astype(jnp.int32)
    cnt_ref[...] = cnt_ref[...] + jnp.sum(onehot, axis=0, keepdims=True)


def _outproj(attn2, ssm2, x2, wo, ag_row, g1_row, b1_row, wr, br_row):
    n_tok = x2.shape[0]
    tm = ROW_BLOCK
    tril = jnp.tril(jnp.ones((tm, tm), BF16), -1)
    row_spec = lambda w: pl.BlockSpec((tm, w), lambda i: (i, 0))
    full = lambda shape: pl.BlockSpec(shape, lambda i: (0,) * len(shape))
    return pl.pallas_call(
        _outproj_kernel,
        grid=(n_tok // tm,),
        in_specs=[row_spec(ATTN_WIDTH), row_spec(SSM_WIDTH), row_spec(D_MODEL),
                  full((D_MODEL, D_MODEL)), full((1, ATTN_WIDTH)), full((1, D_MODEL)), full((1, D_MODEL)),
                  full((D_MODEL, LANES)), full((1, LANES)), full((tm, tm))],
        out_specs=[row_spec(D_MODEL), row_spec(2), row_spec(2), row_spec(2), full((1, LANES))],
        out_shape=[jax.ShapeDtypeStruct((n_tok, D_MODEL), F32),
                   jax.ShapeDtypeStruct((n_tok, 2), jnp.int32),
                   jax.ShapeDtypeStruct((n_tok, 2), F32),
                   jax.ShapeDtypeStruct((n_tok, 2), jnp.int32),
                   jax.ShapeDtypeStruct((1, LANES), F32)],
        compiler_params=pltpu.CompilerParams(dimension_semantics=("arbitrary",),
                                             vmem_limit_bytes=VMEM_LIMIT),
        name="outproj_ln_router",
    )(attn2, ssm2, x2, wo, ag_row, g1_row, b1_row, wr, br_row, tril)


def _dispatch_plan(eid, rank, cnt_row, n_tok):
    counts = cnt_row[0, N_EXPERT_GROUPS:N_EXPERT_GROUPS + N_EXPERTS].astype(jnp.int32)
    padded = ((counts + MOE_BLOCK - 1) // MOE_BLOCK) * MOE_BLOCK
    pends = jnp.cumsum(padded)
    pstarts = pends - padded
    onehot = eid[:, :, None] == jnp.arange(N_EXPERTS, dtype=jnp.int32)[None, None, :]
    slot = rank + jnp.sum(jnp.where(onehot, pstarts[None, None, :], 0), axis=-1)
    n_pad = ((2 * n_tok + N_EXPERTS * (MOE_BLOCK - 1) + MOE_BLOCK - 1) // MOE_BLOCK) * MOE_BLOCK
    n_blocks = n_pad // MOE_BLOCK
    blk0 = jnp.arange(n_blocks, dtype=jnp.int32) * MOE_BLOCK
    block_e = jnp.minimum(jnp.searchsorted(pends, blk0, side='right'), N_EXPERTS - 1).astype(jnp.int32)
    n_used = (pends[-1] // MOE_BLOCK).astype(jnp.int32).reshape(1)
    tail = jnp.where(padded > 0, pends - MOE_BLOCK, -1).astype(jnp.int32)
    return slot, block_e, n_used, tail, n_blocks


def _row_dma(src, src_row, dst, dst_row, sem):
    return pltpu.make_async_copy(src.at[pl.ds(src_row, 1)], dst.at[pl.ds(dst_row, 1)], sem)


def _dispatch_kernel(tail_ref, nu_ref, s0_ref, s1_ref, h_ref, xb_hbm, zbuf, sem_z, sem, *, n_blocks):
    tm = h_ref.shape[0]

    @pl.when(pl.program_id(0) == 0)
    def _():
        zbuf[...] = jnp.zeros_like(zbuf)

        def zero_copy(row0):
            return pltpu.make_async_copy(zbuf, xb_hbm.at[pl.ds(pl.multiple_of(row0, MOE_BLOCK), MOE_BLOCK)], sem_z)

        def each_zero_block(fn):
            for e in range(N_EXPERTS):
                @pl.when(tail_ref[e] >= 0)
                def _():
                    fn(zero_copy(tail_ref[e]))

            def unused(b, carry):
                fn(zero_copy(b * MOE_BLOCK))
                return carry
            lax.fori_loop(nu_ref[0], n_blocks, unused, 0)

        each_zero_block(lambda c: c.start())
        each_zero_block(lambda c: c.wait())

    def scatter_row(r, carry):
        _row_dma(h_ref, r, xb_hbm, s0_ref[0, r], sem).start()
        _row_dma(h_ref, r, xb_hbm, s1_ref[0, r], sem).start()
        return carry
    lax.fori_loop(0, tm, scatter_row, 0, unroll=8)
    for _ in range(2):
        pltpu.make_async_copy(h_ref, xb_hbm.at[pl.ds(0, tm)], sem).wait()


def _dispatch(h, slot, tail, n_used, n_blocks):
    n_tok = h.shape[0]
    tm = MOE_BLOCK
    nb = n_tok // tm
    s0 = slot[:, 0].reshape(nb, 1, tm)
    s1 = slot[:, 1].reshape(nb, 1, tm)
    smem_blk = pl.BlockSpec((None, 1, tm), lambda i, tl, nu: (i, 0, 0), memory_space=pltpu.SMEM)
    grid_spec = pltpu.PrefetchScalarGridSpec(
        num_scalar_prefetch=2,
        grid=(nb,),
        in_specs=[smem_blk, smem_blk, pl.BlockSpec((tm, D_MODEL), lambda i, tl, nu: (i, 0))],
        out_specs=pl.BlockSpec(memory_space=pl.ANY),
        scratch_shapes=[pltpu.VMEM((MOE_BLOCK, D_MODEL), F32), pltpu.SemaphoreType.DMA,
                        pltpu.SemaphoreType.DMA],
    )
    return pl.pallas_call(
        functools.partial(_dispatch_kernel, n_blocks=n_blocks),
        grid_spec=grid_spec,
        out_shape=jax.ShapeDtypeStruct((n_blocks * MOE_BLOCK, D_MODEL), F32),
        compiler_params=pltpu.CompilerParams(dimension_semantics=("arbitrary",),
                                             vmem_limit_bytes=VMEM_LIMIT),
        name="moe_dispatch",
    )(tail, n_used, s0, s1, h)


def _expert_kernel(be_ref, nu_ref, x_ref, wg_ref, wu_ref, wd_ref, y_ref, wg_b, wu_b, wd_b):
    i = pl.program_id(0)
    used = i < nu_ref[0]

    @pl.when(used & ((i == 0) | (be_ref[i] != be_ref[jnp.maximum(i - 1, 0)])))
    def _():
        wg_b[...] = wg_ref[...].astype(BF16)
        wu_b[...] = wu_ref[...].astype(BF16)
        wd_b[...] = wd_ref[...].astype(BF16)

    @pl.when(used)
    def _():
        xb = x_ref[...].astype(BF16)
        gt = jnp.dot(xb, wg_b[...], preferred_element_type=F32)
        up = jnp.dot(xb, wu_b[...], preferred_element_type=F32)
        hid = gt * (1.0 / (1.0 + jnp.exp(-gt))) * up
        y_ref[...] = jnp.dot(hid.astype(BF16), wd_b[...], preferred_element_type=F32)

    @pl.when(jnp.logical_not(used))
    def _():
        y_ref[...] = jnp.zeros_like(y_ref)


def _experts(xb, block_e, n_used, wg, wu, wd):
    n_blocks = xb.shape[0] // MOE_BLOCK
    last = lambda i, nu: jnp.minimum(i, nu[0] - 1)
    grid_spec = pltpu.PrefetchScalarGridSpec(
        num_scalar_prefetch=2,
        grid=(n_blocks,),
        in_specs=[pl.BlockSpec((MOE_BLOCK, D_MODEL), lambda i, be, nu: (last(i, nu), 0)),
                  pl.BlockSpec((None, D_MODEL, D_EXPERT), lambda i, be, nu: (be[last(i, nu)], 0, 0)),
                  pl.BlockSpec((None, D_MODEL, D_EXPERT), lambda i, be, nu: (be[last(i, nu)], 0, 0)),
                  pl.BlockSpec((None, D_EXPERT, D_MODEL), lambda i, be, nu: (be[last(i, nu)], 0, 0))],
        out_specs=pl.BlockSpec((MOE_BLOCK, D_MODEL), lambda i, be, nu: (i, 0)),
        scratch_shapes=[pltpu.VMEM((D_MODEL, D_EXPERT), BF16), pltpu.VMEM((D_MODEL, D_EXPERT), BF16),
                        pltpu.VMEM((D_EXPERT, D_MODEL), BF16)],
    )
    return pl.pallas_call(
        _expert_kernel,
        grid_spec=grid_spec,
        out_shape=jax.ShapeDtypeStruct(xb.shape, F32),
        compiler_params=pltpu.CompilerParams(dimension_semantics=("arbitrary",),
                                             vmem_limit_bytes=VMEM_LIMIT),
        name="moe_experts",
    )(block_e, n_used, xb, wg, wu, wd)


def _combine_kernel(s0_ref, s1_ref, s0n_ref, s1n_ref, h_ref, gate_ref, g2_ref, b2_ref, yb_hbm, o_ref, buf, sem):
    i = pl.program_id(0)
    n = pl.num_programs(0)
    tm = h_ref.shape[0]
    cur = i % 2

    def gather(a_ref, b_ref, which):
        def gather_row(r, carry):
            _row_dma(yb_hbm, a_ref[0, r], buf.at[which, 0], r, sem.at[which]).start()
            _row_dma(yb_hbm, b_ref[0, r], buf.at[which, 1], r, sem.at[which]).start()
            return carry
        lax.fori_loop(0, tm, gather_row, 0, unroll=8)

    @pl.when(i == 0)
    def _():
        gather(s0_ref, s1_ref, 0)

    @pl.when(i + 1 < n)
    def _():
        gather(s0n_ref, s1n_ref, 1 - cur)

    for k in range(2):
        pltpu.make_async_copy(yb_hbm.at[pl.ds(0, tm)], buf.at[cur, k], sem.at[cur]).wait()
    gate = gate_ref[...]
    moe = gate[:, 0:1] * buf[cur, 0] + gate[:, 1:2] * buf[cur, 1]
    o_ref[...] = _layer_norm_rows(DEEPNORM_ALPHA * h_ref[...] + moe, g2_ref[...], b2_ref[...])


def _combine(h, yb, slot, gate, g2_row, b2_row):
    n_tok = h.shape[0]
    tm = MOE_BLOCK
    nb = n_tok // tm
    s0 = slot[:, 0].reshape(nb, 1, tm)
    s1 = slot[:, 1].reshape(nb, 1, tm)
    row_spec = lambda w: pl.BlockSpec((tm, w), lambda i: (i, 0))
    full = lambda shape: pl.BlockSpec(shape, lambda i: (0,) * len(shape))
    smem_cur = pl.BlockSpec((None, 1, tm), lambda i: (i, 0, 0), memory_space=pltpu.SMEM)
    smem_next = pl.BlockSpec((None, 1, tm), lambda i: (jnp.minimum(i + 1, nb - 1), 0, 0), memory_space=pltpu.SMEM)
    return pl.pallas_call(
        _combine_kernel,
        grid=(nb,),
        in_specs=[smem_cur, smem_cur, smem_next, smem_next,
                  row_spec(D_MODEL), row_spec(2), full((1, D_MODEL)), full((1, D_MODEL)),
                  pl.BlockSpec(memory_space=pl.ANY)],
        out_specs=row_spec(D_MODEL),
        out_shape=jax.ShapeDtypeStruct((n_tok, D_MODEL), F32),
        scratch_shapes=[pltpu.VMEM((2, 2, tm, D_MODEL), F32), pltpu.SemaphoreType.DMA((2,))],
        compiler_params=pltpu.CompilerParams(dimension_semantics=("arbitrary",),
                                             vmem_limit_bytes=VMEM_LIMIT),
        name="combine_ln",
    )(s0, s1, s0, s1, h, gate, g2_row, b2_row, yb)


def kernel(x, w_in, b_fgate, s5_lambda_re, s5_lambda_im, s5_log_step, s5_b_re, s5_b_im, s5_c_re, s5_c_im, s5_d, w_glu, b_glu, attn_norm_g, ssm_norm_g, w_out, ln1_g, ln1_b, w_router_group, b_router_group, w_router_expert, b_router_expert, w_gate, w_up, w_down, ln2_g, ln2_b):
    bsz, seq, d = x.shape
    n_tok = bsz * seq
    a = ATTN_WIDTH
    for l in range(DEPTH):
        x2 = x.reshape(n_tok, d)
        w = w_in[l]
        f0 = 3 * a
        w_cat = jnp.concatenate(
            [w[:, :a] * (HEAD_DIM ** -0.5 * LOG2E), w[:, a:2 * a], w[:, f0 + ATTN_HEADS:],
             jnp.pad(w[:, f0:f0 + ATTN_HEADS], ((0, 0), (0, LANES - ATTN_HEADS)))], axis=1).astype(BF16)
        wvt = w[:, 2 * a:f0].T.astype(BF16)
        bf_row = jnp.pad(b_fgate[l], (0, LANES - ATTN_HEADS)).reshape(1, LANES)
        q, k, vt, u, cum = _inproj(x2, w_cat, wvt, bf_row, seq)

        cum_g = cum.reshape(bsz, seq, ATTN_HEADS // ATTN_HEADS_PER_STEP, ATTN_HEADS_PER_STEP).transpose(0, 2, 1, 3)
        attn = _attention(q.reshape(bsz, seq, a), k.reshape(bsz, seq, a), vt, cum_g)

        bw, cw, ar_s, ai_s = _s5_weights(s5_lambda_re[l], s5_lambda_im[l], s5_log_step[l],
                                         s5_b_re[l], s5_b_im[l], s5_c_re[l], s5_c_im[l])
        u_tb = u.reshape(bsz, seq, SSM_WIDTH).transpose(1, 0, 2)
        ssm_tb = _s5(u_tb, bw, cw, ar_s, ai_s, s5_d[l].reshape(1, SSM_WIDTH), w_glu[l].astype(BF16),
                     b_glu[l].reshape(1, SSM_WIDTH), ssm_norm_g[l].reshape(1, SSM_WIDTH))
        ssm2 = ssm_tb.transpose(1, 0, 2).reshape(n_tok, SSM_WIDTH)

        wr = jnp.pad(jnp.concatenate([w_router_group[l], w_router_expert[l]], axis=1),
                     ((0, 0), (0, LANES - N_EXPERT_GROUPS - N_EXPERTS))).astype(BF16)
        br_row = jnp.pad(jnp.concatenate([b_router_group[l], b_router_expert[l]]),
                         (0, LANES - N_EXPERT_GROUPS - N_EXPERTS)).reshape(1, LANES)
        h, eid, gate, rank, cnt_row = _outproj(
            attn.reshape(n_tok, a), ssm2, x2, w_out[l].astype(BF16), attn_norm_g[l].reshape(1, a),
            ln1_g[l].reshape(1, d), ln1_b[l].reshape(1, d), wr, br_row)

        slot, block_e, n_used, tail, n_blocks = _dispatch_plan(eid, rank, cnt_row, n_tok)
        xb = _dispatch(h, slot, tail, n_used, n_blocks)
        yb = _experts(xb, block_e, n_used, w_gate[l], w_up[l], w_down[l])
        x = _combine(h, yb, slot, gate, ln2_g[l].reshape(1, d), ln2_b[l].reshape(1, d)).reshape(bsz, seq, d)
    return x
```

```python
import functools
import math

import jax
import jax.numpy as jnp
from jax import lax
from jax.experimental import pallas as pl
from jax.experimental.pallas import tpu as pltpu

F32 = jnp.float32
BF16 = jnp.bfloat16

D_MODEL = 1024
HEAD_DIM = 64
ATTN_WIDTH = 512
ATTN_HEADS = 8
SSM_WIDTH = 512
SSM_GROUP = 16
SSM_GROUPS = 32
SSM_STATE = 64
N_EXPERT_GROUPS = 4
EXPERTS_PER_GROUP = 8
N_EXPERTS = 32
D_EXPERT = 512
MOE_BLOCK = 256
DEPTH = 1
DEEPNORM_ALPHA = (2.0 * DEPTH) ** 0.25
LN_EPS = 1e-5
RMS_EPS = 1e-6

LANES = 128
NEG_BIG = -1e30
LOG2E = math.log2(math.e)
ONES_ROWS = 16
VMEM_LIMIT = 48 * 1024 * 1024

ROW_BLOCK = 512
ATTN_BLOCK = 256
ATTN_HEADS_PER_STEP = 8
S5_TIME_BLOCK = 32
S5_SLABS = 4
ROUTER_ROWS = 40
SLAB_CH = SSM_WIDTH // S5_SLABS
SLAB_STATE = SSM_GROUPS // S5_SLABS * SSM_STATE


def _split3(c):
    hi = c.astype(BF16)
    r1 = c - hi.astype(F32)
    mid = r1.astype(BF16)
    lo = (r1 - mid.astype(F32)).astype(BF16)
    return hi, mid, lo


def _inproj_kernel(x_ref, w_ref, wvt_ref, bf_ref, q_ref, k_ref, vt_ref, u_ref, cum_ref, carry_ref, *,
                   blocks_per_seq):
    i = pl.program_id(0)

    @pl.when(i % blocks_per_seq == 0)
    def _():
        carry_ref[...] = jnp.zeros_like(carry_ref)

    tm = x_ref.shape[0]
    xb = x_ref[...].astype(BF16)
    z = jnp.dot(xb, w_ref[...], preferred_element_type=F32)
    a = ATTN_WIDTH
    q_ref[...] = z[:, :a].astype(BF16)
    k_ref[...] = z[:, a:2 * a].astype(BF16)
    u_ref[...] = z[:, 2 * a:2 * a + SSM_WIDTH]
    vt = lax.dot_general(wvt_ref[...], xb, (((1,), (1,)), ((), ())), preferred_element_type=F32).astype(BF16)
    for jj in range(tm // ATTN_BLOCK):
        vt_ref[jj] = vt[:, jj * ATTN_BLOCK:(jj + 1) * ATTN_BLOCK]
    f = z[:, 2 * a + SSM_WIDTH:] + bf_ref[...]
    logf = jnp.minimum(f, 0.0) - jnp.log1p(jnp.exp(-jnp.abs(f)))
    row = lax.broadcasted_iota(jnp.int32, (tm, tm), 0)
    col = lax.broadcasted_iota(jnp.int32, (tm, tm), 1)
    tri = (row >= col).astype(BF16)
    hi, mid, lo = _split3(logf)
    cum = (jnp.dot(tri, hi, preferred_element_type=F32)
           + jnp.dot(tri, mid, preferred_element_type=F32)
           + jnp.dot(tri, lo, preferred_element_type=F32)) + carry_ref[...]
    carry_ref[...] = cum[tm - 1:tm, :]
    cum_ref[...] = cum[:, :ATTN_HEADS] * LOG2E


def _inproj(x2, w_cat, wvt, bf_row, seq):
    n_tok = x2.shape[0]
    tm = ROW_BLOCK
    assert seq % tm == 0 and n_tok % tm == 0 and tm % ATTN_BLOCK == 0
    bsz = n_tok // seq
    bps = seq // tm
    ncol = w_cat.shape[1]
    vblk = tm // ATTN_BLOCK
    row_spec = lambda w: pl.BlockSpec((tm, w), lambda i: (i, 0))
    return pl.pallas_call(
        functools.partial(_inproj_kernel, blocks_per_seq=bps),
        grid=(n_tok // tm,),
        in_specs=[row_spec(D_MODEL),
                  pl.BlockSpec((D_MODEL, ncol), lambda i: (0, 0)),
                  pl.BlockSpec((ATTN_WIDTH, D_MODEL), lambda i: (0, 0)),
                  pl.BlockSpec((1, LANES), lambda i: (0, 0))],
        out_specs=[row_spec(ATTN_WIDTH), row_spec(ATTN_WIDTH),
                   pl.BlockSpec((None, vblk, ATTN_WIDTH, ATTN_BLOCK), lambda i: (i // bps, i % bps, 0, 0)),
                   row_spec(SSM_WIDTH), row_spec(ATTN_HEADS)],
        out_shape=[jax.ShapeDtypeStruct((n_tok, ATTN_WIDTH), BF16)] * 2
                  + [jax.ShapeDtypeStruct((bsz, seq // ATTN_BLOCK, ATTN_WIDTH, ATTN_BLOCK), BF16),
                     jax.ShapeDtypeStruct((n_tok, SSM_WIDTH), F32),
                     jax.ShapeDtypeStruct((n_tok, ATTN_HEADS), F32)],
        scratch_shapes=[pltpu.VMEM((1, LANES), F32)],
        compiler_params=pltpu.CompilerParams(dimension_semantics=("arbitrary",),
                                             vmem_limit_bytes=VMEM_LIMIT),
        name="inproj",
    )(x2, w_cat, wvt, bf_row)


def _bias_lanes_q(c):
    hi, mid, lo = (t.astype(F32) for t in _split3(c))
    lane = lax.broadcasted_iota(jnp.int32, (c.shape[0], LANES), 1)
    sel = jnp.where(lane == 0, hi, jnp.where(lane == 1, mid, jnp.where(lane == 2, lo,
                    jnp.where(lane < 6, 1.0, 0.0))))
    return sel.astype(BF16)


def _bias_lanes_k(c):
    hi, mid, lo = (t.astype(F32) for t in _split3(c))
    lane = lax.broadcasted_iota(jnp.int32, (c.shape[0], LANES), 1)
    sel = jnp.where(lane < 3, 1.0, jnp.where(lane == 3, -hi, jnp.where(lane == 4, -mid,
                    jnp.where(lane == 5, -lo, 0.0))))
    return sel.astype(BF16)


def _attn_kernel(q_ref, k_ref, vt_ref, cum_ref, o_ref, extk_ref, *, tq, seq, nh):
    qi = pl.program_id(2)
    n_tiles = seq // tq

    @pl.when(qi == 0)
    def _():
        def fill(j, carry):
            r0 = pl.multiple_of(j * tq, tq)
            c = cum_ref[pl.ds(r0, tq), :]
            for h in range(nh):
                extk_ref[h, pl.ds(r0, tq), :] = _bias_lanes_k(c[:, h:h + 1])
            return carry
        lax.fori_loop(0, n_tiles, fill, 0)

    q0 = pl.multiple_of(qi * tq, tq)
    cq = cum_ref[pl.ds(q0, tq), :]
    lane = lax.broadcasted_iota(jnp.int32, (tq, LANES), 1)
    q_augs = []
    for h in range(nh):
        qb = q_ref[:, LANES * (h // 2):LANES * (h // 2 + 1)]
        head_lanes = (lane < HEAD_DIM) if h % 2 == 0 else (lane >= HEAD_DIM)
        qh = jnp.where(head_lanes, qb, jnp.zeros((), BF16))
        q_augs.append(jnp.concatenate([qh, _bias_lanes_q(cq[:, h:h + 1])], axis=1))
    key_i = lax.broadcasted_iota(jnp.int32, (tq, tq), 0)
    qry_i = lax.broadcasted_iota(jnp.int32, (tq, tq), 1)
    causal = key_i <= qry_i
    ones_rows = jnp.ones((ONES_ROWS, tq), BF16)

    def step(j, carry, masked):
        k0 = pl.multiple_of(j * tq, tq)
        scores = []
        for h in range(nh):
            kj = k_ref[pl.ds(k0, tq), LANES * (h // 2):LANES * (h // 2 + 1)]
            k_aug = jnp.concatenate([kj, extk_ref[h, pl.ds(k0, tq), :]], axis=1)
            s = lax.dot_general(k_aug, q_augs[h], (((1,), (1,)), ((), ())), preferred_element_type=F32)
            scores.append(jnp.where(causal, s, NEG_BIG) if masked else s)
        stats = []
        for h in range(nh):
            m = carry[h][0]
            m_new = jnp.maximum(m, jnp.max(scores[h], axis=0, keepdims=True))
            stats.append((m_new, jnp.exp2(m - m_new), jnp.exp2(scores[h] - m_new).astype(BF16)))
        new = []
        for h in range(nh):
            m_new, corr, p = stats[h]
            v_pair = vt_ref[j, LANES * (h // 2):LANES * (h // 2 + 1), :]
            v_aug = jnp.concatenate([v_pair, ones_rows], axis=0)
            res = jnp.dot(v_aug, p, preferred_element_type=F32)
            hh = h % 2
            acc = corr * carry[h][2] + res[HEAD_DIM * hh:HEAD_DIM * (hh + 1), :]
            l = corr * carry[h][1] + res[2 * HEAD_DIM:2 * HEAD_DIM + 1, :]
            new.append((m_new, l, acc))
        return tuple(new)

    init1 = (jnp.full((1, tq), NEG_BIG, F32), jnp.zeros((1, tq), F32), jnp.zeros((HEAD_DIM, tq), F32))
    carry = lax.fori_loop(0, qi, functools.partial(step, masked=False), (init1,) * nh)
    carry = step(qi, carry, True)
    out_t = jnp.concatenate([c[2] / c[1] for c in carry], axis=0)
    o_ref[...] = out_t.T


def _attention(q, k, vt, cum_g):
    bsz, seq, _ = q.shape
    tq = ATTN_BLOCK
    nh = ATTN_HEADS_PER_STEP
    assert seq % tq == 0 and ATTN_HEADS % nh == 0 and nh % 2 == 0
    w = nh * HEAD_DIM
    return pl.pallas_call(
        functools.partial(_attn_kernel, tq=tq, seq=seq, nh=nh),
        grid=(bsz, ATTN_HEADS // nh, seq // tq),
        in_specs=[pl.BlockSpec((None, tq, w), lambda b, g, i: (b, i, g)),
                  pl.BlockSpec((None, seq, w), lambda b, g, i: (b, 0, g)),
                  pl.BlockSpec((None, seq // tq, w, tq), lambda b, g, i: (b, 0, g, 0)),
                  pl.BlockSpec((None, None, seq, nh), lambda b, g, i: (b, g, 0, 0))],
        out_specs=pl.BlockSpec((None, tq, w), lambda b, g, i: (b, i, g)),
        out_shape=jax.ShapeDtypeStruct((bsz, seq, ATTN_WIDTH), F32),
        scratch_shapes=[pltpu.VMEM((nh, seq, LANES), BF16)],
        compiler_params=pltpu.CompilerParams(dimension_semantics=("parallel", "parallel", "arbitrary"),
                                             vmem_limit_bytes=VMEM_LIMIT),
        name="fox_attention",
    )(q, k, vt, cum_g)


def _s5_kernel(u_ref, bw_ref, cw_ref, ar_ref, ai_ref, d_ref, wglu_ref, bglu_ref, g_ref, o_ref,
               state_ref, bu_ref, *, tc, bsz):
    @pl.when(pl.program_id(0) == 0)
    def _():
        state_ref[...] = jnp.zeros_like(state_ref)

    rows = tc * bsz
    ns = SLAB_STATE
    u = u_ref[...].reshape(rows, SSM_WIDTH)
    ub = u.astype(BF16)
    for i in range(S5_SLABS):
        bu_ref[i] = jnp.dot(ub[:, SLAB_CH * i:SLAB_CH * (i + 1)], bw_ref[i], preferred_element_type=F32)
    for i in range(S5_SLABS):
        ar = jnp.broadcast_to(ar_ref[i], (bsz, ns))
        ai = jnp.broadcast_to(ai_ref[i], (bsz, ns))
        xr = state_ref[i, :, 0:ns]
        xi = state_ref[i, :, ns:2 * ns]
        for t in range(tc):
            r0 = t * bsz
            nr = ar * xr - ai * xi + bu_ref[i, r0:r0 + bsz, 0:ns]
            ni = ar * xi + ai * xr + bu_ref[i, r0:r0 + bsz, ns:2 * ns]
            bu_ref[i, r0:r0 + bsz, 0:ns] = nr
            bu_ref[i, r0:r0 + bsz, ns:2 * ns] = ni
            xr, xi = nr, ni
        state_ref[i, :, 0:ns] = xr
        state_ref[i, :, ns:2 * ns] = xi
    ys = [jnp.dot(bu_ref[i].astype(BF16), cw_ref[i], preferred_element_type=F32) for i in range(S5_SLABS)]
    y = jnp.concatenate(ys, axis=1) + d_ref[...] * u
    y = 0.5 * y * (1.0 + jnp.tanh(math.sqrt(2.0 / math.pi) * (y + 0.044715 * (y * y * y))))
    z = jnp.dot(y.astype(BF16), wglu_ref[...], preferred_element_type=F32) + bglu_ref[...]
    y = y * (1.0 / (1.0 + jnp.exp(-z)))
    y = y * lax.rsqrt(jnp.mean(y * y, axis=-1, keepdims=True) + RMS_EPS) * g_ref[...]
    o_ref[...] = y.astype(BF16).reshape(tc, bsz, SSM_WIDTH)


def _s5_weights(lam_re, lam_im, log_step, b_re, b_im, c_re, c_im):
    delta = jnp.exp(log_step)[:, None]
    mag = jnp.exp(lam_re * delta)
    ar = mag * jnp.cos(lam_im * delta)
    ai = mag * jnp.sin(lam_im * delta)
    den = lam_re * lam_re + lam_im * lam_im
    num_re = ar - 1.0
    coef_re = (num_re * lam_re + ai * lam_im) / den
    coef_im = (ai * lam_re - num_re * lam_im) / den
    bbar_re = coef_re[..., None] * b_re - coef_im[..., None] * b_im
    bbar_im = coef_re[..., None] * b_im + coef_im[..., None] * b_re
    gs = SSM_GROUPS // S5_SLABS
    eye = jnp.eye(gs, dtype=F32)

    def b_slab(bb):
        bb = bb.reshape(S5_SLABS, gs, SSM_STATE, SSM_GROUP)
        w = jnp.einsum('ab,sapc->sacbp', eye, bb)
        return w.reshape(S5_SLABS, gs * SSM_GROUP, gs * SSM_STATE)

    def c_slab(cc):
        cc = cc.reshape(S5_SLABS, gs, SSM_GROUP, SSM_STATE)
        w = jnp.einsum('ab,sacp->sbpac', eye, cc)
        return w.reshape(S5_SLABS, gs * SSM_STATE, gs * SSM_GROUP)

    bw = jnp.concatenate([b_slab(bbar_re), b_slab(bbar_im)], axis=2).astype(BF16)
    cw = jnp.concatenate([c_slab(c_re), -c_slab(c_im)], axis=1).astype(BF16)
    ar_s = ar.reshape(S5_SLABS, 1, SLAB_STATE)
    ai_s = ai.reshape(S5_SLABS, 1, SLAB_STATE)
    return bw, cw, ar_s, ai_s


def _s5(u_tb, bw, cw, ar_s, ai_s, d_row, w_glu, b_glu_row, g_row):
    seq, bsz, _ = u_tb.shape
    tc = S5_TIME_BLOCK
    assert seq % tc == 0 and bsz % 16 == 0
    full = lambda shape: pl.BlockSpec(shape, lambda t: (0,) * len(shape))
    return pl.pallas_call(
        functools.partial(_s5_kernel, tc=tc, bsz=bsz),
        grid=(seq // tc,),
        in_specs=[pl.BlockSpec((tc, bsz, SSM_WIDTH), lambda t: (t, 0, 0)),
                  full(bw.shape), full(cw.shape), full(ar_s.shape), full(ai_s.shape),
                  full((1, SSM_WIDTH)), full((SSM_WIDTH, SSM_WIDTH)), full((1, SSM_WIDTH)),
                  full((1, SSM_WIDTH))],
        out_specs=pl.BlockSpec((tc, bsz, SSM_WIDTH), lambda t: (t, 0, 0)),
        out_shape=jax.ShapeDtypeStruct((seq, bsz, SSM_WIDTH), BF16),
        scratch_shapes=[pltpu.VMEM((S5_SLABS, bsz, 2 * SLAB_STATE), F32),
                        pltpu.VMEM((S5_SLABS, tc * bsz, 2 * SLAB_STATE), F32)],
        compiler_params=pltpu.CompilerParams(dimension_semantics=("arbitrary",),
                                             vmem_limit_bytes=VMEM_LIMIT),
        name="s5_branch",
    )(u_tb, bw, cw, ar_s, ai_s, d_row, w_glu, b_glu_row, g_row)


def _layer_norm_rows(v, g, b):
    mu = jnp.mean(v, axis=-1, keepdims=True)
    c = v - mu
    var = jnp.mean(c * c, axis=-1, keepdims=True)
    return c * lax.rsqrt(var + LN_EPS) * g + b


def _outproj_kernel(attn_ref, ssm_ref, x_ref, wo_ref, ag_ref, g1_ref, b1_ref, wrt_ref, brt_ref, triu_ref,
                    h_ref, eid_ref, gate_ref, rank_ref, cnt_ref):
    a = attn_ref[...]
    a = a * lax.rsqrt(jnp.mean(a * a, axis=-1, keepdims=True) + RMS_EPS) * ag_ref[...]
    mix = (jnp.dot(a.astype(BF16), wo_ref[0:ATTN_WIDTH, :], preferred_element_type=F32)
           + jnp.dot(ssm_ref[...], wo_ref[ATTN_WIDTH:, :], preferred_element_type=F32))
    h = _layer_norm_rows(DEEPNORM_ALPHA * x_ref[...] + mix, g1_ref[...], b1_ref[...])
    h_ref[...] = h

    logits = lax.dot_general(wrt_ref[...], h.astype(BF16), (((1,), (1,)), ((), ())),
                             preferred_element_type=F32)[:ROUTER_ROWS] + brt_ref[...]
    tm = logits.shape[1]
    row = lax.broadcasted_iota(jnp.int32, (ROUTER_ROWS, tm), 0).astype(F32)
    no_row = float(LANES)
    is_g = row < N_EXPERT_GROUPS
    gl = jnp.where(is_g, logits, NEG_BIG)
    gmax = jnp.max(gl, axis=0, keepdims=True)
    g_idx = jnp.min(jnp.where(is_g & (gl == gmax), row, no_row), axis=0, keepdims=True)
    g_p = 1.0 / jnp.sum(jnp.where(is_g, jnp.exp(gl - gmax), 0.0), axis=0, keepdims=True)
    first = N_EXPERT_GROUPS + g_idx * EXPERTS_PER_GROUP
    in_grp = (row >= first) & (row < first + EXPERTS_PER_GROUP)
    el = jnp.where(in_grp, logits, NEG_BIG)
    m1 = jnp.max(el, axis=0, keepdims=True)
    i1 = jnp.min(jnp.where(in_grp & (el == m1), row, no_row), axis=0, keepdims=True)
    rest = in_grp & (row != i1)
    el2 = jnp.where(rest, logits, NEG_BIG)
    m2 = jnp.max(el2, axis=0, keepdims=True)
    i2 = jnp.min(jnp.where(rest & (el2 == m2), row, no_row), axis=0, keepdims=True)
    r = jnp.exp(m2 - m1)
    p1 = 1.0 / (1.0 + r)
    p2 = r / (1.0 + r)
    k2 = lax.broadcasted_iota(jnp.int32, (2, tm), 0)
    eid_ref[...] = (jnp.where(k2 == 0, i1, i2) - N_EXPERT_GROUPS).astype(jnp.int32)
    gate_ref[...] = g_p * jnp.where(k2 == 0, p1, p2)

    @pl.when(pl.program_id(0) == 0)
    def _():
        cnt_ref[...] = jnp.zeros_like(cnt_ref)
    hit1 = row == i1
    hit2 = row == i2
    onehot = jnp.where(hit1 | hit2, 1.0, 0.0)
    cnt = cnt_ref[...]
    before = jnp.dot(onehot.astype(BF16), triu_ref[...], preferred_element_type=F32) + cnt
    r1 = jnp.sum(jnp.where(hit1, before, 0.0), axis=0, keepdims=True)
    r2 = jnp.sum(jnp.where(hit2, before, 0.0), axis=0, keepdims=True)
    rank_ref[...] = jnp.where(k2 == 0, r1, r2).astype(jnp.int32)
    cnt_ref[...] = cnt + jnp.sum(onehot, axis=1, keepdims=True)


def _outproj(attn2, ssm2, x2, wo, ag_row, g1_row, b1_row, wrt, brt):
    n_tok = x2.shape[0]
    tm = ROW_BLOCK
    triu = jnp.triu(jnp.ones((tm, tm), BF16), 1)
    row_spec = lambda w: pl.BlockSpec((tm, w), lambda i: (i, 0))
    col_spec = lambda r: pl.BlockSpec((r, tm), lambda i: (0, i))
    full = lambda shape: pl.BlockSpec(shape, lambda i: (0,) * len(shape))
    return pl.pallas_call(
        _outproj_kernel,
        grid=(n_tok // tm,),
        in_specs=[row_spec(ATTN_WIDTH), row_spec(SSM_WIDTH), row_spec(D_MODEL),
                  full((D_MODEL, D_MODEL)), full((1, ATTN_WIDTH)), full((1, D_MODEL)), full((1, D_MODEL)),
                  full((LANES, D_MODEL)), full((ROUTER_ROWS, 1)), full((tm, tm))],
        out_specs=[row_spec(D_MODEL), col_spec(2), col_spec(2), col_spec(2), full((ROUTER_ROWS, 1))],
        out_shape=[jax.ShapeDtypeStruct((n_tok, D_MODEL), F32),
                   jax.ShapeDtypeStruct((2, n_tok), jnp.int32),
                   jax.ShapeDtypeStruct((2, n_tok), F32),
                   jax.ShapeDtypeStruct((2, n_tok), jnp.int32),
                   jax.ShapeDtypeStruct((ROUTER_ROWS, 1), F32)],
        compiler_params=pltpu.CompilerParams(dimension_semantics=("arbitrary",),
                                             vmem_limit_bytes=VMEM_LIMIT),
        name="outproj_ln_router",
    )(attn2, ssm2, x2, wo, ag_row, g1_row, b1_row, wrt, brt, triu)


def _dispatch_plan(eid, rank, cnt_col, n_tok):
    counts = cnt_col[N_EXPERT_GROUPS:N_EXPERT_GROUPS + N_EXPERTS, 0].astype(jnp.int32)
    padded = ((counts + MOE_BLOCK - 1) // MOE_BLOCK) * MOE_BLOCK
    pends = jnp.cumsum(padded)
    pstarts = pends - padded
    onehot = eid[:, :, None] == jnp.arange(N_EXPERTS, dtype=jnp.int32)[None, None, :]
    slot = rank + jnp.sum(jnp.where(onehot, pstarts[None, None, :], 0), axis=-1)
    n_pad = ((2 * n_tok + N_EXPERTS * (MOE_BLOCK - 1) + MOE_BLOCK - 1) // MOE_BLOCK) * MOE_BLOCK
    n_blocks = n_pad // MOE_BLOCK
    blk0 = jnp.arange(n_blocks, dtype=jnp.int32) * MOE_BLOCK
    block_e = jnp.minimum(jnp.sum((pends[None, :] <= blk0[:, None]).astype(jnp.int32), axis=1), N_EXPERTS - 1)
    n_used = (pends[-1] // MOE_BLOCK).astype(jnp.int32).reshape(1)
    tail = jnp.where(padded > 0, pends - MOE_BLOCK, -1).astype(jnp.int32)
    return slot, block_e, n_used, tail, n_blocks


def _row_dma(src, src_row, dst, dst_row, sem):
    return pltpu.make_async_copy(src.at[pl.ds(src_row, 1)], dst.at[pl.ds(dst_row, 1)], sem)


def _dispatch_kernel(tail_ref, nu_ref, s0_ref, s1_ref, h_ref, xb_hbm, zbuf, sem_z, sem, *, n_blocks):
    tm = h_ref.shape[0]

    @pl.when(pl.program_id(0) == 0)
    def _():
        zbuf[...] = jnp.zeros_like(zbuf)

        def zero_copy(row0):
            return pltpu.make_async_copy(zbuf, xb_hbm.at[pl.ds(pl.multiple_of(row0, MOE_BLOCK), MOE_BLOCK)], sem_z)

        def each_zero_block(fn):
            for e in range(N_EXPERTS):
                @pl.when(tail_ref[e] >= 0)
                def _():
                    fn(zero_copy(tail_ref[e]))

            def unused(b, carry):
                fn(zero_copy(b * MOE_BLOCK))
                return carry
            lax.fori_loop(nu_ref[0], n_blocks, unused, 0)

        each_zero_block(lambda c: c.start())
        each_zero_block(lambda c: c.wait())

    def scatter_row(r, carry):
        _row_dma(h_ref, r, xb_hbm, s0_ref[0, r], sem).start()
        _row_dma(h_ref, r, xb_hbm, s1_ref[0, r], sem).start()
        return carry
    lax.fori_loop(0, tm, scatter_row, 0, unroll=8)
    for _ in range(2):
        pltpu.make_async_copy(h_ref, xb_hbm.at[pl.ds(0, tm)], sem).wait()


def _dispatch(h, slot, tail, n_used, n_blocks):
    n_tok = h.shape[0]
    tm = MOE_BLOCK
    nb = n_tok // tm
    s0 = slot[0].reshape(nb, 1, tm)
    s1 = slot[1].reshape(nb, 1, tm)
    smem_blk = pl.BlockSpec((None, 1, tm), lambda i, tl, nu: (i, 0, 0), memory_space=pltpu.SMEM)
    grid_spec = pltpu.PrefetchScalarGridSpec(
        num_scalar_prefetch=2,
        grid=(nb,),
        in_specs=[smem_blk, smem_blk, pl.BlockSpec((tm, D_MODEL), lambda i, tl, nu: (i, 0))],
        out_specs=pl.BlockSpec(memory_space=pl.ANY),
        scratch_shapes=[pltpu.VMEM((MOE_BLOCK, D_MODEL), F32), pltpu.SemaphoreType.DMA,
                        pltpu.SemaphoreType.DMA],
    )
    return pl.pallas_call(
        functools.partial(_dispatch_kernel, n_blocks=n_blocks),
        grid_spec=grid_spec,
        out_shape=jax.ShapeDtypeStruct((n_blocks * MOE_BLOCK, D_MODEL), F32),
        compiler_params=pltpu.CompilerParams(dimension_semantics=("arbitrary",),
                                             vmem_limit_bytes=VMEM_LIMIT),
        name="moe_dispatch",
    )(tail, n_used, s0, s1, h)


def _expert_kernel(be_ref, nu_ref, x_ref, wg_ref, wu_ref, wd_ref, y_ref, wgu_t, wd_t, yt_ref):
    i = pl.program_id(0)
    used = i < nu_ref[0]

    @pl.when(used & ((i == 0) | (be_ref[i] != be_ref[jnp.maximum(i - 1, 0)])))
    def _():
        wgu_t[0:D_EXPERT, :] = wg_ref[...].T.astype(BF16)
        wgu_t[D_EXPERT:, :] = wu_ref[...].T.astype(BF16)
        wd_t[...] = wd_ref[...].T.astype(BF16)

    @pl.when(used)
    def _():
        xb = x_ref[...].astype(BF16)
        gu = lax.dot_general(wgu_t[...], xb, (((1,), (1,)), ((), ())), preferred_element_type=F32)
        gt = gu[0:D_EXPERT]
        hid = gt * (1.0 / (1.0 + jnp.exp(-gt))) * gu[D_EXPERT:]
        yt_ref[...] = jnp.dot(wd_t[...], hid.astype(BF16), preferred_element_type=F32)
        y_ref[...] = yt_ref[...].T

    @pl.when(jnp.logical_not(used))
    def _():
        y_ref[...] = jnp.zeros_like(y_ref)


def _experts(xb, block_e, n_used, wg, wu, wd):
    n_blocks = xb.shape[0] // MOE_BLOCK
    last = lambda i, nu: jnp.minimum(i, nu[0] - 1)
    grid_spec = pltpu.PrefetchScalarGridSpec(
        num_scalar_prefetch=2,
        grid=(n_blocks,),
        in_specs=[pl.BlockSpec((MOE_BLOCK, D_MODEL), lambda i, be, nu: (last(i, nu), 0)),
                  pl.BlockSpec((None, D_MODEL, D_EXPERT), lambda i, be, nu: (be[last(i, nu)], 0, 0)),
                  pl.BlockSpec((None, D_MODEL, D_EXPERT), lambda i, be, nu: (be[last(i, nu)], 0, 0)),
                  pl.BlockSpec((None, D_EXPERT, D_MODEL), lambda i, be, nu: (be[last(i, nu)], 0, 0))],
        out_specs=pl.BlockSpec((MOE_BLOCK, D_MODEL), lambda i, be, nu: (i, 0)),
        scratch_shapes=[pltpu.VMEM((2 * D_EXPERT, D_MODEL), BF16), pltpu.VMEM((D_MODEL, D_EXPERT), BF16),
                        pltpu.VMEM((D_MODEL, MOE_BLOCK), F32)],
    )
    return pl.pallas_call(
        _expert_kernel,
        grid_spec=grid_spec,
        out_shape=jax.ShapeDtypeStruct(xb.shape, F32),
        compiler_params=pltpu.CompilerParams(dimension_semantics=("arbitrary",),
                                             vmem_limit_bytes=VMEM_LIMIT),
        name="moe_experts",
    )(block_e, n_used, xb, wg, wu, wd)


def _combine_kernel(s0_ref, s1_ref, s0n_ref, s1n_ref, h_ref, gate_ref, g2_ref, b2_ref, yb_hbm, o_ref, buf, sem):
    i = pl.program_id(0)
    n = pl.num_programs(0)
    tm = h_ref.shape[0]
    cur = i % 2

    def gather(a_ref, b_ref, which):
        def gather_row(r, carry):
            _row_dma(yb_hbm, a_ref[0, r], buf.at[which, 0], r, sem.at[which]).start()
            _row_dma(yb_hbm, b_ref[0, r], buf.at[which, 1], r, sem.at[which]).start()
            return carry
        lax.fori_loop(0, tm, gather_row, 0, unroll=8)

    @pl.when(i == 0)
    def _():
        gather(s0_ref, s1_ref, 0)

    @pl.when(i + 1 < n)
    def _():
        gather(s0n_ref, s1n_ref, 1 - cur)

    for k in range(2):
        pltpu.make_async_copy(yb_hbm.at[pl.ds(0, tm)], buf.at[cur, k], sem.at[cur]).wait()
    gate = gate_ref[...]
    moe = gate[:, 0:1] * buf[cur, 0] + gate[:, 1:2] * buf[cur, 1]
    o_ref[...] = _layer_norm_rows(DEEPNORM_ALPHA * h_ref[...] + moe, g2_ref[...], b2_ref[...])


def _combine(h, yb, slot, gate, g2_row, b2_row):
    n_tok = h.shape[0]
    tm = MOE_BLOCK
    nb = n_tok // tm
    s0 = slot[0].reshape(nb, 1, tm)
    s1 = slot[1].reshape(nb, 1, tm)
    row_spec = lambda w: pl.BlockSpec((tm, w), lambda i: (i, 0))
    full = lambda shape: pl.BlockSpec(shape, lambda i: (0,) * len(shape))
    smem_cur = pl.BlockSpec((None, 1, tm), lambda i: (i, 0, 0), memory_space=pltpu.SMEM)
    smem_next = pl.BlockSpec((None, 1, tm), lambda i: (jnp.minimum(i + 1, nb - 1), 0, 0), memory_space=pltpu.SMEM)
    return pl.pallas_call(
        _combine_kernel,
        grid=(nb,),
        in_specs=[smem_cur, smem_cur, smem_next, smem_next,
                  row_spec(D_MODEL), row_spec(2), full((1, D_MODEL)), full((1, D_MODEL)),
                  pl.BlockSpec(memory_space=pl.ANY)],
        out_specs=row_spec(D_MODEL),
        out_shape=jax.ShapeDtypeStruct((n_tok, D_MODEL), F32),
        scratch_shapes=[pltpu.VMEM((2, 2, tm, D_MODEL), F32), pltpu.SemaphoreType.DMA((2,))],
        compiler_params=pltpu.CompilerParams(dimension_semantics=("arbitrary",),
                                             vmem_limit_bytes=VMEM_LIMIT),
        name="combine_ln",
    )(s0, s1, s0, s1, h, gate, g2_row, b2_row, yb)


def kernel(x, w_in, b_fgate, s5_lambda_re, s5_lambda_im, s5_log_step, s5_b_re, s5_b_im, s5_c_re, s5_c_im, s5_d, w_glu, b_glu, attn_norm_g, ssm_norm_g, w_out, ln1_g, ln1_b, w_router_group, b_router_group, w_router_expert, b_router_expert, w_gate, w_up, w_down, ln2_g, ln2_b):
    bsz, seq, d = x.shape
    n_tok = bsz * seq
    a = ATTN_WIDTH
    for l in range(DEPTH):
        x2 = x.reshape(n_tok, d)
        w = w_in[l]
        f0 = 3 * a
        w_cat = jnp.concatenate(
            [w[:, :a] * (HEAD_DIM ** -0.5 * LOG2E), w[:, a:2 * a], w[:, f0 + ATTN_HEADS:],
             jnp.pad(w[:, f0:f0 + ATTN_HEADS], ((0, 0), (0, LANES - ATTN_HEADS)))], axis=1).astype(BF16)
        wvt = w[:, 2 * a:f0].T.astype(BF16)
        bf_row = jnp.pad(b_fgate[l], (0, LANES - ATTN_HEADS)).reshape(1, LANES)
        q, k, vt, u, cum = _inproj(x2, w_cat, wvt, bf_row, seq)

        cum_g = cum.reshape(bsz, seq, ATTN_HEADS // ATTN_HEADS_PER_STEP, ATTN_HEADS_PER_STEP).transpose(0, 2, 1, 3)
        attn = _attention(q.reshape(bsz, seq, a), k.reshape(bsz, seq, a), vt, cum_g)

        bw, cw, ar_s, ai_s = _s5_weights(s5_lambda_re[l], s5_lambda_im[l], s5_log_step[l],
                                         s5_b_re[l], s5_b_im[l], s5_c_re[l], s5_c_im[l])
        u_tb = u.reshape(bsz, seq, SSM_WIDTH).transpose(1, 0, 2)
        ssm_tb = _s5(u_tb, bw, cw, ar_s, ai_s, s5_d[l].reshape(1, SSM_WIDTH), w_glu[l].astype(BF16),
                     b_glu[l].reshape(1, SSM_WIDTH), ssm_norm_g[l].reshape(1, SSM_WIDTH))
        ssm2 = ssm_tb.transpose(1, 0, 2).reshape(n_tok, SSM_WIDTH)

        n_route = N_EXPERT_GROUPS + N_EXPERTS
        wrt = jnp.pad(jnp.concatenate([w_router_group[l], w_router_expert[l]], axis=1).T,
                      ((0, LANES - n_route), (0, 0))).astype(BF16)
        brt = jnp.pad(jnp.concatenate([b_router_group[l], b_router_expert[l]]),
                      (0, ROUTER_ROWS - n_route)).reshape(ROUTER_ROWS, 1)
        h, eid, gate, rank, cnt_col = _outproj(
            attn.reshape(n_tok, a), ssm2, x2, w_out[l].astype(BF16), attn_norm_g[l].reshape(1, a),
            ln1_g[l].reshape(1, d), ln1_b[l].reshape(1, d), wrt, brt)

        slot, block_e, n_used, tail, n_blocks = _dispatch_plan(eid, rank, cnt_col, n_tok)
        xb = _dispatch(h, slot, tail, n_used, n_blocks)
        yb = _experts(xb, block_e, n_used, w_gate[l], w_up[l], w_down[l])
        x = _combine(h, yb, slot, gate.T, ln2_g[l].reshape(1, d), ln2_b[l].reshape(1, d)).reshape(bsz, seq, d)
    return x
```

```python
import functools
import math

import jax
import jax.numpy as jnp
from jax import lax
from jax.experimental import pallas as pl
from jax.experimental.pallas import tpu as pltpu

F32 = jnp.float32
BF16 = jnp.bfloat16

D_MODEL = 1024
HEAD_DIM = 64
ATTN_WIDTH = 512
ATTN_HEADS = 8
SSM_WIDTH = 512
SSM_GROUP = 16
SSM_GROUPS = 32
SSM_STATE = 64
N_EXPERT_GROUPS = 4
EXPERTS_PER_GROUP = 8
N_EXPERTS = 32
D_EXPERT = 512
MOE_BLOCK = 512
MOE_SUB = 256
TOKEN_BLOCK = 256
DEPTH = 1
DEEPNORM_ALPHA = (2.0 * DEPTH) ** 0.25
LN_EPS = 1e-5
RMS_EPS = 1e-6

LANES = 128
NEG_BIG = -1e30
LOG2E = math.log2(math.e)
ONES_ROWS = 16
VT_ROWS = ATTN_HEADS * (HEAD_DIM + ONES_ROWS)
VMEM_LIMIT = 48 * 1024 * 1024

ROW_BLOCK = 512
ATTN_BLOCK = 256
ATTN_HEADS_PER_STEP = 8
S5_TIME_BLOCK = 32
S5_SLABS = 4
ROUTER_ROWS = 40
SLAB_CH = SSM_WIDTH // S5_SLABS
SLAB_STATE = SSM_GROUPS // S5_SLABS * SSM_STATE


def _split3(c):
    hi = c.astype(BF16)
    r1 = c - hi.astype(F32)
    mid = r1.astype(BF16)
    lo = (r1 - mid.astype(F32)).astype(BF16)
    return hi, mid, lo


def _inproj_kernel(x_ref, w_ref, wvt_ref, bf_ref, q_ref, k_ref, vt_ref, u_ref, cum_ref, carry_ref, *,
                   blocks_per_seq):
    i = pl.program_id(0)

    @pl.when(i % blocks_per_seq == 0)
    def _():
        carry_ref[...] = jnp.zeros_like(carry_ref)

    tm = x_ref.shape[0]
    xb = x_ref[...].astype(BF16)
    z = jnp.dot(xb, w_ref[...], preferred_element_type=F32)
    a = ATTN_WIDTH
    q_ref[...] = z[:, :a].astype(BF16)
    k_ref[...] = z[:, a:2 * a].astype(BF16)
    u_ref[...] = z[:, 2 * a:2 * a + SSM_WIDTH]
    vt = lax.dot_general(wvt_ref[...], xb, (((1,), (1,)), ((), ())), preferred_element_type=F32).astype(BF16)
    ones_rows = jnp.ones((ONES_ROWS, tm), BF16)
    vt = jnp.concatenate([piece for h in range(ATTN_HEADS)
                          for piece in (vt[HEAD_DIM * h:HEAD_DIM * (h + 1)], ones_rows)], axis=0)
    for jj in range(tm // ATTN_BLOCK):
        vt_ref[jj] = vt[:, jj * ATTN_BLOCK:(jj + 1) * ATTN_BLOCK]
    f = z[:, 2 * a + SSM_WIDTH:] + bf_ref[...]
    logf = jnp.minimum(f, 0.0) - jnp.log1p(jnp.exp(-jnp.abs(f)))
    row = lax.broadcasted_iota(jnp.int32, (tm, tm), 0)
    col = lax.broadcasted_iota(jnp.int32, (tm, tm), 1)
    tri = (row >= col).astype(BF16)
    hi, mid, lo = _split3(logf)
    cum = (jnp.dot(tri, hi, preferred_element_type=F32)
           + jnp.dot(tri, mid, preferred_element_type=F32)
           + jnp.dot(tri, lo, preferred_element_type=F32)) + carry_ref[...]
    carry_ref[...] = cum[tm - 1:tm, :]
    cum_ref[...] = cum[:, :ATTN_HEADS] * LOG2E


def _inproj(x2, w_cat, wvt, bf_row, seq):
    n_tok = x2.shape[0]
    tm = ROW_BLOCK
    assert seq % tm == 0 and n_tok % tm == 0 and tm % ATTN_BLOCK == 0
    bsz = n_tok // seq
    bps = seq // tm
    ncol = w_cat.shape[1]
    vblk = tm // ATTN_BLOCK
    row_spec = lambda w: pl.BlockSpec((tm, w), lambda i: (i, 0))
    return pl.pallas_call(
        functools.partial(_inproj_kernel, blocks_per_seq=bps),
        grid=(n_tok // tm,),
        in_specs=[row_spec(D_MODEL),
                  pl.BlockSpec((D_MODEL, ncol), lambda i: (0, 0)),
                  pl.BlockSpec((ATTN_WIDTH, D_MODEL), lambda i: (0, 0)),
                  pl.BlockSpec((1, LANES), lambda i: (0, 0))],
        out_specs=[row_spec(ATTN_WIDTH), row_spec(ATTN_WIDTH),
                   pl.BlockSpec((None, vblk, VT_ROWS, ATTN_BLOCK), lambda i: (i // bps, i % bps, 0, 0)),
                   row_spec(SSM_WIDTH), row_spec(ATTN_HEADS)],
        out_shape=[jax.ShapeDtypeStruct((n_tok, ATTN_WIDTH), BF16)] * 2
                  + [jax.ShapeDtypeStruct((bsz, seq // ATTN_BLOCK, VT_ROWS, ATTN_BLOCK), BF16),
                     jax.ShapeDtypeStruct((n_tok, SSM_WIDTH), F32),
                     jax.ShapeDtypeStruct((n_tok, ATTN_HEADS), F32)],
        scratch_shapes=[pltpu.VMEM((1, LANES), F32)],
        compiler_params=pltpu.CompilerParams(dimension_semantics=("arbitrary",),
                                             vmem_limit_bytes=VMEM_LIMIT),
        name="inproj",
    )(x2, w_cat, wvt, bf_row)


def _bias_lanes_q(c):
    hi, mid, lo = (t.astype(F32) for t in _split3(c))
    lane = lax.broadcasted_iota(jnp.int32, (c.shape[0], LANES), 1)
    sel = jnp.where(lane == 0, hi, jnp.where(lane == 1, mid, jnp.where(lane == 2, lo,
                    jnp.where(lane < 6, 1.0, 0.0))))
    return sel.astype(BF16)


def _bias_lanes_k(c):
    hi, mid, lo = (t.astype(F32) for t in _split3(c))
    lane = lax.broadcasted_iota(jnp.int32, (c.shape[0], LANES), 1)
    sel = jnp.where(lane < 3, 1.0, jnp.where(lane == 3, -hi, jnp.where(lane == 4, -mid,
                    jnp.where(lane == 5, -lo, 0.0))))
    return sel.astype(BF16)


def _attn_kernel(q_ref, k_ref, vt_ref, cum_ref, o_ref, extk_ref, *, tq, seq, nh):
    qi = pl.program_id(2)
    n_tiles = seq // tq

    @pl.when(qi == 0)
    def _():
        def fill(j, carry):
            r0 = pl.multiple_of(j * tq, tq)
            c = cum_ref[pl.ds(r0, tq), :]
            for h in range(nh):
                extk_ref[h, pl.ds(r0, tq), :] = _bias_lanes_k(c[:, h:h + 1])
            return carry
        lax.fori_loop(0, n_tiles, fill, 0)

    q0 = pl.multiple_of(qi * tq, tq)
    cq = cum_ref[pl.ds(q0, tq), :]
    lane = lax.broadcasted_iota(jnp.int32, (tq, LANES), 1)
    q_augs = []
    for h in range(nh):
        qb = q_ref[:, LANES * (h // 2):LANES * (h // 2 + 1)]
        head_lanes = (lane < HEAD_DIM) if h % 2 == 0 else (lane >= HEAD_DIM)
        qh = jnp.where(head_lanes, qb, jnp.zeros((), BF16))
        q_augs.append(jnp.concatenate([qh, _bias_lanes_q(cq[:, h:h + 1])], axis=1))
    key_i = lax.broadcasted_iota(jnp.int32, (tq, tq), 0)
    qry_i = lax.broadcasted_iota(jnp.int32, (tq, tq), 1)
    causal = key_i <= qry_i
    vrows = HEAD_DIM + ONES_ROWS

    def scores_of(j):
        k0 = pl.multiple_of(j * tq, tq)
        out = []
        for h in range(nh):
            kj = k_ref[pl.ds(k0, tq), LANES * (h // 2):LANES * (h // 2 + 1)]
            k_aug = jnp.concatenate([kj, extk_ref[h, pl.ds(k0, tq), :]], axis=1)
            out.append(lax.dot_general(k_aug, q_augs[h], (((1,), (1,)), ((), ())), preferred_element_type=F32))
        return tuple(out)

    def absorb(j, scores, carry):
        stats = []
        for h in range(nh):
            m = carry[h][0]
            m_new = jnp.maximum(m, jnp.max(scores[h], axis=0, keepdims=True))
            stats.append((m_new, jnp.exp2(m - m_new), jnp.exp2(scores[h] - m_new).astype(BF16)))
        new = []
        for h in range(nh):
            m_new, corr, p = stats[h]
            res = jnp.dot(vt_ref[j, vrows * h:vrows * (h + 1), :], p, preferred_element_type=F32)
            acc = corr * carry[h][2] + res[0:HEAD_DIM, :]
            l = corr * carry[h][1] + res[HEAD_DIM:HEAD_DIM + 1, :]
            new.append((m_new, l, acc))
        return tuple(new)

    def step(j, carry):
        return absorb(j, scores_of(j), carry)

    init1 = (jnp.full((1, tq), NEG_BIG, F32), jnp.zeros((1, tq), F32), jnp.zeros((HEAD_DIM, tq), F32))
    carry = lax.fori_loop(0, qi, step, (init1,) * nh)
    carry = absorb(qi, tuple(jnp.where(causal, s, NEG_BIG) for s in scores_of(qi)), carry)
    out_t = jnp.concatenate([c[2] / c[1] for c in carry], axis=0)
    o_ref[...] = out_t.T


def _attention(q, k, vt, cum_g):
    bsz, seq, _ = q.shape
    tq = ATTN_BLOCK
    nh = ATTN_HEADS_PER_STEP
    assert seq % tq == 0 and ATTN_HEADS % nh == 0 and nh % 2 == 0
    w = nh * HEAD_DIM
    return pl.pallas_call(
        functools.partial(_attn_kernel, tq=tq, seq=seq, nh=nh),
        grid=(bsz, ATTN_HEADS // nh, seq // tq),
        in_specs=[pl.BlockSpec((None, tq, w), lambda b, g, i: (b, i, g)),
                  pl.BlockSpec((None, seq, w), lambda b, g, i: (b, 0, g)),
                  pl.BlockSpec((None, seq // tq, nh * (HEAD_DIM + ONES_ROWS), tq), lambda b, g, i: (b, 0, g, 0)),
                  pl.BlockSpec((None, None, seq, nh), lambda b, g, i: (b, g, 0, 0))],
        out_specs=pl.BlockSpec((None, tq, w), lambda b, g, i: (b, i, g)),
        out_shape=jax.ShapeDtypeStruct((bsz, seq, ATTN_WIDTH), F32),
        scratch_shapes=[pltpu.VMEM((nh, seq, LANES), BF16)],
        compiler_params=pltpu.CompilerParams(dimension_semantics=("parallel", "parallel", "arbitrary"),
                                             vmem_limit_bytes=VMEM_LIMIT),
        name="fox_attention",
    )(q, k, vt, cum_g)


def _s5_kernel(u_ref, bw_ref, cw_ref, ar_ref, ai_ref, d_ref, wglu_ref, bglu_ref, g_ref, o_ref,
               state_ref, bu_ref, *, tc, bsz):
    @pl.when(pl.program_id(0) == 0)
    def _():
        state_ref[...] = jnp.zeros_like(state_ref)

    rows = tc * bsz
    ns = SLAB_STATE
    u = u_ref[...].reshape(rows, SSM_WIDTH)
    ub = u.astype(BF16)
    for i in range(S5_SLABS):
        bu_ref[i] = jnp.dot(ub[:, SLAB_CH * i:SLAB_CH * (i + 1)], bw_ref[i], preferred_element_type=F32)
    for i in range(S5_SLABS):
        ar = jnp.broadcast_to(ar_ref[i], (bsz, ns))
        ai = jnp.broadcast_to(ai_ref[i], (bsz, ns))
        xr = state_ref[i, :, 0:ns]
        xi = state_ref[i, :, ns:2 * ns]
        for t in range(tc):
            r0 = t * bsz
            nr = ar * xr - ai * xi + bu_ref[i, r0:r0 + bsz, 0:ns]
            ni = ar * xi + ai * xr + bu_ref[i, r0:r0 + bsz, ns:2 * ns]
            bu_ref[i, r0:r0 + bsz, 0:ns] = nr
            bu_ref[i, r0:r0 + bsz, ns:2 * ns] = ni
            xr, xi = nr, ni
        state_ref[i, :, 0:ns] = xr
        state_ref[i, :, ns:2 * ns] = xi
    ys = [jnp.dot(bu_ref[i].astype(BF16), cw_ref[i], preferred_element_type=F32) for i in range(S5_SLABS)]
    y = jnp.concatenate(ys, axis=1) + d_ref[...] * u
    y = 0.5 * y * (1.0 + jnp.tanh(math.sqrt(2.0 / math.pi) * (y + 0.044715 * (y * y * y))))
    z = jnp.dot(y.astype(BF16), wglu_ref[...], preferred_element_type=F32) + bglu_ref[...]
    y = y * (1.0 / (1.0 + jnp.exp(-z)))
    y = y * lax.rsqrt(jnp.mean(y * y, axis=-1, keepdims=True) + RMS_EPS) * g_ref[...]
    o_ref[...] = y.astype(BF16).reshape(tc, bsz, SSM_WIDTH)


def _s5_weights(lam_re, lam_im, log_step, b_re, b_im, c_re, c_im):
    delta = jnp.exp(log_step)[:, None]
    mag = jnp.exp(lam_re * delta)
    ar = mag * jnp.cos(lam_im * delta)
    ai = mag * jnp.sin(lam_im * delta)
    den = lam_re * lam_re + lam_im * lam_im
    num_re = ar - 1.0
    coef_re = (num_re * lam_re + ai * lam_im) / den
    coef_im = (ai * lam_re - num_re * lam_im) / den
    bbar_re = coef_re[..., None] * b_re - coef_im[..., None] * b_im
    bbar_im = coef_re[..., None] * b_im + coef_im[..., None] * b_re
    gs = SSM_GROUPS // S5_SLABS
    eye = jnp.eye(gs, dtype=F32)

    def b_slab(bb):
        bb = bb.reshape(S5_SLABS, gs, SSM_STATE, SSM_GROUP)
        w = jnp.einsum('ab,sapc->sacbp', eye, bb)
        return w.reshape(S5_SLABS, gs * SSM_GROUP, gs * SSM_STATE)

    def c_slab(cc):
        cc = cc.reshape(S5_SLABS, gs, SSM_GROUP, SSM_STATE)
        w = jnp.einsum('ab,sacp->sbpac', eye, cc)
        return w.reshape(S5_SLABS, gs * SSM_STATE, gs * SSM_GROUP)

    bw = jnp.concatenate([b_slab(bbar_re), b_slab(bbar_im)], axis=2).astype(BF16)
    cw = jnp.concatenate([c_slab(c_re), -c_slab(c_im)], axis=1).astype(BF16)
    ar_s = ar.reshape(S5_SLABS, 1, SLAB_STATE)
    ai_s = ai.reshape(S5_SLABS, 1, SLAB_STATE)
    return bw, cw, ar_s, ai_s


def _s5(u_tb, bw, cw, ar_s, ai_s, d_row, w_glu, b_glu_row, g_row):
    seq, bsz, _ = u_tb.shape
    tc = S5_TIME_BLOCK
    assert seq % tc == 0 and bsz % 16 == 0
    full = lambda shape: pl.BlockSpec(shape, lambda t: (0,) * len(shape))
    return pl.pallas_call(
        functools.partial(_s5_kernel, tc=tc, bsz=bsz),
        grid=(seq // tc,),
        in_specs=[pl.BlockSpec((tc, bsz, SSM_WIDTH), lambda t: (t, 0, 0)),
                  full(bw.shape), full(cw.shape), full(ar_s.shape), full(ai_s.shape),
                  full((1, SSM_WIDTH)), full((SSM_WIDTH, SSM_WIDTH)), full((1, SSM_WIDTH)),
                  full((1, SSM_WIDTH))],
        out_specs=pl.BlockSpec((tc, bsz, SSM_WIDTH), lambda t: (t, 0, 0)),
        out_shape=jax.ShapeDtypeStruct((seq, bsz, SSM_WIDTH), BF16),
        scratch_shapes=[pltpu.VMEM((S5_SLABS, bsz, 2 * SLAB_STATE), F32),
                        pltpu.VMEM((S5_SLABS, tc * bsz, 2 * SLAB_STATE), F32)],
        compiler_params=pltpu.CompilerParams(dimension_semantics=("arbitrary",),
                                             vmem_limit_bytes=VMEM_LIMIT),
        name="s5_branch",
    )(u_tb, bw, cw, ar_s, ai_s, d_row, w_glu, b_glu_row, g_row)


def _layer_norm_rows(v, g, b):
    mu = jnp.mean(v, axis=-1, keepdims=True)
    c = v - mu
    var = jnp.mean(c * c, axis=-1, keepdims=True)
    return c * lax.rsqrt(var + LN_EPS) * g + b


def _outproj_kernel(attn_ref, ssm_ref, x_ref, wo_ref, ag_ref, g1_ref, b1_ref, wrt_ref, brt_ref, triu_ref,
                    h_ref, eid_ref, gate_ref, rank_ref, cnt_ref):
    a = attn_ref[...]
    a = a * lax.rsqrt(jnp.mean(a * a, axis=-1, keepdims=True) + RMS_EPS) * ag_ref[...]
    mix = (jnp.dot(a.astype(BF16), wo_ref[0:ATTN_WIDTH, :], preferred_element_type=F32)
           + jnp.dot(ssm_ref[...], wo_ref[ATTN_WIDTH:, :], preferred_element_type=F32))
    h = _layer_norm_rows(DEEPNORM_ALPHA * x_ref[...] + mix, g1_ref[...], b1_ref[...])
    h_ref[...] = h

    logits = lax.dot_general(wrt_ref[...], h.astype(BF16), (((1,), (1,)), ((), ())),
                             preferred_element_type=F32)[:ROUTER_ROWS] + brt_ref[...]
    tm = logits.shape[1]
    row = lax.broadcasted_iota(jnp.int32, (ROUTER_ROWS, tm), 0).astype(F32)
    no_row = float(LANES)
    is_g = row < N_EXPERT_GROUPS
    gl = jnp.where(is_g, logits, NEG_BIG)
    gmax = jnp.max(gl, axis=0, keepdims=True)
    g_idx = jnp.min(jnp.where(is_g & (gl == gmax), row, no_row), axis=0, keepdims=True)
    g_p = 1.0 / jnp.sum(jnp.where(is_g, jnp.exp(gl - gmax), 0.0), axis=0, keepdims=True)
    first = N_EXPERT_GROUPS + g_idx * EXPERTS_PER_GROUP
    in_grp = (row >= first) & (row < first + EXPERTS_PER_GROUP)
    el = jnp.where(in_grp, logits, NEG_BIG)
    m1 = jnp.max(el, axis=0, keepdims=True)
    i1 = jnp.min(jnp.where(in_grp & (el == m1), row, no_row), axis=0, keepdims=True)
    rest = in_grp & (row != i1)
    el2 = jnp.where(rest, logits, NEG_BIG)
    m2 = jnp.max(el2, axis=0, keepdims=True)
    i2 = jnp.min(jnp.where(rest & (el2 == m2), row, no_row), axis=0, keepdims=True)
    r = jnp.exp(m2 - m1)
    p1 = 1.0 / (1.0 + r)
    p2 = r / (1.0 + r)
    k2 = lax.broadcasted_iota(jnp.int32, (2, tm), 0)
    eid_ref[...] = (jnp.where(k2 == 0, i1, i2) - N_EXPERT_GROUPS).astype(jnp.int32)
    gate_ref[...] = g_p * jnp.where(k2 == 0, p1, p2)

    @pl.when(pl.program_id(0) == 0)
    def _():
        cnt_ref[...] = jnp.zeros_like(cnt_ref)
    hit1 = row == i1
    hit2 = row == i2
    onehot = jnp.where(hit1 | hit2, 1.0, 0.0)
    cnt = cnt_ref[...]
    before = jnp.dot(onehot.astype(BF16), triu_ref[...], preferred_element_type=F32) + cnt
    r1 = jnp.sum(jnp.where(hit1, before, 0.0), axis=0, keepdims=True)
    r2 = jnp.sum(jnp.where(hit2, before, 0.0), axis=0, keepdims=True)
    rank_ref[...] = jnp.where(k2 == 0, r1, r2).astype(jnp.int32)
    cnt_ref[...] = cnt + jnp.sum(onehot, axis=1, keepdims=True)


def _outproj(attn2, ssm2, x2, wo, ag_row, g1_row, b1_row, wrt, brt):
    n_tok = x2.shape[0]
    tm = ROW_BLOCK
    triu = jnp.triu(jnp.ones((tm, tm), BF16), 1)
    row_spec = lambda w: pl.BlockSpec((tm, w), lambda i: (i, 0))
    col_spec = lambda r: pl.BlockSpec((r, tm), lambda i: (0, i))
    full = lambda shape: pl.BlockSpec(shape, lambda i: (0,) * len(shape))
    return pl.pallas_call(
        _outproj_kernel,
        grid=(n_tok // tm,),
        in_specs=[row_spec(ATTN_WIDTH), row_spec(SSM_WIDTH), row_spec(D_MODEL),
                  full((D_MODEL, D_MODEL)), full((1, ATTN_WIDTH)), full((1, D_MODEL)), full((1, D_MODEL)),
                  full((LANES, D_MODEL)), full((ROUTER_ROWS, 1)), full((tm, tm))],
        out_specs=[row_spec(D_MODEL), col_spec(2), col_spec(2), col_spec(2), full((ROUTER_ROWS, 1))],
        out_shape=[jax.ShapeDtypeStruct((n_tok, D_MODEL), F32),
                   jax.ShapeDtypeStruct((2, n_tok), jnp.int32),
                   jax.ShapeDtypeStruct((2, n_tok), F32),
                   jax.ShapeDtypeStruct((2, n_tok), jnp.int32),
                   jax.ShapeDtypeStruct((ROUTER_ROWS, 1), F32)],
        compiler_params=pltpu.CompilerParams(dimension_semantics=("arbitrary",),
                                             vmem_limit_bytes=VMEM_LIMIT),
        name="outproj_ln_router",
    )(attn2, ssm2, x2, wo, ag_row, g1_row, b1_row, wrt, brt, triu)


def _dispatch_plan(eid, rank, cnt_col, n_tok):
    counts = cnt_col[N_EXPERT_GROUPS:N_EXPERT_GROUPS + N_EXPERTS, 0].astype(jnp.int32)
    padded = ((counts + MOE_BLOCK - 1) // MOE_BLOCK) * MOE_BLOCK
    pends = jnp.cumsum(padded)
    pstarts = pends - padded
    onehot = eid[:, :, None] == jnp.arange(N_EXPERTS, dtype=jnp.int32)[None, None, :]
    slot = rank + jnp.sum(jnp.where(onehot, pstarts[None, None, :], 0), axis=-1)
    n_pad = ((2 * n_tok + N_EXPERTS * (MOE_BLOCK - 1) + MOE_BLOCK - 1) // MOE_BLOCK) * MOE_BLOCK
    n_blocks = n_pad // MOE_BLOCK
    blk0 = jnp.arange(n_blocks, dtype=jnp.int32) * MOE_BLOCK
    block_e = jnp.minimum(jnp.sum((pends[None, :] <= blk0[:, None]).astype(jnp.int32), axis=1), N_EXPERTS - 1)
    n_used = (pends[-1] // MOE_BLOCK).astype(jnp.int32).reshape(1)
    tail = jnp.where(padded > 0, pends - MOE_BLOCK, -1).astype(jnp.int32)
    return slot, block_e, n_used, tail, n_blocks


def _row_dma(src, src_row, dst, dst_row, sem):
    return pltpu.make_async_copy(src.at[pl.ds(src_row, 1)], dst.at[pl.ds(dst_row, 1)], sem)


def _dispatch_kernel(tail_ref, nu_ref, s0_ref, s1_ref, h_ref, xb_hbm, zbuf, sem_z, sem, *, n_blocks):
    tm = h_ref.shape[0]

    @pl.when(pl.program_id(0) == 0)
    def _():
        zbuf[...] = jnp.zeros_like(zbuf)

        def zero_copy(row0):
            return pltpu.make_async_copy(zbuf, xb_hbm.at[pl.ds(pl.multiple_of(row0, MOE_BLOCK), MOE_BLOCK)], sem_z)

        def each_zero_block(fn):
            for e in range(N_EXPERTS):
                @pl.when(tail_ref[e] >= 0)
                def _():
                    fn(zero_copy(tail_ref[e]))

            def unused(b, carry):
                fn(zero_copy(b * MOE_BLOCK))
                return carry
            lax.fori_loop(nu_ref[0], n_blocks, unused, 0)

        each_zero_block(lambda c: c.start())
        each_zero_block(lambda c: c.wait())

    def scatter_row(r, carry):
        _row_dma(h_ref, r, xb_hbm, s0_ref[0, r], sem).start()
        _row_dma(h_ref, r, xb_hbm, s1_ref[0, r], sem).start()
        return carry
    lax.fori_loop(0, tm, scatter_row, 0, unroll=8)
    for _ in range(2):
        pltpu.make_async_copy(h_ref, xb_hbm.at[pl.ds(0, tm)], sem).wait()


def _dispatch(h, slot, tail, n_used, n_blocks):
    n_tok = h.shape[0]
    tm = TOKEN_BLOCK
    nb = n_tok // tm
    s0 = slot[0].reshape(nb, 1, tm)
    s1 = slot[1].reshape(nb, 1, tm)
    smem_blk = pl.BlockSpec((None, 1, tm), lambda i, tl, nu: (i, 0, 0), memory_space=pltpu.SMEM)
    grid_spec = pltpu.PrefetchScalarGridSpec(
        num_scalar_prefetch=2,
        grid=(nb,),
        in_specs=[smem_blk, smem_blk, pl.BlockSpec((tm, D_MODEL), lambda i, tl, nu: (i, 0))],
        out_specs=pl.BlockSpec(memory_space=pl.ANY),
        scratch_shapes=[pltpu.VMEM((MOE_BLOCK, D_MODEL), F32), pltpu.SemaphoreType.DMA,
                        pltpu.SemaphoreType.DMA],
    )
    return pl.pallas_call(
        functools.partial(_dispatch_kernel, n_blocks=n_blocks),
        grid_spec=grid_spec,
        out_shape=jax.ShapeDtypeStruct((n_blocks * MOE_BLOCK, D_MODEL), F32),
        compiler_params=pltpu.CompilerParams(dimension_semantics=("arbitrary",),
                                             vmem_limit_bytes=VMEM_LIMIT),
        name="moe_dispatch",
    )(tail, n_used, s0, s1, h)


def _expert_kernel(be_ref, nu_ref, x_ref, wg_ref, wu_ref, wd_ref, y_ref, wgu_t, wd_t, yt_ref):
    i = pl.program_id(0)
    used = i < nu_ref[0]

    @pl.when(used & ((i == 0) | (be_ref[i] != be_ref[jnp.maximum(i - 1, 0)])))
    def _():
        wgu_t[0:D_EXPERT, :] = wg_ref[...].T.astype(BF16)
        wgu_t[D_EXPERT:, :] = wu_ref[...].T.astype(BF16)
        wd_t[...] = wd_ref[...].T.astype(BF16)

    @pl.when(used)
    def _():
        sub = range(MOE_BLOCK // MOE_SUB)
        rows = lambda c: slice(c * MOE_SUB, (c + 1) * MOE_SUB)
        gus = [lax.dot_general(wgu_t[...], x_ref[rows(c), :].astype(BF16), (((1,), (1,)), ((), ())),
                               preferred_element_type=F32) for c in sub]
        hids = []
        for c in sub:
            gt = gus[c][0:D_EXPERT]
            hids.append((gt * (1.0 / (1.0 + jnp.exp(-gt))) * gus[c][D_EXPERT:]).astype(BF16))
        for c in sub:
            yt_ref[c] = jnp.dot(wd_t[...], hids[c], preferred_element_type=F32)
        for c in sub:
            y_ref[rows(c), :] = yt_ref[c].T

    @pl.when(jnp.logical_not(used))
    def _():
        y_ref[...] = jnp.zeros_like(y_ref)


def _experts(xb, block_e, n_used, wg, wu, wd):
    n_blocks = xb.shape[0] // MOE_BLOCK
    last = lambda i, nu: jnp.minimum(i, nu[0] - 1)
    grid_spec = pltpu.PrefetchScalarGridSpec(
        num_scalar_prefetch=2,
        grid=(n_blocks,),
        in_specs=[pl.BlockSpec((MOE_BLOCK, D_MODEL), lambda i, be, nu: (last(i, nu), 0)),
                  pl.BlockSpec((None, D_MODEL, D_EXPERT), lambda i, be, nu: (be[last(i, nu)], 0, 0)),
                  pl.BlockSpec((None, D_MODEL, D_EXPERT), lambda i, be, nu: (be[last(i, nu)], 0, 0)),
                  pl.BlockSpec((None, D_EXPERT, D_MODEL), lambda i, be, nu: (be[last(i, nu)], 0, 0))],
        out_specs=pl.BlockSpec((MOE_BLOCK, D_MODEL), lambda i, be, nu: (i, 0)),
        scratch_shapes=[pltpu.VMEM((2 * D_EXPERT, D_MODEL), BF16), pltpu.VMEM((D_MODEL, D_EXPERT), BF16),
                        pltpu.VMEM((MOE_BLOCK // MOE_SUB, D_MODEL, MOE_SUB), F32)],
    )
    return pl.pallas_call(
        _expert_kernel,
        grid_spec=grid_spec,
        out_shape=jax.ShapeDtypeStruct(xb.shape, F32),
        compiler_params=pltpu.CompilerParams(dimension_semantics=("arbitrary",),
                                             vmem_limit_bytes=VMEM_LIMIT),
        name="moe_experts",
    )(block_e, n_used, xb, wg, wu, wd)


def _combine_kernel(s0_ref, s1_ref, s0n_ref, s1n_ref, h_ref, gate_ref, g2_ref, b2_ref, yb_hbm, o_ref, buf, sem):
    i = pl.program_id(0)
    n = pl.num_programs(0)
    tm = h_ref.shape[0]
    cur = i % 2

    def gather(a_ref, b_ref, which):
        def gather_row(r, carry):
            _row_dma(yb_hbm, a_ref[0, r], buf.at[which, 0], r, sem.at[which]).start()
            _row_dma(yb_hbm, b_ref[0, r], buf.at[which, 1], r, sem.at[which]).start()
            return carry
        lax.fori_loop(0, tm, gather_row, 0, unroll=8)

    @pl.when(i == 0)
    def _():
        gather(s0_ref, s1_ref, 0)

    @pl.when(i + 1 < n)
    def _():
        gather(s0n_ref, s1n_ref, 1 - cur)

    for k in range(2):
        pltpu.make_async_copy(yb_hbm.at[pl.ds(0, tm)], buf.at[cur, k], sem.at[cur]).wait()
    gate = gate_ref[...]
    moe = gate[:, 0:1] * buf[cur, 0] + gate[:, 1:2] * buf[cur, 1]
    o_ref[...] = _layer_norm_rows(DEEPNORM_ALPHA * h_ref[...] + moe, g2_ref[...], b2_ref[...])


def _combine(h, yb, slot, gate, g2_row, b2_row):
    n_tok = h.shape[0]
    tm = TOKEN_BLOCK
    nb = n_tok // tm
    s0 = slot[0].reshape(nb, 1, tm)
    s1 = slot[1].reshape(nb, 1, tm)
    row_spec = lambda w: pl.BlockSpec((tm, w), lambda i: (i, 0))
    full = lambda shape: pl.BlockSpec(shape, lambda i: (0,) * len(shape))
    smem_cur = pl.BlockSpec((None, 1, tm), lambda i: (i, 0, 0), memory_space=pltpu.SMEM)
    smem_next = pl.BlockSpec((None, 1, tm), lambda i: (jnp.minimum(i + 1, nb - 1), 0, 0), memory_space=pltpu.SMEM)
    return pl.pallas_call(
        _combine_kernel,
        grid=(nb,),
        in_specs=[smem_cur, smem_cur, smem_next, smem_next,
                  row_spec(D_MODEL), row_spec(2), full((1, D_MODEL)), full((1, D_MODEL)),
                  pl.BlockSpec(memory_space=pl.ANY)],
        out_specs=row_spec(D_MODEL),
        out_shape=jax.ShapeDtypeStruct((n_tok, D_MODEL), F32),
        scratch_shapes=[pltpu.VMEM((2, 2, tm, D_MODEL), F32), pltpu.SemaphoreType.DMA((2,))],
        compiler_params=pltpu.CompilerParams(dimension_semantics=("arbitrary",),
                                             vmem_limit_bytes=VMEM_LIMIT),
        name="combine_ln",
    )(s0, s1, s0, s1, h, gate, g2_row, b2_row, yb)


def kernel(x, w_in, b_fgate, s5_lambda_re, s5_lambda_im, s5_log_step, s5_b_re, s5_b_im, s5_c_re, s5_c_im, s5_d, w_glu, b_glu, attn_norm_g, ssm_norm_g, w_out, ln1_g, ln1_b, w_router_group, b_router_group, w_router_expert, b_router_expert, w_gate, w_up, w_down, ln2_g, ln2_b):
    bsz, seq, d = x.shape
    n_tok = bsz * seq
    a = ATTN_WIDTH
    for l in range(DEPTH):
        x2 = x.reshape(n_tok, d)
        w = w_in[l]
        f0 = 3 * a
        w_cat = jnp.concatenate(
            [w[:, :a] * (HEAD_DIM ** -0.5 * LOG2E), w[:, a:2 * a], w[:, f0 + ATTN_HEADS:],
             jnp.pad(w[:, f0:f0 + ATTN_HEADS], ((0, 0), (0, LANES - ATTN_HEADS)))], axis=1).astype(BF16)
        wvt = w[:, 2 * a:f0].T.astype(BF16)
        bf_row = jnp.pad(b_fgate[l], (0, LANES - ATTN_HEADS)).reshape(1, LANES)
        q, k, vt, u, cum = _inproj(x2, w_cat, wvt, bf_row, seq)

        cum_g = cum.reshape(bsz, seq, ATTN_HEADS // ATTN_HEADS_PER_STEP, ATTN_HEADS_PER_STEP).transpose(0, 2, 1, 3)
        attn = _attention(q.reshape(bsz, seq, a), k.reshape(bsz, seq, a), vt, cum_g)

        bw, cw, ar_s, ai_s = _s5_weights(s5_lambda_re[l], s5_lambda_im[l], s5_log_step[l],
                                         s5_b_re[l], s5_b_im[l], s5_c_re[l], s5_c_im[l])
        u_tb = u.reshape(bsz, seq, SSM_WIDTH).transpose(1, 0, 2)
        ssm_tb = _s5(u_tb, bw, cw, ar_s, ai_s, s5_d[l].reshape(1, SSM_WIDTH), w_glu[l].astype(BF16),
                     b_glu[l].reshape(1, SSM_WIDTH), ssm_norm_g[l].reshape(1, SSM_WIDTH))
        ssm2 = ssm_tb.transpose(1, 0, 2).reshape(n_tok, SSM_WIDTH)

        n_route = N_EXPERT_GROUPS + N_EXPERTS
        wrt = jnp.pad(jnp.concatenate([w_router_group[l], w_router_expert[l]], axis=1).T,
                      ((0, LANES - n_route), (0, 0))).astype(BF16)
        brt = jnp.pad(jnp.concatenate([b_router_group[l], b_router_expert[l]]),
                      (0, ROUTER_ROWS - n_route)).reshape(ROUTER_ROWS, 1)
        h, eid, gate, rank, cnt_col = _outproj(
            attn.reshape(n_tok, a), ssm2, x2, w_out[l].astype(BF16), attn_norm_g[l].reshape(1, a),
            ln1_g[l].reshape(1, d), ln1_b[l].reshape(1, d), wrt, brt)

        slot, block_e, n_used, tail, n_blocks = _dispatch_plan(eid, rank, cnt_col, n_tok)
        xb = _dispatch(h, slot, tail, n_used, n_blocks)
        yb = _experts(xb, block_e, n_used, w_gate[l], w_up[l], w_down[l])
        x = _combine(h, yb, slot, gate.T, ln2_g[l].reshape(1, d), ln2_b[l].reshape(1, d)).reshape(bsz, seq, d)
    return x
```

```python
import functools
import math

import jax
import jax.numpy as jnp
from jax import lax
from jax.experimental import pallas as pl
from jax.experimental.pallas import tpu as pltpu

F32 = jnp.float32
BF16 = jnp.bfloat16

D_MODEL = 1024
HEAD_DIM = 64
ATTN_WIDTH = 512
ATTN_HEADS = 8
SSM_WIDTH = 512
SSM_GROUP = 16
SSM_GROUPS = 32
SSM_STATE = 64
N_EXPERT_GROUPS = 4
EXPERTS_PER_GROUP = 8
N_EXPERTS = 32
D_EXPERT = 512
MOE_BLOCK = 512
MOE_SUB = 256
TOKEN_BLOCK = 256
DEPTH = 1
DEEPNORM_ALPHA = (2.0 * DEPTH) ** 0.25
LN_EPS = 1e-5
RMS_EPS = 1e-6

LANES = 128
NEG_BIG = -1e30
LOG2E = math.log2(math.e)
ONES_ROWS = 16
VT_ROWS = ATTN_HEADS * (HEAD_DIM + ONES_ROWS)
VMEM_LIMIT = 48 * 1024 * 1024

ROW_BLOCK = 512
ATTN_BLOCK = 256
ATTN_HEADS_PER_STEP = 8
S5_TIME_BLOCK = 64
S5_SLABS = 4
ROUTER_ROWS = 40
SLAB_CH = SSM_WIDTH // S5_SLABS
SLAB_STATE = SSM_GROUPS // S5_SLABS * SSM_STATE


def _split3(c):
    hi = c.astype(BF16)
    r1 = c - hi.astype(F32)
    mid = r1.astype(BF16)
    lo = (r1 - mid.astype(F32)).astype(BF16)
    return hi, mid, lo


def _inproj_kernel(x_ref, w_ref, wvt_ref, bf_ref, q_ref, k_ref, vt_ref, u_ref, cum_ref, carry_ref, *,
                   blocks_per_seq):
    i = pl.program_id(0)

    @pl.when(i % blocks_per_seq == 0)
    def _():
        carry_ref[...] = jnp.zeros_like(carry_ref)

    tm = x_ref.shape[0]
    xb = x_ref[...].astype(BF16)
    z = jnp.dot(xb, w_ref[...], preferred_element_type=F32)
    a = ATTN_WIDTH
    q_ref[...] = z[:, :a].astype(BF16)
    k_ref[...] = z[:, a:2 * a].astype(BF16)
    u_ref[...] = z[:, 2 * a:2 * a + SSM_WIDTH]
    vt = lax.dot_general(wvt_ref[...], xb, (((1,), (1,)), ((), ())), preferred_element_type=F32).astype(BF16)
    ones_rows = jnp.ones((ONES_ROWS, tm), BF16)
    vt = jnp.concatenate([piece for h in range(ATTN_HEADS)
                          for piece in (vt[HEAD_DIM * h:HEAD_DIM * (h + 1)], ones_rows)], axis=0)
    for jj in range(tm // ATTN_BLOCK):
        vt_ref[jj] = vt[:, jj * ATTN_BLOCK:(jj + 1) * ATTN_BLOCK]
    f = z[:, 2 * a + SSM_WIDTH:] + bf_ref[...]
    logf = jnp.minimum(f, 0.0) - jnp.log1p(jnp.exp(-jnp.abs(f)))
    row = lax.broadcasted_iota(jnp.int32, (tm, tm), 0)
    col = lax.broadcasted_iota(jnp.int32, (tm, tm), 1)
    tri = (row >= col).astype(BF16)
    hi, mid, lo = _split3(logf)
    cum = (jnp.dot(tri, hi, preferred_element_type=F32)
           + jnp.dot(tri, mid, preferred_element_type=F32)
           + jnp.dot(tri, lo, preferred_element_type=F32)) + carry_ref[...]
    carry_ref[...] = cum[tm - 1:tm, :]
    cum_ref[...] = cum[:, :ATTN_HEADS] * LOG2E


def _inproj(x2, w_cat, wvt, bf_row, seq):
    n_tok = x2.shape[0]
    tm = ROW_BLOCK
    assert seq % tm == 0 and n_tok % tm == 0 and tm % ATTN_BLOCK == 0
    bsz = n_tok // seq
    bps = seq // tm
    ncol = w_cat.shape[1]
    vblk = tm // ATTN_BLOCK
    row_spec = lambda w: pl.BlockSpec((tm, w), lambda i: (i, 0))
    return pl.pallas_call(
        functools.partial(_inproj_kernel, blocks_per_seq=bps),
        grid=(n_tok // tm,),
        in_specs=[row_spec(D_MODEL),
                  pl.BlockSpec((D_MODEL, ncol), lambda i: (0, 0)),
                  pl.BlockSpec((ATTN_WIDTH, D_MODEL), lambda i: (0, 0)),
                  pl.BlockSpec((1, LANES), lambda i: (0, 0))],
        out_specs=[row_spec(ATTN_WIDTH), row_spec(ATTN_WIDTH),
                   pl.BlockSpec((None, vblk, VT_ROWS, ATTN_BLOCK), lambda i: (i // bps, i % bps, 0, 0)),
                   row_spec(SSM_WIDTH), row_spec(ATTN_HEADS)],
        out_shape=[jax.ShapeDtypeStruct((n_tok, ATTN_WIDTH), BF16)] * 2
                  + [jax.ShapeDtypeStruct((bsz, seq // ATTN_BLOCK, VT_ROWS, ATTN_BLOCK), BF16),
                     jax.ShapeDtypeStruct((n_tok, SSM_WIDTH), F32),
                     jax.ShapeDtypeStruct((n_tok, ATTN_HEADS), F32)],
        scratch_shapes=[pltpu.VMEM((1, LANES), F32)],
        compiler_params=pltpu.CompilerParams(dimension_semantics=("arbitrary",),
                                             vmem_limit_bytes=VMEM_LIMIT),
        name="inproj",
    )(x2, w_cat, wvt, bf_row)


def _bias_lanes_q(c):
    hi, mid, lo = (t.astype(F32) for t in _split3(c))
    lane = lax.broadcasted_iota(jnp.int32, (c.shape[0], LANES), 1)
    sel = jnp.where(lane == 0, hi, jnp.where(lane == 1, mid, jnp.where(lane == 2, lo,
                    jnp.where(lane < 6, 1.0, 0.0))))
    return sel.astype(BF16)


def _bias_lanes_k(c):
    hi, mid, lo = (t.astype(F32) for t in _split3(c))
    lane = lax.broadcasted_iota(jnp.int32, (c.shape[0], LANES), 1)
    sel = jnp.where(lane < 3, 1.0, jnp.where(lane == 3, -hi, jnp.where(lane == 4, -mid,
                    jnp.where(lane == 5, -lo, 0.0))))
    return sel.astype(BF16)


def _attn_kernel(q_ref, k_ref, vt_ref, cum_ref, o_ref, extk_ref, *, tq, seq, nh):
    qi = pl.program_id(2)
    n_tiles = seq // tq

    @pl.when(qi == 0)
    def _():
        def fill(j, carry):
            r0 = pl.multiple_of(j * tq, tq)
            c = cum_ref[pl.ds(r0, tq), :]
            for h in range(nh):
                extk_ref[h, pl.ds(r0, tq), :] = _bias_lanes_k(c[:, h:h + 1])
            return carry
        lax.fori_loop(0, n_tiles, fill, 0)

    q0 = pl.multiple_of(qi * tq, tq)
    cq = cum_ref[pl.ds(q0, tq), :]
    lane = lax.broadcasted_iota(jnp.int32, (tq, LANES), 1)
    q_augs = []
    for h in range(nh):
        qb = q_ref[:, LANES * (h // 2):LANES * (h // 2 + 1)]
        head_lanes = (lane < HEAD_DIM) if h % 2 == 0 else (lane >= HEAD_DIM)
        qh = jnp.where(head_lanes, qb, jnp.zeros((), BF16))
        q_augs.append(jnp.concatenate([qh, _bias_lanes_q(cq[:, h:h + 1])], axis=1))
    key_i = lax.broadcasted_iota(jnp.int32, (tq, tq), 0)
    qry_i = lax.broadcasted_iota(jnp.int32, (tq, tq), 1)
    causal = key_i <= qry_i
    vrows = HEAD_DIM + ONES_ROWS

    def scores_of(j):
        k0 = pl.multiple_of(j * tq, tq)
        out = []
        for h in range(nh):
            kj = k_ref[pl.ds(k0, tq), LANES * (h // 2):LANES * (h // 2 + 1)]
            k_aug = jnp.concatenate([kj, extk_ref[h, pl.ds(k0, tq), :]], axis=1)
            out.append(lax.dot_general(k_aug, q_augs[h], (((1,), (1,)), ((), ())), preferred_element_type=F32))
        return tuple(out)

    def absorb(j, scores, carry):
        stats = []
        for h in range(nh):
            m = carry[h][0]
            m_new = jnp.maximum(m, jnp.max(scores[h], axis=0, keepdims=True))
            stats.append((m_new, jnp.exp2(m - m_new), jnp.exp2(scores[h] - m_new).astype(BF16)))
        new = []
        for h in range(nh):
            m_new, corr, p = stats[h]
            res = jnp.dot(vt_ref[j, vrows * h:vrows * (h + 1), :], p, preferred_element_type=F32)
            acc = corr * carry[h][2] + res[0:HEAD_DIM, :]
            l = corr * carry[h][1] + res[HEAD_DIM:HEAD_DIM + 1, :]
            new.append((m_new, l, acc))
        return tuple(new)

    def step(j, carry):
        return absorb(j, scores_of(j), carry)

    init1 = (jnp.full((1, tq), NEG_BIG, F32), jnp.zeros((1, tq), F32), jnp.zeros((HEAD_DIM, tq), F32))
    carry = lax.fori_loop(0, qi, step, (init1,) * nh)
    carry = absorb(qi, tuple(jnp.where(causal, s, NEG_BIG) for s in scores_of(qi)), carry)
    out_t = jnp.concatenate([c[2] / c[1] for c in carry], axis=0)
    o_ref[...] = out_t.T


def _attention(q, k, vt, cum_g):
    bsz, seq, _ = q.shape
    tq = ATTN_BLOCK
    nh = ATTN_HEADS_PER_STEP
    assert seq % tq == 0 and ATTN_HEADS % nh == 0 and nh % 2 == 0
    w = nh * HEAD_DIM
    return pl.pallas_call(
        functools.partial(_attn_kernel, tq=tq, seq=seq, nh=nh),
        grid=(bsz, ATTN_HEADS // nh, seq // tq),
        in_specs=[pl.BlockSpec((None, tq, w), lambda b, g, i: (b, i, g)),
                  pl.BlockSpec((None, seq, w), lambda b, g, i: (b, 0, g)),
                  pl.BlockSpec((None, seq // tq, nh * (HEAD_DIM + ONES_ROWS), tq), lambda b, g, i: (b, 0, g, 0)),
                  pl.BlockSpec((None, None, seq, nh), lambda b, g, i: (b, g, 0, 0))],
        out_specs=pl.BlockSpec((None, tq, w), lambda b, g, i: (b, i, g)),
        out_shape=jax.ShapeDtypeStruct((bsz, seq, ATTN_WIDTH), F32),
        scratch_shapes=[pltpu.VMEM((nh, seq, LANES), BF16)],
        compiler_params=pltpu.CompilerParams(dimension_semantics=("parallel", "parallel", "arbitrary"),
                                             vmem_limit_bytes=VMEM_LIMIT),
        name="fox_attention",
    )(q, k, vt, cum_g)


def _s5_kernel(u_ref, bw_ref, cw_ref, ar_ref, ai_ref, d_ref, wglu_ref, bglu_ref, g_ref, o_ref,
               state_ref, bu_ref, *, tc, bsz):
    @pl.when(pl.program_id(0) == 0)
    def _():
        state_ref[...] = jnp.zeros_like(state_ref)

    rows = tc * bsz
    ns = SLAB_STATE
    u = u_ref[...].reshape(rows, SSM_WIDTH)
    ub = u.astype(BF16)
    for i in range(S5_SLABS):
        bu_ref[i] = jnp.dot(ub[:, SLAB_CH * i:SLAB_CH * (i + 1)], bw_ref[i], preferred_element_type=F32)
    for i in range(S5_SLABS):
        ar = jnp.broadcast_to(ar_ref[i], (bsz, ns))
        ai = jnp.broadcast_to(ai_ref[i], (bsz, ns))
        xr = state_ref[i, :, 0:ns]
        xi = state_ref[i, :, ns:2 * ns]
        for t in range(tc):
            r0 = t * bsz
            nr = ar * xr - ai * xi + bu_ref[i, r0:r0 + bsz, 0:ns]
            ni = ar * xi + ai * xr + bu_ref[i, r0:r0 + bsz, ns:2 * ns]
            bu_ref[i, r0:r0 + bsz, 0:ns] = nr
            bu_ref[i, r0:r0 + bsz, ns:2 * ns] = ni
            xr, xi = nr, ni
        state_ref[i, :, 0:ns] = xr
        state_ref[i, :, ns:2 * ns] = xi
    ys = [jnp.dot(bu_ref[i].astype(BF16), cw_ref[i], preferred_element_type=F32) for i in range(S5_SLABS)]
    y = jnp.concatenate(ys, axis=1) + d_ref[...] * u
    y = 0.5 * y * (1.0 + jnp.tanh(math.sqrt(2.0 / math.pi) * (y + 0.044715 * (y * y * y))))
    z = jnp.dot(y.astype(BF16), wglu_ref[...], preferred_element_type=F32) + bglu_ref[...]
    y = y * (1.0 / (1.0 + jnp.exp(-z)))
    y = y * lax.rsqrt(jnp.mean(y * y, axis=-1, keepdims=True) + RMS_EPS) * g_ref[...]
    o_ref[...] = y.astype(BF16).reshape(tc, bsz, SSM_WIDTH)


def _s5_weights(lam_re, lam_im, log_step, b_re, b_im, c_re, c_im):
    delta = jnp.exp(log_step)[:, None]
    mag = jnp.exp(lam_re * delta)
    ar = mag * jnp.cos(lam_im * delta)
    ai = mag * jnp.sin(lam_im * delta)
    den = lam_re * lam_re + lam_im * lam_im
    num_re = ar - 1.0
    coef_re = (num_re * lam_re + ai * lam_im) / den
    coef_im = (ai * lam_re - num_re * lam_im) / den
    bbar_re = coef_re[..., None] * b_re - coef_im[..., None] * b_im
    bbar_im = coef_re[..., None] * b_im + coef_im[..., None] * b_re
    gs = SSM_GROUPS // S5_SLABS
    eye = jnp.eye(gs, dtype=F32)

    def b_slab(bb):
        bb = bb.reshape(S5_SLABS, gs, SSM_STATE, SSM_GROUP)
        w = jnp.einsum('ab,sapc->sacbp', eye, bb)
        return w.reshape(S5_SLABS, gs * SSM_GROUP, gs * SSM_STATE)

    def c_slab(cc):
        cc = cc.reshape(S5_SLABS, gs, SSM_GROUP, SSM_STATE)
        w = jnp.einsum('ab,sacp->sbpac', eye, cc)
        return w.reshape(S5_SLABS, gs * SSM_STATE, gs * SSM_GROUP)

    bw = jnp.concatenate([b_slab(bbar_re), b_slab(bbar_im)], axis=2).astype(BF16)
    cw = jnp.concatenate([c_slab(c_re), -c_slab(c_im)], axis=1).astype(BF16)
    ar_s = ar.reshape(S5_SLABS, 1, SLAB_STATE)
    ai_s = ai.reshape(S5_SLABS, 1, SLAB_STATE)
    return bw, cw, ar_s, ai_s


def _s5(u_tb, bw, cw, ar_s, ai_s, d_row, w_glu, b_glu_row, g_row):
    seq, bsz, _ = u_tb.shape
    tc = S5_TIME_BLOCK
    assert seq % tc == 0 and bsz % 16 == 0
    full = lambda shape: pl.BlockSpec(shape, lambda t: (0,) * len(shape))
    return pl.pallas_call(
        functools.partial(_s5_kernel, tc=tc, bsz=bsz),
        grid=(seq // tc,),
        in_specs=[pl.BlockSpec((tc, bsz, SSM_WIDTH), lambda t: (t, 0, 0)),
                  full(bw.shape), full(cw.shape), full(ar_s.shape), full(ai_s.shape),
                  full((1, SSM_WIDTH)), full((SSM_WIDTH, SSM_WIDTH)), full((1, SSM_WIDTH)),
                  full((1, SSM_WIDTH))],
        out_specs=pl.BlockSpec((tc, bsz, SSM_WIDTH), lambda t: (t, 0, 0)),
        out_shape=jax.ShapeDtypeStruct((seq, bsz, SSM_WIDTH), BF16),
        scratch_shapes=[pltpu.VMEM((S5_SLABS, bsz, 2 * SLAB_STATE), F32),
                        pltpu.VMEM((S5_SLABS, tc * bsz, 2 * SLAB_STATE), F32)],
        compiler_params=pltpu.CompilerParams(dimension_semantics=("arbitrary",),
                                             vmem_limit_bytes=VMEM_LIMIT),
        name="s5_branch",
    )(u_tb, bw, cw, ar_s, ai_s, d_row, w_glu, b_glu_row, g_row)


def _layer_norm_rows(v, g, b):
    mu = jnp.mean(v, axis=-1, keepdims=True)
    c = v - mu
    var = jnp.mean(c * c, axis=-1, keepdims=True)
    return c * lax.rsqrt(var + LN_EPS) * g + b


def _outproj_kernel(attn_ref, ssm_ref, x_ref, wo_ref, ag_ref, g1_ref, b1_ref, wrt_ref, brt_ref, triu_ref,
                    h_ref, eid_ref, gate_ref, rank_ref, cnt_ref):
    a = attn_ref[...]
    a = a * lax.rsqrt(jnp.mean(a * a, axis=-1, keepdims=True) + RMS_EPS) * ag_ref[...]
    mix = (jnp.dot(a.astype(BF16), wo_ref[0:ATTN_WIDTH, :], preferred_element_type=F32)
           + jnp.dot(ssm_ref[...], wo_ref[ATTN_WIDTH:, :], preferred_element_type=F32))
    h = _layer_norm_rows(DEEPNORM_ALPHA * x_ref[...] + mix, g1_ref[...], b1_ref[...])
    h_ref[...] = h

    logits = lax.dot_general(wrt_ref[...], h.astype(BF16), (((1,), (1,)), ((), ())),
                             preferred_element_type=F32)[:ROUTER_ROWS] + brt_ref[...]
    tm = logits.shape[1]
    row = lax.broadcasted_iota(jnp.int32, (ROUTER_ROWS, tm), 0).astype(F32)
    no_row = float(LANES)
    is_g = row < N_EXPERT_GROUPS
    gl = jnp.where(is_g, logits, NEG_BIG)
    gmax = jnp.max(gl, axis=0, keepdims=True)
    g_idx = jnp.min(jnp.where(is_g & (gl == gmax), row, no_row), axis=0, keepdims=True)
    g_p = 1.0 / jnp.sum(jnp.where(is_g, jnp.exp(gl - gmax), 0.0), axis=0, keepdims=True)
    first = N_EXPERT_GROUPS + g_idx * EXPERTS_PER_GROUP
    in_grp = (row >= first) & (row < first + EXPERTS_PER_GROUP)
    el = jnp.where(in_grp, logits, NEG_BIG)
    m1 = jnp.max(el, axis=0, keepdims=True)
    i1 = jnp.min(jnp.where(in_grp & (el == m1), row, no_row), axis=0, keepdims=True)
    rest = in_grp & (row != i1)
    el2 = jnp.where(rest, logits, NEG_BIG)
    m2 = jnp.max(el2, axis=0, keepdims=True)
    i2 = jnp.min(jnp.where(rest & (el2 == m2), row, no_row), axis=0, keepdims=True)
    r = jnp.exp(m2 - m1)
    p1 = 1.0 / (1.0 + r)
    p2 = r / (1.0 + r)
    k2 = lax.broadcasted_iota(jnp.int32, (2, tm), 0)
    eid_ref[...] = (jnp.where(k2 == 0, i1, i2) - N_EXPERT_GROUPS).astype(jnp.int32)
    gate_ref[...] = g_p * jnp.where(k2 == 0, p1, p2)

    @pl.when(pl.program_id(0) == 0)
    def _():
        cnt_ref[...] = jnp.zeros_like(cnt_ref)
    hit1 = row == i1
    hit2 = row == i2
    onehot = jnp.where(hit1 | hit2, 1.0, 0.0)
    cnt = cnt_ref[...]
    before = jnp.dot(onehot.astype(BF16), triu_ref[...], preferred_element_type=F32) + cnt
    r1 = jnp.sum(jnp.where(hit1, before, 0.0), axis=0, keepdims=True)
    r2 = jnp.sum(jnp.where(hit2, before, 0.0), axis=0, keepdims=True)
    rank_ref[...] = jnp.where(k2 == 0, r1, r2).astype(jnp.int32)
    cnt_ref[...] = cnt + jnp.sum(onehot, axis=1, keepdims=True)


def _outproj(attn2, ssm2, x2, wo, ag_row, g1_row, b1_row, wrt, brt):
    n_tok = x2.shape[0]
    tm = ROW_BLOCK
    triu = jnp.triu(jnp.ones((tm, tm), BF16), 1)
    row_spec = lambda w: pl.BlockSpec((tm, w), lambda i: (i, 0))
    col_spec = lambda r: pl.BlockSpec((r, tm), lambda i: (0, i))
    full = lambda shape: pl.BlockSpec(shape, lambda i: (0,) * len(shape))
    return pl.pallas_call(
        _outproj_kernel,
        grid=(n_tok // tm,),
        in_specs=[row_spec(ATTN_WIDTH), row_spec(SSM_WIDTH), row_spec(D_MODEL),
                  full((D_MODEL, D_MODEL)), full((1, ATTN_WIDTH)), full((1, D_MODEL)), full((1, D_MODEL)),
                  full((LANES, D_MODEL)), full((ROUTER_ROWS, 1)), full((tm, tm))],
        out_specs=[row_spec(D_MODEL), col_spec(2), col_spec(2), col_spec(2), full((ROUTER_ROWS, 1))],
        out_shape=[jax.ShapeDtypeStruct((n_tok, D_MODEL), F32),
                   jax.ShapeDtypeStruct((2, n_tok), jnp.int32),
                   jax.ShapeDtypeStruct((2, n_tok), F32),
                   jax.ShapeDtypeStruct((2, n_tok), jnp.int32),
                   jax.ShapeDtypeStruct((ROUTER_ROWS, 1), F32)],
        compiler_params=pltpu.CompilerParams(dimension_semantics=("arbitrary",),
                                             vmem_limit_bytes=VMEM_LIMIT),
        name="outproj_ln_router",
    )(attn2, ssm2, x2, wo, ag_row, g1_row, b1_row, wrt, brt, triu)


def _dispatch_plan(eid, rank, cnt_col, n_tok):
    counts = cnt_col[N_EXPERT_GROUPS:N_EXPERT_GROUPS + N_EXPERTS, 0].astype(jnp.int32)
    padded = ((counts + MOE_BLOCK - 1) // MOE_BLOCK) * MOE_BLOCK
    pends = jnp.cumsum(padded)
    pstarts = pends - padded
    onehot = eid[:, :, None] == jnp.arange(N_EXPERTS, dtype=jnp.int32)[None, None, :]
    slot = rank + jnp.sum(jnp.where(onehot, pstarts[None, None, :], 0), axis=-1)
    n_pad = ((2 * n_tok + N_EXPERTS * (MOE_BLOCK - 1) + MOE_BLOCK - 1) // MOE_BLOCK) * MOE_BLOCK
    n_blocks = n_pad // MOE_BLOCK
    blk0 = jnp.arange(n_blocks, dtype=jnp.int32) * MOE_BLOCK
    block_e = jnp.minimum(jnp.sum((pends[None, :] <= blk0[:, None]).astype(jnp.int32), axis=1), N_EXPERTS - 1)
    n_used = (pends[-1] // MOE_BLOCK).astype(jnp.int32).reshape(1)
    tail = jnp.where(padded > 0, pends - MOE_BLOCK, -1).astype(jnp.int32)
    return slot, block_e, n_used, tail, n_blocks


def _row_dma(src, src_row, dst, dst_row, sem):
    return pltpu.make_async_copy(src.at[pl.ds(src_row, 1)], dst.at[pl.ds(dst_row, 1)], sem)


def _pack_rows(v):
    half = v.shape[1] // 2
    bits = lambda t: lax.bitcast_convert_type(t.astype(BF16).astype(F32), jnp.int32)
    return lax.shift_right_logical(bits(v[:, :half]), 16) | bits(v[:, half:])


def _unpack_rows(w):
    lo = lax.bitcast_convert_type(lax.shift_left(w, 16), F32)
    hi = lax.bitcast_convert_type(w & jnp.int32(-65536), F32)
    return lo, hi


def _dispatch_kernel(tail_ref, nu_ref, s0_ref, s1_ref, h_ref, xb_hbm, zbuf, pk_ref, sem_z, sem, *, n_blocks):
    tm = h_ref.shape[0]
    pk_ref[...] = _pack_rows(h_ref[...])

    @pl.when(pl.program_id(0) == 0)
    def _():
        zbuf[...] = jnp.zeros_like(zbuf)

        def zero_copy(row0):
            return pltpu.make_async_copy(zbuf, xb_hbm.at[pl.ds(pl.multiple_of(row0, MOE_BLOCK), MOE_BLOCK)], sem_z)

        def each_zero_block(fn):
            for e in range(N_EXPERTS):
                @pl.when(tail_ref[e] >= 0)
                def _():
                    fn(zero_copy(tail_ref[e]))

            def unused(b, carry):
                fn(zero_copy(b * MOE_BLOCK))
                return carry
            lax.fori_loop(nu_ref[0], n_blocks, unused, 0)

        each_zero_block(lambda c: c.start())
        each_zero_block(lambda c: c.wait())

    def scatter_row(r, carry):
        _row_dma(pk_ref, r, xb_hbm, s0_ref[0, r], sem).start()
        _row_dma(pk_ref, r, xb_hbm, s1_ref[0, r], sem).start()
        return carry
    lax.fori_loop(0, tm, scatter_row, 0, unroll=8)
    for _ in range(2):
        pltpu.make_async_copy(pk_ref, xb_hbm.at[pl.ds(0, tm)], sem).wait()


def _dispatch(h, slot, tail, n_used, n_blocks):
    n_tok = h.shape[0]
    tm = TOKEN_BLOCK
    nb = n_tok // tm
    s0 = slot[0].reshape(nb, 1, tm)
    s1 = slot[1].reshape(nb, 1, tm)
    smem_blk = pl.BlockSpec((None, 1, tm), lambda i, tl, nu: (i, 0, 0), memory_space=pltpu.SMEM)
    grid_spec = pltpu.PrefetchScalarGridSpec(
        num_scalar_prefetch=2,
        grid=(nb,),
        in_specs=[smem_blk, smem_blk, pl.BlockSpec((tm, D_MODEL), lambda i, tl, nu: (i, 0))],
        out_specs=pl.BlockSpec(memory_space=pl.ANY),
        scratch_shapes=[pltpu.VMEM((MOE_BLOCK, D_MODEL // 2), jnp.int32),
                        pltpu.VMEM((tm, D_MODEL // 2), jnp.int32),
                        pltpu.SemaphoreType.DMA, pltpu.SemaphoreType.DMA],
    )
    return pl.pallas_call(
        functools.partial(_dispatch_kernel, n_blocks=n_blocks),
        grid_spec=grid_spec,
        out_shape=jax.ShapeDtypeStruct((n_blocks * MOE_BLOCK, D_MODEL // 2), jnp.int32),
        compiler_params=pltpu.CompilerParams(dimension_semantics=("arbitrary",),
                                             vmem_limit_bytes=VMEM_LIMIT),
        name="moe_dispatch",
    )(tail, n_used, s0, s1, h)


def _expert_kernel(be_ref, nu_ref, x_ref, wg_ref, wu_ref, wd_ref, y_ref, wgu_t, wd_t, yt_ref):
    i = pl.program_id(0)
    used = i < nu_ref[0]

    @pl.when(used & ((i == 0) | (be_ref[i] != be_ref[jnp.maximum(i - 1, 0)])))
    def _():
        wgu_t[0:D_EXPERT, :] = wg_ref[...].T.astype(BF16)
        wgu_t[D_EXPERT:, :] = wu_ref[...].T.astype(BF16)
        wd_t[...] = wd_ref[...].T.astype(BF16)

    @pl.when(used)
    def _():
        sub = range(MOE_BLOCK // MOE_SUB)
        rows = lambda c: slice(c * MOE_SUB, (c + 1) * MOE_SUB)
        gus = []
        for c in sub:
            lo, hi = _unpack_rows(x_ref[rows(c), :])
            xb = jnp.concatenate([lo.astype(BF16), hi.astype(BF16)], axis=1)
            gus.append(lax.dot_general(wgu_t[...], xb, (((1,), (1,)), ((), ())),
                                       preferred_element_type=F32))
        hids = []
        for c in sub:
            gt = gus[c][0:D_EXPERT]
            hids.append((gt * (1.0 / (1.0 + jnp.exp(-gt))) * gus[c][D_EXPERT:]).astype(BF16))
        for c in sub:
            yt_ref[c] = jnp.dot(wd_t[...], hids[c], preferred_element_type=F32)
        for c in sub:
            y_ref[rows(c), :] = _pack_rows(yt_ref[c].T)

    @pl.when(jnp.logical_not(used))
    def _():
        y_ref[...] = jnp.zeros_like(y_ref)


def _experts(xb, block_e, n_used, wg, wu, wd):
    n_blocks = xb.shape[0] // MOE_BLOCK
    last = lambda i, nu: jnp.minimum(i, nu[0] - 1)
    grid_spec = pltpu.PrefetchScalarGridSpec(
        num_scalar_prefetch=2,
        grid=(n_blocks,),
        in_specs=[pl.BlockSpec((MOE_BLOCK, D_MODEL // 2), lambda i, be, nu: (last(i, nu), 0)),
                  pl.BlockSpec((None, D_MODEL, D_EXPERT), lambda i, be, nu: (be[last(i, nu)], 0, 0)),
                  pl.BlockSpec((None, D_MODEL, D_EXPERT), lambda i, be, nu: (be[last(i, nu)], 0, 0)),
                  pl.BlockSpec((None, D_EXPERT, D_MODEL), lambda i, be, nu: (be[last(i, nu)], 0, 0))],
        out_specs=pl.BlockSpec((MOE_BLOCK, D_MODEL // 2), lambda i, be, nu: (i, 0)),
        scratch_shapes=[pltpu.VMEM((2 * D_EXPERT, D_MODEL), BF16), pltpu.VMEM((D_MODEL, D_EXPERT), BF16),
                        pltpu.VMEM((MOE_BLOCK // MOE_SUB, D_MODEL, MOE_SUB), F32)],
    )
    return pl.pallas_call(
        _expert_kernel,
        grid_spec=grid_spec,
        out_shape=jax.ShapeDtypeStruct(xb.shape, jnp.int32),
        compiler_params=pltpu.CompilerParams(dimension_semantics=("arbitrary",),
                                             vmem_limit_bytes=VMEM_LIMIT),
        name="moe_experts",
    )(block_e, n_used, xb, wg, wu, wd)


def _combine_kernel(s0_ref, s1_ref, s0n_ref, s1n_ref, h_ref, gate_ref, g2_ref, b2_ref, yb_hbm, o_ref, buf, sem):
    i = pl.program_id(0)
    n = pl.num_programs(0)
    tm = h_ref.shape[0]
    cur = i % 2

    def gather(a_ref, b_ref, which):
        def gather_row(r, carry):
            _row_dma(yb_hbm, a_ref[0, r], buf.at[which, 0], r, sem.at[which]).start()
            _row_dma(yb_hbm, b_ref[0, r], buf.at[which, 1], r, sem.at[which]).start()
            return carry
        lax.fori_loop(0, tm, gather_row, 0, unroll=8)

    @pl.when(i == 0)
    def _():
        gather(s0_ref, s1_ref, 0)

    @pl.when(i + 1 < n)
    def _():
        gather(s0n_ref, s1n_ref, 1 - cur)

    for k in range(2):
        pltpu.make_async_copy(yb_hbm.at[pl.ds(0, tm)], buf.at[cur, k], sem.at[cur]).wait()
    gate = gate_ref[...]
    lo0, hi0 = _unpack_rows(buf[cur, 0])
    lo1, hi1 = _unpack_rows(buf[cur, 1])
    moe = jnp.concatenate([gate[:, 0:1] * lo0 + gate[:, 1:2] * lo1,
                           gate[:, 0:1] * hi0 + gate[:, 1:2] * hi1], axis=1)
    o_ref[...] = _layer_norm_rows(DEEPNORM_ALPHA * h_ref[...] + moe, g2_ref[...], b2_ref[...])


def _combine(h, yb, slot, gate, g2_row, b2_row):
    n_tok = h.shape[0]
    tm = TOKEN_BLOCK
    nb = n_tok // tm
    s0 = slot[0].reshape(nb, 1, tm)
    s1 = slot[1].reshape(nb, 1, tm)
    row_spec = lambda w: pl.BlockSpec((tm, w), lambda i: (i, 0))
    full = lambda shape: pl.BlockSpec(shape, lambda i: (0,) * len(shape))
    smem_cur = pl.BlockSpec((None, 1, tm), lambda i: (i, 0, 0), memory_space=pltpu.SMEM)
    smem_next = pl.BlockSpec((None, 1, tm), lambda i: (jnp.minimum(i + 1, nb - 1), 0, 0), memory_space=pltpu.SMEM)
    return pl.pallas_call(
        _combine_kernel,
        grid=(nb,),
        in_specs=[smem_cur, smem_cur, smem_next, smem_next,
                  row_spec(D_MODEL), row_spec(2), full((1, D_MODEL)), full((1, D_MODEL)),
                  pl.BlockSpec(memory_space=pl.ANY)],
        out_specs=row_spec(D_MODEL),
        out_shape=jax.ShapeDtypeStruct((n_tok, D_MODEL), F32),
        scratch_shapes=[pltpu.VMEM((2, 2, tm, D_MODEL // 2), jnp.int32), pltpu.SemaphoreType.DMA((2,))],
        compiler_params=pltpu.CompilerParams(dimension_semantics=("arbitrary",),
                                             vmem_limit_bytes=VMEM_LIMIT),
        name="combine_ln",
    )(s0, s1, s0, s1, h, gate, g2_row, b2_row, yb)


def kernel(x, w_in, b_fgate, s5_lambda_re, s5_lambda_im, s5_log_step, s5_b_re, s5_b_im, s5_c_re, s5_c_im, s5_d, w_glu, b_glu, attn_norm_g, ssm_norm_g, w_out, ln1_g, ln1_b, w_router_group, b_router_group, w_router_expert, b_router_expert, w_gate, w_up, w_down, ln2_g, ln2_b):
    bsz, seq, d = x.shape
    n_tok = bsz * seq
    a = ATTN_WIDTH
    for l in range(DEPTH):
        x2 = x.reshape(n_tok, d)
        w = w_in[l]
        f0 = 3 * a
        w_cat = jnp.concatenate(
            [w[:, :a] * (HEAD_DIM ** -0.5 * LOG2E), w[:, a:2 * a], w[:, f0 + ATTN_HEADS:],
             jnp.pad(w[:, f0:f0 + ATTN_HEADS], ((0, 0), (0, LANES - ATTN_HEADS)))], axis=1).astype(BF16)
        wvt = w[:, 2 * a:f0].T.astype(BF16)
        bf_row = jnp.pad(b_fgate[l], (0, LANES - ATTN_HEADS)).reshape(1, LANES)
        q, k, vt, u, cum = _inproj(x2, w_cat, wvt, bf_row, seq)

        cum_g = cum.reshape(bsz, seq, ATTN_HEADS // ATTN_HEADS_PER_STEP, ATTN_HEADS_PER_STEP).transpose(0, 2, 1, 3)
        attn = _attention(q.reshape(bsz, seq, a), k.reshape(bsz, seq, a), vt, cum_g)

        bw, cw, ar_s, ai_s = _s5_weights(s5_lambda_re[l], s5_lambda_im[l], s5_log_step[l],
                                         s5_b_re[l], s5_b_im[l], s5_c_re[l], s5_c_im[l])
        u_tb = u.reshape(bsz, seq, SSM_WIDTH).transpose(1, 0, 2)
        ssm_tb = _s5(u_tb, bw, cw, ar_s, ai_s, s5_d[l].reshape(1, SSM_WIDTH), w_glu[l].astype(BF16),
                     b_glu[l].reshape(1, SSM_WIDTH), ssm_norm_g[l].reshape(1, SSM_WIDTH))
        ssm2 = ssm_tb.transpose(1, 0, 2).reshape(n_tok, SSM_WIDTH)

        n_route = N_EXPERT_GROUPS + N_EXPERTS
        wrt = jnp.pad(jnp.concatenate([w_router_group[l], w_router_expert[l]], axis=1).T,
                      ((0, LANES - n_route), (0, 0))).astype(BF16)
        brt = jnp.pad(jnp.concatenate([b_router_group[l], b_router_expert[l]]),
                      (0, ROUTER_ROWS - n_route)).reshape(ROUTER_ROWS, 1)
        h, eid, gate, rank, cnt_col = _outproj(
            attn.reshape(n_tok, a), ssm2, x2, w_out[l].astype(BF16), attn_norm_g[l].reshape(1, a),
            ln1_g[l].reshape(1, d), ln1_b[l].reshape(1, d), wrt, brt)

        slot, block_e, n_used, tail, n_blocks = _dispatch_plan(eid, rank, cnt_col, n_tok)
        xb = _dispatch(h, slot, tail, n_used, n_blocks)
        yb = _experts(xb, block_e, n_used, w_gate[l], w_up[l], w_down[l])
        x = _combine(h, yb, slot, gate.T, ln2_g[l].reshape(1, d), ln2_b[l].reshape(1, d)).reshape(bsz, seq, d)
    return x
```

```python
import functools
import math

import jax
import jax.numpy as jnp
from jax import lax
from jax.experimental import pallas as pl
from jax.experimental.pallas import tpu as pltpu

F32 = jnp.float32
BF16 = jnp.bfloat16

D_MODEL = 1024
HEAD_DIM = 64
ATTN_WIDTH = 512
ATTN_HEADS = 8
SSM_WIDTH = 512
SSM_GROUP = 16
SSM_GROUPS = 32
SSM_STATE = 64
N_EXPERT_GROUPS = 4
EXPERTS_PER_GROUP = 8
N_EXPERTS = 32
D_EXPERT = 512
MOE_BLOCK = 512
MOE_SUB = 256
TOKEN_BLOCK = 256
RUN_BITS = TOKEN_BLOCK.bit_length()
DEPTH = 1
DEEPNORM_ALPHA = (2.0 * DEPTH) ** 0.25
LN_EPS = 1e-5
RMS_EPS = 1e-6

LANES = 128
NEG_BIG = -1e30
LOG2E = math.log2(math.e)
ONES_ROWS = 16
VT_ROWS = ATTN_HEADS * (HEAD_DIM + ONES_ROWS)
VMEM_LIMIT = 48 * 1024 * 1024

ROW_BLOCK = 512
ATTN_BLOCK = 256
ATTN_HEADS_PER_STEP = 8
S5_TIME_BLOCK = 64
S5_SLABS = 4
ROUTER_ROWS = 40
SLAB_CH = SSM_WIDTH // S5_SLABS
SLAB_STATE = SSM_GROUPS // S5_SLABS * SSM_STATE


def _split3(c):
    hi = c.astype(BF16)
    r1 = c - hi.astype(F32)
    mid = r1.astype(BF16)
    lo = (r1 - mid.astype(F32)).astype(BF16)
    return hi, mid, lo


def _inproj_kernel(x_ref, w_ref, wvt_ref, bf_ref, q_ref, k_ref, vt_ref, u_ref, cum_ref, carry_ref, *,
                   blocks_per_seq):
    i = pl.program_id(0)

    @pl.when(i % blocks_per_seq == 0)
    def _():
        carry_ref[...] = jnp.zeros_like(carry_ref)

    tm = x_ref.shape[0]
    xb = x_ref[...].astype(BF16)
    z = jnp.dot(xb, w_ref[...], preferred_element_type=F32)
    a = ATTN_WIDTH
    q_ref[...] = z[:, :a].astype(BF16)
    k_ref[...] = z[:, a:2 * a].astype(BF16)
    u_ref[...] = z[:, 2 * a:2 * a + SSM_WIDTH]
    vt = lax.dot_general(wvt_ref[...], xb, (((1,), (1,)), ((), ())), preferred_element_type=F32).astype(BF16)
    ones_rows = jnp.ones((ONES_ROWS, tm), BF16)
    vt = jnp.concatenate([piece for h in range(ATTN_HEADS)
                          for piece in (vt[HEAD_DIM * h:HEAD_DIM * (h + 1)], ones_rows)], axis=0)
    for jj in range(tm // ATTN_BLOCK):
        vt_ref[jj] = vt[:, jj * ATTN_BLOCK:(jj + 1) * ATTN_BLOCK]
    f = z[:, 2 * a + SSM_WIDTH:] + bf_ref[...]
    logf = jnp.minimum(f, 0.0) - jnp.log1p(jnp.exp(-jnp.abs(f)))
    row = lax.broadcasted_iota(jnp.int32, (tm, tm), 0)
    col = lax.broadcasted_iota(jnp.int32, (tm, tm), 1)
    tri = (row >= col).astype(BF16)
    hi, mid, lo = _split3(logf)
    cum = (jnp.dot(tri, hi, preferred_element_type=F32)
           + jnp.dot(tri, mid, preferred_element_type=F32)
           + jnp.dot(tri, lo, preferred_element_type=F32)) + carry_ref[...]
    carry_ref[...] = cum[tm - 1:tm, :]
    cum_ref[...] = cum[:, :ATTN_HEADS] * LOG2E


def _inproj(x2, w_cat, wvt, bf_row, seq):
    n_tok = x2.shape[0]
    tm = ROW_BLOCK
    assert seq % tm == 0 and n_tok % tm == 0 and tm % ATTN_BLOCK == 0
    bsz = n_tok // seq
    bps = seq // tm
    ncol = w_cat.shape[1]
    vblk = tm // ATTN_BLOCK
    row_spec = lambda w: pl.BlockSpec((tm, w), lambda i: (i, 0))
    return pl.pallas_call(
        functools.partial(_inproj_kernel, blocks_per_seq=bps),
        grid=(n_tok // tm,),
        in_specs=[row_spec(D_MODEL),
                  pl.BlockSpec((D_MODEL, ncol), lambda i: (0, 0)),
                  pl.BlockSpec((ATTN_WIDTH, D_MODEL), lambda i: (0, 0)),
                  pl.BlockSpec((1, LANES), lambda i: (0, 0))],
        out_specs=[row_spec(ATTN_WIDTH), row_spec(ATTN_WIDTH),
                   pl.BlockSpec((None, vblk, VT_ROWS, ATTN_BLOCK), lambda i: (i // bps, i % bps, 0, 0)),
                   row_spec(SSM_WIDTH), row_spec(ATTN_HEADS)],
        out_shape=[jax.ShapeDtypeStruct((n_tok, ATTN_WIDTH), BF16)] * 2
                  + [jax.ShapeDtypeStruct((bsz, seq // ATTN_BLOCK, VT_ROWS, ATTN_BLOCK), BF16),
                     jax.ShapeDtypeStruct((n_tok, SSM_WIDTH), F32),
                     jax.ShapeDtypeStruct((n_tok, ATTN_HEADS), F32)],
        scratch_shapes=[pltpu.VMEM((1, LANES), F32)],
        compiler_params=pltpu.CompilerParams(dimension_semantics=("arbitrary",),
                                             vmem_limit_bytes=VMEM_LIMIT),
        name="inproj",
    )(x2, w_cat, wvt, bf_row)


def _bias_lanes_q(c):
    hi, mid, lo = (t.astype(F32) for t in _split3(c))
    lane = lax.broadcasted_iota(jnp.int32, (c.shape[0], LANES), 1)
    sel = jnp.where(lane == 0, hi, jnp.where(lane == 1, mid, jnp.where(lane == 2, lo,
                    jnp.where(lane < 6, 1.0, 0.0))))
    return sel.astype(BF16)


def _bias_lanes_k(c):
    hi, mid, lo = (t.astype(F32) for t in _split3(c))
    lane = lax.broadcasted_iota(jnp.int32, (c.shape[0], LANES), 1)
    sel = jnp.where(lane < 3, 1.0, jnp.where(lane == 3, -hi, jnp.where(lane == 4, -mid,
                    jnp.where(lane == 5, -lo, 0.0))))
    return sel.astype(BF16)


def _attn_kernel(q_ref, k_ref, vt_ref, cum_ref, o_ref, extk_ref, *, tq, seq, nh):
    qi = pl.program_id(2)
    n_tiles = seq // tq

    @pl.when(qi == 0)
    def _():
        def fill(j, carry):
            r0 = pl.multiple_of(j * tq, tq)
            c = cum_ref[pl.ds(r0, tq), :]
            for h in range(nh):
                extk_ref[h, pl.ds(r0, tq), :] = _bias_lanes_k(c[:, h:h + 1])
            return carry
        lax.fori_loop(0, n_tiles, fill, 0)

    q0 = pl.multiple_of(qi * tq, tq)
    cq = cum_ref[pl.ds(q0, tq), :]
    lane = lax.broadcasted_iota(jnp.int32, (tq, LANES), 1)
    q_augs = []
    for h in range(nh):
        qb = q_ref[:, LANES * (h // 2):LANES * (h // 2 + 1)]
        head_lanes = (lane < HEAD_DIM) if h % 2 == 0 else (lane >= HEAD_DIM)
        qh = jnp.where(head_lanes, qb, jnp.zeros((), BF16))
        q_augs.append(jnp.concatenate([qh, _bias_lanes_q(cq[:, h:h + 1])], axis=1))
    key_i = lax.broadcasted_iota(jnp.int32, (tq, tq), 0)
    qry_i = lax.broadcasted_iota(jnp.int32, (tq, tq), 1)
    causal = key_i <= qry_i
    vrows = HEAD_DIM + ONES_ROWS

    def scores_of(j):
        k0 = pl.multiple_of(j * tq, tq)
        out = []
        for h in range(nh):
            kj = k_ref[pl.ds(k0, tq), LANES * (h // 2):LANES * (h // 2 + 1)]
            k_aug = jnp.concatenate([kj, extk_ref[h, pl.ds(k0, tq), :]], axis=1)
            out.append(lax.dot_general(k_aug, q_augs[h], (((1,), (1,)), ((), ())), preferred_element_type=F32))
        return tuple(out)

    def absorb(j, scores, carry):
        stats = []
        for h in range(nh):
            m = carry[h][0]
            m_new = jnp.maximum(m, jnp.max(scores[h], axis=0, keepdims=True))
            stats.append((m_new, jnp.exp2(m - m_new), jnp.exp2(scores[h] - m_new).astype(BF16)))
        new = []
        for h in range(nh):
            m_new, corr, p = stats[h]
            res = jnp.dot(vt_ref[j, vrows * h:vrows * (h + 1), :], p, preferred_element_type=F32)
            acc = corr * carry[h][2] + res[0:HEAD_DIM, :]
            l = corr * carry[h][1] + res[HEAD_DIM:HEAD_DIM + 1, :]
            new.append((m_new, l, acc))
        return tuple(new)

    def step(j, carry):
        return absorb(j, scores_of(j), carry)

    init1 = (jnp.full((1, tq), NEG_BIG, F32), jnp.zeros((1, tq), F32), jnp.zeros((HEAD_DIM, tq), F32))
    carry = lax.fori_loop(0, qi, step, (init1,) * nh)
    carry = absorb(qi, tuple(jnp.where(causal, s, NEG_BIG) for s in scores_of(qi)), carry)
    out_t = jnp.concatenate([c[2] / c[1] for c in carry], axis=0)
    o_ref[...] = out_t.T


def _attention(q, k, vt, cum_g):
    bsz, seq, _ = q.shape
    tq = ATTN_BLOCK
    nh = ATTN_HEADS_PER_STEP
    assert seq % tq == 0 and ATTN_HEADS % nh == 0 and nh % 2 == 0
    w = nh * HEAD_DIM
    return pl.pallas_call(
        functools.partial(_attn_kernel, tq=tq, seq=seq, nh=nh),
        grid=(bsz, ATTN_HEADS // nh, seq // tq),
        in_specs=[pl.BlockSpec((None, tq, w), lambda b, g, i: (b, i, g)),
                  pl.BlockSpec((None, seq, w), lambda b, g, i: (b, 0, g)),
                  pl.BlockSpec((None, seq // tq, nh * (HEAD_DIM + ONES_ROWS), tq), lambda b, g, i: (b, 0, g, 0)),
                  pl.BlockSpec((None, None, seq, nh), lambda b, g, i: (b, g, 0, 0))],
        out_specs=pl.BlockSpec((None, tq, w), lambda b, g, i: (b, i, g)),
        out_shape=jax.ShapeDtypeStruct((bsz, seq, ATTN_WIDTH), F32),
        scratch_shapes=[pltpu.VMEM((nh, seq, LANES), BF16)],
        compiler_params=pltpu.CompilerParams(dimension_semantics=("parallel", "parallel", "arbitrary"),
                                             vmem_limit_bytes=VMEM_LIMIT),
        name="fox_attention",
    )(q, k, vt, cum_g)


def _s5_kernel(u_ref, bw_ref, cw_ref, ar_ref, ai_ref, d_ref, wglu_ref, bglu_ref, g_ref, o_ref,
               state_ref, bu_ref, *, tc, bsz):
    @pl.when(pl.program_id(0) == 0)
    def _():
        state_ref[...] = jnp.zeros_like(state_ref)

    rows = tc * bsz
    ns = SLAB_STATE
    u = u_ref[...].reshape(rows, SSM_WIDTH)
    ub = u.astype(BF16)
    for i in range(S5_SLABS):
        bu_ref[i] = jnp.dot(ub[:, SLAB_CH * i:SLAB_CH * (i + 1)], bw_ref[i], preferred_element_type=F32)
    for i in range(S5_SLABS):
        ar = jnp.broadcast_to(ar_ref[i], (bsz, ns))
        ai = jnp.broadcast_to(ai_ref[i], (bsz, ns))
        xr = state_ref[i, :, 0:ns]
        xi = state_ref[i, :, ns:2 * ns]
        for t in range(tc):
            r0 = t * bsz
            nr = ar * xr - ai * xi + bu_ref[i, r0:r0 + bsz, 0:ns]
            ni = ar * xi + ai * xr + bu_ref[i, r0:r0 + bsz, ns:2 * ns]
            bu_ref[i, r0:r0 + bsz, 0:ns] = nr
            bu_ref[i, r0:r0 + bsz, ns:2 * ns] = ni
            xr, xi = nr, ni
        state_ref[i, :, 0:ns] = xr
        state_ref[i, :, ns:2 * ns] = xi
    ys = [jnp.dot(bu_ref[i].astype(BF16), cw_ref[i], preferred_element_type=F32) for i in range(S5_SLABS)]
    y = jnp.concatenate(ys, axis=1) + d_ref[...] * u
    y = 0.5 * y * (1.0 + jnp.tanh(math.sqrt(2.0 / math.pi) * (y + 0.044715 * (y * y * y))))
    z = jnp.dot(y.astype(BF16), wglu_ref[...], preferred_element_type=F32) + bglu_ref[...]
    y = y * (1.0 / (1.0 + jnp.exp(-z)))
    y = y * lax.rsqrt(jnp.mean(y * y, axis=-1, keepdims=True) + RMS_EPS) * g_ref[...]
    o_ref[...] = y.astype(BF16).reshape(tc, bsz, SSM_WIDTH)


def _s5_weights(lam_re, lam_im, log_step, b_re, b_im, c_re, c_im):
    delta = jnp.exp(log_step)[:, None]
    mag = jnp.exp(lam_re * delta)
    ar = mag * jnp.cos(lam_im * delta)
    ai = mag * jnp.sin(lam_im * delta)
    den = lam_re * lam_re + lam_im * lam_im
    num_re = ar - 1.0
    coef_re = (num_re * lam_re + ai * lam_im) / den
    coef_im = (ai * lam_re - num_re * lam_im) / den
    bbar_re = coef_re[..., None] * b_re - coef_im[..., None] * b_im
    bbar_im = coef_re[..., None] * b_im + coef_im[..., None] * b_re
    gs = SSM_GROUPS // S5_SLABS
    eye = jnp.eye(gs, dtype=F32)

    def b_slab(bb):
        bb = bb.reshape(S5_SLABS, gs, SSM_STATE, SSM_GROUP)
        w = jnp.einsum('ab,sapc->sacbp', eye, bb)
        return w.reshape(S5_SLABS, gs * SSM_GROUP, gs * SSM_STATE)

    def c_slab(cc):
        cc = cc.reshape(S5_SLABS, gs, SSM_GROUP, SSM_STATE)
        w = jnp.einsum('ab,sacp->sbpac', eye, cc)
        return w.reshape(S5_SLABS, gs * SSM_STATE, gs * SSM_GROUP)

    bw = jnp.concatenate([b_slab(bbar_re), b_slab(bbar_im)], axis=2).astype(BF16)
    cw = jnp.concatenate([c_slab(c_re), -c_slab(c_im)], axis=1).astype(BF16)
    ar_s = ar.reshape(S5_SLABS, 1, SLAB_STATE)
    ai_s = ai.reshape(S5_SLABS, 1, SLAB_STATE)
    return bw, cw, ar_s, ai_s


def _s5(u_tb, bw, cw, ar_s, ai_s, d_row, w_glu, b_glu_row, g_row):
    seq, bsz, _ = u_tb.shape
    tc = S5_TIME_BLOCK
    assert seq % tc == 0 and bsz % 16 == 0
    full = lambda shape: pl.BlockSpec(shape, lambda t: (0,) * len(shape))
    return pl.pallas_call(
        functools.partial(_s5_kernel, tc=tc, bsz=bsz),
        grid=(seq // tc,),
        in_specs=[pl.BlockSpec((tc, bsz, SSM_WIDTH), lambda t: (t, 0, 0)),
                  full(bw.shape), full(cw.shape), full(ar_s.shape), full(ai_s.shape),
                  full((1, SSM_WIDTH)), full((SSM_WIDTH, SSM_WIDTH)), full((1, SSM_WIDTH)),
                  full((1, SSM_WIDTH))],
        out_specs=pl.BlockSpec((tc, bsz, SSM_WIDTH), lambda t: (t, 0, 0)),
        out_shape=jax.ShapeDtypeStruct((seq, bsz, SSM_WIDTH), BF16),
        scratch_shapes=[pltpu.VMEM((S5_SLABS, bsz, 2 * SLAB_STATE), F32),
                        pltpu.VMEM((S5_SLABS, tc * bsz, 2 * SLAB_STATE), F32)],
        compiler_params=pltpu.CompilerParams(dimension_semantics=("arbitrary",),
                                             vmem_limit_bytes=VMEM_LIMIT),
        name="s5_branch",
    )(u_tb, bw, cw, ar_s, ai_s, d_row, w_glu, b_glu_row, g_row)


def _layer_norm_rows(v, g, b):
    mu = jnp.mean(v, axis=-1, keepdims=True)
    c = v - mu
    var = jnp.mean(c * c, axis=-1, keepdims=True)
    return c * lax.rsqrt(var + LN_EPS) * g + b


def _outproj_kernel(attn_ref, ssm_ref, x_ref, wo_ref, ag_ref, g1_ref, b1_ref, wrt_ref, brt_ref, triu_ref,
                    h_ref, eid_ref, gate_ref, rank_ref, cnt_ref):
    a = attn_ref[...]
    a = a * lax.rsqrt(jnp.mean(a * a, axis=-1, keepdims=True) + RMS_EPS) * ag_ref[...]
    mix = (jnp.dot(a.astype(BF16), wo_ref[0:ATTN_WIDTH, :], preferred_element_type=F32)
           + jnp.dot(ssm_ref[...], wo_ref[ATTN_WIDTH:, :], preferred_element_type=F32))
    h = _layer_norm_rows(DEEPNORM_ALPHA * x_ref[...] + mix, g1_ref[...], b1_ref[...])
    h_ref[...] = h

    logits = lax.dot_general(wrt_ref[...], h.astype(BF16), (((1,), (1,)), ((), ())),
                             preferred_element_type=F32)[:ROUTER_ROWS] + brt_ref[...]
    tm = logits.shape[1]
    row = lax.broadcasted_iota(jnp.int32, (ROUTER_ROWS, tm), 0).astype(F32)
    no_row = float(LANES)
    is_g = row < N_EXPERT_GROUPS
    gl = jnp.where(is_g, logits, NEG_BIG)
    gmax = jnp.max(gl, axis=0, keepdims=True)
    g_idx = jnp.min(jnp.where(is_g & (gl == gmax), row, no_row), axis=0, keepdims=True)
    g_p = 1.0 / jnp.sum(jnp.where(is_g, jnp.exp(gl - gmax), 0.0), axis=0, keepdims=True)
    first = N_EXPERT_GROUPS + g_idx * EXPERTS_PER_GROUP
    in_grp = (row >= first) & (row < first + EXPERTS_PER_GROUP)
    el = jnp.where(in_grp, logits, NEG_BIG)
    m1 = jnp.max(el, axis=0, keepdims=True)
    i1 = jnp.min(jnp.where(in_grp & (el == m1), row, no_row), axis=0, keepdims=True)
    rest = in_grp & (row != i1)
    el2 = jnp.where(rest, logits, NEG_BIG)
    m2 = jnp.max(el2, axis=0, keepdims=True)
    i2 = jnp.min(jnp.where(rest & (el2 == m2), row, no_row), axis=0, keepdims=True)
    r = jnp.exp(m2 - m1)
    p1 = 1.0 / (1.0 + r)
    p2 = r / (1.0 + r)
    k2 = lax.broadcasted_iota(jnp.int32, (2, tm), 0)
    eid_ref[...] = (jnp.where(k2 == 0, i1, i2) - N_EXPERT_GROUPS).astype(jnp.int32)
    gate_ref[...] = g_p * jnp.where(k2 == 0, p1, p2)

    @pl.when(pl.program_id(0) == 0)
    def _():
        cnt_ref[...] = jnp.zeros_like(cnt_ref)
    hit1 = row == i1
    hit2 = row == i2
    onehot = jnp.where(hit1 | hit2, 1.0, 0.0)
    cnt = cnt_ref[...]
    before = jnp.dot(onehot.astype(BF16), triu_ref[...], preferred_element_type=F32) + cnt
    r1 = jnp.sum(jnp.where(hit1, before, 0.0), axis=0, keepdims=True)
    r2 = jnp.sum(jnp.where(hit2, before, 0.0), axis=0, keepdims=True)
    rank_ref[...] = jnp.where(k2 == 0, r1, r2).astype(jnp.int32)
    cnt_ref[...] = cnt + jnp.sum(onehot, axis=1, keepdims=True)


def _outproj(attn2, ssm2, x2, wo, ag_row, g1_row, b1_row, wrt, brt):
    n_tok = x2.shape[0]
    tm = ROW_BLOCK
    triu = jnp.triu(jnp.ones((tm, tm), BF16), 1)
    row_spec = lambda w: pl.BlockSpec((tm, w), lambda i: (i, 0))
    col_spec = lambda r: pl.BlockSpec((r, tm), lambda i: (0, i))
    full = lambda shape: pl.BlockSpec(shape, lambda i: (0,) * len(shape))
    return pl.pallas_call(
        _outproj_kernel,
        grid=(n_tok // tm,),
        in_specs=[row_spec(ATTN_WIDTH), row_spec(SSM_WIDTH), row_spec(D_MODEL),
                  full((D_MODEL, D_MODEL)), full((1, ATTN_WIDTH)), full((1, D_MODEL)), full((1, D_MODEL)),
                  full((LANES, D_MODEL)), full((ROUTER_ROWS, 1)), full((tm, tm))],
        out_specs=[row_spec(D_MODEL), col_spec(2), col_spec(2), col_spec(2), full((ROUTER_ROWS, 1))],
        out_shape=[jax.ShapeDtypeStruct((n_tok, D_MODEL), F32),
                   jax.ShapeDtypeStruct((2, n_tok), jnp.int32),
                   jax.ShapeDtypeStruct((2, n_tok), F32),
                   jax.ShapeDtypeStruct((2, n_tok), jnp.int32),
                   jax.ShapeDtypeStruct((ROUTER_ROWS, 1), F32)],
        compiler_params=pltpu.CompilerParams(dimension_semantics=("arbitrary",),
                                             vmem_limit_bytes=VMEM_LIMIT),
        name="outproj_ln_router",
    )(attn2, ssm2, x2, wo, ag_row, g1_row, b1_row, wrt, brt, triu)


def _dispatch_plan(eid, rank, cnt_col, n_tok):
    counts = cnt_col[N_EXPERT_GROUPS:N_EXPERT_GROUPS + N_EXPERTS, 0].astype(jnp.int32)
    padded = ((counts + MOE_BLOCK - 1) // MOE_BLOCK) * MOE_BLOCK
    pends = jnp.cumsum(padded)
    pstarts = pends - padded
    n_pad = ((2 * n_tok + N_EXPERTS * (MOE_BLOCK - 1) + MOE_BLOCK - 1) // MOE_BLOCK) * MOE_BLOCK
    n_blocks = n_pad // MOE_BLOCK
    blk0 = jnp.arange(n_blocks, dtype=jnp.int32) * MOE_BLOCK
    block_e = jnp.minimum(jnp.sum((pends[None, :] <= blk0[:, None]).astype(jnp.int32), axis=1), N_EXPERTS - 1)
    n_used = (pends[-1] // MOE_BLOCK).astype(jnp.int32).reshape(1)
    tail = jnp.where(padded > 0, pends - MOE_BLOCK, -1).astype(jnp.int32)

    tb = TOKEN_BLOCK
    nb = n_tok // tb
    onehot = eid[:, :, None] == jnp.arange(N_EXPERTS, dtype=jnp.int32)[None, None, :]
    run_len = jnp.sum(onehot.reshape(2, nb, tb, N_EXPERTS).astype(jnp.int32), axis=(0, 2))
    before = jnp.cumsum(run_len, axis=0) - run_len
    run_src = jnp.cumsum(run_len, axis=1) - run_len
    run_dst = pstarts[None, :] + before
    shift = jnp.broadcast_to((run_src - before)[:, None, :], (nb, tb, N_EXPERTS)).reshape(n_tok, N_EXPERTS)
    local = rank + jnp.sum(jnp.where(onehot, shift[None], 0), axis=-1)
    runs = (run_len.reshape(-1), run_src.reshape(-1), run_dst.reshape(-1))
    return local, runs, block_e, n_used, tail, n_blocks


ROW_SUB = D_MODEL // LANES


def _rows(start, size):
    return pl.ds(pl.multiple_of(start * ROW_SUB, ROW_SUB), size * ROW_SUB)


def _store_row_tiles(ref, row0, v):
    n = v.shape[0]
    for s in range(ROW_SUB):
        ref[pl.ds(row0 * ROW_SUB + s, n, stride=ROW_SUB), :] = v[:, LANES * s:LANES * (s + 1)]


def _load_row_tiles(ref, row0, n):
    return jnp.concatenate([ref[pl.ds(row0 * ROW_SUB + s, n, stride=ROW_SUB), :] for s in range(ROW_SUB)], axis=1)


def _for_each_run_piece(blk, len_ref, src_ref, dst_ref, make_copy, act):
    for e in range(N_EXPERTS):
        idx = blk * N_EXPERTS + e
        n = len_ref[idx]
        src = src_ref[idx]
        dst = dst_ref[idx]
        for b in reversed(range(RUN_BITS)):
            @pl.when((n & (1 << b)) != 0)
            def _():
                off = lax.shift_left(lax.shift_right_logical(n, b + 1), b + 1)
                act(make_copy(src + off, dst + off, 1 << b))


def _dispatch_kernel(tail_ref, nu_ref, len_ref, src_ref, dst_ref, local_ref, h_ref, xb_hbm, zbuf, pk_ref,
                     sem_z, sem, *, n_blocks):
    i = pl.program_id(0)
    tb = h_ref.shape[0]
    pos = lax.broadcasted_iota(jnp.int32, (2 * tb, tb), 0)
    hit = (pos == local_ref[0:1, :]) | (pos == local_ref[1:2, :])
    perm = jnp.where(hit, 1.0, 0.0).astype(BF16)
    _store_row_tiles(pk_ref, 0, jnp.dot(perm, h_ref[...].astype(BF16), preferred_element_type=F32))

    @pl.when(pl.program_id(0) == 0)
    def _():
        zbuf[...] = jnp.zeros_like(zbuf)

        def zero_copy(row0):
            return pltpu.make_async_copy(zbuf, xb_hbm.at[_rows(row0, MOE_BLOCK)], sem_z)

        def each_zero_block(fn):
            for e in range(N_EXPERTS):
                @pl.when(tail_ref[e] >= 0)
                def _():
                    fn(zero_copy(tail_ref[e]))

            def unused(b, carry):
                fn(zero_copy(b * MOE_BLOCK))
                return carry
            lax.fori_loop(nu_ref[0], n_blocks, unused, 0)

        each_zero_block(lambda c: c.start())
        each_zero_block(lambda c: c.wait())

    def run_copy(src, dst, size):
        return pltpu.make_async_copy(pk_ref.at[_rows(src, size)], xb_hbm.at[_rows(dst, size)], sem)
    _for_each_run_piece(i, len_ref, src_ref, dst_ref, run_copy, lambda c: c.start())
    pltpu.make_async_copy(pk_ref, xb_hbm.at[_rows(0, 2 * tb)], sem).wait()


def _dispatch(h, local, runs, tail, n_used, n_blocks):
    n_tok = h.shape[0]
    tb = TOKEN_BLOCK
    grid_spec = pltpu.PrefetchScalarGridSpec(
        num_scalar_prefetch=5,
        grid=(n_tok // tb,),
        in_specs=[pl.BlockSpec((2, tb), lambda i, *_: (0, i)),
                  pl.BlockSpec((tb, D_MODEL), lambda i, *_: (i, 0))],
        out_specs=pl.BlockSpec(memory_space=pl.ANY),
        scratch_shapes=[pltpu.VMEM((MOE_BLOCK * ROW_SUB, LANES), F32), pltpu.VMEM((2 * tb * ROW_SUB, LANES), F32),
                        pltpu.SemaphoreType.DMA, pltpu.SemaphoreType.DMA],
    )
    return pl.pallas_call(
        functools.partial(_dispatch_kernel, n_blocks=n_blocks),
        grid_spec=grid_spec,
        out_shape=jax.ShapeDtypeStruct((n_blocks * MOE_BLOCK * ROW_SUB, LANES), F32),
        compiler_params=pltpu.CompilerParams(dimension_semantics=("arbitrary",),
                                             vmem_limit_bytes=VMEM_LIMIT),
        name="moe_dispatch",
    )(tail, n_used, *runs, local, h)


def _expert_kernel(be_ref, nu_ref, x_ref, wg_ref, wu_ref, wd_ref, y_ref, wgu_t, wd_t, yt_ref):
    i = pl.program_id(0)
    used = i < nu_ref[0]

    @pl.when(used & ((i == 0) | (be_ref[i] != be_ref[jnp.maximum(i - 1, 0)])))
    def _():
        wgu_t[0:D_EXPERT, :] = wg_ref[...].T.astype(BF16)
        wgu_t[D_EXPERT:, :] = wu_ref[...].T.astype(BF16)
        wd_t[...] = wd_ref[...].T.astype(BF16)

    @pl.when(used)
    def _():
        sub = range(MOE_BLOCK // MOE_SUB)
        gus = []
        for c in sub:
            xb = _load_row_tiles(x_ref, c * MOE_SUB, MOE_SUB).astype(BF16)
            gus.append(lax.dot_general(wgu_t[...], xb, (((1,), (1,)), ((), ())),
                                       preferred_element_type=F32))
        hids = []
        for c in sub:
            gt = gus[c][0:D_EXPERT]
            hids.append((gt * (1.0 / (1.0 + jnp.exp(-gt))) * gus[c][D_EXPERT:]).astype(BF16))
        for c in sub:
            yt_ref[c] = jnp.dot(wd_t[...], hids[c], preferred_element_type=F32)
        for c in sub:
            _store_row_tiles(y_ref, c * MOE_SUB, yt_ref[c].T)

    @pl.when(jnp.logical_not(used))
    def _():
        y_ref[...] = jnp.zeros_like(y_ref)


def _experts(xb, block_e, n_used, wg, wu, wd):
    n_blocks = xb.shape[0] // (MOE_BLOCK * ROW_SUB)
    last = lambda i, nu: jnp.minimum(i, nu[0] - 1)
    grid_spec = pltpu.PrefetchScalarGridSpec(
        num_scalar_prefetch=2,
        grid=(n_blocks,),
        in_specs=[pl.BlockSpec((MOE_BLOCK * ROW_SUB, LANES), lambda i, be, nu: (last(i, nu), 0)),
                  pl.BlockSpec((None, D_MODEL, D_EXPERT), lambda i, be, nu: (be[last(i, nu)], 0, 0)),
                  pl.BlockSpec((None, D_MODEL, D_EXPERT), lambda i, be, nu: (be[last(i, nu)], 0, 0)),
                  pl.BlockSpec((None, D_EXPERT, D_MODEL), lambda i, be, nu: (be[last(i, nu)], 0, 0))],
        out_specs=pl.BlockSpec((MOE_BLOCK * ROW_SUB, LANES), lambda i, be, nu: (i, 0)),
        scratch_shapes=[pltpu.VMEM((2 * D_EXPERT, D_MODEL), BF16), pltpu.VMEM((D_MODEL, D_EXPERT), BF16),
                        pltpu.VMEM((MOE_BLOCK // MOE_SUB, D_MODEL, MOE_SUB), F32)],
    )
    return pl.pallas_call(
        _expert_kernel,
        grid_spec=grid_spec,
        out_shape=jax.ShapeDtypeStruct(xb.shape, F32),
        compiler_params=pltpu.CompilerParams(dimension_semantics=("arbitrary",),
                                             vmem_limit_bytes=VMEM_LIMIT),
        name="moe_experts",
    )(block_e, n_used, xb, wg, wu, wd)


def _combine_kernel(len_ref, src_ref, dst_ref, local_ref, gate_ref, h_ref, g2_ref, b2_ref, yb_hbm, o_ref,
                    buf, sem):
    i = pl.program_id(0)
    n = pl.num_programs(0)
    tb = h_ref.shape[0]
    cur = i % 2

    def fetch(blk, which):
        def run_copy(src, dst, size):
            return pltpu.make_async_copy(yb_hbm.at[_rows(dst, size)], buf.at[which, _rows(src, size)], sem.at[which])
        _for_each_run_piece(blk, len_ref, src_ref, dst_ref, run_copy, lambda c: c.start())

    @pl.when(i == 0)
    def _():
        fetch(i, 0)

    @pl.when(i + 1 < n)
    def _():
        fetch(i + 1, 1 - cur)

    pltpu.make_async_copy(yb_hbm.at[_rows(0, 2 * tb)], buf.at[cur], sem.at[cur]).wait()
    y = _load_row_tiles(buf.at[cur], 0, 2 * tb).astype(BF16)
    pos = lax.broadcasted_iota(jnp.int32, (tb, 2 * tb), 1)
    local = local_ref[...]
    gate = gate_ref[...]
    mix = (jnp.where(pos == local[:, 0:1], gate[:, 0:1], 0.0)
           + jnp.where(pos == local[:, 1:2], gate[:, 1:2], 0.0))
    mix_hi = mix.astype(BF16)
    mix_lo = (mix - mix_hi.astype(F32)).astype(BF16)
    moe = (jnp.dot(mix_hi, y, preferred_element_type=F32) + jnp.dot(mix_lo, y, preferred_element_type=F32))
    o_ref[...] = _layer_norm_rows(DEEPNORM_ALPHA * h_ref[...] + moe, g2_ref[...], b2_ref[...])


def _combine(h, yb, local_t, runs, gate, g2_row, b2_row):
    n_tok = h.shape[0]
    tb = TOKEN_BLOCK
    row_spec = lambda w: pl.BlockSpec((tb, w), lambda i, *_: (i, 0))
    full = lambda shape: pl.BlockSpec(shape, lambda i, *_: (0,) * len(shape))
    grid_spec = pltpu.PrefetchScalarGridSpec(
        num_scalar_prefetch=3,
        grid=(n_tok // tb,),
        in_specs=[row_spec(2), row_spec(2), row_spec(D_MODEL), full((1, D_MODEL)), full((1, D_MODEL)),
                  pl.BlockSpec(memory_space=pl.ANY)],
        out_specs=row_spec(D_MODEL),
        scratch_shapes=[pltpu.VMEM((2, 2 * tb * ROW_SUB, LANES), F32), pltpu.SemaphoreType.DMA((2,))],
    )
    return pl.pallas_call(
        _combine_kernel,
        grid_spec=grid_spec,
        out_shape=jax.ShapeDtypeStruct((n_tok, D_MODEL), F32),
        compiler_params=pltpu.CompilerParams(dimension_semantics=("arbitrary",),
                                             vmem_limit_bytes=VMEM_LIMIT),
        name="combine_ln",
    )(*runs, local_t, gate, h, g2_row, b2_row, yb)


def kernel(x, w_in, b_fgate, s5_lambda_re, s5_lambda_im, s5_log_step, s5_b_re, s5_b_im, s5_c_re, s5_c_im, s5_d, w_glu, b_glu, attn_norm_g, ssm_norm_g, w_out, ln1_g, ln1_b, w_router_group, b_router_group, w_router_expert, b_router_expert, w_gate, w_up, w_down, ln2_g, ln2_b):
    bsz, seq, d = x.shape
    n_tok = bsz * seq
    a = ATTN_WIDTH
    for l in range(DEPTH):
        x2 = x.reshape(n_tok, d)
        w = w_in[l]
        f0 = 3 * a
        w_cat = jnp.concatenate(
            [w[:, :a] * (HEAD_DIM ** -0.5 * LOG2E), w[:, a:2 * a], w[:, f0 + ATTN_HEADS:],
             jnp.pad(w[:, f0:f0 + ATTN_HEADS], ((0, 0), (0, LANES - ATTN_HEADS)))], axis=1).astype(BF16)
        wvt = w[:, 2 * a:f0].T.astype(BF16)
        bf_row = jnp.pad(b_fgate[l], (0, LANES - ATTN_HEADS)).reshape(1, LANES)
        q, k, vt, u, cum = _inproj(x2, w_cat, wvt, bf_row, seq)

        cum_g = cum.reshape(bsz, seq, ATTN_HEADS // ATTN_HEADS_PER_STEP, ATTN_HEADS_PER_STEP).transpose(0, 2, 1, 3)
        attn = _attention(q.reshape(bsz, seq, a), k.reshape(bsz, seq, a), vt, cum_g)

        bw, cw, ar_s, ai_s = _s5_weights(s5_lambda_re[l], s5_lambda_im[l], s5_log_step[l],
                                         s5_b_re[l], s5_b_im[l], s5_c_re[l], s5_c_im[l])
        u_tb = u.reshape(bsz, seq, SSM_WIDTH).transpose(1, 0, 2)
        ssm_tb = _s5(u_tb, bw, cw, ar_s, ai_s, s5_d[l].reshape(1, SSM_WIDTH), w_glu[l].astype(BF16),
                     b_glu[l].reshape(1, SSM_WIDTH), ssm_norm_g[l].reshape(1, SSM_WIDTH))
        ssm2 = ssm_tb.transpose(1, 0, 2).reshape(n_tok, SSM_WIDTH)

        n_route = N_EXPERT_GROUPS + N_EXPERTS
        wrt = jnp.pad(jnp.concatenate([w_router_group[l], w_router_expert[l]], axis=1).T,
                      ((0, LANES - n_route), (0, 0))).astype(BF16)
        brt = jnp.pad(jnp.concatenate([b_router_group[l], b_router_expert[l]]),
                      (0, ROUTER_ROWS - n_route)).reshape(ROUTER_ROWS, 1)
        h, eid, gate, rank, cnt_col = _outproj(
            attn.reshape(n_tok, a), ssm2, x2, w_out[l].astype(BF16), attn_norm_g[l].reshape(1, a),
            ln1_g[l].reshape(1, d), ln1_b[l].reshape(1, d), wrt, brt)

        local, runs, block_e, n_used, tail, n_blocks = _dispatch_plan(eid, rank, cnt_col, n_tok)
        xb = _dispatch(h, local, runs, tail, n_used, n_blocks)
        yb = _experts(xb, block_e, n_used, w_gate[l], w_up[l], w_down[l])
        x = _combine(h, yb, local.T, runs, gate.T, ln2_g[l].reshape(1, d),
                     ln2_b[l].reshape(1, d)).reshape(bsz, seq, d)
    return x
```

```python
import functools
import math

import jax
import jax.numpy as jnp
from jax import lax
from jax.experimental import pallas as pl
from jax.experimental.pallas import tpu as pltpu

F32 = jnp.float32
BF16 = jnp.bfloat16

D_MODEL = 1024
HEAD_DIM = 64
ATTN_WIDTH = 512
ATTN_HEADS = 8
SSM_WIDTH = 512
SSM_GROUP = 16
SSM_GROUPS = 32
SSM_STATE = 64
N_EXPERT_GROUPS = 4
EXPERTS_PER_GROUP = 8
N_EXPERTS = 32
D_EXPERT = 512
MOE_BLOCK = 512
MOE_SUB = 256
TOKEN_BLOCK = 256
RUN_BITS = TOKEN_BLOCK.bit_length()
DEPTH = 1
DEEPNORM_ALPHA = (2.0 * DEPTH) ** 0.25
LN_EPS = 1e-5
RMS_EPS = 1e-6

LANES = 128
NEG_BIG = -1e30
LOG2E = math.log2(math.e)
ONES_ROWS = 16
VT_ROWS = ATTN_HEADS * (HEAD_DIM + ONES_ROWS)
VMEM_LIMIT = 48 * 1024 * 1024

ROW_BLOCK = 512
ATTN_BLOCK = 256
ATTN_HEADS_PER_STEP = 8
S5_TIME_BLOCK = 64
S5_SLABS = 4
ROUTER_ROWS = 40
SLAB_CH = SSM_WIDTH // S5_SLABS
SLAB_STATE = SSM_GROUPS // S5_SLABS * SSM_STATE


def _split3(c):
    hi = c.astype(BF16)
    r1 = c - hi.astype(F32)
    mid = r1.astype(BF16)
    lo = (r1 - mid.astype(F32)).astype(BF16)
    return hi, mid, lo


def _inproj_kernel(x_ref, w_ref, wvt_ref, bf_ref, q_ref, k_ref, vt_ref, u_ref, cum_ref, carry_ref, *,
                   blocks_per_seq):
    i = pl.program_id(0)

    @pl.when(i % blocks_per_seq == 0)
    def _():
        carry_ref[...] = jnp.zeros_like(carry_ref)

    tm = x_ref.shape[0]
    xb = x_ref[...].astype(BF16)
    z = jnp.dot(xb, w_ref[...], preferred_element_type=F32)
    a = ATTN_WIDTH
    q_ref[...] = z[:, :a].astype(BF16)
    k_ref[...] = z[:, a:2 * a].astype(BF16)
    u_ref[...] = z[:, 2 * a:2 * a + SSM_WIDTH]
    vt = lax.dot_general(wvt_ref[...], xb, (((1,), (1,)), ((), ())), preferred_element_type=F32).astype(BF16)
    ones_rows = jnp.ones((ONES_ROWS, tm), BF16)
    vt = jnp.concatenate([piece for h in range(ATTN_HEADS)
                          for piece in (vt[HEAD_DIM * h:HEAD_DIM * (h + 1)], ones_rows)], axis=0)
    for jj in range(tm // ATTN_BLOCK):
        vt_ref[jj] = vt[:, jj * ATTN_BLOCK:(jj + 1) * ATTN_BLOCK]
    f = z[:, 2 * a + SSM_WIDTH:] + bf_ref[...]
    logf = jnp.minimum(f, 0.0) - jnp.log1p(jnp.exp(-jnp.abs(f)))
    row = lax.broadcasted_iota(jnp.int32, (tm, tm), 0)
    col = lax.broadcasted_iota(jnp.int32, (tm, tm), 1)
    tri = (row >= col).astype(BF16)
    hi, mid, lo = _split3(logf)
    cum = (jnp.dot(tri, hi, preferred_element_type=F32)
           + jnp.dot(tri, mid, preferred_element_type=F32)
           + jnp.dot(tri, lo, preferred_element_type=F32)) + carry_ref[...]
    carry_ref[...] = cum[tm - 1:tm, :]
    cum_ref[...] = cum[:, :ATTN_HEADS] * LOG2E


def _inproj(x2, w_cat, wvt, bf_row, seq):
    n_tok = x2.shape[0]
    tm = ROW_BLOCK
    assert seq % tm == 0 and n_tok % tm == 0 and tm % ATTN_BLOCK == 0
    bsz = n_tok // seq
    bps = seq // tm
    ncol = w_cat.shape[1]
    vblk = tm // ATTN_BLOCK
    row_spec = lambda w: pl.BlockSpec((tm, w), lambda i: (i, 0))
    return pl.pallas_call(
        functools.partial(_inproj_kernel, blocks_per_seq=bps),
        grid=(n_tok // tm,),
        in_specs=[row_spec(D_MODEL),
                  pl.BlockSpec((D_MODEL, ncol), lambda i: (0, 0)),
                  pl.BlockSpec((ATTN_WIDTH, D_MODEL), lambda i: (0, 0)),
                  pl.BlockSpec((1, LANES), lambda i: (0, 0))],
        out_specs=[row_spec(ATTN_WIDTH), row_spec(ATTN_WIDTH),
                   pl.BlockSpec((None, vblk, VT_ROWS, ATTN_BLOCK), lambda i: (i // bps, i % bps, 0, 0)),
                   row_spec(SSM_WIDTH), row_spec(ATTN_HEADS)],
        out_shape=[jax.ShapeDtypeStruct((n_tok, ATTN_WIDTH), BF16)] * 2
                  + [jax.ShapeDtypeStruct((bsz, seq // ATTN_BLOCK, VT_ROWS, ATTN_BLOCK), BF16),
                     jax.ShapeDtypeStruct((n_tok, SSM_WIDTH), F32),
                     jax.ShapeDtypeStruct((n_tok, ATTN_HEADS), F32)],
        scratch_shapes=[pltpu.VMEM((1, LANES), F32)],
        compiler_params=pltpu.CompilerParams(dimension_semantics=("arbitrary",),
                                             vmem_limit_bytes=VMEM_LIMIT),
        name="inproj",
    )(x2, w_cat, wvt, bf_row)


def _bias_lanes_q(c):
    hi, mid, lo = (t.astype(F32) for t in _split3(c))
    lane = lax.broadcasted_iota(jnp.int32, (c.shape[0], LANES), 1)
    sel = jnp.where(lane == 0, hi, jnp.where(lane == 1, mid, jnp.where(lane == 2, lo,
                    jnp.where(lane < 6, 1.0, 0.0))))
    return sel.astype(BF16)


def _bias_lanes_k(c):
    hi, mid, lo = (t.astype(F32) for t in _split3(c))
    lane = lax.broadcasted_iota(jnp.int32, (c.shape[0], LANES), 1)
    sel = jnp.where(lane < 3, 1.0, jnp.where(lane == 3, -hi, jnp.where(lane == 4, -mid,
                    jnp.where(lane == 5, -lo, 0.0))))
    return sel.astype(BF16)


def _attn_kernel(q_ref, k_ref, vt_ref, cum_ref, o_ref, extk_ref, *, tq, seq, nh):
    qi = pl.program_id(2)
    n_tiles = seq // tq

    @pl.when(qi == 0)
    def _():
        def fill(j, carry):
            r0 = pl.multiple_of(j * tq, tq)
            c = cum_ref[pl.ds(r0, tq), :]
            for h in range(nh):
                extk_ref[h, pl.ds(r0, tq), :] = _bias_lanes_k(c[:, h:h + 1])
            return carry
        lax.fori_loop(0, n_tiles, fill, 0)

    q0 = pl.multiple_of(qi * tq, tq)
    cq = cum_ref[pl.ds(q0, tq), :]
    lane = lax.broadcasted_iota(jnp.int32, (tq, LANES), 1)
    q_augs = []
    for h in range(nh):
        qb = q_ref[:, LANES * (h // 2):LANES * (h // 2 + 1)]
        head_lanes = (lane < HEAD_DIM) if h % 2 == 0 else (lane >= HEAD_DIM)
        qh = jnp.where(head_lanes, qb, jnp.zeros((), BF16))
        q_augs.append(jnp.concatenate([qh, _bias_lanes_q(cq[:, h:h + 1])], axis=1))
    key_i = lax.broadcasted_iota(jnp.int32, (tq, tq), 0)
    qry_i = lax.broadcasted_iota(jnp.int32, (tq, tq), 1)
    causal = key_i <= qry_i
    vrows = HEAD_DIM + ONES_ROWS

    def scores_of(j):
        k0 = pl.multiple_of(j * tq, tq)
        out = []
        for h in range(nh):
            kj = k_ref[pl.ds(k0, tq), LANES * (h // 2):LANES * (h // 2 + 1)]
            k_aug = jnp.concatenate([kj, extk_ref[h, pl.ds(k0, tq), :]], axis=1)
            out.append(lax.dot_general(k_aug, q_augs[h], (((1,), (1,)), ((), ())), preferred_element_type=F32))
        return tuple(out)

    def absorb(j, scores, carry):
        stats = []
        for h in range(nh):
            m = carry[h][0]
            m_new = jnp.maximum(m, jnp.max(scores[h], axis=0, keepdims=True))
            stats.append((m_new, jnp.exp2(m - m_new), jnp.exp2(scores[h] - m_new).astype(BF16)))
        new = []
        for h in range(nh):
            m_new, corr, p = stats[h]
            res = jnp.dot(vt_ref[j, vrows * h:vrows * (h + 1), :], p, preferred_element_type=F32)
            acc = corr * carry[h][2] + res[0:HEAD_DIM, :]
            l = corr * carry[h][1] + res[HEAD_DIM:HEAD_DIM + 1, :]
            new.append((m_new, l, acc))
        return tuple(new)

    def step(j, carry):
        return absorb(j, scores_of(j), carry)

    init1 = (jnp.full((1, tq), NEG_BIG, F32), jnp.zeros((1, tq), F32), jnp.zeros((HEAD_DIM, tq), F32))
    carry = lax.fori_loop(0, qi, step, (init1,) * nh)
    carry = absorb(qi, tuple(jnp.where(causal, s, NEG_BIG) for s in scores_of(qi)), carry)
    out_t = jnp.concatenate([c[2] / c[1] for c in carry], axis=0)
    o_ref[...] = out_t.T


def _attention(q, k, vt, cum_g):
    bsz, seq, _ = q.shape
    tq = ATTN_BLOCK
    nh = ATTN_HEADS_PER_STEP
    assert seq % tq == 0 and ATTN_HEADS % nh == 0 and nh % 2 == 0
    w = nh * HEAD_DIM
    return pl.pallas_call(
        functools.partial(_attn_kernel, tq=tq, seq=seq, nh=nh),
        grid=(bsz, ATTN_HEADS // nh, seq // tq),
        in_specs=[pl.BlockSpec((None, tq, w), lambda b, g, i: (b, i, g)),
                  pl.BlockSpec((None, seq, w), lambda b, g, i: (b, 0, g)),
                  pl.BlockSpec((None, seq // tq, nh * (HEAD_DIM + ONES_ROWS), tq), lambda b, g, i: (b, 0, g, 0)),
                  pl.BlockSpec((None, None, seq, nh), lambda b, g, i: (b, g, 0, 0))],
        out_specs=pl.BlockSpec((None, tq, w), lambda b, g, i: (b, i, g)),
        out_shape=jax.ShapeDtypeStruct((bsz, seq, ATTN_WIDTH), F32),
        scratch_shapes=[pltpu.VMEM((nh, seq, LANES), BF16)],
        compiler_params=pltpu.CompilerParams(dimension_semantics=("parallel", "parallel", "arbitrary"),
                                             vmem_limit_bytes=VMEM_LIMIT),
        name="fox_attention",
    )(q, k, vt, cum_g)


def _s5_kernel(u_ref, bw_ref, cw_ref, ar_ref, ai_ref, d_ref, wglu_ref, bglu_ref, g_ref, o_ref,
               state_ref, bu_ref, *, tc, bsz):
    @pl.when(pl.program_id(0) == 0)
    def _():
        state_ref[...] = jnp.zeros_like(state_ref)

    rows = tc * bsz
    ns = SLAB_STATE
    u = u_ref[...].reshape(rows, SSM_WIDTH)
    ub = u.astype(BF16)
    for i in range(S5_SLABS):
        bu_ref[i] = jnp.dot(ub[:, SLAB_CH * i:SLAB_CH * (i + 1)], bw_ref[i], preferred_element_type=F32)
    for i in range(S5_SLABS):
        ar = jnp.broadcast_to(ar_ref[i], (bsz, ns))
        ai = jnp.broadcast_to(ai_ref[i], (bsz, ns))
        xr = state_ref[i, :, 0:ns]
        xi = state_ref[i, :, ns:2 * ns]
        for t in range(tc):
            r0 = t * bsz
            nr = ar * xr - ai * xi + bu_ref[i, r0:r0 + bsz, 0:ns]
            ni = ar * xi + ai * xr + bu_ref[i, r0:r0 + bsz, ns:2 * ns]
            bu_ref[i, r0:r0 + bsz, 0:ns] = nr
            bu_ref[i, r0:r0 + bsz, ns:2 * ns] = ni
            xr, xi = nr, ni
        state_ref[i, :, 0:ns] = xr
        state_ref[i, :, ns:2 * ns] = xi
    ys = [jnp.dot(bu_ref[i].astype(BF16), cw_ref[i], preferred_element_type=F32) for i in range(S5_SLABS)]
    y = jnp.concatenate(ys, axis=1) + d_ref[...] * u
    y = 0.5 * y * (1.0 + jnp.tanh(math.sqrt(2.0 / math.pi) * (y + 0.044715 * (y * y * y))))
    z = jnp.dot(y.astype(BF16), wglu_ref[...], preferred_element_type=F32) + bglu_ref[...]
    y = y * (1.0 / (1.0 + jnp.exp(-z)))
    y = y * lax.rsqrt(jnp.mean(y * y, axis=-1, keepdims=True) + RMS_EPS) * g_ref[...]
    o_ref[...] = y.astype(BF16).reshape(tc, bsz, SSM_WIDTH)


def _s5_weights(lam_re, lam_im, log_step, b_re, b_im, c_re, c_im):
    delta = jnp.exp(log_step)[:, None]
    mag = jnp.exp(lam_re * delta)
    ar = mag * jnp.cos(lam_im * delta)
    ai = mag * jnp.sin(lam_im * delta)
    den = lam_re * lam_re + lam_im * lam_im
    num_re = ar - 1.0
    coef_re = (num_re * lam_re + ai * lam_im) / den
    coef_im = (ai * lam_re - num_re * lam_im) / den
    bbar_re = coef_re[..., None] * b_re - coef_im[..., None] * b_im
    bbar_im = coef_re[..., None] * b_im + coef_im[..., None] * b_re
    gs = SSM_GROUPS // S5_SLABS
    eye = jnp.eye(gs, dtype=F32)

    def b_slab(bb):
        bb = bb.reshape(S5_SLABS, gs, SSM_STATE, SSM_GROUP)
        w = jnp.einsum('ab,sapc->sacbp', eye, bb)
        return w.reshape(S5_SLABS, gs * SSM_GROUP, gs * SSM_STATE)

    def c_slab(cc):
        cc = cc.reshape(S5_SLABS, gs, SSM_GROUP, SSM_STATE)
        w = jnp.einsum('ab,sacp->sbpac', eye, cc)
        return w.reshape(S5_SLABS, gs * SSM_STATE, gs * SSM_GROUP)

    bw = jnp.concatenate([b_slab(bbar_re), b_slab(bbar_im)], axis=2).astype(BF16)
    cw = jnp.concatenate([c_slab(c_re), -c_slab(c_im)], axis=1).astype(BF16)
    ar_s = ar.reshape(S5_SLABS, 1, SLAB_STATE)
    ai_s = ai.reshape(S5_SLABS, 1, SLAB_STATE)
    return bw, cw, ar_s, ai_s


def _s5(u_tb, bw, cw, ar_s, ai_s, d_row, w_glu, b_glu_row, g_row):
    seq, bsz, _ = u_tb.shape
    tc = S5_TIME_BLOCK
    assert seq % tc == 0 and bsz % 16 == 0
    full = lambda shape: pl.BlockSpec(shape, lambda t: (0,) * len(shape))
    return pl.pallas_call(
        functools.partial(_s5_kernel, tc=tc, bsz=bsz),
        grid=(seq // tc,),
        in_specs=[pl.BlockSpec((tc, bsz, SSM_WIDTH), lambda t: (t, 0, 0)),
                  full(bw.shape), full(cw.shape), full(ar_s.shape), full(ai_s.shape),
                  full((1, SSM_WIDTH)), full((SSM_WIDTH, SSM_WIDTH)), full((1, SSM_WIDTH)),
                  full((1, SSM_WIDTH))],
        out_specs=pl.BlockSpec((tc, bsz, SSM_WIDTH), lambda t: (t, 0, 0)),
        out_shape=jax.ShapeDtypeStruct((seq, bsz, SSM_WIDTH), BF16),
        scratch_shapes=[pltpu.VMEM((S5_SLABS, bsz, 2 * SLAB_STATE), F32),
                        pltpu.VMEM((S5_SLABS, tc * bsz, 2 * SLAB_STATE), F32)],
        compiler_params=pltpu.CompilerParams(dimension_semantics=("arbitrary",),
                                             vmem_limit_bytes=VMEM_LIMIT),
        name="s5_branch",
    )(u_tb, bw, cw, ar_s, ai_s, d_row, w_glu, b_glu_row, g_row)


def _layer_norm_rows(v, g, b):
    mu = jnp.mean(v, axis=-1, keepdims=True)
    c = v - mu
    var = jnp.mean(c * c, axis=-1, keepdims=True)
    return c * lax.rsqrt(var + LN_EPS) * g + b


def _outproj_kernel(attn_ref, ssm_ref, x_ref, wo_ref, ag_ref, g1_ref, b1_ref, wrt_ref, brt_ref, triu_ref,
                    h_ref, eid_ref, gate_ref, rank_ref, cnt_ref):
    a = attn_ref[...]
    a = a * lax.rsqrt(jnp.mean(a * a, axis=-1, keepdims=True) + RMS_EPS) * ag_ref[...]
    mix = (jnp.dot(a.astype(BF16), wo_ref[0:ATTN_WIDTH, :], preferred_element_type=F32)
           + jnp.dot(ssm_ref[...], wo_ref[ATTN_WIDTH:, :], preferred_element_type=F32))
    h = _layer_norm_rows(DEEPNORM_ALPHA * x_ref[...] + mix, g1_ref[...], b1_ref[...])
    h_ref[...] = h

    logits = lax.dot_general(wrt_ref[...], h.astype(BF16), (((1,), (1,)), ((), ())),
                             preferred_element_type=F32)[:ROUTER_ROWS] + brt_ref[...]
    tm = logits.shape[1]
    row = lax.broadcasted_iota(jnp.int32, (ROUTER_ROWS, tm), 0).astype(F32)
    no_row = float(LANES)
    is_g = row < N_EXPERT_GROUPS
    gl = jnp.where(is_g, logits, NEG_BIG)
    gmax = jnp.max(gl, axis=0, keepdims=True)
    g_idx = jnp.min(jnp.where(is_g & (gl == gmax), row, no_row), axis=0, keepdims=True)
    g_p = 1.0 / jnp.sum(jnp.where(is_g, jnp.exp(gl - gmax), 0.0), axis=0, keepdims=True)
    first = N_EXPERT_GROUPS + g_idx * EXPERTS_PER_GROUP
    in_grp = (row >= first) & (row < first + EXPERTS_PER_GROUP)
    el = jnp.where(in_grp, logits, NEG_BIG)
    m1 = jnp.max(el, axis=0, keepdims=True)
    i1 = jnp.min(jnp.where(in_grp & (el == m1), row, no_row), axis=0, keepdims=True)
    rest = in_grp & (row != i1)
    el2 = jnp.where(rest, logits, NEG_BIG)
    m2 = jnp.max(el2, axis=0, keepdims=True)
    i2 = jnp.min(jnp.where(rest & (el2 == m2), row, no_row), axis=0, keepdims=True)
    r = jnp.exp(m2 - m1)
    p1 = 1.0 / (1.0 + r)
    p2 = r / (1.0 + r)
    k2 = lax.broadcasted_iota(jnp.int32, (2, tm), 0)
    eid_ref[...] = (jnp.where(k2 == 0, i1, i2) - N_EXPERT_GROUPS).astype(jnp.int32)
    gate_ref[...] = g_p * jnp.where(k2 == 0, p1, p2)

    @pl.when(pl.program_id(0) == 0)
    def _():
        cnt_ref[...] = jnp.zeros_like(cnt_ref)
    hit1 = row == i1
    hit2 = row == i2
    onehot = jnp.where(hit1 | hit2, 1.0, 0.0)
    cnt = cnt_ref[...]
    before = jnp.dot(onehot.astype(BF16), triu_ref[...], preferred_element_type=F32) + cnt
    r1 = jnp.sum(jnp.where(hit1, before, 0.0), axis=0, keepdims=True)
    r2 = jnp.sum(jnp.where(hit2, before, 0.0), axis=0, keepdims=True)
    rank_ref[...] = jnp.where(k2 == 0, r1, r2).astype(jnp.int32)
    cnt_ref[...] = cnt + jnp.sum(onehot, axis=1, keepdims=True)


def _outproj(attn2, ssm2, x2, wo, ag_row, g1_row, b1_row, wrt, brt):
    n_tok = x2.shape[0]
    tm = ROW_BLOCK
    triu = jnp.triu(jnp.ones((tm, tm), BF16), 1)
    row_spec = lambda w: pl.BlockSpec((tm, w), lambda i: (i, 0))
    col_spec = lambda r: pl.BlockSpec((r, tm), lambda i: (0, i))
    full = lambda shape: pl.BlockSpec(shape, lambda i: (0,) * len(shape))
    return pl.pallas_call(
        _outproj_kernel,
        grid=(n_tok // tm,),
        in_specs=[row_spec(ATTN_WIDTH), row_spec(SSM_WIDTH), row_spec(D_MODEL),
                  full((D_MODEL, D_MODEL)), full((1, ATTN_WIDTH)), full((1, D_MODEL)), full((1, D_MODEL)),
                  full((LANES, D_MODEL)), full((ROUTER_ROWS, 1)), full((tm, tm))],
        out_specs=[row_spec(D_MODEL), col_spec(2), col_spec(2), col_spec(2), full((ROUTER_ROWS, 1))],
        out_shape=[jax.ShapeDtypeStruct((n_tok, D_MODEL), F32),
                   jax.ShapeDtypeStruct((2, n_tok), jnp.int32),
                   jax.ShapeDtypeStruct((2, n_tok), F32),
                   jax.ShapeDtypeStruct((2, n_tok), jnp.int32),
                   jax.ShapeDtypeStruct((ROUTER_ROWS, 1), F32)],
        compiler_params=pltpu.CompilerParams(dimension_semantics=("arbitrary",),
                                             vmem_limit_bytes=VMEM_LIMIT),
        name="outproj_ln_router",
    )(attn2, ssm2, x2, wo, ag_row, g1_row, b1_row, wrt, brt, triu)


def _dispatch_plan(eid, rank, cnt_col, n_tok):
    counts = cnt_col[N_EXPERT_GROUPS:N_EXPERT_GROUPS + N_EXPERTS, 0].astype(jnp.int32)
    padded = ((counts + MOE_BLOCK - 1) // MOE_BLOCK) * MOE_BLOCK
    pends = jnp.cumsum(padded)
    pstarts = pends - padded
    n_pad = ((2 * n_tok + N_EXPERTS * (MOE_BLOCK - 1) + MOE_BLOCK - 1) // MOE_BLOCK) * MOE_BLOCK
    n_blocks = n_pad // MOE_BLOCK
    blk0 = jnp.arange(n_blocks, dtype=jnp.int32) * MOE_BLOCK
    block_e = jnp.minimum(jnp.sum((pends[None, :] <= blk0[:, None]).astype(jnp.int32), axis=1), N_EXPERTS - 1)
    n_used = (pends[-1] // MOE_BLOCK).astype(jnp.int32).reshape(1)
    tail = jnp.where(padded > 0, pends - MOE_BLOCK, -1).astype(jnp.int32)

    tb = TOKEN_BLOCK
    nb = n_tok // tb
    onehot = eid[:, :, None] == jnp.arange(N_EXPERTS, dtype=jnp.int32)[None, None, :]
    run_len = jnp.sum(onehot.reshape(2, nb, tb, N_EXPERTS).astype(jnp.int32), axis=(0, 2))
    before = jnp.cumsum(run_len, axis=0) - run_len
    run_src = jnp.cumsum(run_len, axis=1) - run_len
    run_dst = pstarts[None, :] + before
    shift = jnp.broadcast_to((run_src - before)[:, None, :], (nb, tb, N_EXPERTS)).reshape(n_tok, N_EXPERTS)
    local = rank + jnp.sum(jnp.where(onehot, shift[None], 0), axis=-1)
    runs = (run_len.reshape(-1), run_src.reshape(-1), run_dst.reshape(-1))
    return local, runs, block_e, n_used, tail, n_blocks


ROW_SUB = D_MODEL // LANES


def _rows(start, size):
    return pl.ds(pl.multiple_of(start * ROW_SUB, ROW_SUB), size * ROW_SUB)


def _store_row_tiles(ref, row0, v):
    n = v.shape[0]
    for s in range(ROW_SUB):
        ref[pl.ds(row0 * ROW_SUB + s, n, stride=ROW_SUB), :] = v[:, LANES * s:LANES * (s + 1)]


def _load_row_tiles(ref, row0, n):
    return jnp.concatenate([ref[pl.ds(row0 * ROW_SUB + s, n, stride=ROW_SUB), :] for s in range(ROW_SUB)], axis=1)


def _for_each_run_piece(blk, len_ref, src_ref, dst_ref, make_copy, act):
    for e in range(N_EXPERTS):
        idx = blk * N_EXPERTS + e
        n = len_ref[idx]
        src = src_ref[idx]
        dst = dst_ref[idx]
        for b in reversed(range(RUN_BITS)):
            @pl.when((n & (1 << b)) != 0)
            def _():
                off = lax.shift_left(lax.shift_right_logical(n, b + 1), b + 1)
                act(make_copy(src + off, dst + off, 1 << b))


def _dispatch_kernel(tail_ref, nu_ref, len_ref, src_ref, dst_ref, local_ref, h_ref, xb_hbm, zbuf, pk_ref,
                     sem_z, sem, *, n_blocks):
    i = pl.program_id(0)
    tb = h_ref.shape[0]
    cur = i % 2
    pos = lax.broadcasted_iota(jnp.int32, (2 * tb, tb), 0)
    hit = (pos == local_ref[0:1, :]) | (pos == local_ref[1:2, :])
    perm = jnp.where(hit, 1.0, 0.0).astype(BF16)
    _store_row_tiles(pk_ref.at[cur], 0, jnp.dot(perm, h_ref[...].astype(BF16), preferred_element_type=F32))

    @pl.when(pl.program_id(0) == 0)
    def _():
        zbuf[...] = jnp.zeros_like(zbuf)

        def zero_copy(row0):
            return pltpu.make_async_copy(zbuf, xb_hbm.at[_rows(row0, MOE_BLOCK)], sem_z)

        def each_zero_block(fn):
            for e in range(N_EXPERTS):
                @pl.when(tail_ref[e] >= 0)
                def _():
                    fn(zero_copy(tail_ref[e]))

            def unused(b, carry):
                fn(zero_copy(b * MOE_BLOCK))
                return carry
            lax.fori_loop(nu_ref[0], n_blocks, unused, 0)

        each_zero_block(lambda c: c.start())
        each_zero_block(lambda c: c.wait())

    def run_copy(src, dst, size):
        return pltpu.make_async_copy(pk_ref.at[cur, _rows(src, size)], xb_hbm.at[_rows(dst, size)], sem.at[cur])
    _for_each_run_piece(i, len_ref, src_ref, dst_ref, run_copy, lambda c: c.start())

    def drain(which):
        pltpu.make_async_copy(pk_ref.at[which], xb_hbm.at[_rows(0, 2 * tb)], sem.at[which]).wait()

    @pl.when(i > 0)
    def _():
        drain(1 - cur)

    @pl.when(i == pl.num_programs(0) - 1)
    def _():
        drain(cur)


def _dispatch(h, local, runs, tail, n_used, n_blocks):
    n_tok = h.shape[0]
    tb = TOKEN_BLOCK
    grid_spec = pltpu.PrefetchScalarGridSpec(
        num_scalar_prefetch=5,
        grid=(n_tok // tb,),
        in_specs=[pl.BlockSpec((2, tb), lambda i, *_: (0, i)),
                  pl.BlockSpec((tb, D_MODEL), lambda i, *_: (i, 0))],
        out_specs=pl.BlockSpec(memory_space=pl.ANY),
        scratch_shapes=[pltpu.VMEM((MOE_BLOCK * ROW_SUB, LANES), F32),
                        pltpu.VMEM((2, 2 * tb * ROW_SUB, LANES), F32),
                        pltpu.SemaphoreType.DMA, pltpu.SemaphoreType.DMA((2,))],
    )
    return pl.pallas_call(
        functools.partial(_dispatch_kernel, n_blocks=n_blocks),
        grid_spec=grid_spec,
        out_shape=jax.ShapeDtypeStruct((n_blocks * MOE_BLOCK * ROW_SUB, LANES), F32),
        compiler_params=pltpu.CompilerParams(dimension_semantics=("arbitrary",),
                                             vmem_limit_bytes=VMEM_LIMIT),
        name="moe_dispatch",
    )(tail, n_used, *runs, local, h)


def _expert_kernel(be_ref, nu_ref, x_ref, wg_ref, wu_ref, wd_ref, y_ref, wgu_t, wd_t, yt_ref):
    i = pl.program_id(0)
    used = i < nu_ref[0]

    @pl.when(used & ((i == 0) | (be_ref[i] != be_ref[jnp.maximum(i - 1, 0)])))
    def _():
        wgu_t[0:D_EXPERT, :] = wg_ref[...].T.astype(BF16)
        wgu_t[D_EXPERT:, :] = wu_ref[...].T.astype(BF16)
        wd_t[...] = wd_ref[...].T.astype(BF16)

    @pl.when(used)
    def _():
        sub = range(MOE_BLOCK // MOE_SUB)
        gus = []
        for c in sub:
            xb = _load_row_tiles(x_ref, c * MOE_SUB, MOE_SUB).astype(BF16)
            gus.append(lax.dot_general(wgu_t[...], xb, (((1,), (1,)), ((), ())),
                                       preferred_element_type=F32))
        hids = []
        for c in sub:
            gt = gus[c][0:D_EXPERT]
            hids.append((gt * (1.0 / (1.0 + jnp.exp(-gt))) * gus[c][D_EXPERT:]).astype(BF16))
        for c in sub:
            yt_ref[c] = jnp.dot(wd_t[...], hids[c], preferred_element_type=F32)
        for c in sub:
            _store_row_tiles(y_ref, c * MOE_SUB, yt_ref[c].T)

    @pl.when(jnp.logical_not(used))
    def _():
        y_ref[...] = jnp.zeros_like(y_ref)


def _experts(xb, block_e, n_used, wg, wu, wd):
    n_blocks = xb.shape[0] // (MOE_BLOCK * ROW_SUB)
    last = lambda i, nu: jnp.minimum(i, nu[0] - 1)
    grid_spec = pltpu.PrefetchScalarGridSpec(
        num_scalar_prefetch=2,
        grid=(n_blocks,),
        in_specs=[pl.BlockSpec((MOE_BLOCK * ROW_SUB, LANES), lambda i, be, nu: (last(i, nu), 0)),
                  pl.BlockSpec((None, D_MODEL, D_EXPERT), lambda i, be, nu: (be[last(i, nu)], 0, 0)),
                  pl.BlockSpec((None, D_MODEL, D_EXPERT), lambda i, be, nu: (be[last(i, nu)], 0, 0)),
                  pl.BlockSpec((None, D_EXPERT, D_MODEL), lambda i, be, nu: (be[last(i, nu)], 0, 0))],
        out_specs=pl.BlockSpec((MOE_BLOCK * ROW_SUB, LANES), lambda i, be, nu: (i, 0)),
        scratch_shapes=[pltpu.VMEM((2 * D_EXPERT, D_MODEL), BF16), pltpu.VMEM((D_MODEL, D_EXPERT), BF16),
                        pltpu.VMEM((MOE_BLOCK // MOE_SUB, D_MODEL, MOE_SUB), F32)],
    )
    return pl.pallas_call(
        _expert_kernel,
        grid_spec=grid_spec,
        out_shape=jax.ShapeDtypeStruct(xb.shape, F32),
        compiler_params=pltpu.CompilerParams(dimension_semantics=("arbitrary",),
                                             vmem_limit_bytes=VMEM_LIMIT),
        name="moe_experts",
    )(block_e, n_used, xb, wg, wu, wd)


def _combine_kernel(len_ref, src_ref, dst_ref, local_ref, gate_ref, h_ref, g2_ref, b2_ref, yb_hbm, o_ref,
                    buf, sem):
    i = pl.program_id(0)
    n = pl.num_programs(0)
    tb = h_ref.shape[0]
    cur = i % 2

    def fetch(blk, which):
        def run_copy(src, dst, size):
            return pltpu.make_async_copy(yb_hbm.at[_rows(dst, size)], buf.at[which, _rows(src, size)], sem.at[which])
        _for_each_run_piece(blk, len_ref, src_ref, dst_ref, run_copy, lambda c: c.start())

    def drain(which):
        pltpu.make_async_copy(yb_hbm.at[_rows(0, 2 * tb)], buf.at[which], sem.at[which]).wait()

    @pl.when(i == 0)
    def _():
        fetch(i, 0)

    drain(cur)
    fetch(jnp.minimum(i + 1, n - 1), 1 - cur)
    y = _load_row_tiles(buf.at[cur], 0, 2 * tb).astype(BF16)
    pos = lax.broadcasted_iota(jnp.int32, (tb, 2 * tb), 1)
    local = local_ref[...]
    gate = gate_ref[...]
    mix = (jnp.where(pos == local[:, 0:1], gate[:, 0:1], 0.0)
           + jnp.where(pos == local[:, 1:2], gate[:, 1:2], 0.0))
    mix_hi = mix.astype(BF16)
    mix_lo = (mix - mix_hi.astype(F32)).astype(BF16)
    moe = (jnp.dot(mix_hi, y, preferred_element_type=F32) + jnp.dot(mix_lo, y, preferred_element_type=F32))
    o_ref[...] = _layer_norm_rows(DEEPNORM_ALPHA * h_ref[...] + moe, g2_ref[...], b2_ref[...])

    @pl.when(i == n - 1)
    def _():
        drain(1 - cur)


def _combine(h, yb, local_t, runs, gate, g2_row, b2_row):
    n_tok = h.shape[0]
    tb = TOKEN_BLOCK
    row_spec = lambda w: pl.BlockSpec((tb, w), lambda i, *_: (i, 0))
    full = lambda shape: pl.BlockSpec(shape, lambda i, *_: (0,) * len(shape))
    grid_spec = pltpu.PrefetchScalarGridSpec(
        num_scalar_prefetch=3,
        grid=(n_tok // tb,),
        in_specs=[row_spec(2), row_spec(2), row_spec(D_MODEL), full((1, D_MODEL)), full((1, D_MODEL)),
                  pl.BlockSpec(memory_space=pl.ANY)],
        out_specs=row_spec(D_MODEL),
        scratch_shapes=[pltpu.VMEM((2, 2 * tb * ROW_SUB, LANES), F32), pltpu.SemaphoreType.DMA((2,))],
    )
    return pl.pallas_call(
        _combine_kernel,
        grid_spec=grid_spec,
        out_shape=jax.ShapeDtypeStruct((n_tok, D_MODEL), F32),
        compiler_params=pltpu.CompilerParams(dimension_semantics=("arbitrary",),
                                             vmem_limit_bytes=VMEM_LIMIT),
        name="combine_ln",
    )(*runs, local_t, gate, h, g2_row, b2_row, yb)


def kernel(x, w_in, b_fgate, s5_lambda_re, s5_lambda_im, s5_log_step, s5_b_re, s5_b_im, s5_c_re, s5_c_im, s5_d, w_glu, b_glu, attn_norm_g, ssm_norm_g, w_out, ln1_g, ln1_b, w_router_group, b_router_group, w_router_expert, b_router_expert, w_gate, w_up, w_down, ln2_g, ln2_b):
    bsz, seq, d = x.shape
    n_tok = bsz * seq
    a = ATTN_WIDTH
    for l in range(DEPTH):
        x2 = x.reshape(n_tok, d)
        w = w_in[l]
        f0 = 3 * a
        w_cat = jnp.concatenate(
            [w[:, :a] * (HEAD_DIM ** -0.5 * LOG2E), w[:, a:2 * a], w[:, f0 + ATTN_HEADS:],
             jnp.pad(w[:, f0:f0 + ATTN_HEADS], ((0, 0), (0, LANES - ATTN_HEADS)))], axis=1).astype(BF16)
        wvt = w[:, 2 * a:f0].T.astype(BF16)
        bf_row = jnp.pad(b_fgate[l], (0, LANES - ATTN_HEADS)).reshape(1, LANES)
        q, k, vt, u, cum = _inproj(x2, w_cat, wvt, bf_row, seq)

        cum_g = cum.reshape(bsz, seq, ATTN_HEADS // ATTN_HEADS_PER_STEP, ATTN_HEADS_PER_STEP).transpose(0, 2, 1, 3)
        attn = _attention(q.reshape(bsz, seq, a), k.reshape(bsz, seq, a), vt, cum_g)

        bw, cw, ar_s, ai_s = _s5_weights(s5_lambda_re[l], s5_lambda_im[l], s5_log_step[l],
                                         s5_b_re[l], s5_b_im[l], s5_c_re[l], s5_c_im[l])
        u_tb = u.reshape(bsz, seq, SSM_WIDTH).transpose(1, 0, 2)
        ssm_tb = _s5(u_tb, bw, cw, ar_s, ai_s, s5_d[l].reshape(1, SSM_WIDTH), w_glu[l].astype(BF16),
                     b_glu[l].reshape(1, SSM_WIDTH), ssm_norm_g[l].reshape(1, SSM_WIDTH))
        ssm2 = ssm_tb.transpose(1, 0, 2).reshape(n_tok, SSM_WIDTH)

        n_route = N_EXPERT_GROUPS + N_EXPERTS
        wrt = jnp.pad(jnp.concatenate([w_router_group[l], w_router_expert[l]], axis=1).T,
                      ((0, LANES - n_route), (0, 0))).astype(BF16)
        brt = jnp.pad(jnp.concatenate([b_router_group[l], b_router_expert[l]]),
                      (0, ROUTER_ROWS - n_route)).reshape(ROUTER_ROWS, 1)
        h, eid, gate, rank, cnt_col = _outproj(
            attn.reshape(n_tok, a), ssm2, x2, w_out[l].astype(BF16), attn_norm_g[l].reshape(1, a),
            ln1_g[l].reshape(1, d), ln1_b[l].reshape(1, d), wrt, brt)

        local, runs, block_e, n_used, tail, n_blocks = _dispatch_plan(eid, rank, cnt_col, n_tok)
        xb = _dispatch(h, local, runs, tail, n_used, n_blocks)
        yb = _experts(xb, block_e, n_used, w_gate[l], w_up[l], w_down[l])
        x = _combine(h, yb, local.T, runs, gate.T, ln2_g[l].reshape(1, d),
                     ln2_b[l].reshape(1, d)).reshape(bsz, seq, d)
    return x
```

```python
import functools
import math

import jax
import jax.numpy as jnp
from jax import lax
from jax.experimental import pallas as pl
from jax.experimental.pallas import tpu as pltpu

F32 = jnp.float32
BF16 = jnp.bfloat16

D_MODEL = 1024
HEAD_DIM = 64
ATTN_WIDTH = 512
ATTN_HEADS = 8
SSM_WIDTH = 512
SSM_GROUP = 16
SSM_GROUPS = 32
SSM_STATE = 64
N_EXPERT_GROUPS = 4
EXPERTS_PER_GROUP = 8
N_EXPERTS = 32
D_EXPERT = 512
MOE_BLOCK = 512
MOE_SUB = 256
TOKEN_BLOCK = 256
RUN_BITS = TOKEN_BLOCK.bit_length()
SHORT_RUN_BITS = 6
DEPTH = 1
DEEPNORM_ALPHA = (2.0 * DEPTH) ** 0.25
LN_EPS = 1e-5
RMS_EPS = 1e-6

LANES = 128
NEG_BIG = -1e30
LOG2E = math.log2(math.e)
ONES_ROWS = 16
VT_ROWS = ATTN_HEADS * (HEAD_DIM + ONES_ROWS)
VMEM_LIMIT = 48 * 1024 * 1024

ROW_BLOCK = 512
ATTN_BLOCK = 256
ATTN_HEADS_PER_STEP = 8
S5_TIME_BLOCK = 64
S5_SLABS = 4
ROUTER_ROWS = 40
SLAB_CH = SSM_WIDTH // S5_SLABS
SLAB_STATE = SSM_GROUPS // S5_SLABS * SSM_STATE


def _split3(c):
    hi = c.astype(BF16)
    r1 = c - hi.astype(F32)
    mid = r1.astype(BF16)
    lo = (r1 - mid.astype(F32)).astype(BF16)
    return hi, mid, lo


def _inproj_kernel(x_ref, w_ref, wvt_ref, bf_ref, q_ref, k_ref, vt_ref, u_ref, cum_ref, carry_ref, *,
                   blocks_per_seq):
    i = pl.program_id(0)

    @pl.when(i % blocks_per_seq == 0)
    def _():
        carry_ref[...] = jnp.zeros_like(carry_ref)

    tm = x_ref.shape[0]
    xb = x_ref[...].astype(BF16)
    z = jnp.dot(xb, w_ref[...], preferred_element_type=F32)
    a = ATTN_WIDTH
    q_ref[...] = z[:, :a].astype(BF16)
    k_ref[...] = z[:, a:2 * a].astype(BF16)
    u_ref[...] = z[:, 2 * a:2 * a + SSM_WIDTH]
    vt = lax.dot_general(wvt_ref[...], xb, (((1,), (1,)), ((), ())), preferred_element_type=F32).astype(BF16)
    ones_rows = jnp.ones((ONES_ROWS, tm), BF16)
    vt = jnp.concatenate([piece for h in range(ATTN_HEADS)
                          for piece in (vt[HEAD_DIM * h:HEAD_DIM * (h + 1)], ones_rows)], axis=0)
    for jj in range(tm // ATTN_BLOCK):
        vt_ref[jj] = vt[:, jj * ATTN_BLOCK:(jj + 1) * ATTN_BLOCK]
    f = z[:, 2 * a + SSM_WIDTH:] + bf_ref[...]
    logf = jnp.minimum(f, 0.0) - jnp.log1p(jnp.exp(-jnp.abs(f)))
    row = lax.broadcasted_iota(jnp.int32, (tm, tm), 0)
    col = lax.broadcasted_iota(jnp.int32, (tm, tm), 1)
    tri = (row >= col).astype(BF16)
    hi, mid, lo = _split3(logf)
    cum = (jnp.dot(tri, hi, preferred_element_type=F32)
           + jnp.dot(tri, mid, preferred_element_type=F32)
           + jnp.dot(tri, lo, preferred_element_type=F32)) + carry_ref[...]
    carry_ref[...] = cum[tm - 1:tm, :]
    cum_ref[...] = cum[:, :ATTN_HEADS] * LOG2E


def _inproj(x2, w_cat, wvt, bf_row, seq):
    n_tok = x2.shape[0]
    tm = ROW_BLOCK
    assert seq % tm == 0 and n_tok % tm == 0 and tm % ATTN_BLOCK == 0
    bsz = n_tok // seq
    bps = seq // tm
    ncol = w_cat.shape[1]
    vblk = tm // ATTN_BLOCK
    row_spec = lambda w: pl.BlockSpec((tm, w), lambda i: (i, 0))
    return pl.pallas_call(
        functools.partial(_inproj_kernel, blocks_per_seq=bps),
        grid=(n_tok // tm,),
        in_specs=[row_spec(D_MODEL),
                  pl.BlockSpec((D_MODEL, ncol), lambda i: (0, 0)),
                  pl.BlockSpec((ATTN_WIDTH, D_MODEL), lambda i: (0, 0)),
                  pl.BlockSpec((1, LANES), lambda i: (0, 0))],
        out_specs=[row_spec(ATTN_WIDTH), row_spec(ATTN_WIDTH),
                   pl.BlockSpec((None, vblk, VT_ROWS, ATTN_BLOCK), lambda i: (i // bps, i % bps, 0, 0)),
                   row_spec(SSM_WIDTH), row_spec(ATTN_HEADS)],
        out_shape=[jax.ShapeDtypeStruct((n_tok, ATTN_WIDTH), BF16)] * 2
                  + [jax.ShapeDtypeStruct((bsz, seq // ATTN_BLOCK, VT_ROWS, ATTN_BLOCK), BF16),
                     jax.ShapeDtypeStruct((n_tok, SSM_WIDTH), F32),
                     jax.ShapeDtypeStruct((n_tok, ATTN_HEADS), F32)],
        scratch_shapes=[pltpu.VMEM((1, LANES), F32)],
        compiler_params=pltpu.CompilerParams(dimension_semantics=("arbitrary",),
                                             vmem_limit_bytes=VMEM_LIMIT),
        name="inproj",
    )(x2, w_cat, wvt, bf_row)


def _bias_lanes_q(c):
    hi, mid, lo = (t.astype(F32) for t in _split3(c))
    lane = lax.broadcasted_iota(jnp.int32, (c.shape[0], LANES), 1)
    sel = jnp.where(lane == 0, hi, jnp.where(lane == 1, mid, jnp.where(lane == 2, lo,
                    jnp.where(lane < 6, 1.0, 0.0))))
    return sel.astype(BF16)


def _bias_lanes_k(c):
    hi, mid, lo = (t.astype(F32) for t in _split3(c))
    lane = lax.broadcasted_iota(jnp.int32, (c.shape[0], LANES), 1)
    sel = jnp.where(lane < 3, 1.0, jnp.where(lane == 3, -hi, jnp.where(lane == 4, -mid,
                    jnp.where(lane == 5, -lo, 0.0))))
    return sel.astype(BF16)


def _attn_kernel(q_ref, k_ref, vt_ref, cum_ref, o_ref, extk_ref, *, tq, seq, nh):
    qi = pl.program_id(2)
    n_tiles = seq // tq

    @pl.when(qi == 0)
    def _():
        def fill(j, carry):
            r0 = pl.multiple_of(j * tq, tq)
            c = cum_ref[pl.ds(r0, tq), :]
            for h in range(nh):
                extk_ref[h, pl.ds(r0, tq), :] = _bias_lanes_k(c[:, h:h + 1])
            return carry
        lax.fori_loop(0, n_tiles, fill, 0)

    q0 = pl.multiple_of(qi * tq, tq)
    cq = cum_ref[pl.ds(q0, tq), :]
    lane = lax.broadcasted_iota(jnp.int32, (tq, LANES), 1)
    q_augs = []
    for h in range(nh):
        qb = q_ref[:, LANES * (h // 2):LANES * (h // 2 + 1)]
        head_lanes = (lane < HEAD_DIM) if h % 2 == 0 else (lane >= HEAD_DIM)
        qh = jnp.where(head_lanes, qb, jnp.zeros((), BF16))
        q_augs.append(jnp.concatenate([qh, _bias_lanes_q(cq[:, h:h + 1])], axis=1))
    key_i = lax.broadcasted_iota(jnp.int32, (tq, tq), 0)
    qry_i = lax.broadcasted_iota(jnp.int32, (tq, tq), 1)
    causal = key_i <= qry_i
    vrows = HEAD_DIM + ONES_ROWS

    def scores_of(j):
        k0 = pl.multiple_of(j * tq, tq)
        out = []
        for h in range(nh):
            kj = k_ref[pl.ds(k0, tq), LANES * (h // 2):LANES * (h // 2 + 1)]
            k_aug = jnp.concatenate([kj, extk_ref[h, pl.ds(k0, tq), :]], axis=1)
            out.append(lax.dot_general(k_aug, q_augs[h], (((1,), (1,)), ((), ())), preferred_element_type=F32))
        return tuple(out)

    def absorb(j, scores, carry):
        stats = []
        for h in range(nh):
            m = carry[h][0]
            m_new = jnp.maximum(m, jnp.max(scores[h], axis=0, keepdims=True))
            stats.append((m_new, jnp.exp2(m - m_new), jnp.exp2(scores[h] - m_new).astype(BF16)))
        new = []
        for h in range(nh):
            m_new, corr, p = stats[h]
            res = jnp.dot(vt_ref[j, vrows * h:vrows * (h + 1), :], p, preferred_element_type=F32)
            acc = corr * carry[h][2] + res[0:HEAD_DIM, :]
            l = corr * carry[h][1] + res[HEAD_DIM:HEAD_DIM + 1, :]
            new.append((m_new, l, acc))
        return tuple(new)

    def step(j, carry):
        return absorb(j, scores_of(j), carry)

    init1 = (jnp.full((1, tq), NEG_BIG, F32), jnp.zeros((1, tq), F32), jnp.zeros((HEAD_DIM, tq), F32))
    carry = lax.fori_loop(0, qi, step, (init1,) * nh)
    carry = absorb(qi, tuple(jnp.where(causal, s, NEG_BIG) for s in scores_of(qi)), carry)
    out_t = jnp.concatenate([c[2] / c[1] for c in carry], axis=0)
    o_ref[...] = out_t.T


def _attention(q, k, vt, cum_g):
    bsz, seq, _ = q.shape
    tq = ATTN_BLOCK
    nh = ATTN_HEADS_PER_STEP
    assert seq % tq == 0 and ATTN_HEADS % nh == 0 and nh % 2 == 0
    w = nh * HEAD_DIM
    return pl.pallas_call(
        functools.partial(_attn_kernel, tq=tq, seq=seq, nh=nh),
        grid=(bsz, ATTN_HEADS // nh, seq // tq),
        in_specs=[pl.BlockSpec((None, tq, w), lambda b, g, i: (b, i, g)),
                  pl.BlockSpec((None, seq, w), lambda b, g, i: (b, 0, g)),
                  pl.BlockSpec((None, seq // tq, nh * (HEAD_DIM + ONES_ROWS), tq), lambda b, g, i: (b, 0, g, 0)),
                  pl.BlockSpec((None, None, seq, nh), lambda b, g, i: (b, g, 0, 0))],
        out_specs=pl.BlockSpec((None, tq, w), lambda b, g, i: (b, i, g)),
        out_shape=jax.ShapeDtypeStruct((bsz, seq, ATTN_WIDTH), F32),
        scratch_shapes=[pltpu.VMEM((nh, seq, LANES), BF16)],
        compiler_params=pltpu.CompilerParams(dimension_semantics=("parallel", "parallel", "arbitrary"),
                                             vmem_limit_bytes=VMEM_LIMIT),
        name="fox_attention",
    )(q, k, vt, cum_g)


def _s5_kernel(u_ref, bw_ref, cw_ref, ar_ref, ai_ref, d_ref, wglu_ref, bglu_ref, g_ref, o_ref,
               state_ref, bu_ref, *, tc, bsz):
    @pl.when(pl.program_id(0) == 0)
    def _():
        state_ref[...] = jnp.zeros_like(state_ref)

    rows = tc * bsz
    ns = SLAB_STATE
    u = u_ref[...].reshape(rows, SSM_WIDTH)
    ub = u.astype(BF16)
    for i in range(S5_SLABS):
        bu_ref[i] = jnp.dot(ub[:, SLAB_CH * i:SLAB_CH * (i + 1)], bw_ref[i], preferred_element_type=F32)
    for i in range(S5_SLABS):
        ar = jnp.broadcast_to(ar_ref[i], (bsz, ns))
        ai = jnp.broadcast_to(ai_ref[i], (bsz, ns))
        xr = state_ref[i, :, 0:ns]
        xi = state_ref[i, :, ns:2 * ns]
        for t in range(tc):
            r0 = t * bsz
            nr = ar * xr - ai * xi + bu_ref[i, r0:r0 + bsz, 0:ns]
            ni = ar * xi + ai * xr + bu_ref[i, r0:r0 + bsz, ns:2 * ns]
            bu_ref[i, r0:r0 + bsz, 0:ns] = nr
            bu_ref[i, r0:r0 + bsz, ns:2 * ns] = ni
            xr, xi = nr, ni
        state_ref[i, :, 0:ns] = xr
        state_ref[i, :, ns:2 * ns] = xi
    ys = [jnp.dot(bu_ref[i].astype(BF16), cw_ref[i], preferred_element_type=F32) for i in range(S5_SLABS)]
    y = jnp.concatenate(ys, axis=1) + d_ref[...] * u
    y = 0.5 * y * (1.0 + jnp.tanh(math.sqrt(2.0 / math.pi) * (y + 0.044715 * (y * y * y))))
    z = jnp.dot(y.astype(BF16), wglu_ref[...], preferred_element_type=F32) + bglu_ref[...]
    y = y * (1.0 / (1.0 + jnp.exp(-z)))
    y = y * lax.rsqrt(jnp.mean(y * y, axis=-1, keepdims=True) + RMS_EPS) * g_ref[...]
    o_ref[...] = y.astype(BF16).reshape(tc, bsz, SSM_WIDTH)


def _s5_weights(lam_re, lam_im, log_step, b_re, b_im, c_re, c_im):
    delta = jnp.exp(log_step)[:, None]
    mag = jnp.exp(lam_re * delta)
    ar = mag * jnp.cos(lam_im * delta)
    ai = mag * jnp.sin(lam_im * delta)
    den = lam_re * lam_re + lam_im * lam_im
    num_re = ar - 1.0
    coef_re = (num_re * lam_re + ai * lam_im) / den
    coef_im = (ai * lam_re - num_re * lam_im) / den
    bbar_re = coef_re[..., None] * b_re - coef_im[..., None] * b_im
    bbar_im = coef_re[..., None] * b_im + coef_im[..., None] * b_re
    gs = SSM_GROUPS // S5_SLABS
    eye = jnp.eye(gs, dtype=F32)

    def b_slab(bb):
        bb = bb.reshape(S5_SLABS, gs, SSM_STATE, SSM_GROUP)
        w = jnp.einsum('ab,sapc->sacbp', eye, bb)
        return w.reshape(S5_SLABS, gs * SSM_GROUP, gs * SSM_STATE)

    def c_slab(cc):
        cc = cc.reshape(S5_SLABS, gs, SSM_GROUP, SSM_STATE)
        w = jnp.einsum('ab,sacp->sbpac', eye, cc)
        return w.reshape(S5_SLABS, gs * SSM_STATE, gs * SSM_GROUP)

    bw = jnp.concatenate([b_slab(bbar_re), b_slab(bbar_im)], axis=2).astype(BF16)
    cw = jnp.concatenate([c_slab(c_re), -c_slab(c_im)], axis=1).astype(BF16)
    ar_s = ar.reshape(S5_SLABS, 1, SLAB_STATE)
    ai_s = ai.reshape(S5_SLABS, 1, SLAB_STATE)
    return bw, cw, ar_s, ai_s


def _s5(u_tb, bw, cw, ar_s, ai_s, d_row, w_glu, b_glu_row, g_row):
    seq, bsz, _ = u_tb.shape
    tc = S5_TIME_BLOCK
    assert seq % tc == 0 and bsz % 16 == 0
    full = lambda shape: pl.BlockSpec(shape, lambda t: (0,) * len(shape))
    return pl.pallas_call(
        functools.partial(_s5_kernel, tc=tc, bsz=bsz),
        grid=(seq // tc,),
        in_specs=[pl.BlockSpec((tc, bsz, SSM_WIDTH), lambda t: (t, 0, 0)),
                  full(bw.shape), full(cw.shape), full(ar_s.shape), full(ai_s.shape),
                  full((1, SSM_WIDTH)), full((SSM_WIDTH, SSM_WIDTH)), full((1, SSM_WIDTH)),
                  full((1, SSM_WIDTH))],
        out_specs=pl.BlockSpec((tc, bsz, SSM_WIDTH), lambda t: (t, 0, 0)),
        out_shape=jax.ShapeDtypeStruct((seq, bsz, SSM_WIDTH), BF16),
        scratch_shapes=[pltpu.VMEM((S5_SLABS, bsz, 2 * SLAB_STATE), F32),
                        pltpu.VMEM((S5_SLABS, tc * bsz, 2 * SLAB_STATE), F32)],
        compiler_params=pltpu.CompilerParams(dimension_semantics=("arbitrary",),
                                             vmem_limit_bytes=VMEM_LIMIT),
        name="s5_branch",
    )(u_tb, bw, cw, ar_s, ai_s, d_row, w_glu, b_glu_row, g_row)


def _layer_norm_rows(v, g, b):
    mu = jnp.mean(v, axis=-1, keepdims=True)
    c = v - mu
    var = jnp.mean(c * c, axis=-1, keepdims=True)
    return c * lax.rsqrt(var + LN_EPS) * g + b


def _outproj_kernel(attn_ref, ssm_ref, x_ref, wo_ref, ag_ref, g1_ref, b1_ref, wrt_ref, brt_ref, triu_ref,
                    h_ref, eid_ref, gate_ref, rank_ref, cnt_ref):
    a = attn_ref[...]
    a = a * lax.rsqrt(jnp.mean(a * a, axis=-1, keepdims=True) + RMS_EPS) * ag_ref[...]
    mix = (jnp.dot(a.astype(BF16), wo_ref[0:ATTN_WIDTH, :], preferred_element_type=F32)
           + jnp.dot(ssm_ref[...], wo_ref[ATTN_WIDTH:, :], preferred_element_type=F32))
    h = _layer_norm_rows(DEEPNORM_ALPHA * x_ref[...] + mix, g1_ref[...], b1_ref[...])
    h_ref[...] = h

    logits = lax.dot_general(wrt_ref[...], h.astype(BF16), (((1,), (1,)), ((), ())),
                             preferred_element_type=F32)[:ROUTER_ROWS] + brt_ref[...]
    tm = logits.shape[1]
    row = lax.broadcasted_iota(jnp.int32, (ROUTER_ROWS, tm), 0).astype(F32)
    no_row = float(LANES)
    is_g = row < N_EXPERT_GROUPS
    gl = jnp.where(is_g, logits, NEG_BIG)
    gmax = jnp.max(gl, axis=0, keepdims=True)
    g_idx = jnp.min(jnp.where(is_g & (gl == gmax), row, no_row), axis=0, keepdims=True)
    g_p = 1.0 / jnp.sum(jnp.where(is_g, jnp.exp(gl - gmax), 0.0), axis=0, keepdims=True)
    first = N_EXPERT_GROUPS + g_idx * EXPERTS_PER_GROUP
    in_grp = (row >= first) & (row < first + EXPERTS_PER_GROUP)
    el = jnp.where(in_grp, logits, NEG_BIG)
    m1 = jnp.max(el, axis=0, keepdims=True)
    i1 = jnp.min(jnp.where(in_grp & (el == m1), row, no_row), axis=0, keepdims=True)
    rest = in_grp & (row != i1)
    el2 = jnp.where(rest, logits, NEG_BIG)
    m2 = jnp.max(el2, axis=0, keepdims=True)
    i2 = jnp.min(jnp.where(rest & (el2 == m2), row, no_row), axis=0, keepdims=True)
    r = jnp.exp(m2 - m1)
    p1 = 1.0 / (1.0 + r)
    p2 = r / (1.0 + r)
    k2 = lax.broadcasted_iota(jnp.int32, (2, tm), 0)
    eid_ref[...] = (jnp.where(k2 == 0, i1, i2) - N_EXPERT_GROUPS).astype(jnp.int32)
    gate_ref[...] = g_p * jnp.where(k2 == 0, p1, p2)

    @pl.when(pl.program_id(0) == 0)
    def _():
        cnt_ref[...] = jnp.zeros_like(cnt_ref)
    hit1 = row == i1
    hit2 = row == i2
    onehot = jnp.where(hit1 | hit2, 1.0, 0.0)
    cnt = cnt_ref[...]
    before = jnp.dot(onehot.astype(BF16), triu_ref[...], preferred_element_type=F32) + cnt
    r1 = jnp.sum(jnp.where(hit1, before, 0.0), axis=0, keepdims=True)
    r2 = jnp.sum(jnp.where(hit2, before, 0.0), axis=0, keepdims=True)
    rank_ref[...] = jnp.where(k2 == 0, r1, r2).astype(jnp.int32)
    cnt_ref[...] = cnt + jnp.sum(onehot, axis=1, keepdims=True)


def _outproj(attn2, ssm2, x2, wo, ag_row, g1_row, b1_row, wrt, brt):
    n_tok = x2.shape[0]
    tm = ROW_BLOCK
    triu = jnp.triu(jnp.ones((tm, tm), BF16), 1)
    row_spec = lambda w: pl.BlockSpec((tm, w), lambda i: (i, 0))
    col_spec = lambda r: pl.BlockSpec((r, tm), lambda i: (0, i))
    full = lambda shape: pl.BlockSpec(shape, lambda i: (0,) * len(shape))
    return pl.pallas_call(
        _outproj_kernel,
        grid=(n_tok // tm,),
        in_specs=[row_spec(ATTN_WIDTH), row_spec(SSM_WIDTH), row_spec(D_MODEL),
                  full((D_MODEL, D_MODEL)), full((1, ATTN_WIDTH)), full((1, D_MODEL)), full((1, D_MODEL)),
                  full((LANES, D_MODEL)), full((ROUTER_ROWS, 1)), full((tm, tm))],
        out_specs=[row_spec(D_MODEL), col_spec(2), col_spec(2), col_spec(2), full((ROUTER_ROWS, 1))],
        out_shape=[jax.ShapeDtypeStruct((n_tok, D_MODEL), F32),
                   jax.ShapeDtypeStruct((2, n_tok), jnp.int32),
                   jax.ShapeDtypeStruct((2, n_tok), F32),
                   jax.ShapeDtypeStruct((2, n_tok), jnp.int32),
                   jax.ShapeDtypeStruct((ROUTER_ROWS, 1), F32)],
        compiler_params=pltpu.CompilerParams(dimension_semantics=("arbitrary",),
                                             vmem_limit_bytes=VMEM_LIMIT),
        name="outproj_ln_router",
    )(attn2, ssm2, x2, wo, ag_row, g1_row, b1_row, wrt, brt, triu)


def _dispatch_plan(eid, rank, cnt_col, n_tok):
    counts = cnt_col[N_EXPERT_GROUPS:N_EXPERT_GROUPS + N_EXPERTS, 0].astype(jnp.int32)
    padded = ((counts + MOE_BLOCK - 1) // MOE_BLOCK) * MOE_BLOCK
    pends = jnp.cumsum(padded)
    pstarts = pends - padded
    n_pad = ((2 * n_tok + N_EXPERTS * (MOE_BLOCK - 1) + MOE_BLOCK - 1) // MOE_BLOCK) * MOE_BLOCK
    n_blocks = n_pad // MOE_BLOCK
    blk0 = jnp.arange(n_blocks, dtype=jnp.int32) * MOE_BLOCK
    block_e = jnp.minimum(jnp.sum((pends[None, :] <= blk0[:, None]).astype(jnp.int32), axis=1), N_EXPERTS - 1)
    n_used = (pends[-1] // MOE_BLOCK).astype(jnp.int32).reshape(1)
    tail = jnp.where(padded > 0, pends - MOE_BLOCK, -1).astype(jnp.int32)

    tb = TOKEN_BLOCK
    nb = n_tok // tb
    onehot = eid[:, :, None] == jnp.arange(N_EXPERTS, dtype=jnp.int32)[None, None, :]
    run_len = jnp.sum(onehot.reshape(2, nb, tb, N_EXPERTS).astype(jnp.int32), axis=(0, 2))
    before = jnp.cumsum(run_len, axis=0) - run_len
    run_src = jnp.cumsum(run_len, axis=1) - run_len
    run_dst = pstarts[None, :] + before
    shift = jnp.broadcast_to((run_src - before)[:, None, :], (nb, tb, N_EXPERTS)).reshape(n_tok, N_EXPERTS)
    local = rank + jnp.sum(jnp.where(onehot, shift[None], 0), axis=-1)
    any_long = (jnp.max(run_len, axis=1) >= (1 << SHORT_RUN_BITS)).astype(jnp.int32)
    runs = (run_len.reshape(-1), run_src.reshape(-1), run_dst.reshape(-1), any_long)
    return local, runs, block_e, n_used, tail, n_blocks


ROW_SUB = D_MODEL // LANES


def _rows(start, size):
    return pl.ds(pl.multiple_of(start * ROW_SUB, ROW_SUB), size * ROW_SUB)


def _store_row_tiles(ref, row0, v):
    n = v.shape[0]
    for s in range(ROW_SUB):
        ref[pl.ds(row0 * ROW_SUB + s, n, stride=ROW_SUB), :] = v[:, LANES * s:LANES * (s + 1)]


def _load_row_tiles(ref, row0, n):
    return jnp.concatenate([ref[pl.ds(row0 * ROW_SUB + s, n, stride=ROW_SUB), :] for s in range(ROW_SUB)], axis=1)


def _for_each_run_piece(blk, len_ref, src_ref, dst_ref, bits, make_copy, act):
    for e in range(N_EXPERTS):
        idx = blk * N_EXPERTS + e
        n = len_ref[idx]
        src = src_ref[idx]
        dst = dst_ref[idx]
        for b in reversed(range(bits)):
            @pl.when((n & (1 << b)) != 0)
            def _():
                off = lax.shift_left(lax.shift_right_logical(n, b + 1), b + 1)
                act(make_copy(src + off, dst + off, 1 << b))


def _by_run_size(long_ref, blk, body):
    @pl.when(long_ref[blk] == 0)
    def _():
        body(SHORT_RUN_BITS)

    @pl.when(long_ref[blk] != 0)
    def _():
        body(RUN_BITS)


def _dispatch_kernel(tail_ref, nu_ref, len_ref, src_ref, dst_ref, long_ref, local_ref, h_ref, xb_hbm, zbuf,
                     pk_ref, sem_z, sem, *, n_blocks):
    i = pl.program_id(0)
    nb = pl.num_programs(0) - 1
    tb = h_ref.shape[0]
    cur = i % 2

    @pl.when(i == 0)
    def _():
        zbuf[...] = jnp.zeros_like(zbuf)

        def zero_copy(row0):
            return pltpu.make_async_copy(zbuf, xb_hbm.at[_rows(row0, MOE_BLOCK)], sem_z)

        def each_zero_block(fn):
            for e in range(N_EXPERTS):
                @pl.when(tail_ref[e] >= 0)
                def _():
                    fn(zero_copy(tail_ref[e]))

            def unused(b, carry):
                fn(zero_copy(b * MOE_BLOCK))
                return carry
            lax.fori_loop(nu_ref[0], n_blocks, unused, 0)

        each_zero_block(lambda c: c.start())
        each_zero_block(lambda c: c.wait())

    def drain(which):
        pltpu.make_async_copy(pk_ref.at[which], xb_hbm.at[_rows(0, 2 * tb)], sem.at[which]).wait()

    @pl.when(i >= 2)
    def _():
        drain(cur)

    def group():
        pos = lax.broadcasted_iota(jnp.int32, (2 * tb, tb), 0)
        hit = (pos == local_ref[0:1, :]) | (pos == local_ref[1:2, :])
        perm = jnp.where(hit, 1.0, 0.0).astype(BF16)
        _store_row_tiles(pk_ref.at[cur], 0, jnp.dot(perm, h_ref[...].astype(BF16), preferred_element_type=F32))

    def send(bits):
        def run_copy(src, dst, size):
            return pltpu.make_async_copy(pk_ref.at[1 - cur, _rows(src, size)], xb_hbm.at[_rows(dst, size)],
                                         sem.at[1 - cur])
        _for_each_run_piece(i - 1, len_ref, src_ref, dst_ref, bits, run_copy, lambda c: c.start())

    @pl.when(i == 0)
    def _():
        group()

    @pl.when((i > 0) & (i < nb))
    def _():
        def send_and_group(bits):
            send(bits)
            group()
        _by_run_size(long_ref, i - 1, send_and_group)

    @pl.when(i == nb)
    def _():
        send(RUN_BITS)
        drain(1 - cur)


def _dispatch(h, local, runs, tail, n_used, n_blocks):
    n_tok = h.shape[0]
    tb = TOKEN_BLOCK
    nb = n_tok // tb
    grid_spec = pltpu.PrefetchScalarGridSpec(
        num_scalar_prefetch=6,
        grid=(nb + 1,),
        in_specs=[pl.BlockSpec((2, tb), lambda i, *_: (0, jnp.minimum(i, nb - 1))),
                  pl.BlockSpec((tb, D_MODEL), lambda i, *_: (jnp.minimum(i, nb - 1), 0))],
        out_specs=pl.BlockSpec(memory_space=pl.ANY),
        scratch_shapes=[pltpu.VMEM((MOE_BLOCK * ROW_SUB, LANES), F32),
                        pltpu.VMEM((2, 2 * tb * ROW_SUB, LANES), F32),
                        pltpu.SemaphoreType.DMA, pltpu.SemaphoreType.DMA((2,))],
    )
    return pl.pallas_call(
        functools.partial(_dispatch_kernel, n_blocks=n_blocks),
        grid_spec=grid_spec,
        out_shape=jax.ShapeDtypeStruct((n_blocks * MOE_BLOCK * ROW_SUB, LANES), F32),
        compiler_params=pltpu.CompilerParams(dimension_semantics=("arbitrary",),
                                             vmem_limit_bytes=VMEM_LIMIT),
        name="moe_dispatch",
    )(tail, n_used, *runs, local, h)


def _expert_kernel(be_ref, nu_ref, x_ref, wg_ref, wu_ref, wd_ref, y_ref, wgu_t, wd_t, yt_ref):
    i = pl.program_id(0)
    used = i < nu_ref[0]

    @pl.when(used & ((i == 0) | (be_ref[i] != be_ref[jnp.maximum(i - 1, 0)])))
    def _():
        wgu_t[0:D_EXPERT, :] = wg_ref[...].T.astype(BF16)
        wgu_t[D_EXPERT:, :] = wu_ref[...].T.astype(BF16)
        wd_t[...] = wd_ref[...].T.astype(BF16)

    @pl.when(used)
    def _():
        sub = range(MOE_BLOCK // MOE_SUB)
        gus = []
        for c in sub:
            xb = _load_row_tiles(x_ref, c * MOE_SUB, MOE_SUB).astype(BF16)
            gus.append(lax.dot_general(wgu_t[...], xb, (((1,), (1,)), ((), ())),
                                       preferred_element_type=F32))
        hids = []
        for c in sub:
            gt = gus[c][0:D_EXPERT]
            hids.append((gt * (1.0 / (1.0 + jnp.exp(-gt))) * gus[c][D_EXPERT:]).astype(BF16))
        for c in sub:
            yt_ref[c] = jnp.dot(wd_t[...], hids[c], preferred_element_type=F32)
        for c in sub:
            _store_row_tiles(y_ref, c * MOE_SUB, yt_ref[c].T)

    @pl.when(jnp.logical_not(used))
    def _():
        y_ref[...] = jnp.zeros_like(y_ref)


def _experts(xb, block_e, n_used, wg, wu, wd):
    n_blocks = xb.shape[0] // (MOE_BLOCK * ROW_SUB)
    last = lambda i, nu: jnp.minimum(i, nu[0] - 1)
    grid_spec = pltpu.PrefetchScalarGridSpec(
        num_scalar_prefetch=2,
        grid=(n_blocks,),
        in_specs=[pl.BlockSpec((MOE_BLOCK * ROW_SUB, LANES), lambda i, be, nu: (last(i, nu), 0)),
                  pl.BlockSpec((None, D_MODEL, D_EXPERT), lambda i, be, nu: (be[last(i, nu)], 0, 0)),
                  pl.BlockSpec((None, D_MODEL, D_EXPERT), lambda i, be, nu: (be[last(i, nu)], 0, 0)),
                  pl.BlockSpec((None, D_EXPERT, D_MODEL), lambda i, be, nu: (be[last(i, nu)], 0, 0))],
        out_specs=pl.BlockSpec((MOE_BLOCK * ROW_SUB, LANES), lambda i, be, nu: (i, 0)),
        scratch_shapes=[pltpu.VMEM((2 * D_EXPERT, D_MODEL), BF16), pltpu.VMEM((D_MODEL, D_EXPERT), BF16),
                        pltpu.VMEM((MOE_BLOCK // MOE_SUB, D_MODEL, MOE_SUB), F32)],
    )
    return pl.pallas_call(
        _expert_kernel,
        grid_spec=grid_spec,
        out_shape=jax.ShapeDtypeStruct(xb.shape, F32),
        compiler_params=pltpu.CompilerParams(dimension_semantics=("arbitrary",),
                                             vmem_limit_bytes=VMEM_LIMIT),
        name="moe_experts",
    )(block_e, n_used, xb, wg, wu, wd)


def _combine_kernel(len_ref, src_ref, dst_ref, long_ref, local_ref, gate_ref, h_ref, g2_ref, b2_ref, yb_hbm,
                    o_ref, buf, sem):
    i = pl.program_id(0)
    n = pl.num_programs(0)
    tb = h_ref.shape[0]
    cur = i % 2

    def fetch(blk, which, bits):
        def run_copy(src, dst, size):
            return pltpu.make_async_copy(yb_hbm.at[_rows(dst, size)], buf.at[which, _rows(src, size)], sem.at[which])
        _for_each_run_piece(blk, len_ref, src_ref, dst_ref, bits, run_copy, lambda c: c.start())

    def drain(which):
        pltpu.make_async_copy(yb_hbm.at[_rows(0, 2 * tb)], buf.at[which], sem.at[which]).wait()

    @pl.when(i == 0)
    def _():
        fetch(i, 0, RUN_BITS)

    drain(cur)
    nxt = jnp.minimum(i + 1, n - 1)

    def fetch_next_and_combine(bits):
        fetch(nxt, 1 - cur, bits)
        y = _load_row_tiles(buf.at[cur], 0, 2 * tb).astype(BF16)
        pos = lax.broadcasted_iota(jnp.int32, (tb, 2 * tb), 1)
        local = local_ref[...]
        gate = gate_ref[...]
        mix = (jnp.where(pos == local[:, 0:1], gate[:, 0:1], 0.0)
               + jnp.where(pos == local[:, 1:2], gate[:, 1:2], 0.0))
        mix_hi = mix.astype(BF16)
        mix_lo = (mix - mix_hi.astype(F32)).astype(BF16)
        moe = (jnp.dot(mix_hi, y, preferred_element_type=F32) + jnp.dot(mix_lo, y, preferred_element_type=F32))
        o_ref[...] = _layer_norm_rows(DEEPNORM_ALPHA * h_ref[...] + moe, g2_ref[...], b2_ref[...])
    _by_run_size(long_ref, nxt, fetch_next_and_combine)

    @pl.when(i == n - 1)
    def _():
        drain(1 - cur)


def _combine(h, yb, local_t, runs, gate, g2_row, b2_row):
    n_tok = h.shape[0]
    tb = TOKEN_BLOCK
    row_spec = lambda w: pl.BlockSpec((tb, w), lambda i, *_: (i, 0))
    full = lambda shape: pl.BlockSpec(shape, lambda i, *_: (0,) * len(shape))
    grid_spec = pltpu.PrefetchScalarGridSpec(
        num_scalar_prefetch=4,
        grid=(n_tok // tb,),
        in_specs=[row_spec(2), row_spec(2), row_spec(D_MODEL), full((1, D_MODEL)), full((1, D_MODEL)),
                  pl.BlockSpec(memory_space=pl.ANY)],
        out_specs=row_spec(D_MODEL),
        scratch_shapes=[pltpu.VMEM((2, 2 * tb * ROW_SUB, LANES), F32), pltpu.SemaphoreType.DMA((2,))],
    )
    return pl.pallas_call(
        _combine_kernel,
        grid_spec=grid_spec,
        out_shape=jax.ShapeDtypeStruct((n_tok, D_MODEL), F32),
        compiler_params=pltpu.CompilerParams(dimension_semantics=("arbitrary",),
                                             vmem_limit_bytes=VMEM_LIMIT),
        name="combine_ln",
    )(*runs, local_t, gate, h, g2_row, b2_row, yb)


def kernel(x, w_in, b_fgate, s5_lambda_re, s5_lambda_im, s5_log_step, s5_b_re, s5_b_im, s5_c_re, s5_c_im, s5_d, w_glu, b_glu, attn_norm_g, ssm_norm_g, w_out, ln1_g, ln1_b, w_router_group, b_router_group, w_router_expert, b_router_expert, w_gate, w_up, w_down, ln2_g, ln2_b):
    bsz, seq, d = x.shape
    n_tok = bsz * seq
    a = ATTN_WIDTH
    for l in range(DEPTH):
        x2 = x.reshape(n_tok, d)
        w = w_in[l]
        f0 = 3 * a
        w_cat = jnp.concatenate(
            [w[:, :a] * (HEAD_DIM ** -0.5 * LOG2E), w[:, a:2 * a], w[:, f0 + ATTN_HEADS:],
             jnp.pad(w[:, f0:f0 + ATTN_HEADS], ((0, 0), (0, LANES - ATTN_HEADS)))], axis=1).astype(BF16)
        wvt = w[:, 2 * a:f0].T.astype(BF16)
        bf_row = jnp.pad(b_fgate[l], (0, LANES - ATTN_HEADS)).reshape(1, LANES)
        q, k, vt, u, cum = _inproj(x2, w_cat, wvt, bf_row, seq)

        cum_g = cum.reshape(bsz, seq, ATTN_HEADS // ATTN_HEADS_PER_STEP, ATTN_HEADS_PER_STEP).transpose(0, 2, 1, 3)
        attn = _attention(q.reshape(bsz, seq, a), k.reshape(bsz, seq, a), vt, cum_g)

        bw, cw, ar_s, ai_s = _s5_weights(s5_lambda_re[l], s5_lambda_im[l], s5_log_step[l],
                                         s5_b_re[l], s5_b_im[l], s5_c_re[l], s5_c_im[l])
        u_tb = u.reshape(bsz, seq, SSM_WIDTH).transpose(1, 0, 2)
        ssm_tb = _s5(u_tb, bw, cw, ar_s, ai_s, s5_d[l].reshape(1, SSM_WIDTH), w_glu[l].astype(BF16),
                     b_glu[l].reshape(1, SSM_WIDTH), ssm_norm_g[l].reshape(1, SSM_WIDTH))
        ssm2 = ssm_tb.transpose(1, 0, 2).reshape(n_tok, SSM_WIDTH)

        n_route = N_EXPERT_GROUPS + N_EXPERTS
        wrt = jnp.pad(jnp.concatenate([w_router_group[l], w_router_expert[l]], axis=1).T,
                      ((0, LANES - n_route), (0, 0))).astype(BF16)
        brt = jnp.pad(jnp.concatenate([b_router_group[l], b_router_expert[l]]),
                      (0, ROUTER_ROWS - n_route)).reshape(ROUTER_ROWS, 1)
        h, eid, gate, rank, cnt_col = _outproj(
            attn.reshape(n_tok, a), ssm2, x2, w_out[l].astype(BF16), attn_norm_g[l].reshape(1, a),
            ln1_g[l].reshape(1, d), ln1_b[l].reshape(1, d), wrt, brt)

        local, runs, block_e, n_used, tail, n_blocks = _dispatch_plan(eid, rank, cnt_col, n_tok)
        xb = _dispatch(h, local, runs, tail, n_used, n_blocks)
        yb = _experts(xb, block_e, n_used, w_gate[l], w_up[l], w_down[l])
        x = _combine(h, yb, local.T, runs, gate.T, ln2_g[l].reshape(1, d),
                     ln2_b[l].reshape(1, d)).reshape(bsz, seq, d)
    return x
```

```python
import functools
import math

import jax
import jax.numpy as jnp
import numpy as np
from jax import lax
from jax.experimental import pallas as pl
from jax.experimental.pallas import tpu as pltpu

F32 = jnp.float32
BF16 = jnp.bfloat16

D_MODEL = 1024
HEAD_DIM = 64
ATTN_WIDTH = 512
ATTN_HEADS = 8
SSM_WIDTH = 512
SSM_GROUP = 16
SSM_GROUPS = 32
SSM_STATE = 64
N_EXPERT_GROUPS = 4
EXPERTS_PER_GROUP = 8
N_EXPERTS = 32
D_EXPERT = 512
MOE_BLOCK = 512
MOE_SUB = 256
TOKEN_BLOCK = 256
RUN_BITS = TOKEN_BLOCK.bit_length()
SHORT_RUN_BITS = 6
DEPTH = 1
DEEPNORM_ALPHA = (2.0 * DEPTH) ** 0.25
LN_EPS = 1e-5
RMS_EPS = 1e-6

LANES = 128
NEG_BIG = -1e30
LOG2E = math.log2(math.e)
ONES_ROWS = 16
VT_ROWS = ATTN_HEADS * (HEAD_DIM + ONES_ROWS)
VMEM_LIMIT = 48 * 1024 * 1024

ROW_BLOCK = 512
ATTN_BLOCK = 256
ATTN_HEADS_PER_STEP = 8
S5_TIME_BLOCK = 64
S5_SLABS = 4
ROUTER_ROWS = 40
SLAB_CH = SSM_WIDTH // S5_SLABS
SLAB_STATE = SSM_GROUPS // S5_SLABS * SSM_STATE


def _split3(c):
    hi = c.astype(BF16)
    r1 = c - hi.astype(F32)
    mid = r1.astype(BF16)
    lo = (r1 - mid.astype(F32)).astype(BF16)
    return hi, mid, lo


def _inproj_kernel(x_ref, w_ref, wvt_ref, bf_ref, sel_ref, q_ref, k_ref, vt_ref, u_ref, eq_ref, ek_ref,
                   carry_ref, *, blocks_per_seq):
    i = pl.program_id(0)

    @pl.when(i % blocks_per_seq == 0)
    def _():
        carry_ref[...] = jnp.zeros_like(carry_ref)

    tm = x_ref.shape[0]
    xb = x_ref[...].astype(BF16)
    z = jnp.dot(xb, w_ref[...], preferred_element_type=F32)
    a = ATTN_WIDTH
    q_ref[...] = z[:, :a].astype(BF16)
    k_ref[...] = z[:, a:2 * a].astype(BF16)
    u_ref[...] = z[:, 2 * a:2 * a + SSM_WIDTH]
    vt = lax.dot_general(wvt_ref[...], xb, (((1,), (1,)), ((), ())), preferred_element_type=F32).astype(BF16)
    ones_rows = jnp.ones((ONES_ROWS, tm), BF16)
    vt = jnp.concatenate([piece for h in range(ATTN_HEADS)
                          for piece in (vt[HEAD_DIM * h:HEAD_DIM * (h + 1)], ones_rows)], axis=0)
    for jj in range(tm // ATTN_BLOCK):
        vt_ref[jj] = vt[:, jj * ATTN_BLOCK:(jj + 1) * ATTN_BLOCK]
    f = z[:, 2 * a + SSM_WIDTH:] + bf_ref[...]
    logf = jnp.minimum(f, 0.0) - jnp.log1p(jnp.exp(-jnp.abs(f)))
    row = lax.broadcasted_iota(jnp.int32, (tm, tm), 0)
    col = lax.broadcasted_iota(jnp.int32, (tm, tm), 1)
    tri = (row >= col).astype(BF16)
    hi, mid, lo = _split3(logf)
    cum = (jnp.dot(tri, hi, preferred_element_type=F32)
           + jnp.dot(tri, mid, preferred_element_type=F32)
           + jnp.dot(tri, lo, preferred_element_type=F32)) + carry_ref[...]
    carry_ref[...] = cum[tm - 1:tm, :]
    lane = lax.broadcasted_iota(jnp.int32, (tm, LANES), 1)
    c2 = jnp.where(lane < ATTN_HEADS, cum * LOG2E, 0.0)
    hi, mid, lo = (t.astype(F32) for t in _split3(c2))
    packed = (hi + pltpu.roll(mid, ATTN_HEADS, 1) + pltpu.roll(lo, 2 * ATTN_HEADS, 1)
              + jnp.where(lane == 3 * ATTN_HEADS, 1.0, 0.0)).astype(BF16)
    ext = jnp.dot(packed, sel_ref[...], preferred_element_type=F32).astype(BF16)
    eq_ref[...] = ext[:, :ATTN_HEADS * LANES]
    ek_ref[...] = ext[:, ATTN_HEADS * LANES:]


def _bias_lane_selector():
    sel = np.zeros((LANES, 2, ATTN_HEADS, LANES), np.float32)
    one = 3 * ATTN_HEADS
    for h in range(ATTN_HEADS):
        for t in range(3):
            sel[ATTN_HEADS * t + h, 0, h, t] = 1.0
            sel[one, 0, h, 3 + t] = 1.0
            sel[one, 1, h, t] = 1.0
            sel[ATTN_HEADS * t + h, 1, h, 3 + t] = -1.0
    return jnp.asarray(sel.reshape(LANES, 2 * ATTN_HEADS * LANES), BF16)


def _inproj(x2, w_cat, wvt, bf_row, seq):
    n_tok = x2.shape[0]
    tm = ROW_BLOCK
    assert seq % tm == 0 and n_tok % tm == 0 and tm % ATTN_BLOCK == 0
    bsz = n_tok // seq
    bps = seq // tm
    ncol = w_cat.shape[1]
    vblk = tm // ATTN_BLOCK
    ext_w = ATTN_HEADS * LANES
    row_spec = lambda w: pl.BlockSpec((tm, w), lambda i: (i, 0))
    return pl.pallas_call(
        functools.partial(_inproj_kernel, blocks_per_seq=bps),
        grid=(n_tok // tm,),
        in_specs=[row_spec(D_MODEL),
                  pl.BlockSpec((D_MODEL, ncol), lambda i: (0, 0)),
                  pl.BlockSpec((ATTN_WIDTH, D_MODEL), lambda i: (0, 0)),
                  pl.BlockSpec((1, LANES), lambda i: (0, 0)),
                  pl.BlockSpec((LANES, 2 * ext_w), lambda i: (0, 0))],
        out_specs=[row_spec(ATTN_WIDTH), row_spec(ATTN_WIDTH),
                   pl.BlockSpec((None, vblk, VT_ROWS, ATTN_BLOCK), lambda i: (i // bps, i % bps, 0, 0)),
                   row_spec(SSM_WIDTH), row_spec(ext_w), row_spec(ext_w)],
        out_shape=[jax.ShapeDtypeStruct((n_tok, ATTN_WIDTH), BF16)] * 2
                  + [jax.ShapeDtypeStruct((bsz, seq // ATTN_BLOCK, VT_ROWS, ATTN_BLOCK), BF16),
                     jax.ShapeDtypeStruct((n_tok, SSM_WIDTH), F32),
                     jax.ShapeDtypeStruct((n_tok, ext_w), BF16),
                     jax.ShapeDtypeStruct((n_tok, ext_w), BF16)],
        scratch_shapes=[pltpu.VMEM((1, LANES), F32)],
        compiler_params=pltpu.CompilerParams(dimension_semantics=("arbitrary",),
                                             vmem_limit_bytes=VMEM_LIMIT),
        name="inproj",
    )(x2, w_cat, wvt, bf_row, _bias_lane_selector())


def _attn_kernel(q_ref, k_ref, vt_ref, eq_ref, ek_ref, o_ref, *, tq, seq, nh):
    qi = pl.program_id(2)
    lane = lax.broadcasted_iota(jnp.int32, (tq, LANES), 1)
    q_augs = []
    for h in range(nh):
        qb = q_ref[:, LANES * (h // 2):LANES * (h // 2 + 1)]
        head_lanes = (lane < HEAD_DIM) if h % 2 == 0 else (lane >= HEAD_DIM)
        qh = jnp.where(head_lanes, qb, jnp.zeros((), BF16))
        q_augs.append(jnp.concatenate([qh, eq_ref[:, LANES * h:LANES * (h + 1)]], axis=1))
    key_i = lax.broadcasted_iota(jnp.int32, (tq, tq), 0)
    qry_i = lax.broadcasted_iota(jnp.int32, (tq, tq), 1)
    causal = key_i <= qry_i
    vrows = HEAD_DIM + ONES_ROWS

    def scores_of(k0, nk):
        out = []
        for h in range(nh):
            kj = k_ref[pl.ds(k0, nk), LANES * (h // 2):LANES * (h // 2 + 1)]
            k_aug = jnp.concatenate([kj, ek_ref[pl.ds(k0, nk), LANES * h:LANES * (h + 1)]], axis=1)
            out.append(lax.dot_general(k_aug, q_augs[h], (((1,), (1,)), ((), ())), preferred_element_type=F32))
        return tuple(out)

    def absorb(j0, scores, carry):
        stats = []
        for h in range(nh):
            m = carry[h][0]
            m_new = jnp.maximum(m, jnp.max(scores[h], axis=0, keepdims=True))
            stats.append((m_new, jnp.exp2(m - m_new), jnp.exp2(scores[h] - m_new).astype(BF16)))
        new = []
        for h in range(nh):
            m_new, corr, p = stats[h]
            res = None
            for c in range(p.shape[0] // tq):
                part = jnp.dot(vt_ref[j0 + c, vrows * h:vrows * (h + 1), :], p[c * tq:(c + 1) * tq],
                               preferred_element_type=F32)
                res = part if res is None else res + part
            acc = corr * carry[h][2] + res[0:HEAD_DIM, :]
            l = corr * carry[h][1] + res[HEAD_DIM:HEAD_DIM + 1, :]
            new.append((m_new, l, acc))
        return tuple(new)

    def pair_step(jj, carry):
        return absorb(2 * jj, scores_of(pl.multiple_of(jj * (2 * tq), 2 * tq), 2 * tq), carry)

    def odd_step(_, carry):
        return absorb(qi - 1, scores_of(pl.multiple_of((qi - 1) * tq, tq), tq), carry)

    init1 = (jnp.full((1, tq), NEG_BIG, F32), jnp.zeros((1, tq), F32), jnp.zeros((HEAD_DIM, tq), F32))
    carry = lax.fori_loop(0, qi // 2, pair_step, (init1,) * nh)
    carry = lax.fori_loop(0, qi % 2, odd_step, carry)
    diag = scores_of(pl.multiple_of(qi * tq, tq), tq)
    carry = absorb(qi, tuple(jnp.where(causal, s, NEG_BIG) for s in diag), carry)
    out_t = jnp.concatenate([c[2] / c[1] for c in carry], axis=0)
    o_ref[...] = out_t.T


def _attention(q, k, vt, eq, ek):
    bsz, seq, _ = q.shape
    tq = ATTN_BLOCK
    nh = ATTN_HEADS_PER_STEP
    assert seq % tq == 0 and ATTN_HEADS % nh == 0 and nh % 2 == 0
    w = nh * HEAD_DIM
    return pl.pallas_call(
        functools.partial(_attn_kernel, tq=tq, seq=seq, nh=nh),
        grid=(bsz, ATTN_HEADS // nh, seq // tq),
        in_specs=[pl.BlockSpec((None, tq, w), lambda b, g, i: (b, i, g)),
                  pl.BlockSpec((None, seq, w), lambda b, g, i: (b, 0, g)),
                  pl.BlockSpec((None, seq // tq, nh * (HEAD_DIM + ONES_ROWS), tq), lambda b, g, i: (b, 0, g, 0)),
                  pl.BlockSpec((None, tq, nh * LANES), lambda b, g, i: (b, i, g)),
                  pl.BlockSpec((None, seq, nh * LANES), lambda b, g, i: (b, 0, g))],
        out_specs=pl.BlockSpec((None, tq, w), lambda b, g, i: (b, i, g)),
        out_shape=jax.ShapeDtypeStruct((bsz, seq, ATTN_WIDTH), F32),
        compiler_params=pltpu.CompilerParams(dimension_semantics=("parallel", "parallel", "arbitrary"),
                                             vmem_limit_bytes=VMEM_LIMIT),
        name="fox_attention",
    )(q, k, vt, eq, ek)


def _s5_kernel(u_ref, bw_ref, cw_ref, ar_ref, ai_ref, d_ref, wglu_ref, bglu_ref, g_ref, o_ref,
               state_ref, bu_ref, *, tc, bsz):
    @pl.when(pl.program_id(0) == 0)
    def _():
        state_ref[...] = jnp.zeros_like(state_ref)

    rows = tc * bsz
    ns = SLAB_STATE
    u = u_ref[...].reshape(rows, SSM_WIDTH)
    ub = u.astype(BF16)
    for i in range(S5_SLABS):
        bu_ref[i] = jnp.dot(ub[:, SLAB_CH * i:SLAB_CH * (i + 1)], bw_ref[i], preferred_element_type=F32)
    for i in range(S5_SLABS):
        ar = jnp.broadcast_to(ar_ref[i], (bsz, ns))
        ai = jnp.broadcast_to(ai_ref[i], (bsz, ns))
        xr = state_ref[i, :, 0:ns]
        xi = state_ref[i, :, ns:2 * ns]
        for t in range(tc):
            r0 = t * bsz
            nr = ar * xr - ai * xi + bu_ref[i, r0:r0 + bsz, 0:ns]
            ni = ar * xi + ai * xr + bu_ref[i, r0:r0 + bsz, ns:2 * ns]
            bu_ref[i, r0:r0 + bsz, 0:ns] = nr
            bu_ref[i, r0:r0 + bsz, ns:2 * ns] = ni
            xr, xi = nr, ni
        state_ref[i, :, 0:ns] = xr
        state_ref[i, :, ns:2 * ns] = xi
    ys = [jnp.dot(bu_ref[i].astype(BF16), cw_ref[i], preferred_element_type=F32) for i in range(S5_SLABS)]
    y = jnp.concatenate(ys, axis=1) + d_ref[...] * u
    y = 0.5 * y * (1.0 + jnp.tanh(math.sqrt(2.0 / math.pi) * (y + 0.044715 * (y * y * y))))
    z = jnp.dot(y.astype(BF16), wglu_ref[...], preferred_element_type=F32) + bglu_ref[...]
    y = y * (1.0 / (1.0 + jnp.exp(-z)))
    y = y * lax.rsqrt(jnp.mean(y * y, axis=-1, keepdims=True) + RMS_EPS) * g_ref[...]
    o_ref[...] = y.astype(BF16).reshape(tc, bsz, SSM_WIDTH)


def _s5_weights(lam_re, lam_im, log_step, b_re, b_im, c_re, c_im):
    delta = jnp.exp(log_step)[:, None]
    mag = jnp.exp(lam_re * delta)
    ar = mag * jnp.cos(lam_im * delta)
    ai = mag * jnp.sin(lam_im * delta)
    den = lam_re * lam_re + lam_im * lam_im
    num_re = ar - 1.0
    coef_re = (num_re * lam_re + ai * lam_im) / den
    coef_im = (ai * lam_re - num_re * lam_im) / den
    bbar_re = coef_re[..., None] * b_re - coef_im[..., None] * b_im
    bbar_im = coef_re[..., None] * b_im + coef_im[..., None] * b_re
    gs = SSM_GROUPS // S5_SLABS
    eye = jnp.eye(gs, dtype=F32)

    def b_slab(bb):
        bb = bb.reshape(S5_SLABS, gs, SSM_STATE, SSM_GROUP)
        w = jnp.einsum('ab,sapc->sacbp', eye, bb)
        return w.reshape(S5_SLABS, gs * SSM_GROUP, gs * SSM_STATE)

    def c_slab(cc):
        cc = cc.reshape(S5_SLABS, gs, SSM_GROUP, SSM_STATE)
        w = jnp.einsum('ab,sacp->sbpac', eye, cc)
        return w.reshape(S5_SLABS, gs * SSM_STATE, gs * SSM_GROUP)

    bw = jnp.concatenate([b_slab(bbar_re), b_slab(bbar_im)], axis=2).astype(BF16)
    cw = jnp.concatenate([c_slab(c_re), -c_slab(c_im)], axis=1).astype(BF16)
    ar_s = ar.reshape(S5_SLABS, 1, SLAB_STATE)
    ai_s = ai.reshape(S5_SLABS, 1, SLAB_STATE)
    return bw, cw, ar_s, ai_s


def _s5(u_tb, bw, cw, ar_s, ai_s, d_row, w_glu, b_glu_row, g_row):
    seq, bsz, _ = u_tb.shape
    tc = S5_TIME_BLOCK
    assert seq % tc == 0 and bsz % 16 == 0
    full = lambda shape: pl.BlockSpec(shape, lambda t: (0,) * len(shape))
    return pl.pallas_call(
        functools.partial(_s5_kernel, tc=tc, bsz=bsz),
        grid=(seq // tc,),
        in_specs=[pl.BlockSpec((tc, bsz, SSM_WIDTH), lambda t: (t, 0, 0)),
                  full(bw.shape), full(cw.shape), full(ar_s.shape), full(ai_s.shape),
                  full((1, SSM_WIDTH)), full((SSM_WIDTH, SSM_WIDTH)), full((1, SSM_WIDTH)),
                  full((1, SSM_WIDTH))],
        out_specs=pl.BlockSpec((tc, bsz, SSM_WIDTH), lambda t: (t, 0, 0)),
        out_shape=jax.ShapeDtypeStruct((seq, bsz, SSM_WIDTH), BF16),
        scratch_shapes=[pltpu.VMEM((S5_SLABS, bsz, 2 * SLAB_STATE), F32),
                        pltpu.VMEM((S5_SLABS, tc * bsz, 2 * SLAB_STATE), F32)],
        compiler_params=pltpu.CompilerParams(dimension_semantics=("arbitrary",),
                                             vmem_limit_bytes=VMEM_LIMIT),
        name="s5_branch",
    )(u_tb, bw, cw, ar_s, ai_s, d_row, w_glu, b_glu_row, g_row)


def _layer_norm_rows(v, g, b):
    mu = jnp.mean(v, axis=-1, keepdims=True)
    c = v - mu
    var = jnp.mean(c * c, axis=-1, keepdims=True)
    return c * lax.rsqrt(var + LN_EPS) * g + b


def _outproj_kernel(attn_ref, ssm_ref, x_ref, wo_ref, ag_ref, g1_ref, b1_ref, wrt_ref, brt_ref, triu_ref,
                    h_ref, eid_ref, gate_ref, rank_ref, cnt_ref):
    a = attn_ref[...]
    a = a * lax.rsqrt(jnp.mean(a * a, axis=-1, keepdims=True) + RMS_EPS) * ag_ref[...]
    mix = (jnp.dot(a.astype(BF16), wo_ref[0:ATTN_WIDTH, :], preferred_element_type=F32)
           + jnp.dot(ssm_ref[...], wo_ref[ATTN_WIDTH:, :], preferred_element_type=F32))
    h = _layer_norm_rows(DEEPNORM_ALPHA * x_ref[...] + mix, g1_ref[...], b1_ref[...])
    h_ref[...] = h

    logits = lax.dot_general(wrt_ref[...], h.astype(BF16), (((1,), (1,)), ((), ())),
                             preferred_element_type=F32)[:ROUTER_ROWS] + brt_ref[...]
    tm = logits.shape[1]
    row = lax.broadcasted_iota(jnp.int32, (ROUTER_ROWS, tm), 0).astype(F32)
    no_row = float(LANES)
    is_g = row < N_EXPERT_GROUPS
    gl = jnp.where(is_g, logits, NEG_BIG)
    gmax = jnp.max(gl, axis=0, keepdims=True)
    g_idx = jnp.min(jnp.where(is_g & (gl == gmax), row, no_row), axis=0, keepdims=True)
    g_p = 1.0 / jnp.sum(jnp.where(is_g, jnp.exp(gl - gmax), 0.0), axis=0, keepdims=True)
    first = N_EXPERT_GROUPS + g_idx * EXPERTS_PER_GROUP
    in_grp = (row >= first) & (row < first + EXPERTS_PER_GROUP)
    el = jnp.where(in_grp, logits, NEG_BIG)
    m1 = jnp.max(el, axis=0, keepdims=True)
    i1 = jnp.min(jnp.where(in_grp & (el == m1), row, no_row), axis=0, keepdims=True)
    rest = in_grp & (row != i1)
    el2 = jnp.where(rest, logits, NEG_BIG)
    m2 = jnp.max(el2, axis=0, keepdims=True)
    i2 = jnp.min(jnp.where(rest & (el2 == m2), row, no_row), axis=0, keepdims=True)
    r = jnp.exp(m2 - m1)
    p1 = 1.0 / (1.0 + r)
    p2 = r / (1.0 + r)
    k2 = lax.broadcasted_iota(jnp.int32, (2, tm), 0)
    eid_ref[...] = (jnp.where(k2 == 0, i1, i2) - N_EXPERT_GROUPS).astype(jnp.int32)
    gate_ref[...] = g_p * jnp.where(k2 == 0, p1, p2)

    @pl.when(pl.program_id(0) == 0)
    def _():
        cnt_ref[...] = jnp.zeros_like(cnt_ref)
    hit1 = row == i1
    hit2 = row == i2
    onehot = jnp.where(hit1 | hit2, 1.0, 0.0)
    cnt = cnt_ref[...]
    before = jnp.dot(onehot.astype(BF16), triu_ref[...], preferred_element_type=F32) + cnt
    r1 = jnp.sum(jnp.where(hit1, before, 0.0), axis=0, keepdims=True)
    r2 = jnp.sum(jnp.where(hit2, before, 0.0), axis=0, keepdims=True)
    rank_ref[...] = jnp.where(k2 == 0, r1, r2).astype(jnp.int32)
    cnt_ref[...] = cnt + jnp.sum(onehot, axis=1, keepdims=True)


def _outproj(attn2, ssm2, x2, wo, ag_row, g1_row, b1_row, wrt, brt):
    n_tok = x2.shape[0]
    tm = ROW_BLOCK
    triu = jnp.triu(jnp.ones((tm, tm), BF16), 1)
    row_spec = lambda w: pl.BlockSpec((tm, w), lambda i: (i, 0))
    col_spec = lambda r: pl.BlockSpec((r, tm), lambda i: (0, i))
    full = lambda shape: pl.BlockSpec(shape, lambda i: (0,) * len(shape))
    return pl.pallas_call(
        _outproj_kernel,
        grid=(n_tok // tm,),
        in_specs=[row_spec(ATTN_WIDTH), row_spec(SSM_WIDTH), row_spec(D_MODEL),
                  full((D_MODEL, D_MODEL)), full((1, ATTN_WIDTH)), full((1, D_MODEL)), full((1, D_MODEL)),
                  full((LANES, D_MODEL)), full((ROUTER_ROWS, 1)), full((tm, tm))],
        out_specs=[row_spec(D_MODEL), col_spec(2), col_spec(2), col_spec(2), full((ROUTER_ROWS, 1))],
        out_shape=[jax.ShapeDtypeStruct((n_tok, D_MODEL), F32),
                   jax.ShapeDtypeStruct((2, n_tok), jnp.int32),
                   jax.ShapeDtypeStruct((2, n_tok), F32),
                   jax.ShapeDtypeStruct((2, n_tok), jnp.int32),
                   jax.ShapeDtypeStruct((ROUTER_ROWS, 1), F32)],
        compiler_params=pltpu.CompilerParams(dimension_semantics=("arbitrary",),
                                             vmem_limit_bytes=VMEM_LIMIT),
        name="outproj_ln_router",
    )(attn2, ssm2, x2, wo, ag_row, g1_row, b1_row, wrt, brt, triu)


def _dispatch_plan(eid, rank, cnt_col, n_tok):
    counts = cnt_col[N_EXPERT_GROUPS:N_EXPERT_GROUPS + N_EXPERTS, 0].astype(jnp.int32)
    padded = ((counts + MOE_BLOCK - 1) // MOE_BLOCK) * MOE_BLOCK
    pends = jnp.cumsum(padded)
    pstarts = pends - padded
    n_pad = ((2 * n_tok + N_EXPERTS * (MOE_BLOCK - 1) + MOE_BLOCK - 1) // MOE_BLOCK) * MOE_BLOCK
    n_blocks = n_pad // MOE_BLOCK
    blk0 = jnp.arange(n_blocks, dtype=jnp.int32) * MOE_BLOCK
    block_e = jnp.minimum(jnp.sum((pends[None, :] <= blk0[:, None]).astype(jnp.int32), axis=1), N_EXPERTS - 1)
    n_used = (pends[-1] // MOE_BLOCK).astype(jnp.int32).reshape(1)
    tail = jnp.where(padded > 0, pends - MOE_BLOCK, -1).astype(jnp.int32)

    tb = TOKEN_BLOCK
    nb = n_tok // tb
    onehot = eid[:, :, None] == jnp.arange(N_EXPERTS, dtype=jnp.int32)[None, None, :]
    run_len = jnp.sum(onehot.reshape(2, nb, tb, N_EXPERTS).astype(jnp.int32), axis=(0, 2))
    before = jnp.cumsum(run_len, axis=0) - run_len
    run_src = jnp.cumsum(run_len, axis=1) - run_len
    run_dst = pstarts[None, :] + before
    shift = jnp.broadcast_to((run_src - before)[:, None, :], (nb, tb, N_EXPERTS)).reshape(n_tok, N_EXPERTS)
    local = rank + jnp.sum(jnp.where(onehot, shift[None], 0), axis=-1)
    any_long = (jnp.max(run_len, axis=1) >= (1 << SHORT_RUN_BITS)).astype(jnp.int32)
    runs = (run_len.reshape(-1), run_src.reshape(-1), run_dst.reshape(-1), any_long)
    return local, runs, block_e, n_used, tail, n_blocks


ROW_SUB = D_MODEL // LANES


def _rows(start, size):
    return pl.ds(pl.multiple_of(start * ROW_SUB, ROW_SUB), size * ROW_SUB)


def _store_row_tiles(ref, row0, v):
    n = v.shape[0]
    for s in range(ROW_SUB):
        ref[pl.ds(row0 * ROW_SUB + s, n, stride=ROW_SUB), :] = v[:, LANES * s:LANES * (s + 1)]


def _load_row_tiles(ref, row0, n):
    return jnp.concatenate([ref[pl.ds(row0 * ROW_SUB + s, n, stride=ROW_SUB), :] for s in range(ROW_SUB)], axis=1)


def _for_each_run_piece(blk, len_ref, src_ref, dst_ref, bits, make_copy, act):
    for e in range(N_EXPERTS):
        idx = blk * N_EXPERTS + e
        n = len_ref[idx]
        src = src_ref[idx]
        dst = dst_ref[idx]
        for b in reversed(range(bits)):
            @pl.when((n & (1 << b)) != 0)
            def _():
                off = lax.shift_left(lax.shift_right_logical(n, b + 1), b + 1)
                act(make_copy(src + off, dst + off, 1 << b))


def _by_run_size(long_ref, blk, body):
    @pl.when(long_ref[blk] == 0)
    def _():
        body(SHORT_RUN_BITS)

    @pl.when(long_ref[blk] != 0)
    def _():
        body(RUN_BITS)


def _dispatch_kernel(tail_ref, nu_ref, len_ref, src_ref, dst_ref, long_ref, local_ref, h_ref, xb_hbm, zbuf,
                     pk_ref, sem_z, sem, *, n_blocks):
    i = pl.program_id(0)
    nb = pl.num_programs(0) - 1
    tb = h_ref.shape[0]
    cur = i % 2

    @pl.when(i == 0)
    def _():
        zbuf[...] = jnp.zeros_like(zbuf)

        def zero_copy(row0):
            return pltpu.make_async_copy(zbuf, xb_hbm.at[_rows(row0, MOE_BLOCK)], sem_z)

        def each_zero_block(fn):
            for e in range(N_EXPERTS):
                @pl.when(tail_ref[e] >= 0)
                def _():
                    fn(zero_copy(tail_ref[e]))

            def unused(b, carry):
                fn(zero_copy(b * MOE_BLOCK))
                return carry
            lax.fori_loop(nu_ref[0], n_blocks, unused, 0)

        each_zero_block(lambda c: c.start())
        each_zero_block(lambda c: c.wait())

    def drain(which):
        pltpu.make_async_copy(pk_ref.at[which], xb_hbm.at[_rows(0, 2 * tb)], sem.at[which]).wait()

    @pl.when(i >= 2)
    def _():
        drain(cur)

    def group():
        pos = lax.broadcasted_iota(jnp.int32, (2 * tb, tb), 0)
        hit = (pos == local_ref[0:1, :]) | (pos == local_ref[1:2, :])
        perm = jnp.where(hit, 1.0, 0.0).astype(BF16)
        _store_row_tiles(pk_ref.at[cur], 0, jnp.dot(perm, h_ref[...].astype(BF16), preferred_element_type=F32))

    def send(bits):
        def run_copy(src, dst, size):
            return pltpu.make_async_copy(pk_ref.at[1 - cur, _rows(src, size)], xb_hbm.at[_rows(dst, size)],
                                         sem.at[1 - cur])
        _for_each_run_piece(i - 1, len_ref, src_ref, dst_ref, bits, run_copy, lambda c: c.start())

    @pl.when(i == 0)
    def _():
        group()

    @pl.when((i > 0) & (i < nb))
    def _():
        def send_and_group(bits):
            send(bits)
            group()
        _by_run_size(long_ref, i - 1, send_and_group)

    @pl.when(i == nb)
    def _():
        send(RUN_BITS)
        drain(1 - cur)


def _dispatch(h, local, runs, tail, n_used, n_blocks):
    n_tok = h.shape[0]
    tb = TOKEN_BLOCK
    nb = n_tok // tb
    grid_spec = pltpu.PrefetchScalarGridSpec(
        num_scalar_prefetch=6,
        grid=(nb + 1,),
        in_specs=[pl.BlockSpec((2, tb), lambda i, *_: (0, jnp.minimum(i, nb - 1))),
                  pl.BlockSpec((tb, D_MODEL), lambda i, *_: (jnp.minimum(i, nb - 1), 0))],
        out_specs=pl.BlockSpec(memory_space=pl.ANY),
        scratch_shapes=[pltpu.VMEM((MOE_BLOCK * ROW_SUB, LANES), F32),
                        pltpu.VMEM((2, 2 * tb * ROW_SUB, LANES), F32),
                        pltpu.SemaphoreType.DMA, pltpu.SemaphoreType.DMA((2,))],
    )
    return pl.pallas_call(
        functools.partial(_dispatch_kernel, n_blocks=n_blocks),
        grid_spec=grid_spec,
        out_shape=jax.ShapeDtypeStruct((n_blocks * MOE_BLOCK * ROW_SUB, LANES), F32),
        compiler_params=pltpu.CompilerParams(dimension_semantics=("arbitrary",),
                                             vmem_limit_bytes=VMEM_LIMIT),
        name="moe_dispatch",
    )(tail, n_used, *runs, local, h)


def _expert_kernel(be_ref, nu_ref, x_ref, wg_ref, wu_ref, wd_ref, y_ref, wgu_t, wd_t, yt_ref):
    i = pl.program_id(0)
    used = i < nu_ref[0]

    @pl.when(used & ((i == 0) | (be_ref[i] != be_ref[jnp.maximum(i - 1, 0)])))
    def _():
        wgu_t[0:D_EXPERT, :] = wg_ref[...].T.astype(BF16)
        wgu_t[D_EXPERT:, :] = wu_ref[...].T.astype(BF16)
        wd_t[...] = wd_ref[...].T.astype(BF16)

    @pl.when(used)
    def _():
        sub = range(MOE_BLOCK // MOE_SUB)
        gus = []
        for c in sub:
            xb = _load_row_tiles(x_ref, c * MOE_SUB, MOE_SUB).astype(BF16)
            gus.append(lax.dot_general(wgu_t[...], xb, (((1,), (1,)), ((), ())),
                                       preferred_element_type=F32))
        hids = []
        for c in sub:
            gt = gus[c][0:D_EXPERT]
            hids.append((gt * (1.0 / (1.0 + jnp.exp(-gt))) * gus[c][D_EXPERT:]).astype(BF16))
        for c in sub:
            yt_ref[c] = jnp.dot(wd_t[...], hids[c], preferred_element_type=F32)
        for c in sub:
            _store_row_tiles(y_ref, c * MOE_SUB, yt_ref[c].T)

    @pl.when(jnp.logical_not(used))
    def _():
        y_ref[...] = jnp.zeros_like(y_ref)


def _experts(xb, block_e, n_used, wg, wu, wd):
    n_blocks = xb.shape[0] // (MOE_BLOCK * ROW_SUB)
    last = lambda i, nu: jnp.minimum(i, nu[0] - 1)
    grid_spec = pltpu.PrefetchScalarGridSpec(
        num_scalar_prefetch=2,
        grid=(n_blocks,),
        in_specs=[pl.BlockSpec((MOE_BLOCK * ROW_SUB, LANES), lambda i, be, nu: (last(i, nu), 0)),
                  pl.BlockSpec((None, D_MODEL, D_EXPERT), lambda i, be, nu: (be[last(i, nu)], 0, 0)),
                  pl.BlockSpec((None, D_MODEL, D_EXPERT), lambda i, be, nu: (be[last(i, nu)], 0, 0)),
                  pl.BlockSpec((None, D_EXPERT, D_MODEL), lambda i, be, nu: (be[last(i, nu)], 0, 0))],
        out_specs=pl.BlockSpec((MOE_BLOCK * ROW_SUB, LANES), lambda i, be, nu: (i, 0)),
        scratch_shapes=[pltpu.VMEM((2 * D_EXPERT, D_MODEL), BF16), pltpu.VMEM((D_MODEL, D_EXPERT), BF16),
                        pltpu.VMEM((MOE_BLOCK // MOE_SUB, D_MODEL, MOE_SUB), F32)],
    )
    return pl.pallas_call(
        _expert_kernel,
        grid_spec=grid_spec,
        out_shape=jax.ShapeDtypeStruct(xb.shape, F32),
        compiler_params=pltpu.CompilerParams(dimension_semantics=("arbitrary",),
                                             vmem_limit_bytes=VMEM_LIMIT),
        name="moe_experts",
    )(block_e, n_used, xb, wg, wu, wd)


def _combine_kernel(len_ref, src_ref, dst_ref, long_ref, local_ref, gate_ref, h_ref, g2_ref, b2_ref, yb_hbm,
                    o_ref, buf, sem):
    i = pl.program_id(0)
    n = pl.num_programs(0)
    tb = h_ref.shape[0]
    cur = i % 2

    def fetch(blk, which, bits):
        def run_copy(src, dst, size):
            return pltpu.make_async_copy(yb_hbm.at[_rows(dst, size)], buf.at[which, _rows(src, size)], sem.at[which])
        _for_each_run_piece(blk, len_ref, src_ref, dst_ref, bits, run_copy, lambda c: c.start())

    def drain(which):
        pltpu.make_async_copy(yb_hbm.at[_rows(0, 2 * tb)], buf.at[which], sem.at[which]).wait()

    @pl.when(i == 0)
    def _():
        fetch(i, 0, RUN_BITS)

    drain(cur)
    nxt = jnp.minimum(i + 1, n - 1)

    def fetch_next_and_combine(bits):
        fetch(nxt, 1 - cur, bits)
        y = _load_row_tiles(buf.at[cur], 0, 2 * tb).astype(BF16)
        pos = lax.broadcasted_iota(jnp.int32, (tb, 2 * tb), 1)
        local = local_ref[...]
        gate = gate_ref[...]
        mix = (jnp.where(pos == local[:, 0:1], gate[:, 0:1], 0.0)
               + jnp.where(pos == local[:, 1:2], gate[:, 1:2], 0.0))
        mix_hi = mix.astype(BF16)
        mix_lo = (mix - mix_hi.astype(F32)).astype(BF16)
        moe = (jnp.dot(mix_hi, y, preferred_element_type=F32) + jnp.dot(mix_lo, y, preferred_element_type=F32))
        o_ref[...] = _layer_norm_rows(DEEPNORM_ALPHA * h_ref[...] + moe, g2_ref[...], b2_ref[...])
    _by_run_size(long_ref, nxt, fetch_next_and_combine)

    @pl.when(i == n - 1)
    def _():
        drain(1 - cur)


def _combine(h, yb, local_t, runs, gate, g2_row, b2_row):
    n_tok = h.shape[0]
    tb = TOKEN_BLOCK
    row_spec = lambda w: pl.BlockSpec((tb, w), lambda i, *_: (i, 0))
    full = lambda shape: pl.BlockSpec(shape, lambda i, *_: (0,) * len(shape))
    grid_spec = pltpu.PrefetchScalarGridSpec(
        num_scalar_prefetch=4,
        grid=(n_tok // tb,),
        in_specs=[row_spec(2), row_spec(2), row_spec(D_MODEL), full((1, D_MODEL)), full((1, D_MODEL)),
                  pl.BlockSpec(memory_space=pl.ANY)],
        out_specs=row_spec(D_MODEL),
        scratch_shapes=[pltpu.VMEM((2, 2 * tb * ROW_SUB, LANES), F32), pltpu.SemaphoreType.DMA((2,))],
    )
    return pl.pallas_call(
        _combine_kernel,
        grid_spec=grid_spec,
        out_shape=jax.ShapeDtypeStruct((n_tok, D_MODEL), F32),
        compiler_params=pltpu.CompilerParams(dimension_semantics=("arbitrary",),
                                             vmem_limit_bytes=VMEM_LIMIT),
        name="combine_ln",
    )(*runs, local_t, gate, h, g2_row, b2_row, yb)


def kernel(x, w_in, b_fgate, s5_lambda_re, s5_lambda_im, s5_log_step, s5_b_re, s5_b_im, s5_c_re, s5_c_im, s5_d, w_glu, b_glu, attn_norm_g, ssm_norm_g, w_out, ln1_g, ln1_b, w_router_group, b_router_group, w_router_expert, b_router_expert, w_gate, w_up, w_down, ln2_g, ln2_b):
    bsz, seq, d = x.shape
    n_tok = bsz * seq
    a = ATTN_WIDTH
    for l in range(DEPTH):
        x2 = x.reshape(n_tok, d)
        w = w_in[l]
        f0 = 3 * a
        w_cat = jnp.concatenate(
            [w[:, :a] * (HEAD_DIM ** -0.5 * LOG2E), w[:, a:2 * a], w[:, f0 + ATTN_HEADS:],
             jnp.pad(w[:, f0:f0 + ATTN_HEADS], ((0, 0), (0, LANES - ATTN_HEADS)))], axis=1).astype(BF16)
        wvt = w[:, 2 * a:f0].T.astype(BF16)
        bf_row = jnp.pad(b_fgate[l], (0, LANES - ATTN_HEADS)).reshape(1, LANES)
        q, k, vt, u, eq, ek = _inproj(x2, w_cat, wvt, bf_row, seq)
        ext_w = ATTN_HEADS * LANES
        attn = _attention(q.reshape(bsz, seq, a), k.reshape(bsz, seq, a), vt,
                          eq.reshape(bsz, seq, ext_w), ek.reshape(bsz, seq, ext_w))

        bw, cw, ar_s, ai_s = _s5_weights(s5_lambda_re[l], s5_lambda_im[l], s5_log_step[l],
                                         s5_b_re[l], s5_b_im[l], s5_c_re[l], s5_c_im[l])
        u_tb = u.reshape(bsz, seq, SSM_WIDTH).transpose(1, 0, 2)
        ssm_tb = _s5(u_tb, bw, cw, ar_s, ai_s, s5_d[l].reshape(1, SSM_WIDTH), w_glu[l].astype(BF16),
                     b_glu[l].reshape(1, SSM_WIDTH), ssm_norm_g[l].reshape(1, SSM_WIDTH))
        ssm2 = ssm_tb.transpose(1, 0, 2).reshape(n_tok, SSM_WIDTH)

        n_route = N_EXPERT_GROUPS + N_EXPERTS
        wrt = jnp.pad(jnp.concatenate([w_router_group[l], w_router_expert[l]], axis=1).T,
                      ((0, LANES - n_route), (0, 0))).astype(BF16)
        brt = jnp.pad(jnp.concatenate([b_router_group[l], b_router_expert[l]]),
                      (0, ROUTER_ROWS - n_route)).reshape(ROUTER_ROWS, 1)
        h, eid, gate, rank, cnt_col = _outproj(
            attn.reshape(n_tok, a), ssm2, x2, w_out[l].astype(BF16), attn_norm_g[l].reshape(1, a),
            ln1_g[l].reshape(1, d), ln1_b[l].reshape(1, d), wrt, brt)

        local, runs, block_e, n_used, tail, n_blocks = _dispatch_plan(eid, rank, cnt_col, n_tok)
        xb = _dispatch(h, local, runs, tail, n_used, n_blocks)
        yb = _experts(xb, block_e, n_used, w_gate[l], w_up[l], w_down[l])
        x = _combine(h, yb, local.T, runs, gate.T, ln2_g[l].reshape(1, d),
                     ln2_b[l].reshape(1, d)).reshape(bsz, seq, d)
    return x
```

```python
import functools
import math

import jax
import jax.numpy as jnp
import numpy as np
from jax import lax
from jax.experimental import pallas as pl
from jax.experimental.pallas import tpu as pltpu

F32 = jnp.float32
BF16 = jnp.bfloat16

D_MODEL = 1024
HEAD_DIM = 64
ATTN_WIDTH = 512
ATTN_HEADS = 8
SSM_WIDTH = 512
SSM_GROUP = 16
SSM_GROUPS = 32
SSM_STATE = 64
N_EXPERT_GROUPS = 4
EXPERTS_PER_GROUP = 8
N_EXPERTS = 32
D_EXPERT = 512
MOE_BLOCK = 512
MOE_SUB = 256
TOKEN_BLOCK = 256
RUN_BITS = TOKEN_BLOCK.bit_length()
SHORT_RUN_BITS = 5
DEPTH = 1
DEEPNORM_ALPHA = (2.0 * DEPTH) ** 0.25
LN_EPS = 1e-5
RMS_EPS = 1e-6

LANES = 128
NEG_BIG = -1e30
LOG2E = math.log2(math.e)
ONES_ROWS = 16
VT_ROWS = ATTN_HEADS * (HEAD_DIM + ONES_ROWS)
VMEM_LIMIT = 48 * 1024 * 1024

ROW_BLOCK = 512
ATTN_BLOCK = 256
ATTN_HEADS_PER_STEP = 8
S5_TIME_BLOCK = 64
S5_SLABS = 4
ROUTER_ROWS = 40
SLAB_CH = SSM_WIDTH // S5_SLABS
SLAB_STATE = SSM_GROUPS // S5_SLABS * SSM_STATE


def _split3(c):
    hi = c.astype(BF16)
    r1 = c - hi.astype(F32)
    mid = r1.astype(BF16)
    lo = (r1 - mid.astype(F32)).astype(BF16)
    return hi, mid, lo


def _inproj_kernel(x_ref, w_ref, wvt_ref, bf_ref, sel_ref, q_ref, k_ref, vt_ref, u_ref, eq_ref, ek_ref,
                   carry_ref, *, blocks_per_seq):
    i = pl.program_id(0)

    @pl.when(i % blocks_per_seq == 0)
    def _():
        carry_ref[...] = jnp.zeros_like(carry_ref)

    tm = x_ref.shape[0]
    xb = x_ref[...].astype(BF16)
    z = jnp.dot(xb, w_ref[...], preferred_element_type=F32)
    a = ATTN_WIDTH
    q_ref[...] = z[:, :a].astype(BF16)
    k_ref[...] = z[:, a:2 * a].astype(BF16)
    u_ref[...] = z[:, 2 * a:2 * a + SSM_WIDTH]
    vt = lax.dot_general(wvt_ref[...], xb, (((1,), (1,)), ((), ())), preferred_element_type=F32).astype(BF16)
    ones_rows = jnp.ones((ONES_ROWS, tm), BF16)
    vt = jnp.concatenate([piece for h in range(ATTN_HEADS)
                          for piece in (vt[HEAD_DIM * h:HEAD_DIM * (h + 1)], ones_rows)], axis=0)
    for jj in range(tm // ATTN_BLOCK):
        vt_ref[jj] = vt[:, jj * ATTN_BLOCK:(jj + 1) * ATTN_BLOCK]
    f = z[:, 2 * a + SSM_WIDTH:] + bf_ref[...]
    logf = jnp.minimum(f, 0.0) - jnp.log1p(jnp.exp(-jnp.abs(f)))
    row = lax.broadcasted_iota(jnp.int32, (tm, tm), 0)
    col = lax.broadcasted_iota(jnp.int32, (tm, tm), 1)
    tri = (row >= col).astype(BF16)
    hi, mid, lo = _split3(logf)
    cum = (jnp.dot(tri, hi, preferred_element_type=F32)
           + jnp.dot(tri, mid, preferred_element_type=F32)
           + jnp.dot(tri, lo, preferred_element_type=F32)) + carry_ref[...]
    carry_ref[...] = cum[tm - 1:tm, :]
    lane = lax.broadcasted_iota(jnp.int32, (tm, LANES), 1)
    c2 = jnp.where(lane < ATTN_HEADS, cum * LOG2E, 0.0)
    hi, mid, lo = (t.astype(F32) for t in _split3(c2))
    packed = (hi + pltpu.roll(mid, ATTN_HEADS, 1) + pltpu.roll(lo, 2 * ATTN_HEADS, 1)
              + jnp.where(lane == 3 * ATTN_HEADS, 1.0, 0.0)).astype(BF16)
    ext = jnp.dot(packed, sel_ref[...], preferred_element_type=F32).astype(BF16)
    eq_ref[...] = ext[:, :ATTN_HEADS * LANES]
    ek_ref[...] = ext[:, ATTN_HEADS * LANES:]


def _bias_lane_selector():
    sel = np.zeros((LANES, 2, ATTN_HEADS, LANES), np.float32)
    one = 3 * ATTN_HEADS
    for h in range(ATTN_HEADS):
        for t in range(3):
            sel[ATTN_HEADS * t + h, 0, h, t] = 1.0
            sel[one, 0, h, 3 + t] = 1.0
            sel[one, 1, h, t] = 1.0
            sel[ATTN_HEADS * t + h, 1, h, 3 + t] = -1.0
    return jnp.asarray(sel.reshape(LANES, 2 * ATTN_HEADS * LANES), BF16)


def _inproj(x2, w_cat, wvt, bf_row, seq):
    n_tok = x2.shape[0]
    tm = ROW_BLOCK
    assert seq % tm == 0 and n_tok % tm == 0 and tm % ATTN_BLOCK == 0
    bsz = n_tok // seq
    bps = seq // tm
    ncol = w_cat.shape[1]
    vblk = tm // ATTN_BLOCK
    ext_w = ATTN_HEADS * LANES
    row_spec = lambda w: pl.BlockSpec((tm, w), lambda i: (i, 0))
    return pl.pallas_call(
        functools.partial(_inproj_kernel, blocks_per_seq=bps),
        grid=(n_tok // tm,),
        in_specs=[row_spec(D_MODEL),
                  pl.BlockSpec((D_MODEL, ncol), lambda i: (0, 0)),
                  pl.BlockSpec((ATTN_WIDTH, D_MODEL), lambda i: (0, 0)),
                  pl.BlockSpec((1, LANES), lambda i: (0, 0)),
                  pl.BlockSpec((LANES, 2 * ext_w), lambda i: (0, 0))],
        out_specs=[row_spec(ATTN_WIDTH), row_spec(ATTN_WIDTH),
                   pl.BlockSpec((None, vblk, VT_ROWS, ATTN_BLOCK), lambda i: (i // bps, i % bps, 0, 0)),
                   row_spec(SSM_WIDTH), row_spec(ext_w), row_spec(ext_w)],
        out_shape=[jax.ShapeDtypeStruct((n_tok, ATTN_WIDTH), BF16)] * 2
                  + [jax.ShapeDtypeStruct((bsz, seq // ATTN_BLOCK, VT_ROWS, ATTN_BLOCK), BF16),
                     jax.ShapeDtypeStruct((n_tok, SSM_WIDTH), F32),
                     jax.ShapeDtypeStruct((n_tok, ext_w), BF16),
                     jax.ShapeDtypeStruct((n_tok, ext_w), BF16)],
        scratch_shapes=[pltpu.VMEM((1, LANES), F32)],
        compiler_params=pltpu.CompilerParams(dimension_semantics=("arbitrary",),
                                             vmem_limit_bytes=VMEM_LIMIT),
        name="inproj",
    )(x2, w_cat, wvt, bf_row, _bias_lane_selector())


def _attn_kernel(q_ref, k_ref, vt_ref, eq_ref, ek_ref, o_ref, *, tq, seq, nh):
    qi = pl.program_id(2)
    lane = lax.broadcasted_iota(jnp.int32, (tq, LANES), 1)
    q_augs = []
    for h in range(nh):
        qb = q_ref[:, LANES * (h // 2):LANES * (h // 2 + 1)]
        head_lanes = (lane < HEAD_DIM) if h % 2 == 0 else (lane >= HEAD_DIM)
        qh = jnp.where(head_lanes, qb, jnp.zeros((), BF16))
        q_augs.append(jnp.concatenate([qh, eq_ref[:, LANES * h:LANES * (h + 1)]], axis=1))
    key_i = lax.broadcasted_iota(jnp.int32, (tq, tq), 0)
    qry_i = lax.broadcasted_iota(jnp.int32, (tq, tq), 1)
    causal = key_i <= qry_i
    vrows = HEAD_DIM + ONES_ROWS

    def scores_of(k0, nk):
        out = []
        for h in range(nh):
            kj = k_ref[pl.ds(k0, nk), LANES * (h // 2):LANES * (h // 2 + 1)]
            k_aug = jnp.concatenate([kj, ek_ref[pl.ds(k0, nk), LANES * h:LANES * (h + 1)]], axis=1)
            out.append(lax.dot_general(k_aug, q_augs[h], (((1,), (1,)), ((), ())), preferred_element_type=F32))
        return tuple(out)

    def absorb(j0, scores, carry):
        stats = []
        for h in range(nh):
            m = carry[h][0]
            m_new = jnp.maximum(m, jnp.max(scores[h], axis=0, keepdims=True))
            stats.append((m_new, jnp.exp2(m - m_new), jnp.exp2(scores[h] - m_new).astype(BF16)))
        new = []
        for h in range(nh):
            m_new, corr, p = stats[h]
            res = None
            for c in range(p.shape[0] // tq):
                part = jnp.dot(vt_ref[j0 + c, vrows * h:vrows * (h + 1), :], p[c * tq:(c + 1) * tq],
                               preferred_element_type=F32)
                res = part if res is None else res + part
            acc = corr * carry[h][2] + res[0:HEAD_DIM, :]
            l = corr * carry[h][1] + res[HEAD_DIM:HEAD_DIM + 1, :]
            new.append((m_new, l, acc))
        return tuple(new)

    def pair_step(jj, carry):
        return absorb(2 * jj, scores_of(pl.multiple_of(jj * (2 * tq), 2 * tq), 2 * tq), carry)

    def finish(carry):
        out_t = jnp.concatenate([c[2] / c[1] for c in carry], axis=0)
        o_ref[...] = out_t.T

    init1 = (jnp.full((1, tq), NEG_BIG, F32), jnp.zeros((1, tq), F32), jnp.zeros((HEAD_DIM, tq), F32))
    carry = lax.fori_loop(0, qi // 2, pair_step, (init1,) * nh)

    @pl.when(qi % 2 == 0)
    def _():
        diag = scores_of(pl.multiple_of(qi * tq, tq), tq)
        finish(absorb(qi, tuple(jnp.where(causal, s, NEG_BIG) for s in diag), carry))

    @pl.when(qi % 2 == 1)
    def _():
        last = scores_of(pl.multiple_of((qi - 1) * tq, 2 * tq), 2 * tq)
        causal2 = (lax.broadcasted_iota(jnp.int32, (2 * tq, tq), 0)
                   <= lax.broadcasted_iota(jnp.int32, (2 * tq, tq), 1) + tq)
        finish(absorb(qi - 1, tuple(jnp.where(causal2, s, NEG_BIG) for s in last), carry))


def _attention(q, k, vt, eq, ek):
    bsz, seq, _ = q.shape
    tq = ATTN_BLOCK
    nh = ATTN_HEADS_PER_STEP
    assert seq % tq == 0 and ATTN_HEADS % nh == 0 and nh % 2 == 0
    w = nh * HEAD_DIM
    return pl.pallas_call(
        functools.partial(_attn_kernel, tq=tq, seq=seq, nh=nh),
        grid=(bsz, ATTN_HEADS // nh, seq // tq),
        in_specs=[pl.BlockSpec((None, tq, w), lambda b, g, i: (b, i, g)),
                  pl.BlockSpec((None, seq, w), lambda b, g, i: (b, 0, g)),
                  pl.BlockSpec((None, seq // tq, nh * (HEAD_DIM + ONES_ROWS), tq), lambda b, g, i: (b, 0, g, 0)),
                  pl.BlockSpec((None, tq, nh * LANES), lambda b, g, i: (b, i, g)),
                  pl.BlockSpec((None, seq, nh * LANES), lambda b, g, i: (b, 0, g))],
        out_specs=pl.BlockSpec((None, tq, w), lambda b, g, i: (b, i, g)),
        out_shape=jax.ShapeDtypeStruct((bsz, seq, ATTN_WIDTH), F32),
        compiler_params=pltpu.CompilerParams(dimension_semantics=("parallel", "parallel", "arbitrary"),
                                             vmem_limit_bytes=VMEM_LIMIT),
        name="fox_attention",
    )(q, k, vt, eq, ek)


def _s5_kernel(u_ref, bw_ref, cw_ref, ar_ref, ai_ref, d_ref, wglu_ref, bglu_ref, g_ref, o_ref,
               state_ref, bu_ref, *, tc, bsz):
    @pl.when(pl.program_id(0) == 0)
    def _():
        state_ref[...] = jnp.zeros_like(state_ref)

    rows = tc * bsz
    ns = SLAB_STATE
    u = u_ref[...].reshape(rows, SSM_WIDTH)
    ub = u.astype(BF16)
    for i in range(S5_SLABS):
        bu_ref[i] = jnp.dot(ub[:, SLAB_CH * i:SLAB_CH * (i + 1)], bw_ref[i], preferred_element_type=F32)
    for i in range(S5_SLABS):
        ar = jnp.broadcast_to(ar_ref[i], (bsz, ns))
        ai = jnp.broadcast_to(ai_ref[i], (bsz, ns))
        xr = state_ref[i, :, 0:ns]
        xi = state_ref[i, :, ns:2 * ns]
        for t in range(tc):
            r0 = t * bsz
            nr = ar * xr - ai * xi + bu_ref[i, r0:r0 + bsz, 0:ns]
            ni = ar * xi + ai * xr + bu_ref[i, r0:r0 + bsz, ns:2 * ns]
            bu_ref[i, r0:r0 + bsz, 0:ns] = nr
            bu_ref[i, r0:r0 + bsz, ns:2 * ns] = ni
            xr, xi = nr, ni
        state_ref[i, :, 0:ns] = xr
        state_ref[i, :, ns:2 * ns] = xi
    ys = [jnp.dot(bu_ref[i].astype(BF16), cw_ref[i], preferred_element_type=F32) for i in range(S5_SLABS)]
    y = jnp.concatenate(ys, axis=1) + d_ref[...] * u
    y = 0.5 * y * (1.0 + jnp.tanh(math.sqrt(2.0 / math.pi) * (y + 0.044715 * (y * y * y))))
    z = jnp.dot(y.astype(BF16), wglu_ref[...], preferred_element_type=F32) + bglu_ref[...]
    y = y * (1.0 / (1.0 + jnp.exp(-z)))
    y = y * lax.rsqrt(jnp.mean(y * y, axis=-1, keepdims=True) + RMS_EPS) * g_ref[...]
    o_ref[...] = y.astype(BF16).reshape(tc, bsz, SSM_WIDTH)


def _s5_weights(lam_re, lam_im, log_step, b_re, b_im, c_re, c_im):
    delta = jnp.exp(log_step)[:, None]
    mag = jnp.exp(lam_re * delta)
    ar = mag * jnp.cos(lam_im * delta)
    ai = mag * jnp.sin(lam_im * delta)
    den = lam_re * lam_re + lam_im * lam_im
    num_re = ar - 1.0
    coef_re = (num_re * lam_re + ai * lam_im) / den
    coef_im = (ai * lam_re - num_re * lam_im) / den
    bbar_re = coef_re[..., None] * b_re - coef_im[..., None] * b_im
    bbar_im = coef_re[..., None] * b_im + coef_im[..., None] * b_re
    gs = SSM_GROUPS // S5_SLABS
    eye = jnp.eye(gs, dtype=F32)

    def b_slab(bb):
        bb = bb.reshape(S5_SLABS, gs, SSM_STATE, SSM_GROUP)
        w = jnp.einsum('ab,sapc->sacbp', eye, bb)
        return w.reshape(S5_SLABS, gs * SSM_GROUP, gs * SSM_STATE)

    def c_slab(cc):
        cc = cc.reshape(S5_SLABS, gs, SSM_GROUP, SSM_STATE)
        w = jnp.einsum('ab,sacp->sbpac', eye, cc)
        return w.reshape(S5_SLABS, gs * SSM_STATE, gs * SSM_GROUP)

    bw = jnp.concatenate([b_slab(bbar_re), b_slab(bbar_im)], axis=2).astype(BF16)
    cw = jnp.concatenate([c_slab(c_re), -c_slab(c_im)], axis=1).astype(BF16)
    ar_s = ar.reshape(S5_SLABS, 1, SLAB_STATE)
    ai_s = ai.reshape(S5_SLABS, 1, SLAB_STATE)
    return bw, cw, ar_s, ai_s


def _s5(u_tb, bw, cw, ar_s, ai_s, d_row, w_glu, b_glu_row, g_row):
    seq, bsz, _ = u_tb.shape
    tc = S5_TIME_BLOCK
    assert seq % tc == 0 and bsz % 16 == 0
    full = lambda shape: pl.BlockSpec(shape, lambda t: (0,) * len(shape))
    return pl.pallas_call(
        functools.partial(_s5_kernel, tc=tc, bsz=bsz),
        grid=(seq // tc,),
        in_specs=[pl.BlockSpec((tc, bsz, SSM_WIDTH), lambda t: (t, 0, 0)),
                  full(bw.shape), full(cw.shape), full(ar_s.shape), full(ai_s.shape),
                  full((1, SSM_WIDTH)), full((SSM_WIDTH, SSM_WIDTH)), full((1, SSM_WIDTH)),
                  full((1, SSM_WIDTH))],
        out_specs=pl.BlockSpec((tc, bsz, SSM_WIDTH), lambda t: (t, 0, 0)),
        out_shape=jax.ShapeDtypeStruct((seq, bsz, SSM_WIDTH), BF16),
        scratch_shapes=[pltpu.VMEM((S5_SLABS, bsz, 2 * SLAB_STATE), F32),
                        pltpu.VMEM((S5_SLABS, tc * bsz, 2 * SLAB_STATE), F32)],
        compiler_params=pltpu.CompilerParams(dimension_semantics=("arbitrary",),
                                             vmem_limit_bytes=VMEM_LIMIT),
        name="s5_branch",
    )(u_tb, bw, cw, ar_s, ai_s, d_row, w_glu, b_glu_row, g_row)


def _layer_norm_rows(v, g, b):
    mu = jnp.mean(v, axis=-1, keepdims=True)
    c = v - mu
    var = jnp.mean(c * c, axis=-1, keepdims=True)
    return c * lax.rsqrt(var + LN_EPS) * g + b


def _outproj_kernel(attn_ref, ssm_ref, x_ref, wo_ref, ag_ref, g1_ref, b1_ref, wrt_ref, brt_ref, triu_ref,
                    h_ref, eid_ref, gate_ref, rank_ref, cnt_ref):
    a = attn_ref[...]
    a = a * lax.rsqrt(jnp.mean(a * a, axis=-1, keepdims=True) + RMS_EPS) * ag_ref[...]
    mix = (jnp.dot(a.astype(BF16), wo_ref[0:ATTN_WIDTH, :], preferred_element_type=F32)
           + jnp.dot(ssm_ref[...], wo_ref[ATTN_WIDTH:, :], preferred_element_type=F32))
    h = _layer_norm_rows(DEEPNORM_ALPHA * x_ref[...] + mix, g1_ref[...], b1_ref[...])
    h_ref[...] = h

    logits = lax.dot_general(wrt_ref[...], h.astype(BF16), (((1,), (1,)), ((), ())),
                             preferred_element_type=F32)[:ROUTER_ROWS] + brt_ref[...]
    tm = logits.shape[1]
    row = lax.broadcasted_iota(jnp.int32, (ROUTER_ROWS, tm), 0).astype(F32)
    no_row = float(LANES)
    is_g = row < N_EXPERT_GROUPS
    gl = jnp.where(is_g, logits, NEG_BIG)
    gmax = jnp.max(gl, axis=0, keepdims=True)
    g_idx = jnp.min(jnp.where(is_g & (gl == gmax), row, no_row), axis=0, keepdims=True)
    g_p = 1.0 / jnp.sum(jnp.where(is_g, jnp.exp(gl - gmax), 0.0), axis=0, keepdims=True)
    first = N_EXPERT_GROUPS + g_idx * EXPERTS_PER_GROUP
    in_grp = (row >= first) & (row < first + EXPERTS_PER_GROUP)
    el = jnp.where(in_grp, logits, NEG_BIG)
    m1 = jnp.max(el, axis=0, keepdims=True)
    i1 = jnp.min(jnp.where(in_grp & (el == m1), row, no_row), axis=0, keepdims=True)
    rest = in_grp & (row != i1)
    el2 = jnp.where(rest, logits, NEG_BIG)
    m2 = jnp.max(el2, axis=0, keepdims=True)
    i2 = jnp.min(jnp.where(rest & (el2 == m2), row, no_row), axis=0, keepdims=True)
    r = jnp.exp(m2 - m1)
    p1 = 1.0 / (1.0 + r)
    p2 = r / (1.0 + r)
    k2 = lax.broadcasted_iota(jnp.int32, (2, tm), 0)
    eid_ref[...] = (jnp.where(k2 == 0, i1, i2) - N_EXPERT_GROUPS).astype(jnp.int32)
    gate_ref[...] = g_p * jnp.where(k2 == 0, p1, p2)

    @pl.when(pl.program_id(0) == 0)
    def _():
        cnt_ref[...] = jnp.zeros_like(cnt_ref)
    hit1 = row == i1
    hit2 = row == i2
    onehot = jnp.where(hit1 | hit2, 1.0, 0.0)
    cnt = cnt_ref[...]
    before = jnp.dot(onehot.astype(BF16), triu_ref[...], preferred_element_type=F32) + cnt
    r1 = jnp.sum(jnp.where(hit1, before, 0.0), axis=0, keepdims=True)
    r2 = jnp.sum(jnp.where(hit2, before, 0.0), axis=0, keepdims=True)
    rank_ref[...] = jnp.where(k2 == 0, r1, r2).astype(jnp.int32)
    cnt_ref[...] = cnt + jnp.sum(onehot, axis=1, keepdims=True)


def _outproj(attn2, ssm2, x2, wo, ag_row, g1_row, b1_row, wrt, brt):
    n_tok = x2.shape[0]
    tm = ROW_BLOCK
    triu = jnp.triu(jnp.ones((tm, tm), BF16), 1)
    row_spec = lambda w: pl.BlockSpec((tm, w), lambda i: (i, 0))
    col_spec = lambda r: pl.BlockSpec((r, tm), lambda i: (0, i))
    full = lambda shape: pl.BlockSpec(shape, lambda i: (0,) * len(shape))
    return pl.pallas_call(
        _outproj_kernel,
        grid=(n_tok // tm,),
        in_specs=[row_spec(ATTN_WIDTH), row_spec(SSM_WIDTH), row_spec(D_MODEL),
                  full((D_MODEL, D_MODEL)), full((1, ATTN_WIDTH)), full((1, D_MODEL)), full((1, D_MODEL)),
                  full((LANES, D_MODEL)), full((ROUTER_ROWS, 1)), full((tm, tm))],
        out_specs=[row_spec(D_MODEL), col_spec(2), col_spec(2), col_spec(2), full((ROUTER_ROWS, 1))],
        out_shape=[jax.ShapeDtypeStruct((n_tok, D_MODEL), F32),
                   jax.ShapeDtypeStruct((2, n_tok), jnp.int32),
                   jax.ShapeDtypeStruct((2, n_tok), F32),
                   jax.ShapeDtypeStruct((2, n_tok), jnp.int32),
                   jax.ShapeDtypeStruct((ROUTER_ROWS, 1), F32)],
        compiler_params=pltpu.CompilerParams(dimension_semantics=("arbitrary",),
                                             vmem_limit_bytes=VMEM_LIMIT),
        name="outproj_ln_router",
    )(attn2, ssm2, x2, wo, ag_row, g1_row, b1_row, wrt, brt, triu)


def _dispatch_plan(eid, rank, cnt_col, n_tok):
    counts = cnt_col[N_EXPERT_GROUPS:N_EXPERT_GROUPS + N_EXPERTS, 0].astype(jnp.int32)
    padded = ((counts + MOE_BLOCK - 1) // MOE_BLOCK) * MOE_BLOCK
    pends = jnp.cumsum(padded)
    pstarts = pends - padded
    n_pad = ((2 * n_tok + N_EXPERTS * (MOE_BLOCK - 1) + MOE_BLOCK - 1) // MOE_BLOCK) * MOE_BLOCK
    n_blocks = n_pad // MOE_BLOCK
    blk0 = jnp.arange(n_blocks, dtype=jnp.int32) * MOE_BLOCK
    block_e = jnp.minimum(jnp.sum((pends[None, :] <= blk0[:, None]).astype(jnp.int32), axis=1), N_EXPERTS - 1)
    n_used = (pends[-1] // MOE_BLOCK).astype(jnp.int32).reshape(1)
    tail = jnp.where(padded > 0, pends - MOE_BLOCK, -1).astype(jnp.int32)

    tb = TOKEN_BLOCK
    nb = n_tok // tb
    onehot = eid[:, :, None] == jnp.arange(N_EXPERTS, dtype=jnp.int32)[None, None, :]
    run_len = jnp.sum(onehot.reshape(2, nb, tb, N_EXPERTS).astype(jnp.int32), axis=(0, 2))
    before = jnp.cumsum(run_len, axis=0) - run_len
    run_src = jnp.cumsum(run_len, axis=1) - run_len
    run_dst = pstarts[None, :] + before
    shift = jnp.broadcast_to((run_src - before)[:, None, :], (nb, tb, N_EXPERTS)).reshape(n_tok, N_EXPERTS)
    local = rank + jnp.sum(jnp.where(onehot, shift[None], 0), axis=-1)
    any_long = (jnp.max(run_len, axis=1) >= (1 << SHORT_RUN_BITS)).astype(jnp.int32)
    runs = (run_len.reshape(-1), run_src.reshape(-1), run_dst.reshape(-1), any_long)
    return local, runs, block_e, n_used, tail, n_blocks


ROW_SUB = D_MODEL // LANES


def _rows(start, size):
    return pl.ds(pl.multiple_of(start * ROW_SUB, ROW_SUB), size * ROW_SUB)


def _store_row_tiles(ref, row0, v):
    n = v.shape[0]
    for s in range(ROW_SUB):
        ref[pl.ds(row0 * ROW_SUB + s, n, stride=ROW_SUB), :] = v[:, LANES * s:LANES * (s + 1)]


def _load_row_tiles(ref, row0, n):
    return jnp.concatenate([ref[pl.ds(row0 * ROW_SUB + s, n, stride=ROW_SUB), :] for s in range(ROW_SUB)], axis=1)


def _for_each_run_piece(blk, len_ref, src_ref, dst_ref, bits, make_copy, act):
    for e in range(N_EXPERTS):
        idx = blk * N_EXPERTS + e
        n = len_ref[idx]
        src = src_ref[idx]
        dst = dst_ref[idx]
        for b in reversed(range(bits)):
            @pl.when((n & (1 << b)) != 0)
            def _():
                off = lax.shift_left(lax.shift_right_logical(n, b + 1), b + 1)
                act(make_copy(src + off, dst + off, 1 << b))


def _by_run_size(long_ref, blk, body):
    @pl.when(long_ref[blk] == 0)
    def _():
        body(SHORT_RUN_BITS)

    @pl.when(long_ref[blk] != 0)
    def _():
        body(RUN_BITS)


def _dispatch_kernel(tail_ref, nu_ref, len_ref, src_ref, dst_ref, long_ref, local_ref, h_ref, xb_hbm, zbuf,
                     pk_ref, sem_z, sem, *, n_blocks):
    i = pl.program_id(0)
    nb = pl.num_programs(0) - 1
    tb = h_ref.shape[0]
    cur = i % 2

    @pl.when(i == 0)
    def _():
        zbuf[...] = jnp.zeros_like(zbuf)

        def zero_copy(row0):
            return pltpu.make_async_copy(zbuf, xb_hbm.at[_rows(row0, MOE_BLOCK)], sem_z)

        def each_zero_block(fn):
            for e in range(N_EXPERTS):
                @pl.when(tail_ref[e] >= 0)
                def _():
                    fn(zero_copy(tail_ref[e]))

            def unused(b, carry):
                fn(zero_copy(b * MOE_BLOCK))
                return carry
            lax.fori_loop(nu_ref[0], n_blocks, unused, 0)

        each_zero_block(lambda c: c.start())
        each_zero_block(lambda c: c.wait())

    def drain(which):
        pltpu.make_async_copy(pk_ref.at[which], xb_hbm.at[_rows(0, 2 * tb)], sem.at[which]).wait()

    @pl.when(i >= 2)
    def _():
        drain(cur)

    def group():
        pos = lax.broadcasted_iota(jnp.int32, (2 * tb, tb), 0)
        hit = (pos == local_ref[0:1, :]) | (pos == local_ref[1:2, :])
        perm = jnp.where(hit, 1.0, 0.0).astype(BF16)
        _store_row_tiles(pk_ref.at[cur], 0, jnp.dot(perm, h_ref[...].astype(BF16), preferred_element_type=F32))

    def send(bits):
        def run_copy(src, dst, size):
            return pltpu.make_async_copy(pk_ref.at[1 - cur, _rows(src, size)], xb_hbm.at[_rows(dst, size)],
                                         sem.at[1 - cur])
        _for_each_run_piece(i - 1, len_ref, src_ref, dst_ref, bits, run_copy, lambda c: c.start())

    @pl.when(i == 0)
    def _():
        group()

    @pl.when((i > 0) & (i < nb))
    def _():
        def send_and_group(bits):
            send(bits)
            group()
        _by_run_size(long_ref, i - 1, send_and_group)

    @pl.when(i == nb)
    def _():
        send(RUN_BITS)
        drain(1 - cur)


def _dispatch(h, local, runs, tail, n_used, n_blocks):
    n_tok = h.shape[0]
    tb = TOKEN_BLOCK
    nb = n_tok // tb
    grid_spec = pltpu.PrefetchScalarGridSpec(
        num_scalar_prefetch=6,
        grid=(nb + 1,),
        in_specs=[pl.BlockSpec((2, tb), lambda i, *_: (0, jnp.minimum(i, nb - 1))),
                  pl.BlockSpec((tb, D_MODEL), lambda i, *_: (jnp.minimum(i, nb - 1), 0))],
        out_specs=pl.BlockSpec(memory_space=pl.ANY),
        scratch_shapes=[pltpu.VMEM((MOE_BLOCK * ROW_SUB, LANES), F32),
                        pltpu.VMEM((2, 2 * tb * ROW_SUB, LANES), F32),
                        pltpu.SemaphoreType.DMA, pltpu.SemaphoreType.DMA((2,))],
    )
    return pl.pallas_call(
        functools.partial(_dispatch_kernel, n_blocks=n_blocks),
        grid_spec=grid_spec,
        out_shape=jax.ShapeDtypeStruct((n_blocks * MOE_BLOCK * ROW_SUB, LANES), F32),
        compiler_params=pltpu.CompilerParams(dimension_semantics=("arbitrary",),
                                             vmem_limit_bytes=VMEM_LIMIT),
        name="moe_dispatch",
    )(tail, n_used, *runs, local, h)


def _expert_kernel(be_ref, nu_ref, x_ref, wg_ref, wu_ref, wd_ref, y_ref, wgu_t, wd_t, yt_ref):
    i = pl.program_id(0)
    used = i < nu_ref[0]

    @pl.when(used & ((i == 0) | (be_ref[i] != be_ref[jnp.maximum(i - 1, 0)])))
    def _():
        wgu_t[0:D_EXPERT, :] = wg_ref[...].T.astype(BF16)
        wgu_t[D_EXPERT:, :] = wu_ref[...].T.astype(BF16)
        wd_t[...] = wd_ref[...].T.astype(BF16)

    @pl.when(used)
    def _():
        sub = range(MOE_BLOCK // MOE_SUB)
        gus = []
        for c in sub:
            xb = _load_row_tiles(x_ref, c * MOE_SUB, MOE_SUB).astype(BF16)
            gus.append(lax.dot_general(wgu_t[...], xb, (((1,), (1,)), ((), ())),
                                       preferred_element_type=F32))
        hids = []
        for c in sub:
            gt = gus[c][0:D_EXPERT]
            hids.append((gt * (1.0 / (1.0 + jnp.exp(-gt))) * gus[c][D_EXPERT:]).astype(BF16))
        for c in sub:
            yt_ref[c] = jnp.dot(wd_t[...], hids[c], preferred_element_type=F32)
        for c in sub:
            _store_row_tiles(y_ref, c * MOE_SUB, yt_ref[c].T)

    @pl.when(jnp.logical_not(used))
    def _():
        y_ref[...] = jnp.zeros_like(y_ref)


def _experts(xb, block_e, n_used, wg, wu, wd):
    n_blocks = xb.shape[0] // (MOE_BLOCK * ROW_SUB)
    last = lambda i, nu: jnp.minimum(i, nu[0] - 1)
    grid_spec = pltpu.PrefetchScalarGridSpec(
        num_scalar_prefetch=2,
        grid=(n_blocks,),
        in_specs=[pl.BlockSpec((MOE_BLOCK * ROW_SUB, LANES), lambda i, be, nu: (last(i, nu), 0)),
                  pl.BlockSpec((None, D_MODEL, D_EXPERT), lambda i, be, nu: (be[last(i, nu)], 0, 0)),
                  pl.BlockSpec((None, D_MODEL, D_EXPERT), lambda i, be, nu: (be[last(i, nu)], 0, 0)),
                  pl.BlockSpec((None, D_EXPERT, D_MODEL), lambda i, be, nu: (be[last(i, nu)], 0, 0))],
        out_specs=pl.BlockSpec((MOE_BLOCK * ROW_SUB, LANES), lambda i, be, nu: (i, 0)),
        scratch_shapes=[pltpu.VMEM((2 * D_EXPERT, D_MODEL), BF16), pltpu.VMEM((D_MODEL, D_EXPERT), BF16),
                        pltpu.VMEM((MOE_BLOCK // MOE_SUB, D_MODEL, MOE_SUB), F32)],
    )
    return pl.pallas_call(
        _expert_kernel,
        grid_spec=grid_spec,
        out_shape=jax.ShapeDtypeStruct(xb.shape, F32),
        compiler_params=pltpu.CompilerParams(dimension_semantics=("arbitrary",),
                                             vmem_limit_bytes=VMEM_LIMIT),
        name="moe_experts",
    )(block_e, n_used, xb, wg, wu, wd)


def _combine_kernel(len_ref, src_ref, dst_ref, long_ref, local_ref, gate_ref, h_ref, g2_ref, b2_ref, yb_hbm,
                    o_ref, buf, sem):
    i = pl.program_id(0)
    n = pl.num_programs(0)
    tb = h_ref.shape[0]
    cur = i % 2

    def fetch(blk, which, bits):
        def run_copy(src, dst, size):
            return pltpu.make_async_copy(yb_hbm.at[_rows(dst, size)], buf.at[which, _rows(src, size)], sem.at[which])
        _for_each_run_piece(blk, len_ref, src_ref, dst_ref, bits, run_copy, lambda c: c.start())

    def drain(which):
        pltpu.make_async_copy(yb_hbm.at[_rows(0, 2 * tb)], buf.at[which], sem.at[which]).wait()

    @pl.when(i == 0)
    def _():
        fetch(i, 0, RUN_BITS)

    drain(cur)
    nxt = jnp.minimum(i + 1, n - 1)

    def fetch_next_and_combine(bits):
        fetch(nxt, 1 - cur, bits)
        y = _load_row_tiles(buf.at[cur], 0, 2 * tb).astype(BF16)
        pos = lax.broadcasted_iota(jnp.int32, (tb, 2 * tb), 1)
        local = local_ref[...]
        gate = gate_ref[...]
        mix = (jnp.where(pos == local[:, 0:1], gate[:, 0:1], 0.0)
               + jnp.where(pos == local[:, 1:2], gate[:, 1:2], 0.0))
        mix_hi = mix.astype(BF16)
        mix_lo = (mix - mix_hi.astype(F32)).astype(BF16)
        moe = (jnp.dot(mix_hi, y, preferred_element_type=F32) + jnp.dot(mix_lo, y, preferred_element_type=F32))
        o_ref[...] = _layer_norm_rows(DEEPNORM_ALPHA * h_ref[...] + moe, g2_ref[...], b2_ref[...])
    _by_run_size(long_ref, nxt, fetch_next_and_combine)

    @pl.when(i == n - 1)
    def _():
        drain(1 - cur)


def _combine(h, yb, local_t, runs, gate, g2_row, b2_row):
    n_tok = h.shape[0]
    tb = TOKEN_BLOCK
    row_spec = lambda w: pl.BlockSpec((tb, w), lambda i, *_: (i, 0))
    full = lambda shape: pl.BlockSpec(shape, lambda i, *_: (0,) * len(shape))
    grid_spec = pltpu.PrefetchScalarGridSpec(
        num_scalar_prefetch=4,
        grid=(n_tok // tb,),
        in_specs=[row_spec(2), row_spec(2), row_spec(D_MODEL), full((1, D_MODEL)), full((1, D_MODEL)),
                  pl.BlockSpec(memory_space=pl.ANY)],
        out_specs=row_spec(D_MODEL),
        scratch_shapes=[pltpu.VMEM((2, 2 * tb * ROW_SUB, LANES), F32), pltpu.SemaphoreType.DMA((2,))],
    )
    return pl.pallas_call(
        _combine_kernel,
        grid_spec=grid_spec,
        out_shape=jax.ShapeDtypeStruct((n_tok, D_MODEL), F32),
        compiler_params=pltpu.CompilerParams(dimension_semantics=("arbitrary",),
                                             vmem_limit_bytes=VMEM_LIMIT),
        name="combine_ln",
    )(*runs, local_t, gate, h, g2_row, b2_row, yb)


def kernel(x, w_in, b_fgate, s5_lambda_re, s5_lambda_im, s5_log_step, s5_b_re, s5_b_im, s5_c_re, s5_c_im, s5_d, w_glu, b_glu, attn_norm_g, ssm_norm_g, w_out, ln1_g, ln1_b, w_router_group, b_router_group, w_router_expert, b_router_expert, w_gate, w_up, w_down, ln2_g, ln2_b):
    bsz, seq, d = x.shape
    n_tok = bsz * seq
    a = ATTN_WIDTH
    for l in range(DEPTH):
        x2 = x.reshape(n_tok, d)
        w = w_in[l]
        f0 = 3 * a
        w_cat = jnp.concatenate(
            [w[:, :a] * (HEAD_DIM ** -0.5 * LOG2E), w[:, a:2 * a], w[:, f0 + ATTN_HEADS:],
             jnp.pad(w[:, f0:f0 + ATTN_HEADS], ((0, 0), (0, LANES - ATTN_HEADS)))], axis=1).astype(BF16)
        wvt = w[:, 2 * a:f0].T.astype(BF16)
        bf_row = jnp.pad(b_fgate[l], (0, LANES - ATTN_HEADS)).reshape(1, LANES)
        q, k, vt, u, eq, ek = _inproj(x2, w_cat, wvt, bf_row, seq)
        ext_w = ATTN_HEADS * LANES
        attn = _attention(q.reshape(bsz, seq, a), k.reshape(bsz, seq, a), vt,
                          eq.reshape(bsz, seq, ext_w), ek.reshape(bsz, seq, ext_w))

        bw, cw, ar_s, ai_s = _s5_weights(s5_lambda_re[l], s5_lambda_im[l], s5_log_step[l],
                                         s5_b_re[l], s5_b_im[l], s5_c_re[l], s5_c_im[l])
        u_tb = u.reshape(bsz, seq, SSM_WIDTH).transpose(1, 0, 2)
        ssm_tb = _s5(u_tb, bw, cw, ar_s, ai_s, s5_d[l].reshape(1, SSM_WIDTH), w_glu[l].astype(BF16),
                     b_glu[l].reshape(1, SSM_WIDTH), ssm_norm_g[l].reshape(1, SSM_WIDTH))
        ssm2 = ssm_tb.transpose(1, 0, 2).reshape(n_tok, SSM_WIDTH)

        n_route = N_EXPERT_GROUPS + N_EXPERTS
        wrt = jnp.pad(jnp.concatenate([w_router_group[l], w_router_expert[l]], axis=1).T,
                      ((0, LANES - n_route), (0, 0))).astype(BF16)
        brt = jnp.pad(jnp.concatenate([b_router_group[l], b_router_expert[l]]),
                      (0, ROUTER_ROWS - n_route)).reshape(ROUTER_ROWS, 1)
        h, eid, gate, rank, cnt_col = _outproj(
            attn.reshape(n_tok, a), ssm2, x2, w_out[l].astype(BF16), attn_norm_g[l].reshape(1, a),
            ln1_g[l].reshape(1, d), ln1_b[l].reshape(1, d), wrt, brt)

        local, runs, block_e, n_used, tail, n_blocks = _dispatch_plan(eid, rank, cnt_col, n_tok)
        xb = _dispatch(h, local, runs, tail, n_used, n_blocks)
        yb = _experts(xb, block_e, n_used, w_gate[l], w_up[l], w_down[l])
        x = _combine(h, yb, local.T, runs, gate.T, ln2_g[l].reshape(1, d),
                     ln2_b[l].reshape(1, d)).reshape(bsz, seq, d)
    return x
```

```python
import functools
import math

import jax
import jax.numpy as jnp
import numpy as np
from jax import lax
from jax.experimental import pallas as pl
from jax.experimental.pallas import tpu as pltpu

F32 = jnp.float32
BF16 = jnp.bfloat16

D_MODEL = 1024
HEAD_DIM = 64
ATTN_WIDTH = 512
ATTN_HEADS = 8
SSM_WIDTH = 512
SSM_GROUP = 16
SSM_GROUPS = 32
SSM_STATE = 64
N_EXPERT_GROUPS = 4
EXPERTS_PER_GROUP = 8
N_EXPERTS = 32
D_EXPERT = 512
MOE_BLOCK = 512
MOE_SUB = 256
TOKEN_BLOCK = 256
RUN_BITS = TOKEN_BLOCK.bit_length()
SHORT_RUN_BITS = 6
DEPTH = 1
DEEPNORM_ALPHA = (2.0 * DEPTH) ** 0.25
LN_EPS = 1e-5
RMS_EPS = 1e-6

LANES = 128
NEG_BIG = -1e30
LOG2E = math.log2(math.e)
ONES_ROWS = 16
VT_ROWS = ATTN_HEADS * (HEAD_DIM + ONES_ROWS)
PAIR_BIAS_LANES = 8
EXT_W = ATTN_HEADS // 2 * LANES
VMEM_LIMIT = 48 * 1024 * 1024

ROW_BLOCK = 512
ATTN_BLOCK = 256
ATTN_HEADS_PER_STEP = 8
S5_TIME_BLOCK = 64
S5_SLABS = 4
ROUTER_ROWS = 40
SLAB_CH = SSM_WIDTH // S5_SLABS
SLAB_STATE = SSM_GROUPS // S5_SLABS * SSM_STATE


def _split3(c):
    hi = c.astype(BF16)
    r1 = c - hi.astype(F32)
    mid = r1.astype(BF16)
    lo = (r1 - mid.astype(F32)).astype(BF16)
    return hi, mid, lo


def _inproj_kernel(x_ref, w_ref, wvt_ref, bf_ref, sel_ref, q_ref, k_ref, vt_ref, u_ref, eq_ref, ek_ref,
                   carry_ref, *, blocks_per_seq):
    i = pl.program_id(0)

    @pl.when(i % blocks_per_seq == 0)
    def _():
        carry_ref[...] = jnp.zeros_like(carry_ref)

    tm = x_ref.shape[0]
    xb = x_ref[...].astype(BF16)
    z = jnp.dot(xb, w_ref[...], preferred_element_type=F32)
    a = ATTN_WIDTH
    q_ref[...] = z[:, :a].astype(BF16)
    k_ref[...] = z[:, a:2 * a].astype(BF16)
    u_ref[...] = z[:, 2 * a:2 * a + SSM_WIDTH].astype(BF16)
    vt = lax.dot_general(wvt_ref[...], xb, (((1,), (1,)), ((), ())), preferred_element_type=F32).astype(BF16)
    ones_rows = jnp.ones((ONES_ROWS, tm), BF16)
    vt = jnp.concatenate([piece for h in range(ATTN_HEADS)
                          for piece in (vt[HEAD_DIM * h:HEAD_DIM * (h + 1)], ones_rows)], axis=0)
    for jj in range(tm // ATTN_BLOCK):
        vt_ref[jj] = vt[:, jj * ATTN_BLOCK:(jj + 1) * ATTN_BLOCK]
    f = z[:, 2 * a + SSM_WIDTH:] + bf_ref[...]
    logf = jnp.minimum(f, 0.0) - jnp.log1p(jnp.exp(-jnp.abs(f)))
    row = lax.broadcasted_iota(jnp.int32, (tm, tm), 0)
    col = lax.broadcasted_iota(jnp.int32, (tm, tm), 1)
    tri = (row >= col).astype(BF16)
    hi, mid, lo = _split3(logf)
    cum = (jnp.dot(tri, hi, preferred_element_type=F32)
           + jnp.dot(tri, mid, preferred_element_type=F32)
           + jnp.dot(tri, lo, preferred_element_type=F32)) + carry_ref[...]
    carry_ref[...] = cum[tm - 1:tm, :]
    lane = lax.broadcasted_iota(jnp.int32, (tm, LANES), 1)
    c2 = jnp.where(lane < ATTN_HEADS, cum * LOG2E, 0.0)
    hi, mid, lo = (t.astype(F32) for t in _split3(c2))
    packed = (hi + pltpu.roll(mid, ATTN_HEADS, 1) + pltpu.roll(lo, 2 * ATTN_HEADS, 1)
              + jnp.where(lane == 3 * ATTN_HEADS, 1.0, 0.0)).astype(BF16)
    ext = jnp.dot(packed, sel_ref[...], preferred_element_type=F32).astype(BF16)
    eq_ref[...] = ext[:, :EXT_W]
    ek_ref[...] = ext[:, EXT_W:]


def _bias_lane_selector():
    sel = np.zeros((LANES, 2, ATTN_HEADS // 2, LANES), np.float32)
    one = 3 * ATTN_HEADS
    for h in range(ATTN_HEADS):
        pair, base = h // 2, PAIR_BIAS_LANES * (h % 2)
        for t in range(3):
            sel[ATTN_HEADS * t + h, 0, pair, base + t] = 1.0
            sel[one, 0, pair, base + 3 + t] = 1.0
            sel[one, 1, pair, base + t] = 1.0
            sel[ATTN_HEADS * t + h, 1, pair, base + 3 + t] = -1.0
    return jnp.asarray(sel.reshape(LANES, 2 * EXT_W), BF16)


def _inproj(x2, w_cat, wvt, bf_row, seq):
    n_tok = x2.shape[0]
    tm = ROW_BLOCK
    assert seq % tm == 0 and n_tok % tm == 0 and tm % ATTN_BLOCK == 0
    bsz = n_tok // seq
    bps = seq // tm
    ncol = w_cat.shape[1]
    vblk = tm // ATTN_BLOCK
    row_spec = lambda w: pl.BlockSpec((tm, w), lambda i: (i, 0))
    return pl.pallas_call(
        functools.partial(_inproj_kernel, blocks_per_seq=bps),
        grid=(n_tok // tm,),
        in_specs=[row_spec(D_MODEL),
                  pl.BlockSpec((D_MODEL, ncol), lambda i: (0, 0)),
                  pl.BlockSpec((ATTN_WIDTH, D_MODEL), lambda i: (0, 0)),
                  pl.BlockSpec((1, LANES), lambda i: (0, 0)),
                  pl.BlockSpec((LANES, 2 * EXT_W), lambda i: (0, 0))],
        out_specs=[row_spec(ATTN_WIDTH), row_spec(ATTN_WIDTH),
                   pl.BlockSpec((None, vblk, VT_ROWS, ATTN_BLOCK), lambda i: (i // bps, i % bps, 0, 0)),
                   row_spec(SSM_WIDTH), row_spec(EXT_W), row_spec(EXT_W)],
        out_shape=[jax.ShapeDtypeStruct((n_tok, ATTN_WIDTH), BF16)] * 2
                  + [jax.ShapeDtypeStruct((bsz, seq // ATTN_BLOCK, VT_ROWS, ATTN_BLOCK), BF16),
                     jax.ShapeDtypeStruct((n_tok, SSM_WIDTH), BF16),
                     jax.ShapeDtypeStruct((n_tok, EXT_W), BF16),
                     jax.ShapeDtypeStruct((n_tok, EXT_W), BF16)],
        scratch_shapes=[pltpu.VMEM((1, LANES), F32)],
        compiler_params=pltpu.CompilerParams(dimension_semantics=("arbitrary",),
                                             vmem_limit_bytes=VMEM_LIMIT),
        name="inproj",
    )(x2, w_cat, wvt, bf_row, _bias_lane_selector())


def _attn_kernel(q_ref, k_ref, vt_ref, eq_ref, ek_ref, o_ref, *, tq, seq, nh):
    qi = pl.program_id(2)
    lane = lax.broadcasted_iota(jnp.int32, (tq, LANES), 1)
    q_augs = []
    for h in range(nh):
        qb = q_ref[:, LANES * (h // 2):LANES * (h // 2 + 1)]
        head_lanes = (lane < HEAD_DIM) if h % 2 == 0 else (lane >= HEAD_DIM)
        qh = jnp.where(head_lanes, qb, jnp.zeros((), BF16))
        own = (lane >= PAIR_BIAS_LANES * (h % 2)) & (lane < PAIR_BIAS_LANES * (h % 2 + 1))
        eqh = jnp.where(own, eq_ref[:, LANES * (h // 2):LANES * (h // 2 + 1)], jnp.zeros((), BF16))
        q_augs.append(jnp.concatenate([qh, eqh], axis=1))
    key_i = lax.broadcasted_iota(jnp.int32, (tq, tq), 0)
    qry_i = lax.broadcasted_iota(jnp.int32, (tq, tq), 1)
    causal = key_i <= qry_i
    vrows = HEAD_DIM + ONES_ROWS

    def scores_of(k0, nk):
        out = []
        for h in range(nh):
            kj = k_ref[pl.ds(k0, nk), LANES * (h // 2):LANES * (h // 2 + 1)]
            k_aug = jnp.concatenate([kj, ek_ref[pl.ds(k0, nk), LANES * (h // 2):LANES * (h // 2 + 1)]], axis=1)
            out.append(lax.dot_general(k_aug, q_augs[h], (((1,), (1,)), ((), ())), preferred_element_type=F32))
        return tuple(out)

    def absorb(j0, scores, carry):
        stats = []
        for h in range(nh):
            m = carry[h][0]
            m_new = jnp.maximum(m, jnp.max(scores[h], axis=0, keepdims=True))
            stats.append((m_new, jnp.exp2(m - m_new), jnp.exp2(scores[h] - m_new).astype(BF16)))
        new = []
        for h in range(nh):
            m_new, corr, p = stats[h]
            res = None
            for c in range(p.shape[0] // tq):
                part = jnp.dot(vt_ref[j0 + c, vrows * h:vrows * (h + 1), :], p[c * tq:(c + 1) * tq],
                               preferred_element_type=F32)
                res = part if res is None else res + part
            acc = corr * carry[h][2] + res[0:HEAD_DIM, :]
            l = corr * carry[h][1] + res[HEAD_DIM:HEAD_DIM + 1, :]
            new.append((m_new, l, acc))
        return tuple(new)

    def pair_step(jj, carry):
        return absorb(2 * jj, scores_of(pl.multiple_of(jj * (2 * tq), 2 * tq), 2 * tq), carry)

    def finish(carry):
        out_t = jnp.concatenate([c[2] / c[1] for c in carry], axis=0)
        o_ref[...] = out_t.T

    init1 = (jnp.full((1, tq), NEG_BIG, F32), jnp.zeros((1, tq), F32), jnp.zeros((HEAD_DIM, tq), F32))
    carry = lax.fori_loop(0, qi // 2, pair_step, (init1,) * nh)

    @pl.when(qi % 2 == 0)
    def _():
        diag = scores_of(pl.multiple_of(qi * tq, tq), tq)
        finish(absorb(qi, tuple(jnp.where(causal, s, NEG_BIG) for s in diag), carry))

    @pl.when(qi % 2 == 1)
    def _():
        last = scores_of(pl.multiple_of((qi - 1) * tq, 2 * tq), 2 * tq)
        causal2 = (lax.broadcasted_iota(jnp.int32, (2 * tq, tq), 0)
                   <= lax.broadcasted_iota(jnp.int32, (2 * tq, tq), 1) + tq)
        finish(absorb(qi - 1, tuple(jnp.where(causal2, s, NEG_BIG) for s in last), carry))


def _attention(q, k, vt, eq, ek):
    bsz, seq, _ = q.shape
    tq = ATTN_BLOCK
    nh = ATTN_HEADS_PER_STEP
    assert seq % tq == 0 and ATTN_HEADS % nh == 0 and nh % 2 == 0
    w = nh * HEAD_DIM
    return pl.pallas_call(
        functools.partial(_attn_kernel, tq=tq, seq=seq, nh=nh),
        grid=(bsz, ATTN_HEADS // nh, seq // tq),
        in_specs=[pl.BlockSpec((None, tq, w), lambda b, g, i: (b, i, g)),
                  pl.BlockSpec((None, seq, w), lambda b, g, i: (b, 0, g)),
                  pl.BlockSpec((None, seq // tq, nh * (HEAD_DIM + ONES_ROWS), tq), lambda b, g, i: (b, 0, g, 0)),
                  pl.BlockSpec((None, tq, nh // 2 * LANES), lambda b, g, i: (b, i, g)),
                  pl.BlockSpec((None, seq, nh // 2 * LANES), lambda b, g, i: (b, 0, g))],
        out_specs=pl.BlockSpec((None, tq, w), lambda b, g, i: (b, i, g)),
        out_shape=jax.ShapeDtypeStruct((bsz, seq, ATTN_WIDTH), F32),
        compiler_params=pltpu.CompilerParams(dimension_semantics=("parallel", "parallel", "arbitrary"),
                                             vmem_limit_bytes=VMEM_LIMIT),
        name="fox_attention",
    )(q, k, vt, eq, ek)


def _s5_kernel(u_ref, bw_ref, cw_ref, ar_ref, ai_ref, d_ref, wglu_ref, bglu_ref, g_ref, o_ref,
               state_ref, bu_ref, *, tc, bsz):
    @pl.when(pl.program_id(0) == 0)
    def _():
        state_ref[...] = jnp.zeros_like(state_ref)

    rows = tc * bsz
    ns = SLAB_STATE
    u = u_ref[...].reshape(rows, SSM_WIDTH)
    ub = u
    for i in range(S5_SLABS):
        bu_ref[i] = jnp.dot(ub[:, SLAB_CH * i:SLAB_CH * (i + 1)], bw_ref[i], preferred_element_type=F32)
    for i in range(S5_SLABS):
        ar = jnp.broadcast_to(ar_ref[i], (bsz, ns))
        ai = jnp.broadcast_to(ai_ref[i], (bsz, ns))
        xr = state_ref[i, :, 0:ns]
        xi = state_ref[i, :, ns:2 * ns]
        for t in range(tc):
            r0 = t * bsz
            nr = ar * xr - ai * xi + bu_ref[i, r0:r0 + bsz, 0:ns]
            ni = ar * xi + ai * xr + bu_ref[i, r0:r0 + bsz, ns:2 * ns]
            bu_ref[i, r0:r0 + bsz, 0:ns] = nr
            bu_ref[i, r0:r0 + bsz, ns:2 * ns] = ni
            xr, xi = nr, ni
        state_ref[i, :, 0:ns] = xr
        state_ref[i, :, ns:2 * ns] = xi
    ys = [jnp.dot(bu_ref[i].astype(BF16), cw_ref[i], preferred_element_type=F32) for i in range(S5_SLABS)]
    y = jnp.concatenate(ys, axis=1) + d_ref[...] * u.astype(F32)
    y = 0.5 * y * (1.0 + jnp.tanh(math.sqrt(2.0 / math.pi) * (y + 0.044715 * (y * y * y))))
    z = jnp.dot(y.astype(BF16), wglu_ref[...], preferred_element_type=F32) + bglu_ref[...]
    y = y * (1.0 / (1.0 + jnp.exp(-z)))
    y = y * lax.rsqrt(jnp.mean(y * y, axis=-1, keepdims=True) + RMS_EPS) * g_ref[...]
    o_ref[...] = y.astype(BF16).reshape(tc, bsz, SSM_WIDTH)


def _s5_weights(lam_re, lam_im, log_step, b_re, b_im, c_re, c_im):
    delta = jnp.exp(log_step)[:, None]
    mag = jnp.exp(lam_re * delta)
    ar = mag * jnp.cos(lam_im * delta)
    ai = mag * jnp.sin(lam_im * delta)
    den = lam_re * lam_re + lam_im * lam_im
    num_re = ar - 1.0
    coef_re = (num_re * lam_re + ai * lam_im) / den
    coef_im = (ai * lam_re - num_re * lam_im) / den
    bbar_re = coef_re[..., None] * b_re - coef_im[..., None] * b_im
    bbar_im = coef_re[..., None] * b_im + coef_im[..., None] * b_re
    gs = SSM_GROUPS // S5_SLABS
    eye = jnp.eye(gs, dtype=F32)

    def b_slab(bb):
        bb = bb.reshape(S5_SLABS, gs, SSM_STATE, SSM_GROUP)
        w = jnp.einsum('ab,sapc->sacbp', eye, bb)
        return w.reshape(S5_SLABS, gs * SSM_GROUP, gs * SSM_STATE)

    def c_slab(cc):
        cc = cc.reshape(S5_SLABS, gs, SSM_GROUP, SSM_STATE)
        w = jnp.einsum('ab,sacp->sbpac', eye, cc)
        return w.reshape(S5_SLABS, gs * SSM_STATE, gs * SSM_GROUP)

    bw = jnp.concatenate([b_slab(bbar_re), b_slab(bbar_im)], axis=2).astype(BF16)
    cw = jnp.concatenate([c_slab(c_re), -c_slab(c_im)], axis=1).astype(BF16)
    ar_s = ar.reshape(S5_SLABS, 1, SLAB_STATE)
    ai_s = ai.reshape(S5_SLABS, 1, SLAB_STATE)
    return bw, cw, ar_s, ai_s


def _s5(u_tb, bw, cw, ar_s, ai_s, d_row, w_glu, b_glu_row, g_row):
    seq, bsz, _ = u_tb.shape
    tc = S5_TIME_BLOCK
    assert seq % tc == 0 and bsz % 16 == 0
    full = lambda shape: pl.BlockSpec(shape, lambda t: (0,) * len(shape))
    return pl.pallas_call(
        functools.partial(_s5_kernel, tc=tc, bsz=bsz),
        grid=(seq // tc,),
        in_specs=[pl.BlockSpec((tc, bsz, SSM_WIDTH), lambda t: (t, 0, 0)),
                  full(bw.shape), full(cw.shape), full(ar_s.shape), full(ai_s.shape),
                  full((1, SSM_WIDTH)), full((SSM_WIDTH, SSM_WIDTH)), full((1, SSM_WIDTH)),
                  full((1, SSM_WIDTH))],
        out_specs=pl.BlockSpec((tc, bsz, SSM_WIDTH), lambda t: (t, 0, 0)),
        out_shape=jax.ShapeDtypeStruct((seq, bsz, SSM_WIDTH), BF16),
        scratch_shapes=[pltpu.VMEM((S5_SLABS, bsz, 2 * SLAB_STATE), F32),
                        pltpu.VMEM((S5_SLABS, tc * bsz, 2 * SLAB_STATE), F32)],
        compiler_params=pltpu.CompilerParams(dimension_semantics=("arbitrary",),
                                             vmem_limit_bytes=VMEM_LIMIT),
        name="s5_branch",
    )(u_tb, bw, cw, ar_s, ai_s, d_row, w_glu, b_glu_row, g_row)


def _layer_norm_rows(v, g, b):
    mu = jnp.mean(v, axis=-1, keepdims=True)
    c = v - mu
    var = jnp.mean(c * c, axis=-1, keepdims=True)
    return c * lax.rsqrt(var + LN_EPS) * g + b


def _outproj_kernel(attn_ref, ssm_ref, x_ref, wo_ref, ag_ref, g1_ref, b1_ref, wrt_ref, brt_ref, triu_ref,
                    h_ref, eid_ref, gate_ref, rank_ref, cnt_ref):
    a = attn_ref[...]
    a = a * lax.rsqrt(jnp.mean(a * a, axis=-1, keepdims=True) + RMS_EPS) * ag_ref[...]
    mix = (jnp.dot(a.astype(BF16), wo_ref[0:ATTN_WIDTH, :], preferred_element_type=F32)
           + jnp.dot(ssm_ref[...], wo_ref[ATTN_WIDTH:, :], preferred_element_type=F32))
    h = _layer_norm_rows(DEEPNORM_ALPHA * x_ref[...] + mix, g1_ref[...], b1_ref[...])
    h_ref[...] = h

    logits = lax.dot_general(wrt_ref[...], h.astype(BF16), (((1,), (1,)), ((), ())),
                             preferred_element_type=F32)[:ROUTER_ROWS] + brt_ref[...]
    tm = logits.shape[1]
    row = lax.broadcasted_iota(jnp.int32, (ROUTER_ROWS, tm), 0).astype(F32)
    no_row = float(LANES)
    is_g = row < N_EXPERT_GROUPS
    gl = jnp.where(is_g, logits, NEG_BIG)
    gmax = jnp.max(gl, axis=0, keepdims=True)
    g_idx = jnp.min(jnp.where(is_g & (gl == gmax), row, no_row), axis=0, keepdims=True)
    g_p = 1.0 / jnp.sum(jnp.where(is_g, jnp.exp(gl - gmax), 0.0), axis=0, keepdims=True)
    first = N_EXPERT_GROUPS + g_idx * EXPERTS_PER_GROUP
    in_grp = (row >= first) & (row < first + EXPERTS_PER_GROUP)
    el = jnp.where(in_grp, logits, NEG_BIG)
    m1 = jnp.max(el, axis=0, keepdims=True)
    i1 = jnp.min(jnp.where(in_grp & (el == m1), row, no_row), axis=0, keepdims=True)
    rest = in_grp & (row != i1)
    el2 = jnp.where(rest, logits, NEG_BIG)
    m2 = jnp.max(el2, axis=0, keepdims=True)
    i2 = jnp.min(jnp.where(rest & (el2 == m2), row, no_row), axis=0, keepdims=True)
    r = jnp.exp(m2 - m1)
    p1 = 1.0 / (1.0 + r)
    p2 = r / (1.0 + r)
    k2 = lax.broadcasted_iota(jnp.int32, (2, tm), 0)
    eid_ref[...] = (jnp.where(k2 == 0, i1, i2) - N_EXPERT_GROUPS).astype(jnp.int32)
    gate_ref[...] = g_p * jnp.where(k2 == 0, p1, p2)

    @pl.when(pl.program_id(0) == 0)
    def _():
        cnt_ref[...] = jnp.zeros_like(cnt_ref)
    hit1 = row == i1
    hit2 = row == i2
    onehot = jnp.where(hit1 | hit2, 1.0, 0.0)
    cnt = cnt_ref[...]
    before = jnp.dot(onehot.astype(BF16), triu_ref[...], preferred_element_type=F32) + cnt
    r1 = jnp.sum(jnp.where(hit1, before, 0.0), axis=0, keepdims=True)
    r2 = jnp.sum(jnp.where(hit2, before, 0.0), axis=0, keepdims=True)
    rank_ref[...] = jnp.where(k2 == 0, r1, r2).astype(jnp.int32)
    cnt_ref[...] = cnt + jnp.sum(onehot, axis=1, keepdims=True)


def _outproj(attn2, ssm2, x2, wo, ag_row, g1_row, b1_row, wrt, brt):
    n_tok = x2.shape[0]
    tm = ROW_BLOCK
    triu = jnp.triu(jnp.ones((tm, tm), BF16), 1)
    row_spec = lambda w: pl.BlockSpec((tm, w), lambda i: (i, 0))
    col_spec = lambda r: pl.BlockSpec((r, tm), lambda i: (0, i))
    full = lambda shape: pl.BlockSpec(shape, lambda i: (0,) * len(shape))
    return pl.pallas_call(
        _outproj_kernel,
        grid=(n_tok // tm,),
        in_specs=[row_spec(ATTN_WIDTH), row_spec(SSM_WIDTH), row_spec(D_MODEL),
                  full((D_MODEL, D_MODEL)), full((1, ATTN_WIDTH)), full((1, D_MODEL)), full((1, D_MODEL)),
                  full((LANES, D_MODEL)), full((ROUTER_ROWS, 1)), full((tm, tm))],
        out_specs=[row_spec(D_MODEL), col_spec(2), col_spec(2), col_spec(2), full((ROUTER_ROWS, 1))],
        out_shape=[jax.ShapeDtypeStruct((n_tok, D_MODEL), F32),
                   jax.ShapeDtypeStruct((2, n_tok), jnp.int32),
                   jax.ShapeDtypeStruct((2, n_tok), F32),
                   jax.ShapeDtypeStruct((2, n_tok), jnp.int32),
                   jax.ShapeDtypeStruct((ROUTER_ROWS, 1), F32)],
        compiler_params=pltpu.CompilerParams(dimension_semantics=("arbitrary",),
                                             vmem_limit_bytes=VMEM_LIMIT),
        name="outproj_ln_router",
    )(attn2, ssm2, x2, wo, ag_row, g1_row, b1_row, wrt, brt, triu)


def _dispatch_plan(eid, rank, cnt_col, n_tok):
    counts = cnt_col[N_EXPERT_GROUPS:N_EXPERT_GROUPS + N_EXPERTS, 0].astype(jnp.int32)
    padded = ((counts + MOE_BLOCK - 1) // MOE_BLOCK) * MOE_BLOCK
    pends = jnp.cumsum(padded)
    pstarts = pends - padded
    n_pad = ((2 * n_tok + N_EXPERTS * (MOE_BLOCK - 1) + MOE_BLOCK - 1) // MOE_BLOCK) * MOE_BLOCK
    n_blocks = n_pad // MOE_BLOCK
    blk0 = jnp.arange(n_blocks, dtype=jnp.int32) * MOE_BLOCK
    block_e = jnp.minimum(jnp.sum((pends[None, :] <= blk0[:, None]).astype(jnp.int32), axis=1), N_EXPERTS - 1)
    n_used = (pends[-1] // MOE_BLOCK).astype(jnp.int32).reshape(1)
    tail = jnp.where(padded > 0, pends - MOE_BLOCK, -1).astype(jnp.int32)

    tb = TOKEN_BLOCK
    nb = n_tok // tb
    onehot = eid[:, :, None] == jnp.arange(N_EXPERTS, dtype=jnp.int32)[None, None, :]
    run_len = jnp.sum(onehot.reshape(2, nb, tb, N_EXPERTS).astype(jnp.int32), axis=(0, 2))
    before = jnp.cumsum(run_len, axis=0) - run_len
    run_src = jnp.cumsum(run_len, axis=1) - run_len
    run_dst = pstarts[None, :] + before
    shift = jnp.broadcast_to((run_src - before)[:, None, :], (nb, tb, N_EXPERTS)).reshape(n_tok, N_EXPERTS)
    local = rank + jnp.sum(jnp.where(onehot, shift[None], 0), axis=-1)
    any_long = (jnp.max(run_len, axis=1) >= (1 << SHORT_RUN_BITS)).astype(jnp.int32)
    runs = (run_len.reshape(-1), run_src.reshape(-1), run_dst.reshape(-1), any_long)
    return local, runs, block_e, n_used, tail, n_blocks


ROW_SUB = D_MODEL // LANES


def _rows(start, size):
    return pl.ds(pl.multiple_of(start * ROW_SUB, ROW_SUB), size * ROW_SUB)


def _store_row_tiles(ref, row0, v):
    n = v.shape[0]
    for s in range(ROW_SUB):
        ref[pl.ds(row0 * ROW_SUB + s, n, stride=ROW_SUB), :] = v[:, LANES * s:LANES * (s + 1)]


def _load_row_tiles(ref, row0, n):
    return jnp.concatenate([ref[pl.ds(row0 * ROW_SUB + s, n, stride=ROW_SUB), :] for s in range(ROW_SUB)], axis=1)


def _for_each_run_piece(blk, len_ref, src_ref, dst_ref, bits, make_copy, act):
    for e in range(N_EXPERTS):
        idx = blk * N_EXPERTS + e
        n = len_ref[idx]
        src = src_ref[idx]
        dst = dst_ref[idx]
        for b in reversed(range(bits)):
            @pl.when((n & (1 << b)) != 0)
            def _():
                off = lax.shift_left(lax.shift_right_logical(n, b + 1), b + 1)
                act(make_copy(src + off, dst + off, 1 << b))


def _by_run_size(long_ref, blk, body):
    @pl.when(long_ref[blk] == 0)
    def _():
        body(SHORT_RUN_BITS)

    @pl.when(long_ref[blk] != 0)
    def _():
        body(RUN_BITS)


def _dispatch_kernel(tail_ref, nu_ref, len_ref, src_ref, dst_ref, long_ref, local_ref, h_ref, xb_hbm, zbuf,
                     pk_ref, sem_z, sem, *, n_blocks):
    i = pl.program_id(0)
    nb = pl.num_programs(0) - 1
    tb = h_ref.shape[0]
    cur = i % 2

    @pl.when(i == 0)
    def _():
        zbuf[...] = jnp.zeros_like(zbuf)

        def zero_copy(row0):
            return pltpu.make_async_copy(zbuf, xb_hbm.at[_rows(row0, MOE_BLOCK)], sem_z)

        def each_zero_block(fn):
            for e in range(N_EXPERTS):
                @pl.when(tail_ref[e] >= 0)
                def _():
                    fn(zero_copy(tail_ref[e]))

            def unused(b, carry):
                fn(zero_copy(b * MOE_BLOCK))
                return carry
            lax.fori_loop(nu_ref[0], n_blocks, unused, 0)

        each_zero_block(lambda c: c.start())
        each_zero_block(lambda c: c.wait())

    def drain(which):
        pltpu.make_async_copy(pk_ref.at[which], xb_hbm.at[_rows(0, 2 * tb)], sem.at[which]).wait()

    @pl.when(i >= 2)
    def _():
        drain(cur)

    def group():
        pos = lax.broadcasted_iota(jnp.int32, (2 * tb, tb), 0)
        hit = (pos == local_ref[0:1, :]) | (pos == local_ref[1:2, :])
        perm = jnp.where(hit, 1.0, 0.0).astype(BF16)
        _store_row_tiles(pk_ref.at[cur], 0, jnp.dot(perm, h_ref[...].astype(BF16), preferred_element_type=F32))

    def send(bits):
        def run_copy(src, dst, size):
            return pltpu.make_async_copy(pk_ref.at[1 - cur, _rows(src, size)], xb_hbm.at[_rows(dst, size)],
                                         sem.at[1 - cur])
        _for_each_run_piece(i - 1, len_ref, src_ref, dst_ref, bits, run_copy, lambda c: c.start())

    @pl.when(i == 0)
    def _():
        group()

    @pl.when((i > 0) & (i < nb))
    def _():
        def send_and_group(bits):
            send(bits)
            group()
        _by_run_size(long_ref, i - 1, send_and_group)

    @pl.when(i == nb)
    def _():
        send(RUN_BITS)
        drain(1 - cur)


def _dispatch(h, local, runs, tail, n_used, n_blocks):
    n_tok = h.shape[0]
    tb = TOKEN_BLOCK
    nb = n_tok // tb
    grid_spec = pltpu.PrefetchScalarGridSpec(
        num_scalar_prefetch=6,
        grid=(nb + 1,),
        in_specs=[pl.BlockSpec((2, tb), lambda i, *_: (0, jnp.minimum(i, nb - 1))),
                  pl.BlockSpec((tb, D_MODEL), lambda i, *_: (jnp.minimum(i, nb - 1), 0))],
        out_specs=pl.BlockSpec(memory_space=pl.ANY),
        scratch_shapes=[pltpu.VMEM((MOE_BLOCK * ROW_SUB, LANES), F32),
                        pltpu.VMEM((2, 2 * tb * ROW_SUB, LANES), F32),
                        pltpu.SemaphoreType.DMA, pltpu.SemaphoreType.DMA((2,))],
    )
    return pl.pallas_call(
        functools.partial(_dispatch_kernel, n_blocks=n_blocks),
        grid_spec=grid_spec,
        out_shape=jax.ShapeDtypeStruct((n_blocks * MOE_BLOCK * ROW_SUB, LANES), F32),
        compiler_params=pltpu.CompilerParams(dimension_semantics=("arbitrary",),
                                             vmem_limit_bytes=VMEM_LIMIT),
        name="moe_dispatch",
    )(tail, n_used, *runs, local, h)


def _expert_kernel(be_ref, nu_ref, x_ref, wg_ref, wu_ref, wd_ref, y_ref, wgu_t, wd_t, yt_ref):
    i = pl.program_id(0)
    used = i < nu_ref[0]

    @pl.when(used & ((i == 0) | (be_ref[i] != be_ref[jnp.maximum(i - 1, 0)])))
    def _():
        wgu_t[0:D_EXPERT, :] = wg_ref[...].T.astype(BF16)
        wgu_t[D_EXPERT:, :] = wu_ref[...].T.astype(BF16)
        wd_t[...] = wd_ref[...].T.astype(BF16)

    @pl.when(used)
    def _():
        sub = range(MOE_BLOCK // MOE_SUB)
        gus = []
        for c in sub:
            xb = _load_row_tiles(x_ref, c * MOE_SUB, MOE_SUB).astype(BF16)
            gus.append(lax.dot_general(wgu_t[...], xb, (((1,), (1,)), ((), ())),
                                       preferred_element_type=F32))
        hids = []
        for c in sub:
            gt = gus[c][0:D_EXPERT]
            hids.append((gt * (1.0 / (1.0 + jnp.exp(-gt))) * gus[c][D_EXPERT:]).astype(BF16))
        for c in sub:
            yt_ref[c] = jnp.dot(wd_t[...], hids[c], preferred_element_type=F32)
        for c in sub:
            _store_row_tiles(y_ref, c * MOE_SUB, yt_ref[c].T)

    @pl.when(jnp.logical_not(used))
    def _():
        y_ref[...] = jnp.zeros_like(y_ref)


def _experts(xb, block_e, n_used, wg, wu, wd):
    n_blocks = xb.shape[0] // (MOE_BLOCK * ROW_SUB)
    last = lambda i, nu: jnp.minimum(i, nu[0] - 1)
    grid_spec = pltpu.PrefetchScalarGridSpec(
        num_scalar_prefetch=2,
        grid=(n_blocks,),
        in_specs=[pl.BlockSpec((MOE_BLOCK * ROW_SUB, LANES), lambda i, be, nu: (last(i, nu), 0)),
                  pl.BlockSpec((None, D_MODEL, D_EXPERT), lambda i, be, nu: (be[last(i, nu)], 0, 0)),
                  pl.BlockSpec((None, D_MODEL, D_EXPERT), lambda i, be, nu: (be[last(i, nu)], 0, 0)),
                  pl.BlockSpec((None, D_EXPERT, D_MODEL), lambda i, be, nu: (be[last(i, nu)], 0, 0))],
        out_specs=pl.BlockSpec((MOE_BLOCK * ROW_SUB, LANES), lambda i, be, nu: (i, 0)),
        scratch_shapes=[pltpu.VMEM((2 * D_EXPERT, D_MODEL), BF16), pltpu.VMEM((D_MODEL, D_EXPERT), BF16),
                        pltpu.VMEM((MOE_BLOCK // MOE_SUB, D_MODEL, MOE_SUB), F32)],
    )
    return pl.pallas_call(
        _expert_kernel,
        grid_spec=grid_spec,
        out_shape=jax.ShapeDtypeStruct(xb.shape, F32),
        compiler_params=pltpu.CompilerParams(dimension_semantics=("arbitrary",),
                                             vmem_limit_bytes=VMEM_LIMIT),
        name="moe_experts",
    )(block_e, n_used, xb, wg, wu, wd)


def _combine_kernel(len_ref, src_ref, dst_ref, long_ref, local_ref, gate_ref, h_ref, g2_ref, b2_ref, yb_hbm,
                    o_ref, buf, sem):
    i = pl.program_id(0)
    n = pl.num_programs(0)
    tb = h_ref.shape[0]
    cur = i % 2

    def fetch(blk, which, bits):
        def run_copy(src, dst, size):
            return pltpu.make_async_copy(yb_hbm.at[_rows(dst, size)], buf.at[which, _rows(src, size)], sem.at[which])
        _for_each_run_piece(blk, len_ref, src_ref, dst_ref, bits, run_copy, lambda c: c.start())

    def drain(which):
        pltpu.make_async_copy(yb_hbm.at[_rows(0, 2 * tb)], buf.at[which], sem.at[which]).wait()

    @pl.when(i == 0)
    def _():
        fetch(i, 0, RUN_BITS)

    drain(cur)
    nxt = jnp.minimum(i + 1, n - 1)

    def fetch_next_and_combine(bits):
        fetch(nxt, 1 - cur, bits)
        y = _load_row_tiles(buf.at[cur], 0, 2 * tb).astype(BF16)
        pos = lax.broadcasted_iota(jnp.int32, (tb, 2 * tb), 1)
        local = local_ref[...]
        gate = gate_ref[...]
        mix = (jnp.where(pos == local[:, 0:1], gate[:, 0:1], 0.0)
               + jnp.where(pos == local[:, 1:2], gate[:, 1:2], 0.0))
        mix_hi = mix.astype(BF16)
        mix_lo = (mix - mix_hi.astype(F32)).astype(BF16)
        moe = (jnp.dot(mix_hi, y, preferred_element_type=F32) + jnp.dot(mix_lo, y, preferred_element_type=F32))
        o_ref[...] = _layer_norm_rows(DEEPNORM_ALPHA * h_ref[...] + moe, g2_ref[...], b2_ref[...])
    _by_run_size(long_ref, nxt, fetch_next_and_combine)

    @pl.when(i == n - 1)
    def _():
        drain(1 - cur)


def _combine(h, yb, local_t, runs, gate, g2_row, b2_row):
    n_tok = h.shape[0]
    tb = TOKEN_BLOCK
    row_spec = lambda w: pl.BlockSpec((tb, w), lambda i, *_: (i, 0))
    full = lambda shape: pl.BlockSpec(shape, lambda i, *_: (0,) * len(shape))
    grid_spec = pltpu.PrefetchScalarGridSpec(
        num_scalar_prefetch=4,
        grid=(n_tok // tb,),
        in_specs=[row_spec(2), row_spec(2), row_spec(D_MODEL), full((1, D_MODEL)), full((1, D_MODEL)),
                  pl.BlockSpec(memory_space=pl.ANY)],
        out_specs=row_spec(D_MODEL),
        scratch_shapes=[pltpu.VMEM((2, 2 * tb * ROW_SUB, LANES), F32), pltpu.SemaphoreType.DMA((2,))],
    )
    return pl.pallas_call(
        _combine_kernel,
        grid_spec=grid_spec,
        out_shape=jax.ShapeDtypeStruct((n_tok, D_MODEL), F32),
        compiler_params=pltpu.CompilerParams(dimension_semantics=("arbitrary",),
                                             vmem_limit_bytes=VMEM_LIMIT),
        name="combine_ln",
    )(*runs, local_t, gate, h, g2_row, b2_row, yb)


def kernel(x, w_in, b_fgate, s5_lambda_re, s5_lambda_im, s5_log_step, s5_b_re, s5_b_im, s5_c_re, s5_c_im, s5_d, w_glu, b_glu, attn_norm_g, ssm_norm_g, w_out, ln1_g, ln1_b, w_router_group, b_router_group, w_router_expert, b_router_expert, w_gate, w_up, w_down, ln2_g, ln2_b):
    bsz, seq, d = x.shape
    n_tok = bsz * seq
    a = ATTN_WIDTH
    for l in range(DEPTH):
        x2 = x.reshape(n_tok, d)
        w = w_in[l]
        f0 = 3 * a
        w_cat = jnp.concatenate(
            [w[:, :a] * (HEAD_DIM ** -0.5 * LOG2E), w[:, a:2 * a], w[:, f0 + ATTN_HEADS:],
             jnp.pad(w[:, f0:f0 + ATTN_HEADS], ((0, 0), (0, LANES - ATTN_HEADS)))], axis=1).astype(BF16)
        wvt = w[:, 2 * a:f0].T.astype(BF16)
        bf_row = jnp.pad(b_fgate[l], (0, LANES - ATTN_HEADS)).reshape(1, LANES)
        q, k, vt, u, eq, ek = _inproj(x2, w_cat, wvt, bf_row, seq)
        attn = _attention(q.reshape(bsz, seq, a), k.reshape(bsz, seq, a), vt,
                          eq.reshape(bsz, seq, EXT_W), ek.reshape(bsz, seq, EXT_W))

        bw, cw, ar_s, ai_s = _s5_weights(s5_lambda_re[l], s5_lambda_im[l], s5_log_step[l],
                                         s5_b_re[l], s5_b_im[l], s5_c_re[l], s5_c_im[l])
        u_tb = u.reshape(bsz, seq, SSM_WIDTH).transpose(1, 0, 2)
        ssm_tb = _s5(u_tb, bw, cw, ar_s, ai_s, s5_d[l].reshape(1, SSM_WIDTH), w_glu[l].astype(BF16),
                     b_glu[l].reshape(1, SSM_WIDTH), ssm_norm_g[l].reshape(1, SSM_WIDTH))
        ssm2 = ssm_tb.transpose(1, 0, 2).reshape(n_tok, SSM_WIDTH)

        n_route = N_EXPERT_GROUPS + N_EXPERTS
        wrt = jnp.pad(jnp.concatenate([w_router_group[l], w_router_expert[l]], axis=1).T,
                      ((0, LANES - n_route), (0, 0))).astype(BF16)
        brt = jnp.pad(jnp.concatenate([b_router_group[l], b_router_expert[l]]),
                      (0, ROUTER_ROWS - n_route)).reshape(ROUTER_ROWS, 1)
        h, eid, gate, rank, cnt_col = _outproj(
            attn.reshape(n_tok, a), ssm2, x2, w_out[l].astype(BF16), attn_norm_g[l].reshape(1, a),
            ln1_g[l].reshape(1, d), ln1_b[l].reshape(1, d), wrt, brt)

        local, runs, block_e, n_used, tail, n_blocks = _dispatch_plan(eid, rank, cnt_col, n_tok)
        xb = _dispatch(h, local, runs, tail, n_used, n_blocks)
        yb = _experts(xb, block_e, n_used, w_gate[l], w_up[l], w_down[l])
        x = _combine(h, yb, local.T, runs, gate.T, ln2_g[l].reshape(1, d),
                     ln2_b[l].reshape(1, d)).reshape(bsz, seq, d)
    return x
```

```python
import functools
import math

import jax
import jax.numpy as jnp
import numpy as np
from jax import lax
from jax.experimental import pallas as pl
from jax.experimental.pallas import tpu as pltpu

F32 = jnp.float32
BF16 = jnp.bfloat16

D_MODEL = 1024
HEAD_DIM = 64
ATTN_WIDTH = 512
ATTN_HEADS = 8
SSM_WIDTH = 512
SSM_GROUP = 16
SSM_GROUPS = 32
SSM_STATE = 64
N_EXPERT_GROUPS = 4
EXPERTS_PER_GROUP = 8
N_EXPERTS = 32
D_EXPERT = 512
MOE_BLOCK = 512
MOE_SUB = 256
TOKEN_BLOCK = 256
RUN_BITS = TOKEN_BLOCK.bit_length()
SHORT_RUN_BITS = 6
DEPTH = 1
DEEPNORM_ALPHA = (2.0 * DEPTH) ** 0.25
LN_EPS = 1e-5
RMS_EPS = 1e-6

LANES = 128
NEG_BIG = -1e30
LOG2E = math.log2(math.e)
ONES_ROWS = 16
VT_ROWS = ATTN_HEADS * (HEAD_DIM + ONES_ROWS)
GATE_ROWS = 16
PAIR_BIAS_LANES = 8
EXT_W = ATTN_HEADS // 2 * LANES
VMEM_LIMIT = 48 * 1024 * 1024

ROW_BLOCK = 512
ATTN_BLOCK = 256
ATTN_HEADS_PER_STEP = 8
S5_TIME_BLOCK = 64
S5_SLABS = 4
ROUTER_ROWS = 40
SLAB_CH = SSM_WIDTH // S5_SLABS
SLAB_STATE = SSM_GROUPS // S5_SLABS * SSM_STATE


def _split3(c):
    hi = c.astype(BF16)
    r1 = c - hi.astype(F32)
    mid = r1.astype(BF16)
    lo = (r1 - mid.astype(F32)).astype(BF16)
    return hi, mid, lo


def _inproj_kernel(x_ref, w_ref, wvt_ref, bf_ref, triu_ref, sel_ref, q_ref, k_ref, vt_ref, u_ref, eq_ref, ek_ref,
                   carry_ref, *, blocks_per_seq):
    i = pl.program_id(0)

    @pl.when(i % blocks_per_seq == 0)
    def _():
        carry_ref[...] = jnp.zeros_like(carry_ref)

    tm = x_ref.shape[0]
    xb = x_ref[...].astype(BF16)
    z = jnp.dot(xb, w_ref[...], preferred_element_type=F32)
    a = ATTN_WIDTH
    q_ref[...] = z[:, :a].astype(BF16)
    k_ref[...] = z[:, a:2 * a].astype(BF16)
    u_ref[...] = z[:, 2 * a:2 * a + SSM_WIDTH].astype(BF16)
    vtf = lax.dot_general(wvt_ref[...], xb, (((1,), (1,)), ((), ())), preferred_element_type=F32)
    vt = vtf[:ATTN_WIDTH].astype(BF16)
    ones_rows = jnp.ones((ONES_ROWS, tm), BF16)
    vt = jnp.concatenate([piece for h in range(ATTN_HEADS)
                          for piece in (vt[HEAD_DIM * h:HEAD_DIM * (h + 1)], ones_rows)], axis=0)
    for jj in range(tm // ATTN_BLOCK):
        vt_ref[jj] = vt[:, jj * ATTN_BLOCK:(jj + 1) * ATTN_BLOCK]

    f = vtf[ATTN_WIDTH:ATTN_WIDTH + ATTN_HEADS] + bf_ref[...]
    logf = jnp.minimum(f, 0.0) - jnp.log1p(jnp.exp(-jnp.abs(f)))
    parts = jnp.dot(jnp.concatenate(_split3(logf), axis=0), triu_ref[...], preferred_element_type=F32)
    nh = ATTN_HEADS
    cum = parts[0:nh] + parts[nh:2 * nh] + parts[2 * nh:3 * nh] + carry_ref[...]
    carry_ref[...] = cum[:, tm - 1:tm]
    hi, mid, lo = (t.astype(F32) for t in _split3(cum * LOG2E))
    one_row = jnp.where(lax.broadcasted_iota(jnp.int32, (nh, tm), 0) == 0, 1.0, 0.0)
    packed_t = jnp.concatenate([hi, mid, lo, one_row, jnp.zeros((LANES - 4 * nh, tm), F32)], axis=0)
    ext = jnp.dot(packed_t.T.astype(BF16), sel_ref[...], preferred_element_type=F32).astype(BF16)
    eq_ref[...] = ext[:, :EXT_W]
    ek_ref[...] = ext[:, EXT_W:]


def _bias_lane_selector():
    sel = np.zeros((LANES, 2, ATTN_HEADS // 2, LANES), np.float32)
    one = 3 * ATTN_HEADS
    for h in range(ATTN_HEADS):
        pair, base = h // 2, PAIR_BIAS_LANES * (h % 2)
        for t in range(3):
            sel[ATTN_HEADS * t + h, 0, pair, base + t] = 1.0
            sel[one, 0, pair, base + 3 + t] = 1.0
            sel[one, 1, pair, base + t] = 1.0
            sel[ATTN_HEADS * t + h, 1, pair, base + 3 + t] = -1.0
    return jnp.asarray(sel.reshape(LANES, 2 * EXT_W), BF16)


def _inproj(x2, w_cat, wvt, bf_col, seq):
    n_tok = x2.shape[0]
    tm = ROW_BLOCK
    assert seq % tm == 0 and n_tok % tm == 0 and tm % ATTN_BLOCK == 0
    bsz = n_tok // seq
    bps = seq // tm
    ncol = w_cat.shape[1]
    vblk = tm // ATTN_BLOCK
    row_spec = lambda w: pl.BlockSpec((tm, w), lambda i: (i, 0))
    return pl.pallas_call(
        functools.partial(_inproj_kernel, blocks_per_seq=bps),
        grid=(n_tok // tm,),
        in_specs=[row_spec(D_MODEL),
                  pl.BlockSpec((D_MODEL, ncol), lambda i: (0, 0)),
                  pl.BlockSpec((ATTN_WIDTH + GATE_ROWS, D_MODEL), lambda i: (0, 0)),
                  pl.BlockSpec((ATTN_HEADS, 1), lambda i: (0, 0)),
                  pl.BlockSpec((tm, tm), lambda i: (0, 0)),
                  pl.BlockSpec((LANES, 2 * EXT_W), lambda i: (0, 0))],
        out_specs=[row_spec(ATTN_WIDTH), row_spec(ATTN_WIDTH),
                   pl.BlockSpec((None, vblk, VT_ROWS, ATTN_BLOCK), lambda i: (i // bps, i % bps, 0, 0)),
                   row_spec(SSM_WIDTH), row_spec(EXT_W), row_spec(EXT_W)],
        out_shape=[jax.ShapeDtypeStruct((n_tok, ATTN_WIDTH), BF16)] * 2
                  + [jax.ShapeDtypeStruct((bsz, seq // ATTN_BLOCK, VT_ROWS, ATTN_BLOCK), BF16),
                     jax.ShapeDtypeStruct((n_tok, SSM_WIDTH), BF16),
                     jax.ShapeDtypeStruct((n_tok, EXT_W), BF16),
                     jax.ShapeDtypeStruct((n_tok, EXT_W), BF16)],
        scratch_shapes=[pltpu.VMEM((ATTN_HEADS, 1), F32)],
        compiler_params=pltpu.CompilerParams(dimension_semantics=("arbitrary",),
                                             vmem_limit_bytes=VMEM_LIMIT),
        name="inproj",
    )(x2, w_cat, wvt, bf_col, jnp.triu(jnp.ones((tm, tm), BF16)), _bias_lane_selector())


def _attn_kernel(q_ref, k_ref, vt_ref, eq_ref, ek_ref, o_ref, *, tq, seq, nh):
    qi = pl.program_id(2)
    lane = lax.broadcasted_iota(jnp.int32, (tq, LANES), 1)
    q_augs = []
    for h in range(nh):
        qb = q_ref[:, LANES * (h // 2):LANES * (h // 2 + 1)]
        head_lanes = (lane < HEAD_DIM) if h % 2 == 0 else (lane >= HEAD_DIM)
        qh = jnp.where(head_lanes, qb, jnp.zeros((), BF16))
        own = (lane >= PAIR_BIAS_LANES * (h % 2)) & (lane < PAIR_BIAS_LANES * (h % 2 + 1))
        eqh = jnp.where(own, eq_ref[:, LANES * (h // 2):LANES * (h // 2 + 1)], jnp.zeros((), BF16))
        q_augs.append(jnp.concatenate([qh, eqh], axis=1))
    key_i = lax.broadcasted_iota(jnp.int32, (tq, tq), 0)
    qry_i = lax.broadcasted_iota(jnp.int32, (tq, tq), 1)
    causal = key_i <= qry_i
    vrows = HEAD_DIM + ONES_ROWS

    def scores_of(k0, nk):
        out = []
        for h in range(nh):
            kj = k_ref[pl.ds(k0, nk), LANES * (h // 2):LANES * (h // 2 + 1)]
            k_aug = jnp.concatenate([kj, ek_ref[pl.ds(k0, nk), LANES * (h // 2):LANES * (h // 2 + 1)]], axis=1)
            out.append(lax.dot_general(k_aug, q_augs[h], (((1,), (1,)), ((), ())), preferred_element_type=F32))
        return tuple(out)

    def absorb(j0, scores, carry):
        stats = []
        for h in range(nh):
            m = carry[h][0]
            m_new = jnp.maximum(m, jnp.max(scores[h], axis=0, keepdims=True))
            stats.append((m_new, jnp.exp2(m - m_new), jnp.exp2(scores[h] - m_new).astype(BF16)))
        new = []
        for h in range(nh):
            m_new, corr, p = stats[h]
            res = None
            for c in range(p.shape[0] // tq):
                part = jnp.dot(vt_ref[j0 + c, vrows * h:vrows * (h + 1), :], p[c * tq:(c + 1) * tq],
                               preferred_element_type=F32)
                res = part if res is None else res + part
            acc = corr * carry[h][2] + res[0:HEAD_DIM, :]
            l = corr * carry[h][1] + res[HEAD_DIM:HEAD_DIM + 1, :]
            new.append((m_new, l, acc))
        return tuple(new)

    def pair_step(jj, carry):
        return absorb(2 * jj, scores_of(pl.multiple_of(jj * (2 * tq), 2 * tq), 2 * tq), carry)

    def finish(carry):
        out_t = jnp.concatenate([c[2] / c[1] for c in carry], axis=0)
        o_ref[...] = out_t.T

    init1 = (jnp.full((1, tq), NEG_BIG, F32), jnp.zeros((1, tq), F32), jnp.zeros((HEAD_DIM, tq), F32))
    carry = lax.fori_loop(0, qi // 2, pair_step, (init1,) * nh)

    @pl.when(qi % 2 == 0)
    def _():
        diag = scores_of(pl.multiple_of(qi * tq, tq), tq)
        finish(absorb(qi, tuple(jnp.where(causal, s, NEG_BIG) for s in diag), carry))

    @pl.when(qi % 2 == 1)
    def _():
        last = scores_of(pl.multiple_of((qi - 1) * tq, 2 * tq), 2 * tq)
        causal2 = (lax.broadcasted_iota(jnp.int32, (2 * tq, tq), 0)
                   <= lax.broadcasted_iota(jnp.int32, (2 * tq, tq), 1) + tq)
        finish(absorb(qi - 1, tuple(jnp.where(causal2, s, NEG_BIG) for s in last), carry))


def _attention(q, k, vt, eq, ek):
    bsz, seq, _ = q.shape
    tq = ATTN_BLOCK
    nh = ATTN_HEADS_PER_STEP
    assert seq % tq == 0 and ATTN_HEADS % nh == 0 and nh % 2 == 0
    w = nh * HEAD_DIM
    return pl.pallas_call(
        functools.partial(_attn_kernel, tq=tq, seq=seq, nh=nh),
        grid=(bsz, ATTN_HEADS // nh, seq // tq),
        in_specs=[pl.BlockSpec((None, tq, w), lambda b, g, i: (b, i, g)),
                  pl.BlockSpec((None, seq, w), lambda b, g, i: (b, 0, g)),
                  pl.BlockSpec((None, seq // tq, nh * (HEAD_DIM + ONES_ROWS), tq), lambda b, g, i: (b, 0, g, 0)),
                  pl.BlockSpec((None, tq, nh // 2 * LANES), lambda b, g, i: (b, i, g)),
                  pl.BlockSpec((None, seq, nh // 2 * LANES), lambda b, g, i: (b, 0, g))],
        out_specs=pl.BlockSpec((None, tq, w), lambda b, g, i: (b, i, g)),
        out_shape=jax.ShapeDtypeStruct((bsz, seq, ATTN_WIDTH), F32),
        compiler_params=pltpu.CompilerParams(dimension_semantics=("parallel", "parallel", "arbitrary"),
                                             vmem_limit_bytes=VMEM_LIMIT),
        name="fox_attention",
    )(q, k, vt, eq, ek)


def _s5_kernel(u_ref, bw_ref, cw_ref, ar_ref, ai_ref, d_ref, wglu_ref, bglu_ref, g_ref, o_ref,
               state_ref, bu_ref, *, tc, bsz):
    @pl.when(pl.program_id(0) == 0)
    def _():
        state_ref[...] = jnp.zeros_like(state_ref)

    rows = tc * bsz
    ns = SLAB_STATE
    u = u_ref[...].reshape(rows, SSM_WIDTH)
    ub = u
    for i in range(S5_SLABS):
        bu_ref[i] = jnp.dot(ub[:, SLAB_CH * i:SLAB_CH * (i + 1)], bw_ref[i], preferred_element_type=F32)
    for i in range(S5_SLABS):
        ar = jnp.broadcast_to(ar_ref[i], (bsz, ns))
        ai = jnp.broadcast_to(ai_ref[i], (bsz, ns))
        xr = state_ref[i, :, 0:ns]
        xi = state_ref[i, :, ns:2 * ns]
        for t in range(tc):
            r0 = t * bsz
            nr = ar * xr - ai * xi + bu_ref[i, r0:r0 + bsz, 0:ns]
            ni = ar * xi + ai * xr + bu_ref[i, r0:r0 + bsz, ns:2 * ns]
            bu_ref[i, r0:r0 + bsz, 0:ns] = nr
            bu_ref[i, r0:r0 + bsz, ns:2 * ns] = ni
            xr, xi = nr, ni
        state_ref[i, :, 0:ns] = xr
        state_ref[i, :, ns:2 * ns] = xi
    ys = [jnp.dot(bu_ref[i].astype(BF16), cw_ref[i], preferred_element_type=F32) for i in range(S5_SLABS)]
    y = jnp.concatenate(ys, axis=1) + d_ref[...] * u.astype(F32)
    y = 0.5 * y * (1.0 + jnp.tanh(math.sqrt(2.0 / math.pi) * (y + 0.044715 * (y * y * y))))
    z = jnp.dot(y.astype(BF16), wglu_ref[...], preferred_element_type=F32) + bglu_ref[...]
    y = y * (1.0 / (1.0 + jnp.exp(-z)))
    y = y * lax.rsqrt(jnp.mean(y * y, axis=-1, keepdims=True) + RMS_EPS) * g_ref[...]
    o_ref[...] = y.astype(BF16).reshape(tc, bsz, SSM_WIDTH)


def _s5_weights(lam_re, lam_im, log_step, b_re, b_im, c_re, c_im):
    delta = jnp.exp(log_step)[:, None]
    mag = jnp.exp(lam_re * delta)
    ar = mag * jnp.cos(lam_im * delta)
    ai = mag * jnp.sin(lam_im * delta)
    den = lam_re * lam_re + lam_im * lam_im
    num_re = ar - 1.0
    coef_re = (num_re * lam_re + ai * lam_im) / den
    coef_im = (ai * lam_re - num_re * lam_im) / den
    bbar_re = coef_re[..., None] * b_re - coef_im[..., None] * b_im
    bbar_im = coef_re[..., None] * b_im + coef_im[..., None] * b_re
    gs = SSM_GROUPS // S5_SLABS
    eye = jnp.eye(gs, dtype=F32)

    def b_slab(bb):
        bb = bb.reshape(S5_SLABS, gs, SSM_STATE, SSM_GROUP)
        w = jnp.einsum('ab,sapc->sacbp', eye, bb)
        return w.reshape(S5_SLABS, gs * SSM_GROUP, gs * SSM_STATE)

    def c_slab(cc):
        cc = cc.reshape(S5_SLABS, gs, SSM_GROUP, SSM_STATE)
        w = jnp.einsum('ab,sacp->sbpac', eye, cc)
        return w.reshape(S5_SLABS, gs * SSM_STATE, gs * SSM_GROUP)

    bw = jnp.concatenate([b_slab(bbar_re), b_slab(bbar_im)], axis=2).astype(BF16)
    cw = jnp.concatenate([c_slab(c_re), -c_slab(c_im)], axis=1).astype(BF16)
    ar_s = ar.reshape(S5_SLABS, 1, SLAB_STATE)
    ai_s = ai.reshape(S5_SLABS, 1, SLAB_STATE)
    return bw, cw, ar_s, ai_s


def _s5(u_tb, bw, cw, ar_s, ai_s, d_row, w_glu, b_glu_row, g_row):
    seq, bsz, _ = u_tb.shape
    tc = S5_TIME_BLOCK
    assert seq % tc == 0 and bsz % 16 == 0
    full = lambda shape: pl.BlockSpec(shape, lambda t: (0,) * len(shape))
    return pl.pallas_call(
        functools.partial(_s5_kernel, tc=tc, bsz=bsz),
        grid=(seq // tc,),
        in_specs=[pl.BlockSpec((tc, bsz, SSM_WIDTH), lambda t: (t, 0, 0)),
                  full(bw.shape), full(cw.shape), full(ar_s.shape), full(ai_s.shape),
                  full((1, SSM_WIDTH)), full((SSM_WIDTH, SSM_WIDTH)), full((1, SSM_WIDTH)),
                  full((1, SSM_WIDTH))],
        out_specs=pl.BlockSpec((tc, bsz, SSM_WIDTH), lambda t: (t, 0, 0)),
        out_shape=jax.ShapeDtypeStruct((seq, bsz, SSM_WIDTH), BF16),
        scratch_shapes=[pltpu.VMEM((S5_SLABS, bsz, 2 * SLAB_STATE), F32),
                        pltpu.VMEM((S5_SLABS, tc * bsz, 2 * SLAB_STATE), F32)],
        compiler_params=pltpu.CompilerParams(dimension_semantics=("arbitrary",),
                                             vmem_limit_bytes=VMEM_LIMIT),
        name="s5_branch",
    )(u_tb, bw, cw, ar_s, ai_s, d_row, w_glu, b_glu_row, g_row)


def _layer_norm_rows(v, g, b):
    mu = jnp.mean(v, axis=-1, keepdims=True)
    c = v - mu
    var = jnp.mean(c * c, axis=-1, keepdims=True)
    return c * lax.rsqrt(var + LN_EPS) * g + b


def _outproj_kernel(attn_ref, ssm_ref, x_ref, wo_ref, ag_ref, g1_ref, b1_ref, wrt_ref, brt_ref, triu_ref,
                    h_ref, eid_ref, gate_ref, rank_ref, cnt_ref):
    a = attn_ref[...]
    a = a * lax.rsqrt(jnp.mean(a * a, axis=-1, keepdims=True) + RMS_EPS) * ag_ref[...]
    mix = (jnp.dot(a.astype(BF16), wo_ref[0:ATTN_WIDTH, :], preferred_element_type=F32)
           + jnp.dot(ssm_ref[...], wo_ref[ATTN_WIDTH:, :], preferred_element_type=F32))
    h = _layer_norm_rows(DEEPNORM_ALPHA * x_ref[...] + mix, g1_ref[...], b1_ref[...])
    h_ref[...] = h

    logits = lax.dot_general(wrt_ref[...], h.astype(BF16), (((1,), (1,)), ((), ())),
                             preferred_element_type=F32)[:ROUTER_ROWS] + brt_ref[...]
    tm = logits.shape[1]
    row = lax.broadcasted_iota(jnp.int32, (ROUTER_ROWS, tm), 0).astype(F32)
    no_row = float(LANES)
    is_g = row < N_EXPERT_GROUPS
    gl = jnp.where(is_g, logits, NEG_BIG)
    gmax = jnp.max(gl, axis=0, keepdims=True)
    g_idx = jnp.min(jnp.where(is_g & (gl == gmax), row, no_row), axis=0, keepdims=True)
    g_p = 1.0 / jnp.sum(jnp.where(is_g, jnp.exp(gl - gmax), 0.0), axis=0, keepdims=True)
    first = N_EXPERT_GROUPS + g_idx * EXPERTS_PER_GROUP
    in_grp = (row >= first) & (row < first + EXPERTS_PER_GROUP)
    el = jnp.where(in_grp, logits, NEG_BIG)
    m1 = jnp.max(el, axis=0, keepdims=True)
    i1 = jnp.min(jnp.where(in_grp & (el == m1), row, no_row), axis=0, keepdims=True)
    rest = in_grp & (row != i1)
    el2 = jnp.where(rest, logits, NEG_BIG)
    m2 = jnp.max(el2, axis=0, keepdims=True)
    i2 = jnp.min(jnp.where(rest & (el2 == m2), row, no_row), axis=0, keepdims=True)
    r = jnp.exp(m2 - m1)
    p1 = 1.0 / (1.0 + r)
    p2 = r / (1.0 + r)
    k2 = lax.broadcasted_iota(jnp.int32, (2, tm), 0)
    eid_ref[...] = (jnp.where(k2 == 0, i1, i2) - N_EXPERT_GROUPS).astype(jnp.int32)
    gate_ref[...] = g_p * jnp.where(k2 == 0, p1, p2)

    @pl.when(pl.program_id(0) == 0)
    def _():
        cnt_ref[...] = jnp.zeros_like(cnt_ref)
    hit1 = row == i1
    hit2 = row == i2
    onehot = jnp.where(hit1 | hit2, 1.0, 0.0)
    cnt = cnt_ref[...]
    before = jnp.dot(onehot.astype(BF16), triu_ref[...], preferred_element_type=F32) + cnt
    r1 = jnp.sum(jnp.where(hit1, before, 0.0), axis=0, keepdims=True)
    r2 = jnp.sum(jnp.where(hit2, before, 0.0), axis=0, keepdims=True)
    rank_ref[...] = jnp.where(k2 == 0, r1, r2).astype(jnp.int32)
    cnt_ref[...] = cnt + jnp.sum(onehot, axis=1, keepdims=True)


def _outproj(attn2, ssm2, x2, wo, ag_row, g1_row, b1_row, wrt, brt):
    n_tok = x2.shape[0]
    tm = ROW_BLOCK
    triu = jnp.triu(jnp.ones((tm, tm), BF16), 1)
    row_spec = lambda w: pl.BlockSpec((tm, w), lambda i: (i, 0))
    col_spec = lambda r: pl.BlockSpec((r, tm), lambda i: (0, i))
    full = lambda shape: pl.BlockSpec(shape, lambda i: (0,) * len(shape))
    return pl.pallas_call(
        _outproj_kernel,
        grid=(n_tok // tm,),
        in_specs=[row_spec(ATTN_WIDTH), row_spec(SSM_WIDTH), row_spec(D_MODEL),
                  full((D_MODEL, D_MODEL)), full((1, ATTN_WIDTH)), full((1, D_MODEL)), full((1, D_MODEL)),
                  full((LANES, D_MODEL)), full((ROUTER_ROWS, 1)), full((tm, tm))],
        out_specs=[row_spec(D_MODEL), col_spec(2), col_spec(2), col_spec(2), full((ROUTER_ROWS, 1))],
        out_shape=[jax.ShapeDtypeStruct((n_tok, D_MODEL), F32),
                   jax.ShapeDtypeStruct((2, n_tok), jnp.int32),
                   jax.ShapeDtypeStruct((2, n_tok), F32),
                   jax.ShapeDtypeStruct((2, n_tok), jnp.int32),
                   jax.ShapeDtypeStruct((ROUTER_ROWS, 1), F32)],
        compiler_params=pltpu.CompilerParams(dimension_semantics=("arbitrary",),
                                             vmem_limit_bytes=VMEM_LIMIT),
        name="outproj_ln_router",
    )(attn2, ssm2, x2, wo, ag_row, g1_row, b1_row, wrt, brt, triu)


def _dispatch_plan(eid, rank, cnt_col, n_tok):
    counts = cnt_col[N_EXPERT_GROUPS:N_EXPERT_GROUPS + N_EXPERTS, 0].astype(jnp.int32)
    padded = ((counts + MOE_BLOCK - 1) // MOE_BLOCK) * MOE_BLOCK
    pends = jnp.cumsum(padded)
    pstarts = pends - padded
    n_pad = ((2 * n_tok + N_EXPERTS * (MOE_BLOCK - 1) + MOE_BLOCK - 1) // MOE_BLOCK) * MOE_BLOCK
    n_blocks = n_pad // MOE_BLOCK
    blk0 = jnp.arange(n_blocks, dtype=jnp.int32) * MOE_BLOCK
    block_e = jnp.minimum(jnp.sum((pends[None, :] <= blk0[:, None]).astype(jnp.int32), axis=1), N_EXPERTS - 1)
    n_used = (pends[-1] // MOE_BLOCK).astype(jnp.int32).reshape(1)
    tail = jnp.where(padded > 0, pends - MOE_BLOCK, -1).astype(jnp.int32)

    tb = TOKEN_BLOCK
    nb = n_tok // tb
    onehot = eid[:, :, None] == jnp.arange(N_EXPERTS, dtype=jnp.int32)[None, None, :]
    run_len = jnp.sum(onehot.reshape(2, nb, tb, N_EXPERTS).astype(jnp.int32), axis=(0, 2))
    before = jnp.cumsum(run_len, axis=0) - run_len
    run_src = jnp.cumsum(run_len, axis=1) - run_len
    run_dst = pstarts[None, :] + before
    shift = jnp.broadcast_to((run_src - before)[:, None, :], (nb, tb, N_EXPERTS)).reshape(n_tok, N_EXPERTS)
    local = rank + jnp.sum(jnp.where(onehot, shift[None], 0), axis=-1)
    any_long = (jnp.max(run_len, axis=1) >= (1 << SHORT_RUN_BITS)).astype(jnp.int32)
    runs = (run_len.reshape(-1), run_src.reshape(-1), run_dst.reshape(-1), any_long)
    return local, runs, block_e, n_used, tail, n_blocks


ROW_SUB = D_MODEL // LANES


def _rows(start, size):
    return pl.ds(pl.multiple_of(start * ROW_SUB, ROW_SUB), size * ROW_SUB)


def _store_row_tiles(ref, row0, v):
    n = v.shape[0]
    for s in range(ROW_SUB):
        ref[pl.ds(row0 * ROW_SUB + s, n, stride=ROW_SUB), :] = v[:, LANES * s:LANES * (s + 1)]


def _load_row_tiles(ref, row0, n):
    return jnp.concatenate([ref[pl.ds(row0 * ROW_SUB + s, n, stride=ROW_SUB), :] for s in range(ROW_SUB)], axis=1)


def _for_each_run_piece(blk, len_ref, src_ref, dst_ref, bits, make_copy, act):
    for e in range(N_EXPERTS):
        idx = blk * N_EXPERTS + e
        n = len_ref[idx]
        src = src_ref[idx]
        dst = dst_ref[idx]
        for b in reversed(range(bits)):
            @pl.when((n & (1 << b)) != 0)
            def _():
                off = lax.shift_left(lax.shift_right_logical(n, b + 1), b + 1)
                act(make_copy(src + off, dst + off, 1 << b))


def _by_run_size(long_ref, blk, body):
    @pl.when(long_ref[blk] == 0)
    def _():
        body(SHORT_RUN_BITS)

    @pl.when(long_ref[blk] != 0)
    def _():
        body(RUN_BITS)


def _dispatch_kernel(tail_ref, nu_ref, len_ref, src_ref, dst_ref, long_ref, local_ref, h_ref, xb_hbm, zbuf,
                     pk_ref, sem_z, sem, *, n_blocks):
    i = pl.program_id(0)
    nb = pl.num_programs(0) - 1
    tb = h_ref.shape[0]
    cur = i % 2

    @pl.when(i == 0)
    def _():
        zbuf[...] = jnp.zeros_like(zbuf)

        def zero_copy(row0):
            return pltpu.make_async_copy(zbuf, xb_hbm.at[_rows(row0, MOE_BLOCK)], sem_z)

        def each_zero_block(fn):
            for e in range(N_EXPERTS):
                @pl.when(tail_ref[e] >= 0)
                def _():
                    fn(zero_copy(tail_ref[e]))

            def unused(b, carry):
                fn(zero_copy(b * MOE_BLOCK))
                return carry
            lax.fori_loop(nu_ref[0], n_blocks, unused, 0)

        each_zero_block(lambda c: c.start())
        each_zero_block(lambda c: c.wait())

    def drain(which):
        pltpu.make_async_copy(pk_ref.at[which], xb_hbm.at[_rows(0, 2 * tb)], sem.at[which]).wait()

    @pl.when(i >= 2)
    def _():
        drain(cur)

    def group():
        pos = lax.broadcasted_iota(jnp.int32, (2 * tb, tb), 0)
        hit = (pos == local_ref[0:1, :]) | (pos == local_ref[1:2, :])
        perm = jnp.where(hit, 1.0, 0.0).astype(BF16)
        _store_row_tiles(pk_ref.at[cur], 0, jnp.dot(perm, h_ref[...].astype(BF16), preferred_element_type=F32))

    def send(bits):
        def run_copy(src, dst, size):
            return pltpu.make_async_copy(pk_ref.at[1 - cur, _rows(src, size)], xb_hbm.at[_rows(dst, size)],
                                         sem.at[1 - cur])
        _for_each_run_piece(i - 1, len_ref, src_ref, dst_ref, bits, run_copy, lambda c: c.start())

    @pl.when(i == 0)
    def _():
        group()

    @pl.when((i > 0) & (i < nb))
    def _():
        def send_and_group(bits):
            send(bits)
            group()
        _by_run_size(long_ref, i - 1, send_and_group)

    @pl.when(i == nb)
    def _():
        send(RUN_BITS)
        drain(1 - cur)


def _dispatch(h, local, runs, tail, n_used, n_blocks):
    n_tok = h.shape[0]
    tb = TOKEN_BLOCK
    nb = n_tok // tb
    grid_spec = pltpu.PrefetchScalarGridSpec(
        num_scalar_prefetch=6,
        grid=(nb + 1,),
        in_specs=[pl.BlockSpec((2, tb), lambda i, *_: (0, jnp.minimum(i, nb - 1))),
                  pl.BlockSpec((tb, D_MODEL), lambda i, *_: (jnp.minimum(i, nb - 1), 0))],
        out_specs=pl.BlockSpec(memory_space=pl.ANY),
        scratch_shapes=[pltpu.VMEM((MOE_BLOCK * ROW_SUB, LANES), F32),
                        pltpu.VMEM((2, 2 * tb * ROW_SUB, LANES), F32),
                        pltpu.SemaphoreType.DMA, pltpu.SemaphoreType.DMA((2,))],
    )
    return pl.pallas_call(
        functools.partial(_dispatch_kernel, n_blocks=n_blocks),
        grid_spec=grid_spec,
        out_shape=jax.ShapeDtypeStruct((n_blocks * MOE_BLOCK * ROW_SUB, LANES), F32),
        compiler_params=pltpu.CompilerParams(dimension_semantics=("arbitrary",),
                                             vmem_limit_bytes=VMEM_LIMIT),
        name="moe_dispatch",
    )(tail, n_used, *runs, local, h)


def _expert_kernel(be_ref, nu_ref, x_ref, wg_ref, wu_ref, wd_ref, y_ref, wgu_t, wd_t, yt_ref):
    i = pl.program_id(0)
    used = i < nu_ref[0]

    @pl.when(used & ((i == 0) | (be_ref[i] != be_ref[jnp.maximum(i - 1, 0)])))
    def _():
        wgu_t[0:D_EXPERT, :] = wg_ref[...].T.astype(BF16)
        wgu_t[D_EXPERT:, :] = wu_ref[...].T.astype(BF16)
        wd_t[...] = wd_ref[...].T.astype(BF16)

    @pl.when(used)
    def _():
        sub = range(MOE_BLOCK // MOE_SUB)
        gus = []
        for c in sub:
            xb = _load_row_tiles(x_ref, c * MOE_SUB, MOE_SUB).astype(BF16)
            gus.append(lax.dot_general(wgu_t[...], xb, (((1,), (1,)), ((), ())),
                                       preferred_element_type=F32))
        hids = []
        for c in sub:
            gt = gus[c][0:D_EXPERT]
            hids.append((gt * (1.0 / (1.0 + jnp.exp(-gt))) * gus[c][D_EXPERT:]).astype(BF16))
        for c in sub:
            yt_ref[c] = jnp.dot(wd_t[...], hids[c], preferred_element_type=F32)
        for c in sub:
            _store_row_tiles(y_ref, c * MOE_SUB, yt_ref[c].T)

    @pl.when(jnp.logical_not(used))
    def _():
        y_ref[...] = jnp.zeros_like(y_ref)


def _experts(xb, block_e, n_used, wg, wu, wd):
    n_blocks = xb.shape[0] // (MOE_BLOCK * ROW_SUB)
    last = lambda i, nu: jnp.minimum(i, nu[0] - 1)
    grid_spec = pltpu.PrefetchScalarGridSpec(
        num_scalar_prefetch=2,
        grid=(n_blocks,),
        in_specs=[pl.BlockSpec((MOE_BLOCK * ROW_SUB, LANES), lambda i, be, nu: (last(i, nu), 0)),
                  pl.BlockSpec((None, D_MODEL, D_EXPERT), lambda i, be, nu: (be[last(i, nu)], 0, 0)),
                  pl.BlockSpec((None, D_MODEL, D_EXPERT), lambda i, be, nu: (be[last(i, nu)], 0, 0)),
                  pl.BlockSpec((None, D_EXPERT, D_MODEL), lambda i, be, nu: (be[last(i, nu)], 0, 0))],
        out_specs=pl.BlockSpec((MOE_BLOCK * ROW_SUB, LANES), lambda i, be, nu: (i, 0)),
        scratch_shapes=[pltpu.VMEM((2 * D_EXPERT, D_MODEL), BF16), pltpu.VMEM((D_MODEL, D_EXPERT), BF16),
                        pltpu.VMEM((MOE_BLOCK // MOE_SUB, D_MODEL, MOE_SUB), F32)],
    )
    return pl.pallas_call(
        _expert_kernel,
        grid_spec=grid_spec,
        out_shape=jax.ShapeDtypeStruct(xb.shape, F32),
        compiler_params=pltpu.CompilerParams(dimension_semantics=("arbitrary",),
                                             vmem_limit_bytes=VMEM_LIMIT),
        name="moe_experts",
    )(block_e, n_used, xb, wg, wu, wd)


def _combine_kernel(len_ref, src_ref, dst_ref, long_ref, local_ref, gate_ref, h_ref, g2_ref, b2_ref, yb_hbm,
                    o_ref, buf, sem):
    i = pl.program_id(0)
    n = pl.num_programs(0)
    tb = h_ref.shape[0]
    cur = i % 2

    def fetch(blk, which, bits):
        def run_copy(src, dst, size):
            return pltpu.make_async_copy(yb_hbm.at[_rows(dst, size)], buf.at[which, _rows(src, size)], sem.at[which])
        _for_each_run_piece(blk, len_ref, src_ref, dst_ref, bits, run_copy, lambda c: c.start())

    def drain(which):
        pltpu.make_async_copy(yb_hbm.at[_rows(0, 2 * tb)], buf.at[which], sem.at[which]).wait()

    @pl.when(i == 0)
    def _():
        fetch(i, 0, RUN_BITS)

    drain(cur)
    nxt = jnp.minimum(i + 1, n - 1)

    def fetch_next_and_combine(bits):
        fetch(nxt, 1 - cur, bits)
        y = _load_row_tiles(buf.at[cur], 0, 2 * tb).astype(BF16)
        pos = lax.broadcasted_iota(jnp.int32, (tb, 2 * tb), 1)
        local = local_ref[...]
        gate = gate_ref[...]
        mix = (jnp.where(pos == local[:, 0:1], gate[:, 0:1], 0.0)
               + jnp.where(pos == local[:, 1:2], gate[:, 1:2], 0.0))
        mix_hi = mix.astype(BF16)
        mix_lo = (mix - mix_hi.astype(F32)).astype(BF16)
        moe = (jnp.dot(mix_hi, y, preferred_element_type=F32) + jnp.dot(mix_lo, y, preferred_element_type=F32))
        o_ref[...] = _layer_norm_rows(DEEPNORM_ALPHA * h_ref[...] + moe, g2_ref[...], b2_ref[...])
    _by_run_size(long_ref, nxt, fetch_next_and_combine)

    @pl.when(i == n - 1)
    def _():
        drain(1 - cur)


def _combine(h, yb, local_t, runs, gate, g2_row, b2_row):
    n_tok = h.shape[0]
    tb = TOKEN_BLOCK
    row_spec = lambda w: pl.BlockSpec((tb, w), lambda i, *_: (i, 0))
    full = lambda shape: pl.BlockSpec(shape, lambda i, *_: (0,) * len(shape))
    grid_spec = pltpu.PrefetchScalarGridSpec(
        num_scalar_prefetch=4,
        grid=(n_tok // tb,),
        in_specs=[row_spec(2), row_spec(2), row_spec(D_MODEL), full((1, D_MODEL)), full((1, D_MODEL)),
                  pl.BlockSpec(memory_space=pl.ANY)],
        out_specs=row_spec(D_MODEL),
        scratch_shapes=[pltpu.VMEM((2, 2 * tb * ROW_SUB, LANES), F32), pltpu.SemaphoreType.DMA((2,))],
    )
    return pl.pallas_call(
        _combine_kernel,
        grid_spec=grid_spec,
        out_shape=jax.ShapeDtypeStruct((n_tok, D_MODEL), F32),
        compiler_params=pltpu.CompilerParams(dimension_semantics=("arbitrary",),
                                             vmem_limit_bytes=VMEM_LIMIT),
        name="combine_ln",
    )(*runs, local_t, gate, h, g2_row, b2_row, yb)


def kernel(x, w_in, b_fgate, s5_lambda_re, s5_lambda_im, s5_log_step, s5_b_re, s5_b_im, s5_c_re, s5_c_im, s5_d, w_glu, b_glu, attn_norm_g, ssm_norm_g, w_out, ln1_g, ln1_b, w_router_group, b_router_group, w_router_expert, b_router_expert, w_gate, w_up, w_down, ln2_g, ln2_b):
    bsz, seq, d = x.shape
    n_tok = bsz * seq
    a = ATTN_WIDTH
    for l in range(DEPTH):
        x2 = x.reshape(n_tok, d)
        w = w_in[l]
        f0 = 3 * a
        w_cat = jnp.concatenate(
            [w[:, :a] * (HEAD_DIM ** -0.5 * LOG2E), w[:, a:2 * a], w[:, f0 + ATTN_HEADS:]], axis=1).astype(BF16)
        wvt = jnp.pad(w[:, 2 * a:f0 + ATTN_HEADS].T, ((0, GATE_ROWS - ATTN_HEADS), (0, 0))).astype(BF16)
        q, k, vt, u, eq, ek = _inproj(x2, w_cat, wvt, b_fgate[l].reshape(ATTN_HEADS, 1), seq)
        attn = _attention(q.reshape(bsz, seq, a), k.reshape(bsz, seq, a), vt,
                          eq.reshape(bsz, seq, EXT_W), ek.reshape(bsz, seq, EXT_W))

        bw, cw, ar_s, ai_s = _s5_weights(s5_lambda_re[l], s5_lambda_im[l], s5_log_step[l],
                                         s5_b_re[l], s5_b_im[l], s5_c_re[l], s5_c_im[l])
        u_tb = u.reshape(bsz, seq, SSM_WIDTH).transpose(1, 0, 2)
        ssm_tb = _s5(u_tb, bw, cw, ar_s, ai_s, s5_d[l].reshape(1, SSM_WIDTH), w_glu[l].astype(BF16),
                     b_glu[l].reshape(1, SSM_WIDTH), ssm_norm_g[l].reshape(1, SSM_WIDTH))
        ssm2 = ssm_tb.transpose(1, 0, 2).reshape(n_tok, SSM_WIDTH)

        n_route = N_EXPERT_GROUPS + N_EXPERTS
        wrt = jnp.pad(jnp.concatenate([w_router_group[l], w_router_expert[l]], axis=1).T,
                      ((0, LANES - n_route), (0, 0))).astype(BF16)
        brt = jnp.pad(jnp.concatenate([b_router_group[l], b_router_expert[l]]),
                      (0, ROUTER_ROWS - n_route)).reshape(ROUTER_ROWS, 1)
        h, eid, gate, rank, cnt_col = _outproj(
            attn.reshape(n_tok, a), ssm2, x2, w_out[l].astype(BF16), attn_norm_g[l].reshape(1, a),
            ln1_g[l].reshape(1, d), ln1_b[l].reshape(1, d), wrt, brt)

        local, runs, block_e, n_used, tail, n_blocks = _dispatch_plan(eid, rank, cnt_col, n_tok)
        xb = _dispatch(h, local, runs, tail, n_used, n_blocks)
        yb = _experts(xb, block_e, n_used, w_gate[l], w_up[l], w_down[l])
        x = _combine(h, yb, local.T, runs, gate.T, ln2_g[l].reshape(1, d),
                     ln2_b[l].reshape(1, d)).reshape(bsz, seq, d)
    return x
```

```python
import functools
import math

import jax
import jax.numpy as jnp
import numpy as np
from jax import lax
from jax.experimental import pallas as pl
from jax.experimental.pallas import tpu as pltpu

F32 = jnp.float32
BF16 = jnp.bfloat16

D_MODEL = 1024
HEAD_DIM = 64
ATTN_WIDTH = 512
ATTN_HEADS = 8
SSM_WIDTH = 512
SSM_GROUP = 16
SSM_GROUPS = 32
SSM_STATE = 64
N_EXPERT_GROUPS = 4
EXPERTS_PER_GROUP = 8
N_EXPERTS = 32
D_EXPERT = 512
MOE_BLOCK = 512
MOE_SUB = 256
TOKEN_BLOCK = 256
RUN_BITS = TOKEN_BLOCK.bit_length()
SHORT_RUN_BITS = 6
DEPTH = 1
DEEPNORM_ALPHA = (2.0 * DEPTH) ** 0.25
LN_EPS = 1e-5
RMS_EPS = 1e-6

LANES = 128
NEG_BIG = -1e30
LOG2E = math.log2(math.e)
ONES_ROWS = 16
VT_ROWS = ATTN_HEADS * (HEAD_DIM + ONES_ROWS)
GATE_ROWS = 16
PAIR_BIAS_LANES = 8
EXT_W = ATTN_HEADS // 2 * LANES
VMEM_LIMIT = 48 * 1024 * 1024

ROW_BLOCK = 1024
ATTN_BLOCK = 256
ATTN_HEADS_PER_STEP = 8
S5_TIME_BLOCK = 64
S5_SLABS = 4
ROUTER_ROWS = 40
SLAB_CH = SSM_WIDTH // S5_SLABS
SLAB_STATE = SSM_GROUPS // S5_SLABS * SSM_STATE


def _split3(c):
    hi = c.astype(BF16)
    r1 = c - hi.astype(F32)
    mid = r1.astype(BF16)
    lo = (r1 - mid.astype(F32)).astype(BF16)
    return hi, mid, lo


def _inproj_kernel(x_ref, w_ref, wvt_ref, bf_ref, triu_ref, sel_ref, q_ref, k_ref, vt_ref, u_ref, eq_ref, ek_ref,
                   carry_ref, *, blocks_per_seq):
    i = pl.program_id(0)

    @pl.when(i % blocks_per_seq == 0)
    def _():
        carry_ref[...] = jnp.zeros_like(carry_ref)

    tm = x_ref.shape[0]
    xb = x_ref[...].astype(BF16)
    z = jnp.dot(xb, w_ref[...], preferred_element_type=F32)
    a = ATTN_WIDTH
    q_ref[...] = z[:, :a].astype(BF16)
    k_ref[...] = z[:, a:2 * a].astype(BF16)
    u_ref[...] = z[:, 2 * a:2 * a + SSM_WIDTH].astype(BF16)
    vtf = lax.dot_general(wvt_ref[...], xb, (((1,), (1,)), ((), ())), preferred_element_type=F32)
    vt = vtf[:ATTN_WIDTH].astype(BF16)
    ones_rows = jnp.ones((ONES_ROWS, tm), BF16)
    vt = jnp.concatenate([piece for h in range(ATTN_HEADS)
                          for piece in (vt[HEAD_DIM * h:HEAD_DIM * (h + 1)], ones_rows)], axis=0)
    for jj in range(tm // ATTN_BLOCK):
        vt_ref[jj] = vt[:, jj * ATTN_BLOCK:(jj + 1) * ATTN_BLOCK]

    f = vtf[ATTN_WIDTH:ATTN_WIDTH + ATTN_HEADS] + bf_ref[...]
    logf = jnp.minimum(f, 0.0) - jnp.log1p(jnp.exp(-jnp.abs(f)))
    parts = jnp.dot(jnp.concatenate(_split3(logf), axis=0), triu_ref[...], preferred_element_type=F32)
    nh = ATTN_HEADS
    cum = parts[0:nh] + parts[nh:2 * nh] + parts[2 * nh:3 * nh] + carry_ref[...]
    carry_ref[...] = cum[:, tm - 1:tm]
    hi, mid, lo = (t.astype(F32) for t in _split3(cum * LOG2E))
    one_row = jnp.where(lax.broadcasted_iota(jnp.int32, (nh, tm), 0) == 0, 1.0, 0.0)
    packed_t = jnp.concatenate([hi, mid, lo, one_row, jnp.zeros((LANES - 4 * nh, tm), F32)], axis=0)
    ext = jnp.dot(packed_t.T.astype(BF16), sel_ref[...], preferred_element_type=F32).astype(BF16)
    eq_ref[...] = ext[:, :EXT_W]
    ek_ref[...] = ext[:, EXT_W:]


def _bias_lane_selector():
    sel = np.zeros((LANES, 2, ATTN_HEADS // 2, LANES), np.float32)
    one = 3 * ATTN_HEADS
    for h in range(ATTN_HEADS):
        pair, base = h // 2, PAIR_BIAS_LANES * (h % 2)
        for t in range(3):
            sel[ATTN_HEADS * t + h, 0, pair, base + t] = 1.0
            sel[one, 0, pair, base + 3 + t] = 1.0
            sel[one, 1, pair, base + t] = 1.0
            sel[ATTN_HEADS * t + h, 1, pair, base + 3 + t] = -1.0
    return jnp.asarray(sel.reshape(LANES, 2 * EXT_W), BF16)


def _inproj(x2, w_cat, wvt, bf_col, seq):
    n_tok = x2.shape[0]
    tm = ROW_BLOCK
    assert seq % tm == 0 and n_tok % tm == 0 and tm % ATTN_BLOCK == 0
    bsz = n_tok // seq
    bps = seq // tm
    ncol = w_cat.shape[1]
    vblk = tm // ATTN_BLOCK
    row_spec = lambda w: pl.BlockSpec((tm, w), lambda i: (i, 0))
    return pl.pallas_call(
        functools.partial(_inproj_kernel, blocks_per_seq=bps),
        grid=(n_tok // tm,),
        in_specs=[row_spec(D_MODEL),
                  pl.BlockSpec((D_MODEL, ncol), lambda i: (0, 0)),
                  pl.BlockSpec((ATTN_WIDTH + GATE_ROWS, D_MODEL), lambda i: (0, 0)),
                  pl.BlockSpec((ATTN_HEADS, 1), lambda i: (0, 0)),
                  pl.BlockSpec((tm, tm), lambda i: (0, 0)),
                  pl.BlockSpec((LANES, 2 * EXT_W), lambda i: (0, 0))],
        out_specs=[row_spec(ATTN_WIDTH), row_spec(ATTN_WIDTH),
                   pl.BlockSpec((None, vblk, VT_ROWS, ATTN_BLOCK), lambda i: (i // bps, i % bps, 0, 0)),
                   row_spec(SSM_WIDTH), row_spec(EXT_W), row_spec(EXT_W)],
        out_shape=[jax.ShapeDtypeStruct((n_tok, ATTN_WIDTH), BF16)] * 2
                  + [jax.ShapeDtypeStruct((bsz, seq // ATTN_BLOCK, VT_ROWS, ATTN_BLOCK), BF16),
                     jax.ShapeDtypeStruct((n_tok, SSM_WIDTH), BF16),
                     jax.ShapeDtypeStruct((n_tok, EXT_W), BF16),
                     jax.ShapeDtypeStruct((n_tok, EXT_W), BF16)],
        scratch_shapes=[pltpu.VMEM((ATTN_HEADS, 1), F32)],
        compiler_params=pltpu.CompilerParams(dimension_semantics=("arbitrary",),
                                             vmem_limit_bytes=VMEM_LIMIT),
        name="inproj",
    )(x2, w_cat, wvt, bf_col, jnp.triu(jnp.ones((tm, tm), BF16)), _bias_lane_selector())


def _attn_kernel(q_ref, k_ref, vt_ref, eq_ref, ek_ref, o_ref, *, tq, seq, nh):
    qi = pl.program_id(2)
    lane = lax.broadcasted_iota(jnp.int32, (tq, LANES), 1)
    q_augs = []
    for h in range(nh):
        qb = q_ref[:, LANES * (h // 2):LANES * (h // 2 + 1)]
        head_lanes = (lane < HEAD_DIM) if h % 2 == 0 else (lane >= HEAD_DIM)
        qh = jnp.where(head_lanes, qb, jnp.zeros((), BF16))
        own = (lane >= PAIR_BIAS_LANES * (h % 2)) & (lane < PAIR_BIAS_LANES * (h % 2 + 1))
        eqh = jnp.where(own, eq_ref[:, LANES * (h // 2):LANES * (h // 2 + 1)], jnp.zeros((), BF16))
        q_augs.append(jnp.concatenate([qh, eqh], axis=1))
    key_i = lax.broadcasted_iota(jnp.int32, (tq, tq), 0)
    qry_i = lax.broadcasted_iota(jnp.int32, (tq, tq), 1)
    causal = key_i <= qry_i
    vrows = HEAD_DIM + ONES_ROWS

    def scores_of(k0, nk):
        out = []
        for h in range(nh):
            kj = k_ref[pl.ds(k0, nk), LANES * (h // 2):LANES * (h // 2 + 1)]
            k_aug = jnp.concatenate([kj, ek_ref[pl.ds(k0, nk), LANES * (h // 2):LANES * (h // 2 + 1)]], axis=1)
            out.append(lax.dot_general(k_aug, q_augs[h], (((1,), (1,)), ((), ())), preferred_element_type=F32))
        return tuple(out)

    def absorb(j0, scores, carry):
        stats = []
        for h in range(nh):
            m = carry[h][0]
            m_new = jnp.maximum(m, jnp.max(scores[h], axis=0, keepdims=True))
            stats.append((m_new, jnp.exp2(m - m_new), jnp.exp2(scores[h] - m_new).astype(BF16)))
        new = []
        for h in range(nh):
            m_new, corr, p = stats[h]
            res = None
            for c in range(p.shape[0] // tq):
                part = jnp.dot(vt_ref[j0 + c, vrows * h:vrows * (h + 1), :], p[c * tq:(c + 1) * tq],
                               preferred_element_type=F32)
                res = part if res is None else res + part
            acc = corr * carry[h][2] + res[0:HEAD_DIM, :]
            l = corr * carry[h][1] + res[HEAD_DIM:HEAD_DIM + 1, :]
            new.append((m_new, l, acc))
        return tuple(new)

    def pair_step(jj, carry):
        return absorb(2 * jj, scores_of(pl.multiple_of(jj * (2 * tq), 2 * tq), 2 * tq), carry)

    def finish(carry):
        out_t = jnp.concatenate([c[2] / c[1] for c in carry], axis=0)
        o_ref[...] = out_t.T

    init1 = (jnp.full((1, tq), NEG_BIG, F32), jnp.zeros((1, tq), F32), jnp.zeros((HEAD_DIM, tq), F32))
    carry = lax.fori_loop(0, qi // 2, pair_step, (init1,) * nh)

    @pl.when(qi % 2 == 0)
    def _():
        diag = scores_of(pl.multiple_of(qi * tq, tq), tq)
        finish(absorb(qi, tuple(jnp.where(causal, s, NEG_BIG) for s in diag), carry))

    @pl.when(qi % 2 == 1)
    def _():
        last = scores_of(pl.multiple_of((qi - 1) * tq, 2 * tq), 2 * tq)
        causal2 = (lax.broadcasted_iota(jnp.int32, (2 * tq, tq), 0)
                   <= lax.broadcasted_iota(jnp.int32, (2 * tq, tq), 1) + tq)
        finish(absorb(qi - 1, tuple(jnp.where(causal2, s, NEG_BIG) for s in last), carry))


def _attention(q, k, vt, eq, ek):
    bsz, seq, _ = q.shape
    tq = ATTN_BLOCK
    nh = ATTN_HEADS_PER_STEP
    assert seq % tq == 0 and ATTN_HEADS % nh == 0 and nh % 2 == 0
    w = nh * HEAD_DIM
    return pl.pallas_call(
        functools.partial(_attn_kernel, tq=tq, seq=seq, nh=nh),
        grid=(bsz, ATTN_HEADS // nh, seq // tq),
        in_specs=[pl.BlockSpec((None, tq, w), lambda b, g, i: (b, i, g)),
                  pl.BlockSpec((None, seq, w), lambda b, g, i: (b, 0, g)),
                  pl.BlockSpec((None, seq // tq, nh * (HEAD_DIM + ONES_ROWS), tq), lambda b, g, i: (b, 0, g, 0)),
                  pl.BlockSpec((None, tq, nh // 2 * LANES), lambda b, g, i: (b, i, g)),
                  pl.BlockSpec((None, seq, nh // 2 * LANES), lambda b, g, i: (b, 0, g))],
        out_specs=pl.BlockSpec((None, tq, w), lambda b, g, i: (b, i, g)),
        out_shape=jax.ShapeDtypeStruct((bsz, seq, ATTN_WIDTH), F32),
        compiler_params=pltpu.CompilerParams(dimension_semantics=("parallel", "parallel", "arbitrary"),
                                             vmem_limit_bytes=VMEM_LIMIT),
        name="fox_attention",
    )(q, k, vt, eq, ek)


def _s5_kernel(u_ref, bw_ref, cw_ref, ar_ref, ai_ref, d_ref, wglu_ref, bglu_ref, g_ref, o_ref,
               state_ref, bu_ref, *, tc, bsz):
    @pl.when(pl.program_id(0) == 0)
    def _():
        state_ref[...] = jnp.zeros_like(state_ref)

    rows = tc * bsz
    ns = SLAB_STATE
    u = u_ref[...].reshape(rows, SSM_WIDTH)
    ub = u
    for i in range(S5_SLABS):
        bu_ref[i] = jnp.dot(ub[:, SLAB_CH * i:SLAB_CH * (i + 1)], bw_ref[i], preferred_element_type=F32)
    for i in range(S5_SLABS):
        ar = jnp.broadcast_to(ar_ref[i], (bsz, ns))
        ai = jnp.broadcast_to(ai_ref[i], (bsz, ns))
        xr = state_ref[i, :, 0:ns]
        xi = state_ref[i, :, ns:2 * ns]
        for t in range(tc):
            r0 = t * bsz
            nr = ar * xr - ai * xi + bu_ref[i, r0:r0 + bsz, 0:ns]
            ni = ar * xi + ai * xr + bu_ref[i, r0:r0 + bsz, ns:2 * ns]
            bu_ref[i, r0:r0 + bsz, 0:ns] = nr
            bu_ref[i, r0:r0 + bsz, ns:2 * ns] = ni
            xr, xi = nr, ni
        state_ref[i, :, 0:ns] = xr
        state_ref[i, :, ns:2 * ns] = xi
    ys = [jnp.dot(bu_ref[i].astype(BF16), cw_ref[i], preferred_element_type=F32) for i in range(S5_SLABS)]
    y = jnp.concatenate(ys, axis=1) + d_ref[...] * u.astype(F32)
    y = 0.5 * y * (1.0 + jnp.tanh(math.sqrt(2.0 / math.pi) * (y + 0.044715 * (y * y * y))))
    z = jnp.dot(y.astype(BF16), wglu_ref[...], preferred_element_type=F32) + bglu_ref[...]
    y = y * (1.0 / (1.0 + jnp.exp(-z)))
    y = y * lax.rsqrt(jnp.mean(y * y, axis=-1, keepdims=True) + RMS_EPS) * g_ref[...]
    o_ref[...] = y.astype(BF16).reshape(tc, bsz, SSM_WIDTH)


def _s5_weights(lam_re, lam_im, log_step, b_re, b_im, c_re, c_im):
    delta = jnp.exp(log_step)[:, None]
    mag = jnp.exp(lam_re * delta)
    ar = mag * jnp.cos(lam_im * delta)
    ai = mag * jnp.sin(lam_im * delta)
    den = lam_re * lam_re + lam_im * lam_im
    num_re = ar - 1.0
    coef_re = (num_re * lam_re + ai * lam_im) / den
    coef_im = (ai * lam_re - num_re * lam_im) / den
    bbar_re = coef_re[..., None] * b_re - coef_im[..., None] * b_im
    bbar_im = coef_re[..., None] * b_im + coef_im[..., None] * b_re
    gs = SSM_GROUPS // S5_SLABS
    eye = jnp.eye(gs, dtype=F32)

    def b_slab(bb):
        bb = bb.reshape(S5_SLABS, gs, SSM_STATE, SSM_GROUP)
        w = jnp.einsum('ab,sapc->sacbp', eye, bb)
        return w.reshape(S5_SLABS, gs * SSM_GROUP, gs * SSM_STATE)

    def c_slab(cc):
        cc = cc.reshape(S5_SLABS, gs, SSM_GROUP, SSM_STATE)
        w = jnp.einsum('ab,sacp->sbpac', eye, cc)
        return w.reshape(S5_SLABS, gs * SSM_STATE, gs * SSM_GROUP)

    bw = jnp.concatenate([b_slab(bbar_re), b_slab(bbar_im)], axis=2).astype(BF16)
    cw = jnp.concatenate([c_slab(c_re), -c_slab(c_im)], axis=1).astype(BF16)
    ar_s = ar.reshape(S5_SLABS, 1, SLAB_STATE)
    ai_s = ai.reshape(S5_SLABS, 1, SLAB_STATE)
    return bw, cw, ar_s, ai_s


def _s5(u_tb, bw, cw, ar_s, ai_s, d_row, w_glu, b_glu_row, g_row):
    seq, bsz, _ = u_tb.shape
    tc = S5_TIME_BLOCK
    assert seq % tc == 0 and bsz % 16 == 0
    full = lambda shape: pl.BlockSpec(shape, lambda t: (0,) * len(shape))
    return pl.pallas_call(
        functools.partial(_s5_kernel, tc=tc, bsz=bsz),
        grid=(seq // tc,),
        in_specs=[pl.BlockSpec((tc, bsz, SSM_WIDTH), lambda t: (t, 0, 0)),
                  full(bw.shape), full(cw.shape), full(ar_s.shape), full(ai_s.shape),
                  full((1, SSM_WIDTH)), full((SSM_WIDTH, SSM_WIDTH)), full((1, SSM_WIDTH)),
                  full((1, SSM_WIDTH))],
        out_specs=pl.BlockSpec((tc, bsz, SSM_WIDTH), lambda t: (t, 0, 0)),
        out_shape=jax.ShapeDtypeStruct((seq, bsz, SSM_WIDTH), BF16),
        scratch_shapes=[pltpu.VMEM((S5_SLABS, bsz, 2 * SLAB_STATE), F32),
                        pltpu.VMEM((S5_SLABS, tc * bsz, 2 * SLAB_STATE), F32)],
        compiler_params=pltpu.CompilerParams(dimension_semantics=("arbitrary",),
                                             vmem_limit_bytes=VMEM_LIMIT),
        name="s5_branch",
    )(u_tb, bw, cw, ar_s, ai_s, d_row, w_glu, b_glu_row, g_row)


def _layer_norm_rows(v, g, b):
    mu = jnp.mean(v, axis=-1, keepdims=True)
    c = v - mu
    var = jnp.mean(c * c, axis=-1, keepdims=True)
    return c * lax.rsqrt(var + LN_EPS) * g + b


def _outproj_kernel(attn_ref, ssm_ref, x_ref, wo_ref, ag_ref, g1_ref, b1_ref, wrt_ref, brt_ref, triu_ref,
                    h_ref, eid_ref, gate_ref, rank_ref, cnt_ref):
    a = attn_ref[...]
    a = a * lax.rsqrt(jnp.mean(a * a, axis=-1, keepdims=True) + RMS_EPS) * ag_ref[...]
    mix = (jnp.dot(a.astype(BF16), wo_ref[0:ATTN_WIDTH, :], preferred_element_type=F32)
           + jnp.dot(ssm_ref[...], wo_ref[ATTN_WIDTH:, :], preferred_element_type=F32))
    h = _layer_norm_rows(DEEPNORM_ALPHA * x_ref[...] + mix, g1_ref[...], b1_ref[...])
    h_ref[...] = h

    logits = lax.dot_general(wrt_ref[...], h.astype(BF16), (((1,), (1,)), ((), ())),
                             preferred_element_type=F32)[:ROUTER_ROWS] + brt_ref[...]
    tm = logits.shape[1]
    row = lax.broadcasted_iota(jnp.int32, (ROUTER_ROWS, tm), 0).astype(F32)
    no_row = float(LANES)
    is_g = row < N_EXPERT_GROUPS
    gl = jnp.where(is_g, logits, NEG_BIG)
    gmax = jnp.max(gl, axis=0, keepdims=True)
    g_idx = jnp.min(jnp.where(is_g & (gl == gmax), row, no_row), axis=0, keepdims=True)
    g_p = 1.0 / jnp.sum(jnp.where(is_g, jnp.exp(gl - gmax), 0.0), axis=0, keepdims=True)
    first = N_EXPERT_GROUPS + g_idx * EXPERTS_PER_GROUP
    in_grp = (row >= first) & (row < first + EXPERTS_PER_GROUP)
    el = jnp.where(in_grp, logits, NEG_BIG)
    m1 = jnp.max(el, axis=0, keepdims=True)
    i1 = jnp.min(jnp.where(in_grp & (el == m1), row, no_row), axis=0, keepdims=True)
    rest = in_grp & (row != i1)
    el2 = jnp.where(rest, logits, NEG_BIG)
    m2 = jnp.max(el2, axis=0, keepdims=True)
    i2 = jnp.min(jnp.where(rest & (el2 == m2), row, no_row), axis=0, keepdims=True)
    r = jnp.exp(m2 - m1)
    p1 = 1.0 / (1.0 + r)
    p2 = r / (1.0 + r)
    k2 = lax.broadcasted_iota(jnp.int32, (2, tm), 0)
    eid_ref[...] = (jnp.where(k2 == 0, i1, i2) - N_EXPERT_GROUPS).astype(jnp.int32)
    gate_ref[...] = g_p * jnp.where(k2 == 0, p1, p2)

    @pl.when(pl.program_id(0) == 0)
    def _():
        cnt_ref[...] = jnp.zeros_like(cnt_ref)
    hit1 = row == i1
    hit2 = row == i2
    onehot = jnp.where(hit1 | hit2, 1.0, 0.0)
    cnt = cnt_ref[...]
    before = jnp.dot(onehot.astype(BF16), triu_ref[...], preferred_element_type=F32) + cnt
    r1 = jnp.sum(jnp.where(hit1, before, 0.0), axis=0, keepdims=True)
    r2 = jnp.sum(jnp.where(hit2, before, 0.0), axis=0, keepdims=True)
    rank_ref[...] = jnp.where(k2 == 0, r1, r2).astype(jnp.int32)
    cnt_ref[...] = cnt + jnp.sum(onehot, axis=1, keepdims=True)


def _outproj(attn2, ssm2, x2, wo, ag_row, g1_row, b1_row, wrt, brt):
    n_tok = x2.shape[0]
    tm = ROW_BLOCK
    triu = jnp.triu(jnp.ones((tm, tm), BF16), 1)
    row_spec = lambda w: pl.BlockSpec((tm, w), lambda i: (i, 0))
    col_spec = lambda r: pl.BlockSpec((r, tm), lambda i: (0, i))
    full = lambda shape: pl.BlockSpec(shape, lambda i: (0,) * len(shape))
    return pl.pallas_call(
        _outproj_kernel,
        grid=(n_tok // tm,),
        in_specs=[row_spec(ATTN_WIDTH), row_spec(SSM_WIDTH), row_spec(D_MODEL),
                  full((D_MODEL, D_MODEL)), full((1, ATTN_WIDTH)), full((1, D_MODEL)), full((1, D_MODEL)),
                  full((LANES, D_MODEL)), full((ROUTER_ROWS, 1)), full((tm, tm))],
        out_specs=[row_spec(D_MODEL), col_spec(2), col_spec(2), col_spec(2), full((ROUTER_ROWS, 1))],
        out_shape=[jax.ShapeDtypeStruct((n_tok, D_MODEL), F32),
                   jax.ShapeDtypeStruct((2, n_tok), jnp.int32),
                   jax.ShapeDtypeStruct((2, n_tok), F32),
                   jax.ShapeDtypeStruct((2, n_tok), jnp.int32),
                   jax.ShapeDtypeStruct((ROUTER_ROWS, 1), F32)],
        compiler_params=pltpu.CompilerParams(dimension_semantics=("arbitrary",),
                                             vmem_limit_bytes=VMEM_LIMIT),
        name="outproj_ln_router",
    )(attn2, ssm2, x2, wo, ag_row, g1_row, b1_row, wrt, brt, triu)


def _dispatch_plan(eid, rank, cnt_col, n_tok):
    counts = cnt_col[N_EXPERT_GROUPS:N_EXPERT_GROUPS + N_EXPERTS, 0].astype(jnp.int32)
    padded = ((counts + MOE_BLOCK - 1) // MOE_BLOCK) * MOE_BLOCK
    pends = jnp.cumsum(padded)
    pstarts = pends - padded
    n_pad = ((2 * n_tok + N_EXPERTS * (MOE_BLOCK - 1) + MOE_BLOCK - 1) // MOE_BLOCK) * MOE_BLOCK
    n_blocks = n_pad // MOE_BLOCK
    blk0 = jnp.arange(n_blocks, dtype=jnp.int32) * MOE_BLOCK
    block_e = jnp.minimum(jnp.sum((pends[None, :] <= blk0[:, None]).astype(jnp.int32), axis=1), N_EXPERTS - 1)
    n_used = (pends[-1] // MOE_BLOCK).astype(jnp.int32).reshape(1)
    tail = jnp.where(padded > 0, pends - MOE_BLOCK, -1).astype(jnp.int32)

    tb = TOKEN_BLOCK
    nb = n_tok // tb
    onehot = eid[:, :, None] == jnp.arange(N_EXPERTS, dtype=jnp.int32)[None, None, :]
    run_len = jnp.sum(onehot.reshape(2, nb, tb, N_EXPERTS).astype(jnp.int32), axis=(0, 2))
    before = jnp.cumsum(run_len, axis=0) - run_len
    run_src = jnp.cumsum(run_len, axis=1) - run_len
    run_dst = pstarts[None, :] + before
    shift = jnp.broadcast_to((run_src - before)[:, None, :], (nb, tb, N_EXPERTS)).reshape(n_tok, N_EXPERTS)
    local = rank + jnp.sum(jnp.where(onehot, shift[None], 0), axis=-1)
    any_long = (jnp.max(run_len, axis=1) >= (1 << SHORT_RUN_BITS)).astype(jnp.int32)
    runs = (run_len.reshape(-1), run_src.reshape(-1), run_dst.reshape(-1), any_long)
    return local, runs, block_e, n_used, tail, n_blocks


ROW_SUB = D_MODEL // LANES


def _rows(start, size):
    return pl.ds(pl.multiple_of(start * ROW_SUB, ROW_SUB), size * ROW_SUB)


def _store_row_tiles(ref, row0, v):
    n = v.shape[0]
    for s in range(ROW_SUB):
        ref[pl.ds(row0 * ROW_SUB + s, n, stride=ROW_SUB), :] = v[:, LANES * s:LANES * (s + 1)]


def _load_row_tiles(ref, row0, n):
    return jnp.concatenate([ref[pl.ds(row0 * ROW_SUB + s, n, stride=ROW_SUB), :] for s in range(ROW_SUB)], axis=1)


def _for_each_run_piece(blk, len_ref, src_ref, dst_ref, bits, make_copy, act):
    for e in range(N_EXPERTS):
        idx = blk * N_EXPERTS + e
        n = len_ref[idx]
        src = src_ref[idx]
        dst = dst_ref[idx]
        for b in reversed(range(bits)):
            @pl.when((n & (1 << b)) != 0)
            def _():
                off = lax.shift_left(lax.shift_right_logical(n, b + 1), b + 1)
                act(make_copy(src + off, dst + off, 1 << b))


def _by_run_size(long_ref, blk, body):
    @pl.when(long_ref[blk] == 0)
    def _():
        body(SHORT_RUN_BITS)

    @pl.when(long_ref[blk] != 0)
    def _():
        body(RUN_BITS)


def _dispatch_kernel(tail_ref, nu_ref, len_ref, src_ref, dst_ref, long_ref, local_ref, h_ref, xb_hbm, zbuf,
                     pk_ref, sem_z, sem, *, n_blocks):
    i = pl.program_id(0)
    nb = pl.num_programs(0) - 1
    tb = h_ref.shape[0]
    cur = i % 2

    @pl.when(i == 0)
    def _():
        zbuf[...] = jnp.zeros_like(zbuf)

        def zero_copy(row0):
            return pltpu.make_async_copy(zbuf, xb_hbm.at[_rows(row0, MOE_BLOCK)], sem_z)

        def each_zero_block(fn):
            for e in range(N_EXPERTS):
                @pl.when(tail_ref[e] >= 0)
                def _():
                    fn(zero_copy(tail_ref[e]))

            def unused(b, carry):
                fn(zero_copy(b * MOE_BLOCK))
                return carry
            lax.fori_loop(nu_ref[0], n_blocks, unused, 0)

        each_zero_block(lambda c: c.start())
        each_zero_block(lambda c: c.wait())

    def drain(which):
        pltpu.make_async_copy(pk_ref.at[which], xb_hbm.at[_rows(0, 2 * tb)], sem.at[which]).wait()

    @pl.when(i >= 2)
    def _():
        drain(cur)

    def group():
        pos = lax.broadcasted_iota(jnp.int32, (2 * tb, tb), 0)
        hit = (pos == local_ref[0:1, :]) | (pos == local_ref[1:2, :])
        perm = jnp.where(hit, 1.0, 0.0).astype(BF16)
        _store_row_tiles(pk_ref.at[cur], 0, jnp.dot(perm, h_ref[...].astype(BF16), preferred_element_type=F32))

    def send(bits):
        def run_copy(src, dst, size):
            return pltpu.make_async_copy(pk_ref.at[1 - cur, _rows(src, size)], xb_hbm.at[_rows(dst, size)],
                                         sem.at[1 - cur])
        _for_each_run_piece(i - 1, len_ref, src_ref, dst_ref, bits, run_copy, lambda c: c.start())

    @pl.when(i == 0)
    def _():
        group()

    @pl.when((i > 0) & (i < nb))
    def _():
        def send_and_group(bits):
            send(bits)
            group()
        _by_run_size(long_ref, i - 1, send_and_group)

    @pl.when(i == nb)
    def _():
        send(RUN_BITS)
        drain(1 - cur)


def _dispatch(h, local, runs, tail, n_used, n_blocks):
    n_tok = h.shape[0]
    tb = TOKEN_BLOCK
    nb = n_tok // tb
    grid_spec = pltpu.PrefetchScalarGridSpec(
        num_scalar_prefetch=6,
        grid=(nb + 1,),
        in_specs=[pl.BlockSpec((2, tb), lambda i, *_: (0, jnp.minimum(i, nb - 1))),
                  pl.BlockSpec((tb, D_MODEL), lambda i, *_: (jnp.minimum(i, nb - 1), 0))],
        out_specs=pl.BlockSpec(memory_space=pl.ANY),
        scratch_shapes=[pltpu.VMEM((MOE_BLOCK * ROW_SUB, LANES), F32),
                        pltpu.VMEM((2, 2 * tb * ROW_SUB, LANES), F32),
                        pltpu.SemaphoreType.DMA, pltpu.SemaphoreType.DMA((2,))],
    )
    return pl.pallas_call(
        functools.partial(_dispatch_kernel, n_blocks=n_blocks),
        grid_spec=grid_spec,
        out_shape=jax.ShapeDtypeStruct((n_blocks * MOE_BLOCK * ROW_SUB, LANES), F32),
        compiler_params=pltpu.CompilerParams(dimension_semantics=("arbitrary",),
                                             vmem_limit_bytes=VMEM_LIMIT),
        name="moe_dispatch",
    )(tail, n_used, *runs, local, h)


def _expert_kernel(be_ref, nu_ref, x_ref, wg_ref, wu_ref, wd_ref, y_ref, wgu_t, wd_t, yt_ref):
    i = pl.program_id(0)
    used = i < nu_ref[0]

    @pl.when(used & ((i == 0) | (be_ref[i] != be_ref[jnp.maximum(i - 1, 0)])))
    def _():
        wgu_t[0:D_EXPERT, :] = wg_ref[...].T.astype(BF16)
        wgu_t[D_EXPERT:, :] = wu_ref[...].T.astype(BF16)
        wd_t[...] = wd_ref[...].T.astype(BF16)

    @pl.when(used)
    def _():
        sub = range(MOE_BLOCK // MOE_SUB)
        gus = []
        for c in sub:
            xb = _load_row_tiles(x_ref, c * MOE_SUB, MOE_SUB).astype(BF16)
            gus.append(lax.dot_general(wgu_t[...], xb, (((1,), (1,)), ((), ())),
                                       preferred_element_type=F32))
        hids = []
        for c in sub:
            gt = gus[c][0:D_EXPERT]
            hids.append((gt * (1.0 / (1.0 + jnp.exp(-gt))) * gus[c][D_EXPERT:]).astype(BF16))
        for c in sub:
            yt_ref[c] = jnp.dot(wd_t[...], hids[c], preferred_element_type=F32)
        for c in sub:
            _store_row_tiles(y_ref, c * MOE_SUB, yt_ref[c].T)

    @pl.when(jnp.logical_not(used))
    def _():
        y_ref[...] = jnp.zeros_like(y_ref)


def _experts(xb, block_e, n_used, wg, wu, wd):
    n_blocks = xb.shape[0] // (MOE_BLOCK * ROW_SUB)
    last = lambda i, nu: jnp.minimum(i, nu[0] - 1)
    grid_spec = pltpu.PrefetchScalarGridSpec(
        num_scalar_prefetch=2,
        grid=(n_blocks,),
        in_specs=[pl.BlockSpec((MOE_BLOCK * ROW_SUB, LANES), lambda i, be, nu: (last(i, nu), 0)),
                  pl.BlockSpec((None, D_MODEL, D_EXPERT), lambda i, be, nu: (be[last(i, nu)], 0, 0)),
                  pl.BlockSpec((None, D_MODEL, D_EXPERT), lambda i, be, nu: (be[last(i, nu)], 0, 0)),
                  pl.BlockSpec((None, D_EXPERT, D_MODEL), lambda i, be, nu: (be[last(i, nu)], 0, 0))],
        out_specs=pl.BlockSpec((MOE_BLOCK * ROW_SUB, LANES), lambda i, be, nu: (i, 0)),
        scratch_shapes=[pltpu.VMEM((2 * D_EXPERT, D_MODEL), BF16), pltpu.VMEM((D_MODEL, D_EXPERT), BF16),
                        pltpu.VMEM((MOE_BLOCK // MOE_SUB, D_MODEL, MOE_SUB), F32)],
    )
    return pl.pallas_call(
        _expert_kernel,
        grid_spec=grid_spec,
        out_shape=jax.ShapeDtypeStruct(xb.shape, F32),
        compiler_params=pltpu.CompilerParams(dimension_semantics=("arbitrary",),
                                             vmem_limit_bytes=VMEM_LIMIT),
        name="moe_experts",
    )(block_e, n_used, xb, wg, wu, wd)


def _combine_kernel(len_ref, src_ref, dst_ref, long_ref, local_ref, gate_ref, h_ref, g2_ref, b2_ref, yb_hbm,
                    o_ref, buf, sem):
    i = pl.program_id(0)
    n = pl.num_programs(0)
    tb = h_ref.shape[0]
    cur = i % 2

    def fetch(blk, which, bits):
        def run_copy(src, dst, size):
            return pltpu.make_async_copy(yb_hbm.at[_rows(dst, size)], buf.at[which, _rows(src, size)], sem.at[which])
        _for_each_run_piece(blk, len_ref, src_ref, dst_ref, bits, run_copy, lambda c: c.start())

    def drain(which):
        pltpu.make_async_copy(yb_hbm.at[_rows(0, 2 * tb)], buf.at[which], sem.at[which]).wait()

    @pl.when(i == 0)
    def _():
        fetch(i, 0, RUN_BITS)

    drain(cur)
    nxt = jnp.minimum(i + 1, n - 1)

    def fetch_next_and_combine(bits):
        fetch(nxt, 1 - cur, bits)
        y = _load_row_tiles(buf.at[cur], 0, 2 * tb).astype(BF16)
        pos = lax.broadcasted_iota(jnp.int32, (tb, 2 * tb), 1)
        local = local_ref[...]
        gate = gate_ref[...]
        mix = (jnp.where(pos == local[:, 0:1], gate[:, 0:1], 0.0)
               + jnp.where(pos == local[:, 1:2], gate[:, 1:2], 0.0))
        mix_hi = mix.astype(BF16)
        mix_lo = (mix - mix_hi.astype(F32)).astype(BF16)
        moe = (jnp.dot(mix_hi, y, preferred_element_type=F32) + jnp.dot(mix_lo, y, preferred_element_type=F32))
        o_ref[...] = _layer_norm_rows(DEEPNORM_ALPHA * h_ref[...] + moe, g2_ref[...], b2_ref[...])
    _by_run_size(long_ref, nxt, fetch_next_and_combine)

    @pl.when(i == n - 1)
    def _():
        drain(1 - cur)


def _combine(h, yb, local_t, runs, gate, g2_row, b2_row):
    n_tok = h.shape[0]
    tb = TOKEN_BLOCK
    row_spec = lambda w: pl.BlockSpec((tb, w), lambda i, *_: (i, 0))
    full = lambda shape: pl.BlockSpec(shape, lambda i, *_: (0,) * len(shape))
    grid_spec = pltpu.PrefetchScalarGridSpec(
        num_scalar_prefetch=4,
        grid=(n_tok // tb,),
        in_specs=[row_spec(2), row_spec(2), row_spec(D_MODEL), full((1, D_MODEL)), full((1, D_MODEL)),
                  pl.BlockSpec(memory_space=pl.ANY)],
        out_specs=row_spec(D_MODEL),
        scratch_shapes=[pltpu.VMEM((2, 2 * tb * ROW_SUB, LANES), F32), pltpu.SemaphoreType.DMA((2,))],
    )
    return pl.pallas_call(
        _combine_kernel,
        grid_spec=grid_spec,
        out_shape=jax.ShapeDtypeStruct((n_tok, D_MODEL), F32),
        compiler_params=pltpu.CompilerParams(dimension_semantics=("arbitrary",),
                                             vmem_limit_bytes=VMEM_LIMIT),
        name="combine_ln",
    )(*runs, local_t, gate, h, g2_row, b2_row, yb)


def kernel(x, w_in, b_fgate, s5_lambda_re, s5_lambda_im, s5_log_step, s5_b_re, s5_b_im, s5_c_re, s5_c_im, s5_d, w_glu, b_glu, attn_norm_g, ssm_norm_g, w_out, ln1_g, ln1_b, w_router_group, b_router_group, w_router_expert, b_router_expert, w_gate, w_up, w_down, ln2_g, ln2_b):
    bsz, seq, d = x.shape
    n_tok = bsz * seq
    a = ATTN_WIDTH
    for l in range(DEPTH):
        x2 = x.reshape(n_tok, d)
        w = w_in[l]
        f0 = 3 * a
        w_cat = jnp.concatenate(
            [w[:, :a] * (HEAD_DIM ** -0.5 * LOG2E), w[:, a:2 * a], w[:, f0 + ATTN_HEADS:]], axis=1).astype(BF16)
        wvt = jnp.pad(w[:, 2 * a:f0 + ATTN_HEADS].T, ((0, GATE_ROWS - ATTN_HEADS), (0, 0))).astype(BF16)
        q, k, vt, u, eq, ek = _inproj(x2, w_cat, wvt, b_fgate[l].reshape(ATTN_HEADS, 1), seq)
        attn = _attention(q.reshape(bsz, seq, a), k.reshape(bsz, seq, a), vt,
                          eq.reshape(bsz, seq, EXT_W), ek.reshape(bsz, seq, EXT_W))

        bw, cw, ar_s, ai_s = _s5_weights(s5_lambda_re[l], s5_lambda_im[l], s5_log_step[l],
                                         s5_b_re[l], s5_b_im[l], s5_c_re[l], s5_c_im[l])
        u_tb = u.reshape(bsz, seq, SSM_WIDTH).transpose(1, 0, 2)
        ssm_tb = _s5(u_tb, bw, cw, ar_s, ai_s, s5_d[l].reshape(1, SSM_WIDTH), w_glu[l].astype(BF16),
                     b_glu[l].reshape(1, SSM_WIDTH), ssm_norm_g[l].reshape(1, SSM_WIDTH))
        ssm2 = ssm_tb.transpose(1, 0, 2).reshape(n_tok, SSM_WIDTH)

        n_route = N_EXPERT_GROUPS + N_EXPERTS
        wrt = jnp.pad(jnp.concatenate([w_router_group[l], w_router_expert[l]], axis=1).T,
                      ((0, LANES - n_route), (0, 0))).astype(BF16)
        brt = jnp.pad(jnp.concatenate([b_router_group[l], b_router_expert[l]]),
                      (0, ROUTER_ROWS - n_route)).reshape(ROUTER_ROWS, 1)
        h, eid, gate, rank, cnt_col = _outproj(
            attn.reshape(n_tok, a), ssm2, x2, w_out[l].astype(BF16), attn_norm_g[l].reshape(1, a),
            ln1_g[l].reshape(1, d), ln1_b[l].reshape(1, d), wrt, brt)

        local, runs, block_e, n_used, tail, n_blocks = _dispatch_plan(eid, rank, cnt_col, n_tok)
        xb = _dispatch(h, local, runs, tail, n_used, n_blocks)
        yb = _experts(xb, block_e, n_used, w_gate[l], w_up[l], w_down[l])
        x = _combine(h, yb, local.T, runs, gate.T, ln2_g[l].reshape(1, d),
                     ln2_b[l].reshape(1, d)).reshape(bsz, seq, d)
    return x
```

```python
import functools
import math

import jax
import jax.numpy as jnp
import numpy as np
from jax import lax
from jax.experimental import pallas as pl
from jax.experimental.pallas import tpu as pltpu

F32 = jnp.float32
BF16 = jnp.bfloat16

D_MODEL = 1024
HEAD_DIM = 64
ATTN_WIDTH = 512
ATTN_HEADS = 8
SSM_WIDTH = 512
SSM_GROUP = 16
SSM_GROUPS = 32
SSM_STATE = 64
N_EXPERT_GROUPS = 4
EXPERTS_PER_GROUP = 8
N_EXPERTS = 32
D_EXPERT = 512
MOE_BLOCK = 512
MOE_SUB = 256
TOKEN_BLOCK = 256
RUN_BITS = TOKEN_BLOCK.bit_length()
SHORT_RUN_BITS = 6
DEPTH = 1
DEEPNORM_ALPHA = (2.0 * DEPTH) ** 0.25
LN_EPS = 1e-5
RMS_EPS = 1e-6

LANES = 128
NEG_BIG = -1e30
LOG2E = math.log2(math.e)
ONES_ROWS = 16
VT_ROWS = ATTN_HEADS * (HEAD_DIM + ONES_ROWS)
GATE_ROWS = 16
PAIR_BIAS_LANES = 8
EXT_W = ATTN_HEADS // 2 * LANES
VMEM_LIMIT = 48 * 1024 * 1024

ROW_BLOCK = 1024
ATTN_BLOCK = 256
ATTN_HEADS_PER_STEP = 8
S5_TIME_BLOCK = 64
S5_SLABS = 4
ROUTER_ROWS = 40
SLAB_CH = SSM_WIDTH // S5_SLABS
SLAB_STATE = SSM_GROUPS // S5_SLABS * SSM_STATE


def _split3(c):
    hi = c.astype(BF16)
    r1 = c - hi.astype(F32)
    mid = r1.astype(BF16)
    lo = (r1 - mid.astype(F32)).astype(BF16)
    return hi, mid, lo


def _inproj_kernel(x_ref, w_ref, wvt_ref, bf_ref, triu_ref, sel_ref, q_ref, k_ref, vt_ref, u_ref, eq_ref, ek_ref,
                   carry_ref, *, blocks_per_seq):
    i = pl.program_id(0)

    @pl.when(i % blocks_per_seq == 0)
    def _():
        carry_ref[...] = jnp.zeros_like(carry_ref)

    tm = x_ref.shape[0]
    xb = x_ref[...].astype(BF16)
    z = jnp.dot(xb, w_ref[...], preferred_element_type=F32)
    a = ATTN_WIDTH
    q_ref[...] = z[:, :a].astype(BF16)
    k_ref[...] = z[:, a:2 * a].astype(BF16)
    u_ref[...] = z[:, 2 * a:2 * a + SSM_WIDTH].astype(BF16)
    vtf = lax.dot_general(wvt_ref[...], xb, (((1,), (1,)), ((), ())), preferred_element_type=F32)
    vt = vtf[:ATTN_WIDTH].astype(BF16)
    ones_rows = jnp.ones((ONES_ROWS, tm), BF16)
    vt = jnp.concatenate([piece for h in range(ATTN_HEADS)
                          for piece in (vt[HEAD_DIM * h:HEAD_DIM * (h + 1)], ones_rows)], axis=0)
    for jj in range(tm // ATTN_BLOCK):
        vt_ref[jj] = vt[:, jj * ATTN_BLOCK:(jj + 1) * ATTN_BLOCK]

    f = vtf[ATTN_WIDTH:ATTN_WIDTH + ATTN_HEADS] + bf_ref[...]
    logf = jnp.minimum(f, 0.0) - jnp.log1p(jnp.exp(-jnp.abs(f)))
    parts = jnp.dot(jnp.concatenate(_split3(logf), axis=0), triu_ref[...], preferred_element_type=F32)
    nh = ATTN_HEADS
    cum = parts[0:nh] + parts[nh:2 * nh] + parts[2 * nh:3 * nh] + carry_ref[...]
    carry_ref[...] = cum[:, tm - 1:tm]
    hi, mid, lo = (t.astype(F32) for t in _split3(cum * LOG2E))
    one_row = jnp.where(lax.broadcasted_iota(jnp.int32, (nh, tm), 0) == 0, 1.0, 0.0)
    packed_t = jnp.concatenate([hi, mid, lo, one_row, jnp.zeros((LANES - 4 * nh, tm), F32)], axis=0)
    ext = jnp.dot(packed_t.T.astype(BF16), sel_ref[...], preferred_element_type=F32).astype(BF16)
    eq_ref[...] = ext[:, :EXT_W]
    ek_ref[...] = ext[:, EXT_W:]


def _bias_lane_selector():
    sel = np.zeros((LANES, 2, ATTN_HEADS // 2, LANES), np.float32)
    one = 3 * ATTN_HEADS
    for h in range(ATTN_HEADS):
        pair, base = h // 2, PAIR_BIAS_LANES * (h % 2)
        for t in range(3):
            sel[ATTN_HEADS * t + h, 0, pair, base + t] = 1.0
            sel[one, 0, pair, base + 3 + t] = 1.0
            sel[one, 1, pair, base + t] = 1.0
            sel[ATTN_HEADS * t + h, 1, pair, base + 3 + t] = -1.0
    return jnp.asarray(sel.reshape(LANES, 2 * EXT_W), BF16)


def _inproj(x2, w_cat, wvt, bf_col, seq):
    n_tok = x2.shape[0]
    tm = ROW_BLOCK
    assert seq % tm == 0 and n_tok % tm == 0 and tm % ATTN_BLOCK == 0
    bsz = n_tok // seq
    bps = seq // tm
    ncol = w_cat.shape[1]
    vblk = tm // ATTN_BLOCK
    row_spec = lambda w: pl.BlockSpec((tm, w), lambda i: (i, 0))
    return pl.pallas_call(
        functools.partial(_inproj_kernel, blocks_per_seq=bps),
        grid=(n_tok // tm,),
        in_specs=[row_spec(D_MODEL),
                  pl.BlockSpec((D_MODEL, ncol), lambda i: (0, 0)),
                  pl.BlockSpec((ATTN_WIDTH + GATE_ROWS, D_MODEL), lambda i: (0, 0)),
                  pl.BlockSpec((ATTN_HEADS, 1), lambda i: (0, 0)),
                  pl.BlockSpec((tm, tm), lambda i: (0, 0)),
                  pl.BlockSpec((LANES, 2 * EXT_W), lambda i: (0, 0))],
        out_specs=[row_spec(ATTN_WIDTH), row_spec(ATTN_WIDTH),
                   pl.BlockSpec((None, vblk, VT_ROWS, ATTN_BLOCK), lambda i: (i // bps, i % bps, 0, 0)),
                   row_spec(SSM_WIDTH), row_spec(EXT_W), row_spec(EXT_W)],
        out_shape=[jax.ShapeDtypeStruct((n_tok, ATTN_WIDTH), BF16)] * 2
                  + [jax.ShapeDtypeStruct((bsz, seq // ATTN_BLOCK, VT_ROWS, ATTN_BLOCK), BF16),
                     jax.ShapeDtypeStruct((n_tok, SSM_WIDTH), BF16),
                     jax.ShapeDtypeStruct((n_tok, EXT_W), BF16),
                     jax.ShapeDtypeStruct((n_tok, EXT_W), BF16)],
        scratch_shapes=[pltpu.VMEM((ATTN_HEADS, 1), F32)],
        compiler_params=pltpu.CompilerParams(dimension_semantics=("arbitrary",),
                                             vmem_limit_bytes=VMEM_LIMIT),
        name="inproj",
    )(x2, w_cat, wvt, bf_col, jnp.triu(jnp.ones((tm, tm), BF16)), _bias_lane_selector())


def _attn_kernel(q_ref, k_ref, vt_ref, eq_ref, ek_ref, o_ref, *, tq, seq, nh):
    qi = pl.program_id(2)
    lane = lax.broadcasted_iota(jnp.int32, (tq, LANES), 1)
    q_augs = []
    for h in range(nh):
        qb = q_ref[:, LANES * (h // 2):LANES * (h // 2 + 1)]
        head_lanes = (lane < HEAD_DIM) if h % 2 == 0 else (lane >= HEAD_DIM)
        qh = jnp.where(head_lanes, qb, jnp.zeros((), BF16))
        own = (lane >= PAIR_BIAS_LANES * (h % 2)) & (lane < PAIR_BIAS_LANES * (h % 2 + 1))
        eqh = jnp.where(own, eq_ref[:, LANES * (h // 2):LANES * (h // 2 + 1)], jnp.zeros((), BF16))
        q_augs.append(jnp.concatenate([qh, eqh], axis=1))
    key_i = lax.broadcasted_iota(jnp.int32, (tq, tq), 0)
    qry_i = lax.broadcasted_iota(jnp.int32, (tq, tq), 1)
    causal = key_i <= qry_i
    vrows = HEAD_DIM + ONES_ROWS

    def scores_of(k0, nk):
        out = []
        for h in range(nh):
            kj = k_ref[pl.ds(k0, nk), LANES * (h // 2):LANES * (h // 2 + 1)]
            k_aug = jnp.concatenate([kj, ek_ref[pl.ds(k0, nk), LANES * (h // 2):LANES * (h // 2 + 1)]], axis=1)
            out.append(lax.dot_general(k_aug, q_augs[h], (((1,), (1,)), ((), ())), preferred_element_type=F32))
        return tuple(out)

    def absorb(j0, scores, carry):
        stats = []
        for h in range(nh):
            m = carry[h][0]
            m_new = jnp.maximum(m, jnp.max(scores[h], axis=0, keepdims=True))
            stats.append((m_new, jnp.exp2(m - m_new), jnp.exp2(scores[h] - m_new).astype(BF16)))
        new = []
        for h in range(nh):
            m_new, corr, p = stats[h]
            res = None
            for c in range(p.shape[0] // tq):
                part = jnp.dot(vt_ref[j0 + c, vrows * h:vrows * (h + 1), :], p[c * tq:(c + 1) * tq],
                               preferred_element_type=F32)
                res = part if res is None else res + part
            acc = corr * carry[h][2] + res[0:HEAD_DIM, :]
            l = corr * carry[h][1] + res[HEAD_DIM:HEAD_DIM + 1, :]
            new.append((m_new, l, acc))
        return tuple(new)

    def pair_step(jj, carry):
        return absorb(2 * jj, scores_of(pl.multiple_of(jj * (2 * tq), 2 * tq), 2 * tq), carry)

    def finish(carry):
        out_t = jnp.concatenate([c[2] / c[1] for c in carry], axis=0)
        o_ref[...] = out_t.T

    init1 = (jnp.full((1, tq), NEG_BIG, F32), jnp.zeros((1, tq), F32), jnp.zeros((HEAD_DIM, tq), F32))
    carry = lax.fori_loop(0, qi // 2, pair_step, (init1,) * nh)

    @pl.when(qi % 2 == 0)
    def _():
        diag = scores_of(pl.multiple_of(qi * tq, tq), tq)
        finish(absorb(qi, tuple(jnp.where(causal, s, NEG_BIG) for s in diag), carry))

    @pl.when(qi % 2 == 1)
    def _():
        last = scores_of(pl.multiple_of((qi - 1) * tq, 2 * tq), 2 * tq)
        causal2 = (lax.broadcasted_iota(jnp.int32, (2 * tq, tq), 0)
                   <= lax.broadcasted_iota(jnp.int32, (2 * tq, tq), 1) + tq)
        finish(absorb(qi - 1, tuple(jnp.where(causal2, s, NEG_BIG) for s in last), carry))


def _attention(q, k, vt, eq, ek):
    bsz, seq, _ = q.shape
    tq = ATTN_BLOCK
    nh = ATTN_HEADS_PER_STEP
    assert seq % tq == 0 and ATTN_HEADS % nh == 0 and nh % 2 == 0
    w = nh * HEAD_DIM
    return pl.pallas_call(
        functools.partial(_attn_kernel, tq=tq, seq=seq, nh=nh),
        grid=(bsz, ATTN_HEADS // nh, seq // tq),
        in_specs=[pl.BlockSpec((None, tq, w), lambda b, g, i: (b, i, g)),
                  pl.BlockSpec((None, seq, w), lambda b, g, i: (b, 0, g)),
                  pl.BlockSpec((None, seq // tq, nh * (HEAD_DIM + ONES_ROWS), tq), lambda b, g, i: (b, 0, g, 0)),
                  pl.BlockSpec((None, tq, nh // 2 * LANES), lambda b, g, i: (b, i, g)),
                  pl.BlockSpec((None, seq, nh // 2 * LANES), lambda b, g, i: (b, 0, g))],
        out_specs=pl.BlockSpec((None, tq, w), lambda b, g, i: (b, i, g)),
        out_shape=jax.ShapeDtypeStruct((bsz, seq, ATTN_WIDTH), F32),
        compiler_params=pltpu.CompilerParams(dimension_semantics=("parallel", "parallel", "arbitrary"),
                                             vmem_limit_bytes=VMEM_LIMIT),
        name="fox_attention",
    )(q, k, vt, eq, ek)


def _s5_kernel(u_ref, bw_ref, cw_ref, ar_ref, ai_ref, d_ref, wglu_ref, bglu_ref, g_ref, o_ref,
               state_ref, bu_ref, *, tc, bsz):
    @pl.when(pl.program_id(0) == 0)
    def _():
        state_ref[...] = jnp.zeros_like(state_ref)

    rows = tc * bsz
    ns = SLAB_STATE
    u = u_ref[...].reshape(rows, SSM_WIDTH)
    ub = u
    for i in range(S5_SLABS):
        bu_ref[i] = jnp.dot(ub[:, SLAB_CH * i:SLAB_CH * (i + 1)], bw_ref[i], preferred_element_type=F32)
    for i in range(S5_SLABS):
        ar = jnp.broadcast_to(ar_ref[i], (bsz, ns))
        ai = jnp.broadcast_to(ai_ref[i], (bsz, ns))
        xr = state_ref[i, :, 0:ns]
        xi = state_ref[i, :, ns:2 * ns]
        for t in range(tc):
            r0 = t * bsz
            nr = ar * xr - ai * xi + bu_ref[i, r0:r0 + bsz, 0:ns]
            ni = ar * xi + ai * xr + bu_ref[i, r0:r0 + bsz, ns:2 * ns]
            bu_ref[i, r0:r0 + bsz, 0:ns] = nr
            bu_ref[i, r0:r0 + bsz, ns:2 * ns] = ni
            xr, xi = nr, ni
        state_ref[i, :, 0:ns] = xr
        state_ref[i, :, ns:2 * ns] = xi
    ys = [jnp.dot(bu_ref[i].astype(BF16), cw_ref[i], preferred_element_type=F32) for i in range(S5_SLABS)]
    y = jnp.concatenate(ys, axis=1) + d_ref[...] * u.astype(F32)
    y = 0.5 * y * (1.0 + jnp.tanh(math.sqrt(2.0 / math.pi) * (y + 0.044715 * (y * y * y))))
    z = jnp.dot(y.astype(BF16), wglu_ref[...], preferred_element_type=F32) + bglu_ref[...]
    y = y * (1.0 / (1.0 + jnp.exp(-z)))
    y = y * lax.rsqrt(jnp.mean(y * y, axis=-1, keepdims=True) + RMS_EPS) * g_ref[...]
    o_ref[...] = y.astype(BF16).reshape(tc, bsz, SSM_WIDTH)


def _s5_weights(lam_re, lam_im, log_step, b_re, b_im, c_re, c_im):
    delta = jnp.exp(log_step)[:, None]
    mag = jnp.exp(lam_re * delta)
    ar = mag * jnp.cos(lam_im * delta)
    ai = mag * jnp.sin(lam_im * delta)
    den = lam_re * lam_re + lam_im * lam_im
    num_re = ar - 1.0
    coef_re = (num_re * lam_re + ai * lam_im) / den
    coef_im = (ai * lam_re - num_re * lam_im) / den
    bbar_re = coef_re[..., None] * b_re - coef_im[..., None] * b_im
    bbar_im = coef_re[..., None] * b_im + coef_im[..., None] * b_re
    gs = SSM_GROUPS // S5_SLABS
    eye = jnp.eye(gs, dtype=F32)

    def b_slab(bb):
        bb = bb.reshape(S5_SLABS, gs, SSM_STATE, SSM_GROUP)
        w = jnp.einsum('ab,sapc->sacbp', eye, bb)
        return w.reshape(S5_SLABS, gs * SSM_GROUP, gs * SSM_STATE)

    def c_slab(cc):
        cc = cc.reshape(S5_SLABS, gs, SSM_GROUP, SSM_STATE)
        w = jnp.einsum('ab,sacp->sbpac', eye, cc)
        return w.reshape(S5_SLABS, gs * SSM_STATE, gs * SSM_GROUP)

    bw = jnp.concatenate([b_slab(bbar_re), b_slab(bbar_im)], axis=2).astype(BF16)
    cw = jnp.concatenate([c_slab(c_re), -c_slab(c_im)], axis=1).astype(BF16)
    ar_s = ar.reshape(S5_SLABS, 1, SLAB_STATE)
    ai_s = ai.reshape(S5_SLABS, 1, SLAB_STATE)
    return bw, cw, ar_s, ai_s


def _s5(u_tb, bw, cw, ar_s, ai_s, d_row, w_glu, b_glu_row, g_row):
    seq, bsz, _ = u_tb.shape
    tc = S5_TIME_BLOCK
    assert seq % tc == 0 and bsz % 16 == 0
    full = lambda shape: pl.BlockSpec(shape, lambda t: (0,) * len(shape))
    return pl.pallas_call(
        functools.partial(_s5_kernel, tc=tc, bsz=bsz),
        grid=(seq // tc,),
        in_specs=[pl.BlockSpec((tc, bsz, SSM_WIDTH), lambda t: (t, 0, 0)),
                  full(bw.shape), full(cw.shape), full(ar_s.shape), full(ai_s.shape),
                  full((1, SSM_WIDTH)), full((SSM_WIDTH, SSM_WIDTH)), full((1, SSM_WIDTH)),
                  full((1, SSM_WIDTH))],
        out_specs=pl.BlockSpec((tc, bsz, SSM_WIDTH), lambda t: (t, 0, 0)),
        out_shape=jax.ShapeDtypeStruct((seq, bsz, SSM_WIDTH), BF16),
        scratch_shapes=[pltpu.VMEM((S5_SLABS, bsz, 2 * SLAB_STATE), F32),
                        pltpu.VMEM((S5_SLABS, tc * bsz, 2 * SLAB_STATE), F32)],
        compiler_params=pltpu.CompilerParams(dimension_semantics=("arbitrary",),
                                             vmem_limit_bytes=VMEM_LIMIT),
        name="s5_branch",
    )(u_tb, bw, cw, ar_s, ai_s, d_row, w_glu, b_glu_row, g_row)


def _layer_norm_rows(v, g, b):
    mu = jnp.mean(v, axis=-1, keepdims=True)
    c = v - mu
    var = jnp.mean(c * c, axis=-1, keepdims=True)
    return c * lax.rsqrt(var + LN_EPS) * g + b


def _outproj_kernel(attn_ref, ssm_ref, x_ref, wo_ref, ag_ref, g1_ref, b1_ref, wrt_ref, brt_ref, triu_ref,
                    h_ref, eid_ref, gate_ref, rank_ref, cnt_ref):
    a = attn_ref[...]
    a = a * lax.rsqrt(jnp.mean(a * a, axis=-1, keepdims=True) + RMS_EPS) * ag_ref[...]
    mix = (jnp.dot(a.astype(BF16), wo_ref[0:ATTN_WIDTH, :], preferred_element_type=F32)
           + jnp.dot(ssm_ref[...], wo_ref[ATTN_WIDTH:, :], preferred_element_type=F32))
    h = _layer_norm_rows(DEEPNORM_ALPHA * x_ref[...] + mix, g1_ref[...], b1_ref[...])
    h_ref[...] = h

    logits = lax.dot_general(wrt_ref[...], h.astype(BF16), (((1,), (1,)), ((), ())),
                             preferred_element_type=F32)[:ROUTER_ROWS] + brt_ref[...]
    tm = logits.shape[1]
    row = lax.broadcasted_iota(jnp.int32, (ROUTER_ROWS, tm), 0).astype(F32)
    no_row = float(LANES)
    is_g = row < N_EXPERT_GROUPS
    gl = jnp.where(is_g, logits, NEG_BIG)
    gmax = jnp.max(gl, axis=0, keepdims=True)
    g_idx = jnp.min(jnp.where(is_g & (gl == gmax), row, no_row), axis=0, keepdims=True)
    g_p = 1.0 / jnp.sum(jnp.where(is_g, jnp.exp(gl - gmax), 0.0), axis=0, keepdims=True)
    first = N_EXPERT_GROUPS + g_idx * EXPERTS_PER_GROUP
    in_grp = (row >= first) & (row < first + EXPERTS_PER_GROUP)
    el = jnp.where(in_grp, logits, NEG_BIG)
    m1 = jnp.max(el, axis=0, keepdims=True)
    i1 = jnp.min(jnp.where(in_grp & (el == m1), row, no_row), axis=0, keepdims=True)
    rest = in_grp & (row != i1)
    el2 = jnp.where(rest, logits, NEG_BIG)
    m2 = jnp.max(el2, axis=0, keepdims=True)
    i2 = jnp.min(jnp.where(rest & (el2 == m2), row, no_row), axis=0, keepdims=True)
    r = jnp.exp(m2 - m1)
    p1 = 1.0 / (1.0 + r)
    p2 = r / (1.0 + r)
    k2 = lax.broadcasted_iota(jnp.int32, (2, tm), 0)
    eid_ref[...] = (jnp.where(k2 == 0, i1, i2) - N_EXPERT_GROUPS).astype(jnp.int32)
    gate_ref[...] = g_p * jnp.where(k2 == 0, p1, p2)

    @pl.when(pl.program_id(0) == 0)
    def _():
        cnt_ref[...] = jnp.zeros_like(cnt_ref)
    hit1 = row == i1
    hit2 = row == i2
    onehot = jnp.where(hit1 | hit2, 1.0, 0.0)
    cnt = cnt_ref[...]
    before = jnp.dot(onehot.astype(BF16), triu_ref[...], preferred_element_type=F32) + cnt
    r1 = jnp.sum(jnp.where(hit1, before, 0.0), axis=0, keepdims=True)
    r2 = jnp.sum(jnp.where(hit2, before, 0.0), axis=0, keepdims=True)
    rank_ref[...] = jnp.where(k2 == 0, r1, r2).astype(jnp.int32)
    cnt_ref[...] = cnt + jnp.sum(onehot, axis=1, keepdims=True)


def _outproj(attn2, ssm2, x2, wo, ag_row, g1_row, b1_row, wrt, brt):
    n_tok = x2.shape[0]
    tm = ROW_BLOCK
    triu = jnp.triu(jnp.ones((tm, tm), BF16), 1)
    row_spec = lambda w: pl.BlockSpec((tm, w), lambda i: (i, 0))
    col_spec = lambda r: pl.BlockSpec((r, tm), lambda i: (0, i))
    full = lambda shape: pl.BlockSpec(shape, lambda i: (0,) * len(shape))
    return pl.pallas_call(
        _outproj_kernel,
        grid=(n_tok // tm,),
        in_specs=[row_spec(ATTN_WIDTH), row_spec(SSM_WIDTH), row_spec(D_MODEL),
                  full((D_MODEL, D_MODEL)), full((1, ATTN_WIDTH)), full((1, D_MODEL)), full((1, D_MODEL)),
                  full((LANES, D_MODEL)), full((ROUTER_ROWS, 1)), full((tm, tm))],
        out_specs=[row_spec(D_MODEL), col_spec(2), col_spec(2), col_spec(2), full((ROUTER_ROWS, 1))],
        out_shape=[jax.ShapeDtypeStruct((n_tok, D_MODEL), F32),
                   jax.ShapeDtypeStruct((2, n_tok), jnp.int32),
                   jax.ShapeDtypeStruct((2, n_tok), F32),
                   jax.ShapeDtypeStruct((2, n_tok), jnp.int32),
                   jax.ShapeDtypeStruct((ROUTER_ROWS, 1), F32)],
        compiler_params=pltpu.CompilerParams(dimension_semantics=("arbitrary",),
                                             vmem_limit_bytes=VMEM_LIMIT),
        name="outproj_ln_router",
    )(attn2, ssm2, x2, wo, ag_row, g1_row, b1_row, wrt, brt, triu)


def _dispatch_plan(eid, rank, cnt_col, n_tok):
    counts = cnt_col[N_EXPERT_GROUPS:N_EXPERT_GROUPS + N_EXPERTS, 0].astype(jnp.int32)
    padded = ((counts + MOE_BLOCK - 1) // MOE_BLOCK) * MOE_BLOCK
    pends = jnp.cumsum(padded)
    pstarts = pends - padded
    n_pad = ((2 * n_tok + N_EXPERTS * (MOE_BLOCK - 1) + MOE_BLOCK - 1) // MOE_BLOCK) * MOE_BLOCK
    n_blocks = n_pad // MOE_BLOCK
    blk0 = jnp.arange(n_blocks, dtype=jnp.int32) * MOE_BLOCK
    block_e = jnp.minimum(jnp.sum((pends[None, :] <= blk0[:, None]).astype(jnp.int32), axis=1), N_EXPERTS - 1)
    n_used = (pends[-1] // MOE_BLOCK).astype(jnp.int32).reshape(1)
    tail = jnp.where(padded > 0, pends - MOE_BLOCK, -1).astype(jnp.int32)

    tb = TOKEN_BLOCK
    nb = n_tok // tb
    onehot = eid[:, :, None] == jnp.arange(N_EXPERTS, dtype=jnp.int32)[None, None, :]
    run_len = jnp.sum(onehot.reshape(2, nb, tb, N_EXPERTS).astype(jnp.int32), axis=(0, 2))
    before = jnp.cumsum(run_len, axis=0) - run_len
    run_src = jnp.cumsum(run_len, axis=1) - run_len
    run_dst = pstarts[None, :] + before
    shift = jnp.broadcast_to((run_src - before)[:, None, :], (nb, tb, N_EXPERTS)).reshape(n_tok, N_EXPERTS)
    local = rank + jnp.sum(jnp.where(onehot, shift[None], 0), axis=-1)
    any_long = (jnp.max(run_len, axis=1) >= (1 << SHORT_RUN_BITS)).astype(jnp.int32)
    runs = (run_len.reshape(-1), run_src.reshape(-1), run_dst.reshape(-1), any_long)
    return local, runs, block_e, n_used, tail, n_blocks


ROW_SUB = D_MODEL // LANES


def _rows(start, size):
    return pl.ds(pl.multiple_of(start * ROW_SUB, ROW_SUB), size * ROW_SUB)


def _store_row_tiles(ref, row0, v):
    n = v.shape[0]
    for s in range(ROW_SUB):
        ref[pl.ds(row0 * ROW_SUB + s, n, stride=ROW_SUB), :] = v[:, LANES * s:LANES * (s + 1)]


def _load_row_tiles(ref, row0, n):
    return jnp.concatenate([ref[pl.ds(row0 * ROW_SUB + s, n, stride=ROW_SUB), :] for s in range(ROW_SUB)], axis=1)


def _for_each_run_piece(blk, len_ref, src_ref, dst_ref, bits, make_copy, act):
    for e in range(N_EXPERTS):
        idx = blk * N_EXPERTS + e
        n = len_ref[idx]
        src = src_ref[idx]
        dst = dst_ref[idx]
        for b in reversed(range(bits)):
            @pl.when((n & (1 << b)) != 0)
            def _():
                off = lax.shift_left(lax.shift_right_logical(n, b + 1), b + 1)
                act(make_copy(src + off, dst + off, 1 << b))


def _by_run_size(long_ref, blk, body):
    @pl.when(long_ref[blk] == 0)
    def _():
        body(SHORT_RUN_BITS)

    @pl.when(long_ref[blk] != 0)
    def _():
        body(RUN_BITS)


def _dispatch_kernel(tail_ref, nu_ref, len_ref, src_ref, dst_ref, long_ref, local_ref, h_ref, xb_hbm, zbuf,
                     pk_ref, sem_z, sem, *, n_blocks):
    i = pl.program_id(0)
    nb = pl.num_programs(0) - 1
    tb = h_ref.shape[0]
    cur = i % 2

    @pl.when(i == 0)
    def _():
        zbuf[...] = jnp.zeros_like(zbuf)

        def zero_copy(row0):
            return pltpu.make_async_copy(zbuf, xb_hbm.at[_rows(row0, MOE_BLOCK)], sem_z)

        def each_zero_block(fn):
            for e in range(N_EXPERTS):
                @pl.when(tail_ref[e] >= 0)
                def _():
                    fn(zero_copy(tail_ref[e]))

            def unused(b, carry):
                fn(zero_copy(b * MOE_BLOCK))
                return carry
            lax.fori_loop(nu_ref[0], n_blocks, unused, 0)

        each_zero_block(lambda c: c.start())
        each_zero_block(lambda c: c.wait())

    def drain(which):
        pltpu.make_async_copy(pk_ref.at[which], xb_hbm.at[_rows(0, 2 * tb)], sem.at[which]).wait()

    @pl.when(i >= 2)
    def _():
        drain(cur)

    def group():
        pos = lax.broadcasted_iota(jnp.int32, (2 * tb, tb), 0)
        hit = (pos == local_ref[0:1, :]) | (pos == local_ref[1:2, :])
        perm = jnp.where(hit, 1.0, 0.0).astype(BF16)
        _store_row_tiles(pk_ref.at[cur], 0, jnp.dot(perm, h_ref[...].astype(BF16), preferred_element_type=F32))

    def send(bits):
        def run_copy(src, dst, size):
            return pltpu.make_async_copy(pk_ref.at[1 - cur, _rows(src, size)], xb_hbm.at[_rows(dst, size)],
                                         sem.at[1 - cur])
        _for_each_run_piece(i - 1, len_ref, src_ref, dst_ref, bits, run_copy, lambda c: c.start())

    @pl.when(i == 0)
    def _():
        group()

    @pl.when((i > 0) & (i < nb))
    def _():
        def send_and_group(bits):
            send(bits)
            group()
        _by_run_size(long_ref, i - 1, send_and_group)

    @pl.when(i == nb)
    def _():
        send(RUN_BITS)
        drain(1 - cur)


def _dispatch(h, local, runs, tail, n_used, n_blocks):
    n_tok = h.shape[0]
    tb = TOKEN_BLOCK
    nb = n_tok // tb
    grid_spec = pltpu.PrefetchScalarGridSpec(
        num_scalar_prefetch=6,
        grid=(nb + 1,),
        in_specs=[pl.BlockSpec((2, tb), lambda i, *_: (0, jnp.minimum(i, nb - 1))),
                  pl.BlockSpec((tb, D_MODEL), lambda i, *_: (jnp.minimum(i, nb - 1), 0))],
        out_specs=pl.BlockSpec(memory_space=pl.ANY),
        scratch_shapes=[pltpu.VMEM((MOE_BLOCK * ROW_SUB, LANES), F32),
                        pltpu.VMEM((2, 2 * tb * ROW_SUB, LANES), F32),
                        pltpu.SemaphoreType.DMA, pltpu.SemaphoreType.DMA((2,))],
    )
    return pl.pallas_call(
        functools.partial(_dispatch_kernel, n_blocks=n_blocks),
        grid_spec=grid_spec,
        out_shape=jax.ShapeDtypeStruct((n_blocks * MOE_BLOCK * ROW_SUB, LANES), F32),
        compiler_params=pltpu.CompilerParams(dimension_semantics=("arbitrary",),
                                             vmem_limit_bytes=VMEM_LIMIT),
        name="moe_dispatch",
    )(tail, n_used, *runs, local, h)


def _expert_kernel(be_ref, nu_ref, x_ref, wg_ref, wu_ref, wd_ref, y_ref, wgu_t, wd_t, yt_ref):
    i = pl.program_id(0)
    used = i < nu_ref[0]

    @pl.when(used & ((i == 0) | (be_ref[i] != be_ref[jnp.maximum(i - 1, 0)])))
    def _():
        wgu_t[0:D_EXPERT, :] = wg_ref[...].astype(BF16).T
        wgu_t[D_EXPERT:, :] = wu_ref[...].astype(BF16).T
        wd_t[...] = wd_ref[...].astype(BF16).T

    @pl.when(used)
    def _():
        sub = range(MOE_BLOCK // MOE_SUB)
        gus = []
        for c in sub:
            xb = _load_row_tiles(x_ref, c * MOE_SUB, MOE_SUB).astype(BF16)
            gus.append(lax.dot_general(wgu_t[...], xb, (((1,), (1,)), ((), ())),
                                       preferred_element_type=F32))
        hids = []
        for c in sub:
            gt = gus[c][0:D_EXPERT]
            hids.append((gt * (1.0 / (1.0 + jnp.exp(-gt))) * gus[c][D_EXPERT:]).astype(BF16))
        for c in sub:
            yt_ref[c] = jnp.dot(wd_t[...], hids[c], preferred_element_type=F32)
        for c in sub:
            _store_row_tiles(y_ref, c * MOE_SUB, yt_ref[c].astype(BF16).T.astype(F32))

    @pl.when(jnp.logical_not(used))
    def _():
        y_ref[...] = jnp.zeros_like(y_ref)


def _experts(xb, block_e, n_used, wg, wu, wd):
    n_blocks = xb.shape[0] // (MOE_BLOCK * ROW_SUB)
    last = lambda i, nu: jnp.minimum(i, nu[0] - 1)
    grid_spec = pltpu.PrefetchScalarGridSpec(
        num_scalar_prefetch=2,
        grid=(n_blocks,),
        in_specs=[pl.BlockSpec((MOE_BLOCK * ROW_SUB, LANES), lambda i, be, nu: (last(i, nu), 0)),
                  pl.BlockSpec((None, D_MODEL, D_EXPERT), lambda i, be, nu: (be[last(i, nu)], 0, 0)),
                  pl.BlockSpec((None, D_MODEL, D_EXPERT), lambda i, be, nu: (be[last(i, nu)], 0, 0)),
                  pl.BlockSpec((None, D_EXPERT, D_MODEL), lambda i, be, nu: (be[last(i, nu)], 0, 0))],
        out_specs=pl.BlockSpec((MOE_BLOCK * ROW_SUB, LANES), lambda i, be, nu: (i, 0)),
        scratch_shapes=[pltpu.VMEM((2 * D_EXPERT, D_MODEL), BF16), pltpu.VMEM((D_MODEL, D_EXPERT), BF16),
                        pltpu.VMEM((MOE_BLOCK // MOE_SUB, D_MODEL, MOE_SUB), F32)],
    )
    return pl.pallas_call(
        _expert_kernel,
        grid_spec=grid_spec,
        out_shape=jax.ShapeDtypeStruct(xb.shape, F32),
        compiler_params=pltpu.CompilerParams(dimension_semantics=("arbitrary",),
                                             vmem_limit_bytes=VMEM_LIMIT),
        name="moe_experts",
    )(block_e, n_used, xb, wg, wu, wd)


def _combine_kernel(len_ref, src_ref, dst_ref, long_ref, local_ref, gate_ref, h_ref, g2_ref, b2_ref, yb_hbm,
                    o_ref, buf, sem):
    i = pl.program_id(0)
    n = pl.num_programs(0)
    tb = h_ref.shape[0]
    cur = i % 2

    def fetch(blk, which, bits):
        def run_copy(src, dst, size):
            return pltpu.make_async_copy(yb_hbm.at[_rows(dst, size)], buf.at[which, _rows(src, size)], sem.at[which])
        _for_each_run_piece(blk, len_ref, src_ref, dst_ref, bits, run_copy, lambda c: c.start())

    def drain(which):
        pltpu.make_async_copy(yb_hbm.at[_rows(0, 2 * tb)], buf.at[which], sem.at[which]).wait()

    @pl.when(i == 0)
    def _():
        fetch(i, 0, RUN_BITS)

    drain(cur)
    nxt = jnp.minimum(i + 1, n - 1)

    def fetch_next_and_combine(bits):
        fetch(nxt, 1 - cur, bits)
        y = _load_row_tiles(buf.at[cur], 0, 2 * tb).astype(BF16)
        pos = lax.broadcasted_iota(jnp.int32, (tb, 2 * tb), 1)
        local = local_ref[...]
        gate = gate_ref[...]
        mix = (jnp.where(pos == local[:, 0:1], gate[:, 0:1], 0.0)
               + jnp.where(pos == local[:, 1:2], gate[:, 1:2], 0.0))
        mix_hi = mix.astype(BF16)
        mix_lo = (mix - mix_hi.astype(F32)).astype(BF16)
        moe = (jnp.dot(mix_hi, y, preferred_element_type=F32) + jnp.dot(mix_lo, y, preferred_element_type=F32))
        o_ref[...] = _layer_norm_rows(DEEPNORM_ALPHA * h_ref[...] + moe, g2_ref[...], b2_ref[...])
    _by_run_size(long_ref, nxt, fetch_next_and_combine)

    @pl.when(i == n - 1)
    def _():
        drain(1 - cur)


def _combine(h, yb, local_t, runs, gate, g2_row, b2_row):
    n_tok = h.shape[0]
    tb = TOKEN_BLOCK
    row_spec = lambda w: pl.BlockSpec((tb, w), lambda i, *_: (i, 0))
    full = lambda shape: pl.BlockSpec(shape, lambda i, *_: (0,) * len(shape))
    grid_spec = pltpu.PrefetchScalarGridSpec(
        num_scalar_prefetch=4,
        grid=(n_tok // tb,),
        in_specs=[row_spec(2), row_spec(2), row_spec(D_MODEL), full((1, D_MODEL)), full((1, D_MODEL)),
                  pl.BlockSpec(memory_space=pl.ANY)],
        out_specs=row_spec(D_MODEL),
        scratch_shapes=[pltpu.VMEM((2, 2 * tb * ROW_SUB, LANES), F32), pltpu.SemaphoreType.DMA((2,))],
    )
    return pl.pallas_call(
        _combine_kernel,
        grid_spec=grid_spec,
        out_shape=jax.ShapeDtypeStruct((n_tok, D_MODEL), F32),
        compiler_params=pltpu.CompilerParams(dimension_semantics=("arbitrary",),
                                             vmem_limit_bytes=VMEM_LIMIT),
        name="combine_ln",
    )(*runs, local_t, gate, h, g2_row, b2_row, yb)


def kernel(x, w_in, b_fgate, s5_lambda_re, s5_lambda_im, s5_log_step, s5_b_re, s5_b_im, s5_c_re, s5_c_im, s5_d, w_glu, b_glu, attn_norm_g, ssm_norm_g, w_out, ln1_g, ln1_b, w_router_group, b_router_group, w_router_expert, b_router_expert, w_gate, w_up, w_down, ln2_g, ln2_b):
    bsz, seq, d = x.shape
    n_tok = bsz * seq
    a = ATTN_WIDTH
    for l in range(DEPTH):
        x2 = x.reshape(n_tok, d)
        w = w_in[l]
        f0 = 3 * a
        w_cat = jnp.concatenate(
            [w[:, :a] * (HEAD_DIM ** -0.5 * LOG2E), w[:, a:2 * a], w[:, f0 + ATTN_HEADS:]], axis=1).astype(BF16)
        wvt = jnp.pad(w[:, 2 * a:f0 + ATTN_HEADS].T, ((0, GATE_ROWS - ATTN_HEADS), (0, 0))).astype(BF16)
        q, k, vt, u, eq, ek = _inproj(x2, w_cat, wvt, b_fgate[l].reshape(ATTN_HEADS, 1), seq)
        attn = _attention(q.reshape(bsz, seq, a), k.reshape(bsz, seq, a), vt,
                          eq.reshape(bsz, seq, EXT_W), ek.reshape(bsz, seq, EXT_W))

        bw, cw, ar_s, ai_s = _s5_weights(s5_lambda_re[l], s5_lambda_im[l], s5_log_step[l],
                                         s5_b_re[l], s5_b_im[l], s5_c_re[l], s5_c_im[l])
        u_tb = u.reshape(bsz, seq, SSM_WIDTH).transpose(1, 0, 2)
        ssm_tb = _s5(u_tb, bw, cw, ar_s, ai_s, s5_d[l].reshape(1, SSM_WIDTH), w_glu[l].astype(BF16),
                     b_glu[l].reshape(1, SSM_WIDTH), ssm_norm_g[l].reshape(1, SSM_WIDTH))
        ssm2 = ssm_tb.transpose(1, 0, 2).reshape(n_tok, SSM_WIDTH)

        n_route = N_EXPERT_GROUPS + N_EXPERTS
        wrt = jnp.pad(jnp.concatenate([w_router_group[l], w_router_expert[l]], axis=1).T,
                      ((0, LANES - n_route), (0, 0))).astype(BF16)
        brt = jnp.pad(jnp.concatenate([b_router_group[l], b_router_expert[l]]),
                      (0, ROUTER_ROWS - n_route)).reshape(ROUTER_ROWS, 1)
        h, eid, gate, rank, cnt_col = _outproj(
            attn.reshape(n_tok, a), ssm2, x2, w_out[l].astype(BF16), attn_norm_g[l].reshape(1, a),
            ln1_g[l].reshape(1, d), ln1_b[l].reshape(1, d), wrt, brt)

        local, runs, block_e, n_used, tail, n_blocks = _dispatch_plan(eid, rank, cnt_col, n_tok)
        xb = _dispatch(h, local, runs, tail, n_used, n_blocks)
        yb = _experts(xb, block_e, n_used, w_gate[l], w_up[l], w_down[l])
        x = _combine(h, yb, local.T, runs, gate.T, ln2_g[l].reshape(1, d),
                     ln2_b[l].reshape(1, d)).reshape(bsz, seq, d)
    return x
```

```python
import functools
import math

import jax
import jax.numpy as jnp
import numpy as np
from jax import lax
from jax.experimental import pallas as pl
from jax.experimental.pallas import tpu as pltpu

F32 = jnp.float32
BF16 = jnp.bfloat16

D_MODEL = 1024
HEAD_DIM = 64
ATTN_WIDTH = 512
ATTN_HEADS = 8
SSM_WIDTH = 512
SSM_GROUP = 16
SSM_GROUPS = 32
SSM_STATE = 64
N_EXPERT_GROUPS = 4
EXPERTS_PER_GROUP = 8
N_EXPERTS = 32
D_EXPERT = 512
MOE_BLOCK = 512
MOE_SUB = 256
TOKEN_BLOCK = 512
RUN_BITS = TOKEN_BLOCK.bit_length()
SHORT_RUN_BITS = 6
DEPTH = 1
DEEPNORM_ALPHA = (2.0 * DEPTH) ** 0.25
LN_EPS = 1e-5
RMS_EPS = 1e-6

LANES = 128
NEG_BIG = -1e30
LOG2E = math.log2(math.e)
ONES_ROWS = 16
VT_ROWS = ATTN_HEADS * (HEAD_DIM + ONES_ROWS)
GATE_ROWS = 16
PAIR_BIAS_LANES = 8
EXT_W = ATTN_HEADS // 2 * LANES
VMEM_LIMIT = 48 * 1024 * 1024

ROW_BLOCK = 1024
ATTN_BLOCK = 256
ATTN_HEADS_PER_STEP = 8
S5_TIME_BLOCK = 64
S5_SLABS = 4
ROUTER_ROWS = 40
SLAB_CH = SSM_WIDTH // S5_SLABS
SLAB_STATE = SSM_GROUPS // S5_SLABS * SSM_STATE


def _split3(c):
    hi = c.astype(BF16)
    r1 = c - hi.astype(F32)
    mid = r1.astype(BF16)
    lo = (r1 - mid.astype(F32)).astype(BF16)
    return hi, mid, lo


def _inproj_kernel(x_ref, w_ref, wvt_ref, bf_ref, triu_ref, sel_ref, q_ref, k_ref, vt_ref, u_ref, eq_ref, ek_ref,
                   carry_ref, *, blocks_per_seq):
    i = pl.program_id(0)

    @pl.when(i % blocks_per_seq == 0)
    def _():
        carry_ref[...] = jnp.zeros_like(carry_ref)

    tm = x_ref.shape[0]
    xb = x_ref[...].astype(BF16)
    z = jnp.dot(xb, w_ref[...], preferred_element_type=F32)
    a = ATTN_WIDTH
    q_ref[...] = z[:, :a].astype(BF16)
    k_ref[...] = z[:, a:2 * a].astype(BF16)
    u_ref[...] = z[:, 2 * a:2 * a + SSM_WIDTH].astype(BF16)
    vtf = lax.dot_general(wvt_ref[...], xb, (((1,), (1,)), ((), ())), preferred_element_type=F32)
    vt = vtf[:ATTN_WIDTH].astype(BF16)
    ones_rows = jnp.ones((ONES_ROWS, tm), BF16)
    vt = jnp.concatenate([piece for h in range(ATTN_HEADS)
                          for piece in (vt[HEAD_DIM * h:HEAD_DIM * (h + 1)], ones_rows)], axis=0)
    for jj in range(tm // ATTN_BLOCK):
        vt_ref[jj] = vt[:, jj * ATTN_BLOCK:(jj + 1) * ATTN_BLOCK]

    f = vtf[ATTN_WIDTH:ATTN_WIDTH + ATTN_HEADS] + bf_ref[...]
    logf = jnp.minimum(f, 0.0) - jnp.log1p(jnp.exp(-jnp.abs(f)))
    parts = jnp.dot(jnp.concatenate(_split3(logf), axis=0), triu_ref[...], preferred_element_type=F32)
    nh = ATTN_HEADS
    cum = parts[0:nh] + parts[nh:2 * nh] + parts[2 * nh:3 * nh] + carry_ref[...]
    carry_ref[...] = cum[:, tm - 1:tm]
    hi, mid, lo = (t.astype(F32) for t in _split3(cum * LOG2E))
    one_row = jnp.where(lax.broadcasted_iota(jnp.int32, (nh, tm), 0) == 0, 1.0, 0.0)
    packed_t = jnp.concatenate([hi, mid, lo, one_row, jnp.zeros((LANES - 4 * nh, tm), F32)], axis=0)
    ext = jnp.dot(packed_t.T.astype(BF16), sel_ref[...], preferred_element_type=F32).astype(BF16)
    eq_ref[...] = ext[:, :EXT_W]
    ek_ref[...] = ext[:, EXT_W:]


def _bias_lane_selector():
    sel = np.zeros((LANES, 2, ATTN_HEADS // 2, LANES), np.float32)
    one = 3 * ATTN_HEADS
    for h in range(ATTN_HEADS):
        pair, base = h // 2, PAIR_BIAS_LANES * (h % 2)
        for t in range(3):
            sel[ATTN_HEADS * t + h, 0, pair, base + t] = 1.0
            sel[one, 0, pair, base + 3 + t] = 1.0
            sel[one, 1, pair, base + t] = 1.0
            sel[ATTN_HEADS * t + h, 1, pair, base + 3 + t] = -1.0
    return jnp.asarray(sel.reshape(LANES, 2 * EXT_W), BF16)


def _inproj(x2, w_cat, wvt, bf_col, seq):
    n_tok = x2.shape[0]
    tm = ROW_BLOCK
    assert seq % tm == 0 and n_tok % tm == 0 and tm % ATTN_BLOCK == 0
    bsz = n_tok // seq
    bps = seq // tm
    ncol = w_cat.shape[1]
    vblk = tm // ATTN_BLOCK
    row_spec = lambda w: pl.BlockSpec((tm, w), lambda i: (i, 0))
    return pl.pallas_call(
        functools.partial(_inproj_kernel, blocks_per_seq=bps),
        grid=(n_tok // tm,),
        in_specs=[row_spec(D_MODEL),
                  pl.BlockSpec((D_MODEL, ncol), lambda i: (0, 0)),
                  pl.BlockSpec((ATTN_WIDTH + GATE_ROWS, D_MODEL), lambda i: (0, 0)),
                  pl.BlockSpec((ATTN_HEADS, 1), lambda i: (0, 0)),
                  pl.BlockSpec((tm, tm), lambda i: (0, 0)),
                  pl.BlockSpec((LANES, 2 * EXT_W), lambda i: (0, 0))],
        out_specs=[row_spec(ATTN_WIDTH), row_spec(ATTN_WIDTH),
                   pl.BlockSpec((None, vblk, VT_ROWS, ATTN_BLOCK), lambda i: (i // bps, i % bps, 0, 0)),
                   row_spec(SSM_WIDTH), row_spec(EXT_W), row_spec(EXT_W)],
        out_shape=[jax.ShapeDtypeStruct((n_tok, ATTN_WIDTH), BF16)] * 2
                  + [jax.ShapeDtypeStruct((bsz, seq // ATTN_BLOCK, VT_ROWS, ATTN_BLOCK), BF16),
                     jax.ShapeDtypeStruct((n_tok, SSM_WIDTH), BF16),
                     jax.ShapeDtypeStruct((n_tok, EXT_W), BF16),
                     jax.ShapeDtypeStruct((n_tok, EXT_W), BF16)],
        scratch_shapes=[pltpu.VMEM((ATTN_HEADS, 1), F32)],
        compiler_params=pltpu.CompilerParams(dimension_semantics=("arbitrary",),
                                             vmem_limit_bytes=VMEM_LIMIT),
        name="inproj",
    )(x2, w_cat, wvt, bf_col, jnp.triu(jnp.ones((tm, tm), BF16)), _bias_lane_selector())


def _attn_kernel(q_ref, k_ref, vt_ref, eq_ref, ek_ref, o_ref, *, tq, seq, nh):
    qi = pl.program_id(2)
    lane = lax.broadcasted_iota(jnp.int32, (tq, LANES), 1)
    q_augs = []
    for h in range(nh):
        qb = q_ref[:, LANES * (h // 2):LANES * (h // 2 + 1)]
        head_lanes = (lane < HEAD_DIM) if h % 2 == 0 else (lane >= HEAD_DIM)
        qh = jnp.where(head_lanes, qb, jnp.zeros((), BF16))
        own = (lane >= PAIR_BIAS_LANES * (h % 2)) & (lane < PAIR_BIAS_LANES * (h % 2 + 1))
        eqh = jnp.where(own, eq_ref[:, LANES * (h // 2):LANES * (h // 2 + 1)], jnp.zeros((), BF16))
        q_augs.append(jnp.concatenate([qh, eqh], axis=1))
    key_i = lax.broadcasted_iota(jnp.int32, (tq, tq), 0)
    qry_i = lax.broadcasted_iota(jnp.int32, (tq, tq), 1)
    causal = key_i <= qry_i
    vrows = HEAD_DIM + ONES_ROWS

    def scores_of(k0, nk):
        out = []
        for h in range(nh):
            kj = k_ref[pl.ds(k0, nk), LANES * (h // 2):LANES * (h // 2 + 1)]
            k_aug = jnp.concatenate([kj, ek_ref[pl.ds(k0, nk), LANES * (h // 2):LANES * (h // 2 + 1)]], axis=1)
            out.append(lax.dot_general(k_aug, q_augs[h], (((1,), (1,)), ((), ())), preferred_element_type=F32))
        return tuple(out)

    def absorb(j0, scores, carry):
        stats = []
        for h in range(nh):
            m = carry[h][0]
            m_new = jnp.maximum(m, jnp.max(scores[h], axis=0, keepdims=True))
            stats.append((m_new, jnp.exp2(m - m_new), jnp.exp2(scores[h] - m_new).astype(BF16)))
        new = []
        for h in range(nh):
            m_new, corr, p = stats[h]
            res = None
            for c in range(p.shape[0] // tq):
                part = jnp.dot(vt_ref[j0 + c, vrows * h:vrows * (h + 1), :], p[c * tq:(c + 1) * tq],
                               preferred_element_type=F32)
                res = part if res is None else res + part
            acc = corr * carry[h][2] + res[0:HEAD_DIM, :]
            l = corr * carry[h][1] + res[HEAD_DIM:HEAD_DIM + 1, :]
            new.append((m_new, l, acc))
        return tuple(new)

    def pair_step(jj, carry):
        return absorb(2 * jj, scores_of(pl.multiple_of(jj * (2 * tq), 2 * tq), 2 * tq), carry)

    def finish(carry):
        out_t = jnp.concatenate([c[2] / c[1] for c in carry], axis=0)
        o_ref[...] = out_t.T

    init1 = (jnp.full((1, tq), NEG_BIG, F32), jnp.zeros((1, tq), F32), jnp.zeros((HEAD_DIM, tq), F32))
    carry = lax.fori_loop(0, qi // 2, pair_step, (init1,) * nh)

    @pl.when(qi % 2 == 0)
    def _():
        diag = scores_of(pl.multiple_of(qi * tq, tq), tq)
        finish(absorb(qi, tuple(jnp.where(causal, s, NEG_BIG) for s in diag), carry))

    @pl.when(qi % 2 == 1)
    def _():
        last = scores_of(pl.multiple_of((qi - 1) * tq, 2 * tq), 2 * tq)
        causal2 = (lax.broadcasted_iota(jnp.int32, (2 * tq, tq), 0)
                   <= lax.broadcasted_iota(jnp.int32, (2 * tq, tq), 1) + tq)
        finish(absorb(qi - 1, tuple(jnp.where(causal2, s, NEG_BIG) for s in last), carry))


def _attention(q, k, vt, eq, ek):
    bsz, seq, _ = q.shape
    tq = ATTN_BLOCK
    nh = ATTN_HEADS_PER_STEP
    assert seq % tq == 0 and ATTN_HEADS % nh == 0 and nh % 2 == 0
    w = nh * HEAD_DIM
    return pl.pallas_call(
        functools.partial(_attn_kernel, tq=tq, seq=seq, nh=nh),
        grid=(bsz, ATTN_HEADS // nh, seq // tq),
        in_specs=[pl.BlockSpec((None, tq, w), lambda b, g, i: (b, i, g)),
                  pl.BlockSpec((None, seq, w), lambda b, g, i: (b, 0, g)),
                  pl.BlockSpec((None, seq // tq, nh * (HEAD_DIM + ONES_ROWS), tq), lambda b, g, i: (b, 0, g, 0)),
                  pl.BlockSpec((None, tq, nh // 2 * LANES), lambda b, g, i: (b, i, g)),
                  pl.BlockSpec((None, seq, nh // 2 * LANES), lambda b, g, i: (b, 0, g))],
        out_specs=pl.BlockSpec((None, tq, w), lambda b, g, i: (b, i, g)),
        out_shape=jax.ShapeDtypeStruct((bsz, seq, ATTN_WIDTH), F32),
        compiler_params=pltpu.CompilerParams(dimension_semantics=("parallel", "parallel", "arbitrary"),
                                             vmem_limit_bytes=VMEM_LIMIT),
        name="fox_attention",
    )(q, k, vt, eq, ek)


def _s5_kernel(u_ref, bw_ref, cw_ref, ar_ref, ai_ref, d_ref, wglu_ref, bglu_ref, g_ref, o_ref,
               state_ref, bu_ref, *, tc, bsz):
    @pl.when(pl.program_id(0) == 0)
    def _():
        state_ref[...] = jnp.zeros_like(state_ref)

    rows = tc * bsz
    ns = SLAB_STATE
    u = u_ref[...].reshape(rows, SSM_WIDTH)
    ub = u
    for i in range(S5_SLABS):
        bu_ref[i] = jnp.dot(ub[:, SLAB_CH * i:SLAB_CH * (i + 1)], bw_ref[i], preferred_element_type=F32)
    for i in range(S5_SLABS):
        ar = jnp.broadcast_to(ar_ref[i], (bsz, ns))
        ai = jnp.broadcast_to(ai_ref[i], (bsz, ns))
        xr = state_ref[i, :, 0:ns]
        xi = state_ref[i, :, ns:2 * ns]
        for t in range(tc):
            r0 = t * bsz
            nr = ar * xr - ai * xi + bu_ref[i, r0:r0 + bsz, 0:ns]
            ni = ar * xi + ai * xr + bu_ref[i, r0:r0 + bsz, ns:2 * ns]
            bu_ref[i, r0:r0 + bsz, 0:ns] = nr
            bu_ref[i, r0:r0 + bsz, ns:2 * ns] = ni
            xr, xi = nr, ni
        state_ref[i, :, 0:ns] = xr
        state_ref[i, :, ns:2 * ns] = xi
    ys = [jnp.dot(bu_ref[i].astype(BF16), cw_ref[i], preferred_element_type=F32) for i in range(S5_SLABS)]
    y = jnp.concatenate(ys, axis=1) + d_ref[...] * u.astype(F32)
    y = 0.5 * y * (1.0 + jnp.tanh(math.sqrt(2.0 / math.pi) * (y + 0.044715 * (y * y * y))))
    z = jnp.dot(y.astype(BF16), wglu_ref[...], preferred_element_type=F32) + bglu_ref[...]
    y = y * (1.0 / (1.0 + jnp.exp(-z)))
    y = y * lax.rsqrt(jnp.mean(y * y, axis=-1, keepdims=True) + RMS_EPS) * g_ref[...]
    o_ref[...] = y.astype(BF16).reshape(tc, bsz, SSM_WIDTH)


def _s5_weights(lam_re, lam_im, log_step, b_re, b_im, c_re, c_im):
    delta = jnp.exp(log_step)[:, None]
    mag = jnp.exp(lam_re * delta)
    ar = mag * jnp.cos(lam_im * delta)
    ai = mag * jnp.sin(lam_im * delta)
    den = lam_re * lam_re + lam_im * lam_im
    num_re = ar - 1.0
    coef_re = (num_re * lam_re + ai * lam_im) / den
    coef_im = (ai * lam_re - num_re * lam_im) / den
    bbar_re = coef_re[..., None] * b_re - coef_im[..., None] * b_im
    bbar_im = coef_re[..., None] * b_im + coef_im[..., None] * b_re
    gs = SSM_GROUPS // S5_SLABS
    eye = jnp.eye(gs, dtype=F32)

    def b_slab(bb):
        bb = bb.reshape(S5_SLABS, gs, SSM_STATE, SSM_GROUP)
        w = jnp.einsum('ab,sapc->sacbp', eye, bb)
        return w.reshape(S5_SLABS, gs * SSM_GROUP, gs * SSM_STATE)

    def c_slab(cc):
        cc = cc.reshape(S5_SLABS, gs, SSM_GROUP, SSM_STATE)
        w = jnp.einsum('ab,sacp->sbpac', eye, cc)
        return w.reshape(S5_SLABS, gs * SSM_STATE, gs * SSM_GROUP)

    bw = jnp.concatenate([b_slab(bbar_re), b_slab(bbar_im)], axis=2).astype(BF16)
    cw = jnp.concatenate([c_slab(c_re), -c_slab(c_im)], axis=1).astype(BF16)
    ar_s = ar.reshape(S5_SLABS, 1, SLAB_STATE)
    ai_s = ai.reshape(S5_SLABS, 1, SLAB_STATE)
    return bw, cw, ar_s, ai_s


def _s5(u_tb, bw, cw, ar_s, ai_s, d_row, w_glu, b_glu_row, g_row):
    seq, bsz, _ = u_tb.shape
    tc = S5_TIME_BLOCK
    assert seq % tc == 0 and bsz % 16 == 0
    full = lambda shape: pl.BlockSpec(shape, lambda t: (0,) * len(shape))
    return pl.pallas_call(
        functools.partial(_s5_kernel, tc=tc, bsz=bsz),
        grid=(seq // tc,),
        in_specs=[pl.BlockSpec((tc, bsz, SSM_WIDTH), lambda t: (t, 0, 0)),
                  full(bw.shape), full(cw.shape), full(ar_s.shape), full(ai_s.shape),
                  full((1, SSM_WIDTH)), full((SSM_WIDTH, SSM_WIDTH)), full((1, SSM_WIDTH)),
                  full((1, SSM_WIDTH))],
        out_specs=pl.BlockSpec((tc, bsz, SSM_WIDTH), lambda t: (t, 0, 0)),
        out_shape=jax.ShapeDtypeStruct((seq, bsz, SSM_WIDTH), BF16),
        scratch_shapes=[pltpu.VMEM((S5_SLABS, bsz, 2 * SLAB_STATE), F32),
                        pltpu.VMEM((S5_SLABS, tc * bsz, 2 * SLAB_STATE), F32)],
        compiler_params=pltpu.CompilerParams(dimension_semantics=("arbitrary",),
                                             vmem_limit_bytes=VMEM_LIMIT),
        name="s5_branch",
    )(u_tb, bw, cw, ar_s, ai_s, d_row, w_glu, b_glu_row, g_row)


def _layer_norm_rows(v, g, b):
    mu = jnp.mean(v, axis=-1, keepdims=True)
    c = v - mu
    var = jnp.mean(c * c, axis=-1, keepdims=True)
    return c * lax.rsqrt(var + LN_EPS) * g + b


def _outproj_kernel(attn_ref, ssm_ref, x_ref, wo_ref, ag_ref, g1_ref, b1_ref, wrt_ref, brt_ref, triu_ref,
                    h_ref, eid_ref, gate_ref, rank_ref, cnt_ref):
    a = attn_ref[...]
    a = a * lax.rsqrt(jnp.mean(a * a, axis=-1, keepdims=True) + RMS_EPS) * ag_ref[...]
    mix = (jnp.dot(a.astype(BF16), wo_ref[0:ATTN_WIDTH, :], preferred_element_type=F32)
           + jnp.dot(ssm_ref[...], wo_ref[ATTN_WIDTH:, :], preferred_element_type=F32))
    h = _layer_norm_rows(DEEPNORM_ALPHA * x_ref[...] + mix, g1_ref[...], b1_ref[...])
    h_ref[...] = h

    logits = lax.dot_general(wrt_ref[...], h.astype(BF16), (((1,), (1,)), ((), ())),
                             preferred_element_type=F32)[:ROUTER_ROWS] + brt_ref[...]
    tm = logits.shape[1]
    row = lax.broadcasted_iota(jnp.int32, (ROUTER_ROWS, tm), 0).astype(F32)
    no_row = float(LANES)
    is_g = row < N_EXPERT_GROUPS
    gl = jnp.where(is_g, logits, NEG_BIG)
    gmax = jnp.max(gl, axis=0, keepdims=True)
    g_idx = jnp.min(jnp.where(is_g & (gl == gmax), row, no_row), axis=0, keepdims=True)
    g_p = 1.0 / jnp.sum(jnp.where(is_g, jnp.exp(gl - gmax), 0.0), axis=0, keepdims=True)
    first = N_EXPERT_GROUPS + g_idx * EXPERTS_PER_GROUP
    in_grp = (row >= first) & (row < first + EXPERTS_PER_GROUP)
    el = jnp.where(in_grp, logits, NEG_BIG)
    m1 = jnp.max(el, axis=0, keepdims=True)
    i1 = jnp.min(jnp.where(in_grp & (el == m1), row, no_row), axis=0, keepdims=True)
    rest = in_grp & (row != i1)
    el2 = jnp.where(rest, logits, NEG_BIG)
    m2 = jnp.max(el2, axis=0, keepdims=True)
    i2 = jnp.min(jnp.where(rest & (el2 == m2), row, no_row), axis=0, keepdims=True)
    r = jnp.exp(m2 - m1)
    p1 = 1.0 / (1.0 + r)
    p2 = r / (1.0 + r)
    k2 = lax.broadcasted_iota(jnp.int32, (2, tm), 0)
    eid_ref[...] = (jnp.where(k2 == 0, i1, i2) - N_EXPERT_GROUPS).astype(jnp.int32)
    gate_ref[...] = g_p * jnp.where(k2 == 0, p1, p2)

    @pl.when(pl.program_id(0) == 0)
    def _():
        cnt_ref[...] = jnp.zeros_like(cnt_ref)
    hit1 = row == i1
    hit2 = row == i2
    onehot = jnp.where(hit1 | hit2, 1.0, 0.0)
    cnt = cnt_ref[...]
    before = jnp.dot(onehot.astype(BF16), triu_ref[...], preferred_element_type=F32) + cnt
    r1 = jnp.sum(jnp.where(hit1, before, 0.0), axis=0, keepdims=True)
    r2 = jnp.sum(jnp.where(hit2, before, 0.0), axis=0, keepdims=True)
    rank_ref[...] = jnp.where(k2 == 0, r1, r2).astype(jnp.int32)
    cnt_ref[...] = cnt + jnp.sum(onehot, axis=1, keepdims=True)


def _outproj(attn2, ssm2, x2, wo, ag_row, g1_row, b1_row, wrt, brt):
    n_tok = x2.shape[0]
    tm = ROW_BLOCK
    triu = jnp.triu(jnp.ones((tm, tm), BF16), 1)
    row_spec = lambda w: pl.BlockSpec((tm, w), lambda i: (i, 0))
    col_spec = lambda r: pl.BlockSpec((r, tm), lambda i: (0, i))
    full = lambda shape: pl.BlockSpec(shape, lambda i: (0,) * len(shape))
    return pl.pallas_call(
        _outproj_kernel,
        grid=(n_tok // tm,),
        in_specs=[row_spec(ATTN_WIDTH), row_spec(SSM_WIDTH), row_spec(D_MODEL),
                  full((D_MODEL, D_MODEL)), full((1, ATTN_WIDTH)), full((1, D_MODEL)), full((1, D_MODEL)),
                  full((LANES, D_MODEL)), full((ROUTER_ROWS, 1)), full((tm, tm))],
        out_specs=[row_spec(D_MODEL), col_spec(2), col_spec(2), col_spec(2), full((ROUTER_ROWS, 1))],
        out_shape=[jax.ShapeDtypeStruct((n_tok, D_MODEL), F32),
                   jax.ShapeDtypeStruct((2, n_tok), jnp.int32),
                   jax.ShapeDtypeStruct((2, n_tok), F32),
                   jax.ShapeDtypeStruct((2, n_tok), jnp.int32),
                   jax.ShapeDtypeStruct((ROUTER_ROWS, 1), F32)],
        compiler_params=pltpu.CompilerParams(dimension_semantics=("arbitrary",),
                                             vmem_limit_bytes=VMEM_LIMIT),
        name="outproj_ln_router",
    )(attn2, ssm2, x2, wo, ag_row, g1_row, b1_row, wrt, brt, triu)


def _dispatch_plan(eid, rank, cnt_col, n_tok):
    counts = cnt_col[N_EXPERT_GROUPS:N_EXPERT_GROUPS + N_EXPERTS, 0].astype(jnp.int32)
    padded = ((counts + MOE_BLOCK - 1) // MOE_BLOCK) * MOE_BLOCK
    pends = jnp.cumsum(padded)
    pstarts = pends - padded
    n_pad = ((2 * n_tok + N_EXPERTS * (MOE_BLOCK - 1) + MOE_BLOCK - 1) // MOE_BLOCK) * MOE_BLOCK
    n_blocks = n_pad // MOE_BLOCK
    blk0 = jnp.arange(n_blocks, dtype=jnp.int32) * MOE_BLOCK
    block_e = jnp.minimum(jnp.sum((pends[None, :] <= blk0[:, None]).astype(jnp.int32), axis=1), N_EXPERTS - 1)
    n_used = (pends[-1] // MOE_BLOCK).astype(jnp.int32).reshape(1)
    tail = jnp.where(padded > 0, pends - MOE_BLOCK, -1).astype(jnp.int32)

    tb = TOKEN_BLOCK
    nb = n_tok // tb
    onehot = eid[:, :, None] == jnp.arange(N_EXPERTS, dtype=jnp.int32)[None, None, :]
    run_len = jnp.sum(onehot.reshape(2, nb, tb, N_EXPERTS).astype(jnp.int32), axis=(0, 2))
    before = jnp.cumsum(run_len, axis=0) - run_len
    run_src = jnp.cumsum(run_len, axis=1) - run_len
    run_dst = pstarts[None, :] + before
    shift = jnp.broadcast_to((run_src - before)[:, None, :], (nb, tb, N_EXPERTS)).reshape(n_tok, N_EXPERTS)
    local = rank + jnp.sum(jnp.where(onehot, shift[None], 0), axis=-1)
    any_long = (jnp.max(run_len, axis=1) >= (1 << SHORT_RUN_BITS)).astype(jnp.int32)
    runs = (run_len.reshape(-1), run_src.reshape(-1), run_dst.reshape(-1), any_long)
    return local, runs, block_e, n_used, tail, n_blocks


ROW_SUB = D_MODEL // LANES


def _rows(start, size):
    return pl.ds(pl.multiple_of(start * ROW_SUB, ROW_SUB), size * ROW_SUB)


def _store_row_tiles(ref, row0, v):
    n = v.shape[0]
    for s in range(ROW_SUB):
        ref[pl.ds(row0 * ROW_SUB + s, n, stride=ROW_SUB), :] = v[:, LANES * s:LANES * (s + 1)]


def _load_row_tiles(ref, row0, n):
    return jnp.concatenate([ref[pl.ds(row0 * ROW_SUB + s, n, stride=ROW_SUB), :] for s in range(ROW_SUB)], axis=1)


def _for_each_run_piece(blk, len_ref, src_ref, dst_ref, bits, make_copy, act):
    for e in range(N_EXPERTS):
        idx = blk * N_EXPERTS + e
        n = len_ref[idx]
        src = src_ref[idx]
        dst = dst_ref[idx]
        for b in reversed(range(bits)):
            @pl.when((n & (1 << b)) != 0)
            def _():
                off = lax.shift_left(lax.shift_right_logical(n, b + 1), b + 1)
                act(make_copy(src + off, dst + off, 1 << b))


def _by_run_size(long_ref, blk, body):
    @pl.when(long_ref[blk] == 0)
    def _():
        body(SHORT_RUN_BITS)

    @pl.when(long_ref[blk] != 0)
    def _():
        body(RUN_BITS)


def _dispatch_kernel(tail_ref, nu_ref, len_ref, src_ref, dst_ref, long_ref, local_ref, h_ref, xb_hbm, zbuf,
                     pk_ref, sem_z, sem, *, n_blocks):
    i = pl.program_id(0)
    nb = pl.num_programs(0) - 1
    tb = h_ref.shape[0]
    cur = i % 2

    @pl.when(i == 0)
    def _():
        zbuf[...] = jnp.zeros_like(zbuf)

        def zero_copy(row0):
            return pltpu.make_async_copy(zbuf, xb_hbm.at[_rows(row0, MOE_BLOCK)], sem_z)

        def each_zero_block(fn):
            for e in range(N_EXPERTS):
                @pl.when(tail_ref[e] >= 0)
                def _():
                    fn(zero_copy(tail_ref[e]))

            def unused(b, carry):
                fn(zero_copy(b * MOE_BLOCK))
                return carry
            lax.fori_loop(nu_ref[0], n_blocks, unused, 0)

        each_zero_block(lambda c: c.start())
        each_zero_block(lambda c: c.wait())

    def drain(which):
        pltpu.make_async_copy(pk_ref.at[which], xb_hbm.at[_rows(0, 2 * tb)], sem.at[which]).wait()

    @pl.when(i >= 2)
    def _():
        drain(cur)

    def group():
        pos = lax.broadcasted_iota(jnp.int32, (2 * tb, tb), 0)
        hit = (pos == local_ref[0:1, :]) | (pos == local_ref[1:2, :])
        perm = jnp.where(hit, 1.0, 0.0).astype(BF16)
        _store_row_tiles(pk_ref.at[cur], 0, jnp.dot(perm, h_ref[...].astype(BF16), preferred_element_type=F32))

    def send(bits):
        def run_copy(src, dst, size):
            return pltpu.make_async_copy(pk_ref.at[1 - cur, _rows(src, size)], xb_hbm.at[_rows(dst, size)],
                                         sem.at[1 - cur])
        _for_each_run_piece(i - 1, len_ref, src_ref, dst_ref, bits, run_copy, lambda c: c.start())

    @pl.when(i == 0)
    def _():
        group()

    @pl.when((i > 0) & (i < nb))
    def _():
        def send_and_group(bits):
            send(bits)
            group()
        _by_run_size(long_ref, i - 1, send_and_group)

    @pl.when(i == nb)
    def _():
        send(RUN_BITS)
        drain(1 - cur)


def _dispatch(h, local, runs, tail, n_used, n_blocks):
    n_tok = h.shape[0]
    tb = TOKEN_BLOCK
    nb = n_tok // tb
    grid_spec = pltpu.PrefetchScalarGridSpec(
        num_scalar_prefetch=6,
        grid=(nb + 1,),
        in_specs=[pl.BlockSpec((2, tb), lambda i, *_: (0, jnp.minimum(i, nb - 1))),
                  pl.BlockSpec((tb, D_MODEL), lambda i, *_: (jnp.minimum(i, nb - 1), 0))],
        out_specs=pl.BlockSpec(memory_space=pl.ANY),
        scratch_shapes=[pltpu.VMEM((MOE_BLOCK * ROW_SUB, LANES), F32),
                        pltpu.VMEM((2, 2 * tb * ROW_SUB, LANES), F32),
                        pltpu.SemaphoreType.DMA, pltpu.SemaphoreType.DMA((2,))],
    )
    return pl.pallas_call(
        functools.partial(_dispatch_kernel, n_blocks=n_blocks),
        grid_spec=grid_spec,
        out_shape=jax.ShapeDtypeStruct((n_blocks * MOE_BLOCK * ROW_SUB, LANES), F32),
        compiler_params=pltpu.CompilerParams(dimension_semantics=("arbitrary",),
                                             vmem_limit_bytes=VMEM_LIMIT),
        name="moe_dispatch",
    )(tail, n_used, *runs, local, h)


def _expert_kernel(be_ref, nu_ref, x_ref, wg_ref, wu_ref, wd_ref, y_ref, wgu_t, wd_t, yt_ref):
    i = pl.program_id(0)
    used = i < nu_ref[0]

    @pl.when(used & ((i == 0) | (be_ref[i] != be_ref[jnp.maximum(i - 1, 0)])))
    def _():
        wgu_t[0:D_EXPERT, :] = wg_ref[...].astype(BF16).T
        wgu_t[D_EXPERT:, :] = wu_ref[...].astype(BF16).T
        wd_t[...] = wd_ref[...].astype(BF16).T

    @pl.when(used)
    def _():
        sub = range(MOE_BLOCK // MOE_SUB)
        gus = []
        for c in sub:
            xb = _load_row_tiles(x_ref, c * MOE_SUB, MOE_SUB).astype(BF16)
            gus.append(lax.dot_general(wgu_t[...], xb, (((1,), (1,)), ((), ())),
                                       preferred_element_type=F32))
        hids = []
        for c in sub:
            gt = gus[c][0:D_EXPERT]
            hids.append((gt * (1.0 / (1.0 + jnp.exp(-gt))) * gus[c][D_EXPERT:]).astype(BF16))
        for c in sub:
            yt_ref[c] = jnp.dot(wd_t[...], hids[c], preferred_element_type=F32)
        for c in sub:
            _store_row_tiles(y_ref, c * MOE_SUB, yt_ref[c].astype(BF16).T.astype(F32))

    @pl.when(jnp.logical_not(used))
    def _():
        y_ref[...] = jnp.zeros_like(y_ref)


def _experts(xb, block_e, n_used, wg, wu, wd):
    n_blocks = xb.shape[0] // (MOE_BLOCK * ROW_SUB)
    last = lambda i, nu: jnp.minimum(i, nu[0] - 1)
    grid_spec = pltpu.PrefetchScalarGridSpec(
        num_scalar_prefetch=2,
        grid=(n_blocks,),
        in_specs=[pl.BlockSpec((MOE_BLOCK * ROW_SUB, LANES), lambda i, be, nu: (last(i, nu), 0)),
                  pl.BlockSpec((None, D_MODEL, D_EXPERT), lambda i, be, nu: (be[last(i, nu)], 0, 0)),
                  pl.BlockSpec((None, D_MODEL, D_EXPERT), lambda i, be, nu: (be[last(i, nu)], 0, 0)),
                  pl.BlockSpec((None, D_EXPERT, D_MODEL), lambda i, be, nu: (be[last(i, nu)], 0, 0))],
        out_specs=pl.BlockSpec((MOE_BLOCK * ROW_SUB, LANES), lambda i, be, nu: (i, 0)),
        scratch_shapes=[pltpu.VMEM((2 * D_EXPERT, D_MODEL), BF16), pltpu.VMEM((D_MODEL, D_EXPERT), BF16),
                        pltpu.VMEM((MOE_BLOCK // MOE_SUB, D_MODEL, MOE_SUB), F32)],
    )
    return pl.pallas_call(
        _expert_kernel,
        grid_spec=grid_spec,
        out_shape=jax.ShapeDtypeStruct(xb.shape, F32),
        compiler_params=pltpu.CompilerParams(dimension_semantics=("arbitrary",),
                                             vmem_limit_bytes=VMEM_LIMIT),
        name="moe_experts",
    )(block_e, n_used, xb, wg, wu, wd)


def _combine_kernel(len_ref, src_ref, dst_ref, long_ref, local_ref, gate_ref, h_ref, g2_ref, b2_ref, yb_hbm,
                    o_ref, buf, sem):
    i = pl.program_id(0)
    n = pl.num_programs(0)
    tb = h_ref.shape[0]
    cur = i % 2

    def fetch(blk, which, bits):
        def run_copy(src, dst, size):
            return pltpu.make_async_copy(yb_hbm.at[_rows(dst, size)], buf.at[which, _rows(src, size)], sem.at[which])
        _for_each_run_piece(blk, len_ref, src_ref, dst_ref, bits, run_copy, lambda c: c.start())

    def drain(which):
        pltpu.make_async_copy(yb_hbm.at[_rows(0, 2 * tb)], buf.at[which], sem.at[which]).wait()

    @pl.when(i == 0)
    def _():
        fetch(i, 0, RUN_BITS)

    drain(cur)
    nxt = jnp.minimum(i + 1, n - 1)

    def fetch_next_and_combine(bits):
        fetch(nxt, 1 - cur, bits)
        y = _load_row_tiles(buf.at[cur], 0, 2 * tb).astype(BF16)
        pos = lax.broadcasted_iota(jnp.int32, (tb, 2 * tb), 1)
        local = local_ref[...]
        gate = gate_ref[...]
        mix = (jnp.where(pos == local[:, 0:1], gate[:, 0:1], 0.0)
               + jnp.where(pos == local[:, 1:2], gate[:, 1:2], 0.0))
        mix_hi = mix.astype(BF16)
        mix_lo = (mix - mix_hi.astype(F32)).astype(BF16)
        moe = (jnp.dot(mix_hi, y, preferred_element_type=F32) + jnp.dot(mix_lo, y, preferred_element_type=F32))
        o_ref[...] = _layer_norm_rows(DEEPNORM_ALPHA * h_ref[...] + moe, g2_ref[...], b2_ref[...])
    _by_run_size(long_ref, nxt, fetch_next_and_combine)

    @pl.when(i == n - 1)
    def _():
        drain(1 - cur)


def _combine(h, yb, local_t, runs, gate, g2_row, b2_row):
    n_tok = h.shape[0]
    tb = TOKEN_BLOCK
    row_spec = lambda w: pl.BlockSpec((tb, w), lambda i, *_: (i, 0))
    full = lambda shape: pl.BlockSpec(shape, lambda i, *_: (0,) * len(shape))
    grid_spec = pltpu.PrefetchScalarGridSpec(
        num_scalar_prefetch=4,
        grid=(n_tok // tb,),
        in_specs=[row_spec(2), row_spec(2), row_spec(D_MODEL), full((1, D_MODEL)), full((1, D_MODEL)),
                  pl.BlockSpec(memory_space=pl.ANY)],
        out_specs=row_spec(D_MODEL),
        scratch_shapes=[pltpu.VMEM((2, 2 * tb * ROW_SUB, LANES), F32), pltpu.SemaphoreType.DMA((2,))],
    )
    return pl.pallas_call(
        _combine_kernel,
        grid_spec=grid_spec,
        out_shape=jax.ShapeDtypeStruct((n_tok, D_MODEL), F32),
        compiler_params=pltpu.CompilerParams(dimension_semantics=("arbitrary",),
                                             vmem_limit_bytes=VMEM_LIMIT),
        name="combine_ln",
    )(*runs, local_t, gate, h, g2_row, b2_row, yb)


def kernel(x, w_in, b_fgate, s5_lambda_re, s5_lambda_im, s5_log_step, s5_b_re, s5_b_im, s5_c_re, s5_c_im, s5_d, w_glu, b_glu, attn_norm_g, ssm_norm_g, w_out, ln1_g, ln1_b, w_router_group, b_router_group, w_router_expert, b_router_expert, w_gate, w_up, w_down, ln2_g, ln2_b):
    bsz, seq, d = x.shape
    n_tok = bsz * seq
    a = ATTN_WIDTH
    for l in range(DEPTH):
        x2 = x.reshape(n_tok, d)
        w = w_in[l]
        f0 = 3 * a
        w_cat = jnp.concatenate(
            [w[:, :a] * (HEAD_DIM ** -0.5 * LOG2E), w[:, a:2 * a], w[:, f0 + ATTN_HEADS:]], axis=1).astype(BF16)
        wvt = jnp.pad(w[:, 2 * a:f0 + ATTN_HEADS].T, ((0, GATE_ROWS - ATTN_HEADS), (0, 0))).astype(BF16)
        q, k, vt, u, eq, ek = _inproj(x2, w_cat, wvt, b_fgate[l].reshape(ATTN_HEADS, 1), seq)
        attn = _attention(q.reshape(bsz, seq, a), k.reshape(bsz, seq, a), vt,
                          eq.reshape(bsz, seq, EXT_W), ek.reshape(bsz, seq, EXT_W))

        bw, cw, ar_s, ai_s = _s5_weights(s5_lambda_re[l], s5_lambda_im[l], s5_log_step[l],
                                         s5_b_re[l], s5_b_im[l], s5_c_re[l], s5_c_im[l])
        u_tb = u.reshape(bsz, seq, SSM_WIDTH).transpose(1, 0, 2)
        ssm_tb = _s5(u_tb, bw, cw, ar_s, ai_s, s5_d[l].reshape(1, SSM_WIDTH), w_glu[l].astype(BF16),
                     b_glu[l].reshape(1, SSM_WIDTH), ssm_norm_g[l].reshape(1, SSM_WIDTH))
        ssm2 = ssm_tb.transpose(1, 0, 2).reshape(n_tok, SSM_WIDTH)

        n_route = N_EXPERT_GROUPS + N_EXPERTS
        wrt = jnp.pad(jnp.concatenate([w_router_group[l], w_router_expert[l]], axis=1).T,
                      ((0, LANES - n_route), (0, 0))).astype(BF16)
        brt = jnp.pad(jnp.concatenate([b_router_group[l], b_router_expert[l]]),
                      (0, ROUTER_ROWS - n_route)).reshape(ROUTER_ROWS, 1)
        h, eid, gate, rank, cnt_col = _outproj(
            attn.reshape(n_tok, a), ssm2, x2, w_out[l].astype(BF16), attn_norm_g[l].reshape(1, a),
            ln1_g[l].reshape(1, d), ln1_b[l].reshape(1, d), wrt, brt)

        local, runs, block_e, n_used, tail, n_blocks = _dispatch_plan(eid, rank, cnt_col, n_tok)
        xb = _dispatch(h, local, runs, tail, n_used, n_blocks)
        yb = _experts(xb, block_e, n_used, w_gate[l], w_up[l], w_down[l])
        x = _combine(h, yb, local.T, runs, gate.T, ln2_g[l].reshape(1, d),
                     ln2_b[l].reshape(1, d)).reshape(bsz, seq, d)
    return x
```

```python
import functools
import math

import jax
import jax.numpy as jnp
import numpy as np
from jax import lax
from jax.experimental import pallas as pl
from jax.experimental.pallas import tpu as pltpu

F32 = jnp.float32
BF16 = jnp.bfloat16

D_MODEL = 1024
HEAD_DIM = 64
ATTN_WIDTH = 512
ATTN_HEADS = 8
SSM_WIDTH = 512
SSM_GROUP = 16
SSM_GROUPS = 32
SSM_STATE = 64
N_EXPERT_GROUPS = 4
EXPERTS_PER_GROUP = 8
N_EXPERTS = 32
D_EXPERT = 512
MOE_BLOCK = 512
MOE_SUB = 256
TOKEN_BLOCK = 512
RUN_BITS = TOKEN_BLOCK.bit_length()
SHORT_RUN_BITS = 6
DEPTH = 1
DEEPNORM_ALPHA = (2.0 * DEPTH) ** 0.25
LN_EPS = 1e-5
RMS_EPS = 1e-6

LANES = 128
NEG_BIG = -1e30
LOG2E = math.log2(math.e)
ONES_ROWS = 16
VT_ROWS = ATTN_HEADS * (HEAD_DIM + ONES_ROWS)
GATE_ROWS = 16
PAIR_BIAS_LANES = 8
EXT_W = ATTN_HEADS // 2 * LANES
VMEM_LIMIT = 48 * 1024 * 1024

ROW_BLOCK = 1024
ATTN_BLOCK = 256
ATTN_HEADS_PER_STEP = 8
S5_TIME_BLOCK = 64
S5_SLABS = 4
ROUTER_ROWS = 40
SLAB_CH = SSM_WIDTH // S5_SLABS
SLAB_STATE = SSM_GROUPS // S5_SLABS * SSM_STATE


def _split3(c):
    hi = c.astype(BF16)
    r1 = c - hi.astype(F32)
    mid = r1.astype(BF16)
    lo = (r1 - mid.astype(F32)).astype(BF16)
    return hi, mid, lo


def _inproj_kernel(x_ref, w_ref, wvt_ref, bf_ref, triu_ref, sel_ref, q_ref, k_ref, vt_ref, u_ref, eq_ref, ek_ref,
                   carry_ref, *, blocks_per_seq):
    i = pl.program_id(0)

    @pl.when(i % blocks_per_seq == 0)
    def _():
        carry_ref[...] = jnp.zeros_like(carry_ref)

    tm = x_ref.shape[0]
    xb = x_ref[...].astype(BF16)
    z = jnp.dot(xb, w_ref[...], preferred_element_type=F32)
    a = ATTN_WIDTH
    q_ref[...] = z[:, :a].astype(BF16)
    k_ref[...] = z[:, a:2 * a].astype(BF16)
    u_ref[...] = z[:, 2 * a:2 * a + SSM_WIDTH].astype(BF16)
    vtf = lax.dot_general(wvt_ref[...], xb, (((1,), (1,)), ((), ())), preferred_element_type=F32)
    vt = vtf[:ATTN_WIDTH].astype(BF16)
    ones_rows = jnp.ones((ONES_ROWS, tm), BF16)
    vt = jnp.concatenate([piece for h in range(ATTN_HEADS)
                          for piece in (vt[HEAD_DIM * h:HEAD_DIM * (h + 1)], ones_rows)], axis=0)
    for jj in range(tm // ATTN_BLOCK):
        vt_ref[jj] = vt[:, jj * ATTN_BLOCK:(jj + 1) * ATTN_BLOCK]

    f = vtf[ATTN_WIDTH:ATTN_WIDTH + ATTN_HEADS] + bf_ref[...]
    logf = jnp.minimum(f, 0.0) - jnp.log1p(jnp.exp(-jnp.abs(f)))
    parts = jnp.dot(jnp.concatenate(_split3(logf), axis=0), triu_ref[...], preferred_element_type=F32)
    nh = ATTN_HEADS
    cum = parts[0:nh] + parts[nh:2 * nh] + parts[2 * nh:3 * nh] + carry_ref[...]
    carry_ref[...] = cum[:, tm - 1:tm]
    hi, mid, lo = (t.astype(F32) for t in _split3(cum * LOG2E))
    one_row = jnp.where(lax.broadcasted_iota(jnp.int32, (nh, tm), 0) == 0, 1.0, 0.0)
    packed_t = jnp.concatenate([hi, mid, lo, one_row, jnp.zeros((LANES - 4 * nh, tm), F32)], axis=0)
    ext = jnp.dot(packed_t.T.astype(BF16), sel_ref[...], preferred_element_type=F32).astype(BF16)
    eq_ref[...] = ext[:, :EXT_W]
    ek_ref[...] = ext[:, EXT_W:]


def _bias_lane_selector():
    sel = np.zeros((LANES, 2, ATTN_HEADS // 2, LANES), np.float32)
    one = 3 * ATTN_HEADS
    for h in range(ATTN_HEADS):
        pair, base = h // 2, PAIR_BIAS_LANES * (h % 2)
        for t in range(3):
            sel[ATTN_HEADS * t + h, 0, pair, base + t] = 1.0
            sel[one, 0, pair, base + 3 + t] = 1.0
            sel[one, 1, pair, base + t] = 1.0
            sel[ATTN_HEADS * t + h, 1, pair, base + 3 + t] = -1.0
    return jnp.asarray(sel.reshape(LANES, 2 * EXT_W), BF16)


def _inproj(x2, w_cat, wvt, bf_col, seq):
    n_tok = x2.shape[0]
    tm = ROW_BLOCK
    assert seq % tm == 0 and n_tok % tm == 0 and tm % ATTN_BLOCK == 0
    bsz = n_tok // seq
    bps = seq // tm
    ncol = w_cat.shape[1]
    vblk = tm // ATTN_BLOCK
    row_spec = lambda w: pl.BlockSpec((tm, w), lambda i: (i, 0))
    return pl.pallas_call(
        functools.partial(_inproj_kernel, blocks_per_seq=bps),
        grid=(n_tok // tm,),
        in_specs=[row_spec(D_MODEL),
                  pl.BlockSpec((D_MODEL, ncol), lambda i: (0, 0)),
                  pl.BlockSpec((ATTN_WIDTH + GATE_ROWS, D_MODEL), lambda i: (0, 0)),
                  pl.BlockSpec((ATTN_HEADS, 1), lambda i: (0, 0)),
                  pl.BlockSpec((tm, tm), lambda i: (0, 0)),
                  pl.BlockSpec((LANES, 2 * EXT_W), lambda i: (0, 0))],
        out_specs=[row_spec(ATTN_WIDTH), row_spec(ATTN_WIDTH),
                   pl.BlockSpec((None, vblk, VT_ROWS, ATTN_BLOCK), lambda i: (i // bps, i % bps, 0, 0)),
                   row_spec(SSM_WIDTH), row_spec(EXT_W), row_spec(EXT_W)],
        out_shape=[jax.ShapeDtypeStruct((n_tok, ATTN_WIDTH), BF16)] * 2
                  + [jax.ShapeDtypeStruct((bsz, seq // ATTN_BLOCK, VT_ROWS, ATTN_BLOCK), BF16),
                     jax.ShapeDtypeStruct((n_tok, SSM_WIDTH), BF16),
                     jax.ShapeDtypeStruct((n_tok, EXT_W), BF16),
                     jax.ShapeDtypeStruct((n_tok, EXT_W), BF16)],
        scratch_shapes=[pltpu.VMEM((ATTN_HEADS, 1), F32)],
        compiler_params=pltpu.CompilerParams(dimension_semantics=("arbitrary",),
                                             vmem_limit_bytes=VMEM_LIMIT),
        name="inproj",
    )(x2, w_cat, wvt, bf_col, jnp.triu(jnp.ones((tm, tm), BF16)), _bias_lane_selector())


def _attn_kernel(q_ref, k_ref, vt_ref, eq_ref, ek_ref, o_ref, *, tq, seq, nh):
    qi = pl.program_id(2)
    lane = lax.broadcasted_iota(jnp.int32, (tq, LANES), 1)
    q_augs = []
    for h in range(nh):
        qb = q_ref[:, LANES * (h // 2):LANES * (h // 2 + 1)]
        head_lanes = (lane < HEAD_DIM) if h % 2 == 0 else (lane >= HEAD_DIM)
        qh = jnp.where(head_lanes, qb, jnp.zeros((), BF16))
        own = (lane >= PAIR_BIAS_LANES * (h % 2)) & (lane < PAIR_BIAS_LANES * (h % 2 + 1))
        eqh = jnp.where(own, eq_ref[:, LANES * (h // 2):LANES * (h // 2 + 1)], jnp.zeros((), BF16))
        q_augs.append(jnp.concatenate([qh, eqh], axis=1))
    key_i = lax.broadcasted_iota(jnp.int32, (tq, tq), 0)
    qry_i = lax.broadcasted_iota(jnp.int32, (tq, tq), 1)
    causal = key_i <= qry_i
    vrows = HEAD_DIM + ONES_ROWS

    def scores_of(k0, nk):
        out = []
        for h in range(nh):
            kj = k_ref[pl.ds(k0, nk), LANES * (h // 2):LANES * (h // 2 + 1)]
            k_aug = jnp.concatenate([kj, ek_ref[pl.ds(k0, nk), LANES * (h // 2):LANES * (h // 2 + 1)]], axis=1)
            out.append(lax.dot_general(k_aug, q_augs[h], (((1,), (1,)), ((), ())), preferred_element_type=F32))
        return tuple(out)

    def absorb(j0, scores, carry):
        stats = []
        for h in range(nh):
            m = carry[h][0]
            m_new = jnp.maximum(m, jnp.max(scores[h], axis=0, keepdims=True))
            stats.append((m_new, jnp.exp2(m - m_new), jnp.exp2(scores[h] - m_new).astype(BF16)))
        new = []
        for h in range(nh):
            m_new, corr, p = stats[h]
            res = None
            for c in range(p.shape[0] // tq):
                part = jnp.dot(vt_ref[j0 + c, vrows * h:vrows * (h + 1), :], p[c * tq:(c + 1) * tq],
                               preferred_element_type=F32)
                res = part if res is None else res + part
            acc = corr * carry[h][2] + res[0:HEAD_DIM, :]
            l = corr * carry[h][1] + res[HEAD_DIM:HEAD_DIM + 1, :]
            new.append((m_new, l, acc))
        return tuple(new)

    def pair_step(jj, carry):
        return absorb(2 * jj, scores_of(pl.multiple_of(jj * (2 * tq), 2 * tq), 2 * tq), carry)

    def finish(carry):
        out_t = jnp.concatenate([c[2] / c[1] for c in carry], axis=0)
        o_ref[...] = out_t.T

    init1 = (jnp.full((1, tq), NEG_BIG, F32), jnp.zeros((1, tq), F32), jnp.zeros((HEAD_DIM, tq), F32))
    carry = lax.fori_loop(0, qi // 2, pair_step, (init1,) * nh)

    @pl.when(qi % 2 == 0)
    def _():
        diag = scores_of(pl.multiple_of(qi * tq, tq), tq)
        finish(absorb(qi, tuple(jnp.where(causal, s, NEG_BIG) for s in diag), carry))

    @pl.when(qi % 2 == 1)
    def _():
        last = scores_of(pl.multiple_of((qi - 1) * tq, 2 * tq), 2 * tq)
        causal2 = (lax.broadcasted_iota(jnp.int32, (2 * tq, tq), 0)
                   <= lax.broadcasted_iota(jnp.int32, (2 * tq, tq), 1) + tq)
        finish(absorb(qi - 1, tuple(jnp.where(causal2, s, NEG_BIG) for s in last), carry))


def _attention(q, k, vt, eq, ek):
    bsz, seq, _ = q.shape
    tq = ATTN_BLOCK
    nh = ATTN_HEADS_PER_STEP
    assert seq % tq == 0 and ATTN_HEADS % nh == 0 and nh % 2 == 0
    w = nh * HEAD_DIM
    return pl.pallas_call(
        functools.partial(_attn_kernel, tq=tq, seq=seq, nh=nh),
        grid=(bsz, ATTN_HEADS // nh, seq // tq),
        in_specs=[pl.BlockSpec((None, tq, w), lambda b, g, i: (b, i, g)),
                  pl.BlockSpec((None, seq, w), lambda b, g, i: (b, 0, g)),
                  pl.BlockSpec((None, seq // tq, nh * (HEAD_DIM + ONES_ROWS), tq), lambda b, g, i: (b, 0, g, 0)),
                  pl.BlockSpec((None, tq, nh // 2 * LANES), lambda b, g, i: (b, i, g)),
                  pl.BlockSpec((None, seq, nh // 2 * LANES), lambda b, g, i: (b, 0, g))],
        out_specs=pl.BlockSpec((None, tq, w), lambda b, g, i: (b, i, g)),
        out_shape=jax.ShapeDtypeStruct((bsz, seq, ATTN_WIDTH), F32),
        compiler_params=pltpu.CompilerParams(dimension_semantics=("parallel", "parallel", "arbitrary"),
                                             vmem_limit_bytes=VMEM_LIMIT),
        name="fox_attention",
    )(q, k, vt, eq, ek)


def _s5_kernel(u_ref, bw_ref, cw_ref, ar_ref, ai_ref, d_ref, wglu_ref, bglu_ref, g_ref, o_ref,
               state_ref, bu_ref, *, tc, bsz):
    @pl.when(pl.program_id(0) == 0)
    def _():
        state_ref[...] = jnp.zeros_like(state_ref)

    rows = tc * bsz
    ns = SLAB_STATE
    u = u_ref[...].reshape(rows, SSM_WIDTH)
    ub = u
    for i in range(S5_SLABS):
        bu_ref[i] = jnp.dot(ub[:, SLAB_CH * i:SLAB_CH * (i + 1)], bw_ref[i], preferred_element_type=F32)
    for i in range(S5_SLABS):
        ar = jnp.broadcast_to(ar_ref[i], (bsz, ns))
        ai = jnp.broadcast_to(ai_ref[i], (bsz, ns))
        xr = state_ref[i, :, 0:ns]
        xi = state_ref[i, :, ns:2 * ns]
        for t in range(tc):
            r0 = t * bsz
            nr = ar * xr - ai * xi + bu_ref[i, r0:r0 + bsz, 0:ns]
            ni = ar * xi + ai * xr + bu_ref[i, r0:r0 + bsz, ns:2 * ns]
            bu_ref[i, r0:r0 + bsz, 0:ns] = nr
            bu_ref[i, r0:r0 + bsz, ns:2 * ns] = ni
            xr, xi = nr, ni
        state_ref[i, :, 0:ns] = xr
        state_ref[i, :, ns:2 * ns] = xi
    ys = [jnp.dot(bu_ref[i].astype(BF16), cw_ref[i], preferred_element_type=F32) for i in range(S5_SLABS)]
    y = jnp.concatenate(ys, axis=1) + d_ref[...] * u.astype(F32)
    y = 0.5 * y * (1.0 + jnp.tanh(math.sqrt(2.0 / math.pi) * (y + 0.044715 * (y * y * y))))
    z = jnp.dot(y.astype(BF16), wglu_ref[...], preferred_element_type=F32) + bglu_ref[...]
    y = y * (1.0 / (1.0 + jnp.exp(-z)))
    y = y * lax.rsqrt(jnp.mean(y * y, axis=-1, keepdims=True) + RMS_EPS) * g_ref[...]
    o_ref[...] = y.astype(BF16).reshape(tc, bsz, SSM_WIDTH)


def _s5_weights(lam_re, lam_im, log_step, b_re, b_im, c_re, c_im):
    delta = jnp.exp(log_step)[:, None]
    mag = jnp.exp(lam_re * delta)
    ar = mag * jnp.cos(lam_im * delta)
    ai = mag * jnp.sin(lam_im * delta)
    den = lam_re * lam_re + lam_im * lam_im
    num_re = ar - 1.0
    coef_re = (num_re * lam_re + ai * lam_im) / den
    coef_im = (ai * lam_re - num_re * lam_im) / den
    bbar_re = coef_re[..., None] * b_re - coef_im[..., None] * b_im
    bbar_im = coef_re[..., None] * b_im + coef_im[..., None] * b_re
    gs = SSM_GROUPS // S5_SLABS
    eye = jnp.eye(gs, dtype=F32)

    def b_slab(bb):
        bb = bb.reshape(S5_SLABS, gs, SSM_STATE, SSM_GROUP)
        w = jnp.einsum('ab,sapc->sacbp', eye, bb)
        return w.reshape(S5_SLABS, gs * SSM_GROUP, gs * SSM_STATE)

    def c_slab(cc):
        cc = cc.reshape(S5_SLABS, gs, SSM_GROUP, SSM_STATE)
        w = jnp.einsum('ab,sacp->sbpac', eye, cc)
        return w.reshape(S5_SLABS, gs * SSM_STATE, gs * SSM_GROUP)

    bw = jnp.concatenate([b_slab(bbar_re), b_slab(bbar_im)], axis=2).astype(BF16)
    cw = jnp.concatenate([c_slab(c_re), -c_slab(c_im)], axis=1).astype(BF16)
    ar_s = ar.reshape(S5_SLABS, 1, SLAB_STATE)
    ai_s = ai.reshape(S5_SLABS, 1, SLAB_STATE)
    return bw, cw, ar_s, ai_s


def _s5(u_tb, bw, cw, ar_s, ai_s, d_row, w_glu, b_glu_row, g_row):
    seq, bsz, _ = u_tb.shape
    tc = S5_TIME_BLOCK
    assert seq % tc == 0 and bsz % 16 == 0
    full = lambda shape: pl.BlockSpec(shape, lambda t: (0,) * len(shape))
    return pl.pallas_call(
        functools.partial(_s5_kernel, tc=tc, bsz=bsz),
        grid=(seq // tc,),
        in_specs=[pl.BlockSpec((tc, bsz, SSM_WIDTH), lambda t: (t, 0, 0)),
                  full(bw.shape), full(cw.shape), full(ar_s.shape), full(ai_s.shape),
                  full((1, SSM_WIDTH)), full((SSM_WIDTH, SSM_WIDTH)), full((1, SSM_WIDTH)),
                  full((1, SSM_WIDTH))],
        out_specs=pl.BlockSpec((tc, bsz, SSM_WIDTH), lambda t: (t, 0, 0)),
        out_shape=jax.ShapeDtypeStruct((seq, bsz, SSM_WIDTH), BF16),
        scratch_shapes=[pltpu.VMEM((S5_SLABS, bsz, 2 * SLAB_STATE), F32),
                        pltpu.VMEM((S5_SLABS, tc * bsz, 2 * SLAB_STATE), F32)],
        compiler_params=pltpu.CompilerParams(dimension_semantics=("arbitrary",),
                                             vmem_limit_bytes=VMEM_LIMIT),
        name="s5_branch",
    )(u_tb, bw, cw, ar_s, ai_s, d_row, w_glu, b_glu_row, g_row)


def _layer_norm_rows(v, g, b):
    mu = jnp.mean(v, axis=-1, keepdims=True)
    c = v - mu
    var = jnp.mean(c * c, axis=-1, keepdims=True)
    return c * lax.rsqrt(var + LN_EPS) * g + b


def _outproj_kernel(attn_ref, ssm_ref, x_ref, wo_ref, ag_ref, g1_ref, b1_ref, wrt_ref, brt_ref, triu_ref,
                    h_ref, eid_ref, gate_ref, rank_ref, cnt_ref):
    a = attn_ref[...]
    a = a * lax.rsqrt(jnp.mean(a * a, axis=-1, keepdims=True) + RMS_EPS) * ag_ref[...]
    mix = (jnp.dot(a.astype(BF16), wo_ref[0:ATTN_WIDTH, :], preferred_element_type=F32)
           + jnp.dot(ssm_ref[...], wo_ref[ATTN_WIDTH:, :], preferred_element_type=F32))
    h = _layer_norm_rows(DEEPNORM_ALPHA * x_ref[...] + mix, g1_ref[...], b1_ref[...])
    h_ref[...] = h

    logits = lax.dot_general(wrt_ref[...], h.astype(BF16), (((1,), (1,)), ((), ())),
                             preferred_element_type=F32)[:ROUTER_ROWS] + brt_ref[...]
    tm = logits.shape[1]
    row = lax.broadcasted_iota(jnp.int32, (ROUTER_ROWS, tm), 0).astype(F32)
    no_row = float(LANES)
    is_g = row < N_EXPERT_GROUPS
    gl = jnp.where(is_g, logits, NEG_BIG)
    gmax = jnp.max(gl, axis=0, keepdims=True)
    g_idx = jnp.min(jnp.where(is_g & (gl == gmax), row, no_row), axis=0, keepdims=True)
    g_p = 1.0 / jnp.sum(jnp.where(is_g, jnp.exp(gl - gmax), 0.0), axis=0, keepdims=True)
    first = N_EXPERT_GROUPS + g_idx * EXPERTS_PER_GROUP
    in_grp = (row >= first) & (row < first + EXPERTS_PER_GROUP)
    el = jnp.where(in_grp, logits, NEG_BIG)
    m1 = jnp.max(el, axis=0, keepdims=True)
    i1 = jnp.min(jnp.where(in_grp & (el == m1), row, no_row), axis=0, keepdims=True)
    rest = in_grp & (row != i1)
    el2 = jnp.where(rest, logits, NEG_BIG)
    m2 = jnp.max(el2, axis=0, keepdims=True)
    i2 = jnp.min(jnp.where(rest & (el2 == m2), row, no_row), axis=0, keepdims=True)
    r = jnp.exp(m2 - m1)
    p1 = 1.0 / (1.0 + r)
    p2 = r / (1.0 + r)
    k2 = lax.broadcasted_iota(jnp.int32, (2, tm), 0)
    eid_ref[...] = (jnp.where(k2 == 0, i1, i2) - N_EXPERT_GROUPS).astype(jnp.int32)
    gate_ref[...] = g_p * jnp.where(k2 == 0, p1, p2)

    @pl.when(pl.program_id(0) == 0)
    def _():
        cnt_ref[...] = jnp.zeros_like(cnt_ref)
    hit1 = row == i1
    hit2 = row == i2
    onehot = jnp.where(hit1 | hit2, 1.0, 0.0)
    cnt = cnt_ref[...]
    before = jnp.dot(onehot.astype(BF16), triu_ref[...], preferred_element_type=F32) + cnt
    r1 = jnp.sum(jnp.where(hit1, before, 0.0), axis=0, keepdims=True)
    r2 = jnp.sum(jnp.where(hit2, before, 0.0), axis=0, keepdims=True)
    rank_ref[...] = jnp.where(k2 == 0, r1, r2).astype(jnp.int32)
    cnt_ref[...] = cnt + jnp.sum(onehot, axis=1, keepdims=True)


def _outproj(attn2, ssm2, x2, wo, ag_row, g1_row, b1_row, wrt, brt):
    n_tok = x2.shape[0]
    tm = ROW_BLOCK
    triu = jnp.triu(jnp.ones((tm, tm), BF16), 1)
    row_spec = lambda w: pl.BlockSpec((tm, w), lambda i: (i, 0))
    col_spec = lambda r: pl.BlockSpec((r, tm), lambda i: (0, i))
    full = lambda shape: pl.BlockSpec(shape, lambda i: (0,) * len(shape))
    return pl.pallas_call(
        _outproj_kernel,
        grid=(n_tok // tm,),
        in_specs=[row_spec(ATTN_WIDTH), row_spec(SSM_WIDTH), row_spec(D_MODEL),
                  full((D_MODEL, D_MODEL)), full((1, ATTN_WIDTH)), full((1, D_MODEL)), full((1, D_MODEL)),
                  full((LANES, D_MODEL)), full((ROUTER_ROWS, 1)), full((tm, tm))],
        out_specs=[row_spec(D_MODEL), col_spec(2), col_spec(2), col_spec(2), full((ROUTER_ROWS, 1))],
        out_shape=[jax.ShapeDtypeStruct((n_tok, D_MODEL), F32),
                   jax.ShapeDtypeStruct((2, n_tok), jnp.int32),
                   jax.ShapeDtypeStruct((2, n_tok), F32),
                   jax.ShapeDtypeStruct((2, n_tok), jnp.int32),
                   jax.ShapeDtypeStruct((ROUTER_ROWS, 1), F32)],
        compiler_params=pltpu.CompilerParams(dimension_semantics=("arbitrary",),
                                             vmem_limit_bytes=VMEM_LIMIT),
        name="outproj_ln_router",
    )(attn2, ssm2, x2, wo, ag_row, g1_row, b1_row, wrt, brt, triu)


def _dispatch_plan(eid, rank, cnt_col, n_tok):
    counts = cnt_col[N_EXPERT_GROUPS:N_EXPERT_GROUPS + N_EXPERTS, 0].astype(jnp.int32)
    padded = ((counts + MOE_BLOCK - 1) // MOE_BLOCK) * MOE_BLOCK
    pends = jnp.cumsum(padded)
    pstarts = pends - padded
    n_pad = ((2 * n_tok + N_EXPERTS * (MOE_BLOCK - 1) + MOE_BLOCK - 1) // MOE_BLOCK) * MOE_BLOCK
    n_blocks = n_pad // MOE_BLOCK
    blk0 = jnp.arange(n_blocks, dtype=jnp.int32) * MOE_BLOCK
    block_e = jnp.minimum(jnp.sum((pends[None, :] <= blk0[:, None]).astype(jnp.int32), axis=1), N_EXPERTS - 1)
    n_used = (pends[-1] // MOE_BLOCK).astype(jnp.int32).reshape(1)
    pads = (padded - counts, jnp.zeros_like(counts), pstarts + counts)

    tb = TOKEN_BLOCK
    nb = n_tok // tb
    onehot = eid[:, :, None] == jnp.arange(N_EXPERTS, dtype=jnp.int32)[None, None, :]
    run_len = jnp.sum(onehot.reshape(2, nb, tb, N_EXPERTS).astype(jnp.int32), axis=(0, 2))
    before = jnp.cumsum(run_len, axis=0) - run_len
    run_src = jnp.cumsum(run_len, axis=1) - run_len
    run_dst = pstarts[None, :] + before
    shift = jnp.broadcast_to((run_src - before)[:, None, :], (nb, tb, N_EXPERTS)).reshape(n_tok, N_EXPERTS)
    local = rank + jnp.sum(jnp.where(onehot, shift[None], 0), axis=-1)
    any_long = (jnp.max(run_len, axis=1) >= (1 << SHORT_RUN_BITS)).astype(jnp.int32)
    runs = (run_len.reshape(-1), run_src.reshape(-1), run_dst.reshape(-1), any_long)
    return local, runs, block_e, n_used, pads, n_blocks


ROW_SUB = D_MODEL // LANES


def _rows(start, size):
    return pl.ds(pl.multiple_of(start * ROW_SUB, ROW_SUB), size * ROW_SUB)


def _store_row_tiles(ref, row0, v):
    n = v.shape[0]
    for s in range(ROW_SUB):
        ref[pl.ds(row0 * ROW_SUB + s, n, stride=ROW_SUB), :] = v[:, LANES * s:LANES * (s + 1)]


def _load_row_tiles(ref, row0, n):
    return jnp.concatenate([ref[pl.ds(row0 * ROW_SUB + s, n, stride=ROW_SUB), :] for s in range(ROW_SUB)], axis=1)


def _for_each_run_piece(blk, len_ref, src_ref, dst_ref, bits, make_copy, act):
    for e in range(N_EXPERTS):
        idx = blk * N_EXPERTS + e
        n = len_ref[idx]
        src = src_ref[idx]
        dst = dst_ref[idx]
        for b in reversed(range(bits)):
            @pl.when((n & (1 << b)) != 0)
            def _():
                off = lax.shift_left(lax.shift_right_logical(n, b + 1), b + 1)
                act(make_copy(src + off, dst + off, 1 << b))


def _by_run_size(long_ref, blk, body):
    @pl.when(long_ref[blk] == 0)
    def _():
        body(SHORT_RUN_BITS)

    @pl.when(long_ref[blk] != 0)
    def _():
        body(RUN_BITS)


def _dispatch_kernel(nu_ref, plen_ref, psrc_ref, pdst_ref, len_ref, src_ref, dst_ref, long_ref, local_ref, h_ref,
                     xb_hbm, zbuf, pk_ref, sem_z, sem, *, n_blocks):
    i = pl.program_id(0)
    nb = pl.num_programs(0) - 1
    tb = h_ref.shape[0]
    cur = i % 2

    def each_zero_copy(act):
        def zero_copy(src, dst, size):
            return pltpu.make_async_copy(zbuf.at[_rows(src, size)], xb_hbm.at[_rows(dst, size)], sem_z)
        _for_each_run_piece(0, plen_ref, psrc_ref, pdst_ref, MOE_BLOCK.bit_length() - 1, zero_copy, act)

        def unused(b, carry):
            act(zero_copy(0, b * MOE_BLOCK, MOE_BLOCK))
            return carry
        lax.fori_loop(nu_ref[0], n_blocks, unused, 0)

    @pl.when(i == 0)
    def _():
        zbuf[...] = jnp.zeros_like(zbuf)
        each_zero_copy(lambda c: c.start())

    def drain(which):
        pltpu.make_async_copy(pk_ref.at[which], xb_hbm.at[_rows(0, 2 * tb)], sem.at[which]).wait()

    @pl.when(i >= 2)
    def _():
        drain(cur)

    def group():
        pos = lax.broadcasted_iota(jnp.int32, (2 * tb, tb), 0)
        hit = (pos == local_ref[0:1, :]) | (pos == local_ref[1:2, :])
        perm = jnp.where(hit, 1.0, 0.0).astype(BF16)
        _store_row_tiles(pk_ref.at[cur], 0, jnp.dot(perm, h_ref[...].astype(BF16), preferred_element_type=F32))

    def send(bits):
        def run_copy(src, dst, size):
            return pltpu.make_async_copy(pk_ref.at[1 - cur, _rows(src, size)], xb_hbm.at[_rows(dst, size)],
                                         sem.at[1 - cur])
        _for_each_run_piece(i - 1, len_ref, src_ref, dst_ref, bits, run_copy, lambda c: c.start())

    @pl.when(i == 0)
    def _():
        group()

    @pl.when((i > 0) & (i < nb))
    def _():
        def send_and_group(bits):
            send(bits)
            group()
        _by_run_size(long_ref, i - 1, send_and_group)

    @pl.when(i == nb)
    def _():
        send(RUN_BITS)
        drain(1 - cur)
        each_zero_copy(lambda c: c.wait())


def _dispatch(h, local, runs, pads, n_used, n_blocks):
    n_tok = h.shape[0]
    tb = TOKEN_BLOCK
    nb = n_tok // tb
    grid_spec = pltpu.PrefetchScalarGridSpec(
        num_scalar_prefetch=8,
        grid=(nb + 1,),
        in_specs=[pl.BlockSpec((2, tb), lambda i, *_: (0, jnp.minimum(i, nb - 1))),
                  pl.BlockSpec((tb, D_MODEL), lambda i, *_: (jnp.minimum(i, nb - 1), 0))],
        out_specs=pl.BlockSpec(memory_space=pl.ANY),
        scratch_shapes=[pltpu.VMEM((MOE_BLOCK * ROW_SUB, LANES), F32),
                        pltpu.VMEM((2, 2 * tb * ROW_SUB, LANES), F32),
                        pltpu.SemaphoreType.DMA, pltpu.SemaphoreType.DMA((2,))],
    )
    return pl.pallas_call(
        functools.partial(_dispatch_kernel, n_blocks=n_blocks),
        grid_spec=grid_spec,
        out_shape=jax.ShapeDtypeStruct((n_blocks * MOE_BLOCK * ROW_SUB, LANES), F32),
        compiler_params=pltpu.CompilerParams(dimension_semantics=("arbitrary",),
                                             vmem_limit_bytes=VMEM_LIMIT),
        name="moe_dispatch",
    )(n_used, *pads, *runs, local, h)


def _expert_kernel(be_ref, nu_ref, x_ref, wg_ref, wu_ref, wd_ref, y_ref, wgu_t, wd_t, yt_ref):
    i = pl.program_id(0)
    used = i < nu_ref[0]

    @pl.when(used & ((i == 0) | (be_ref[i] != be_ref[jnp.maximum(i - 1, 0)])))
    def _():
        wgu_t[0:D_EXPERT, :] = wg_ref[...].astype(BF16).T
        wgu_t[D_EXPERT:, :] = wu_ref[...].astype(BF16).T
        wd_t[...] = wd_ref[...].astype(BF16).T

    @pl.when(used)
    def _():
        sub = range(MOE_BLOCK // MOE_SUB)
        gus = []
        for c in sub:
            xb = _load_row_tiles(x_ref, c * MOE_SUB, MOE_SUB).astype(BF16)
            gus.append(lax.dot_general(wgu_t[...], xb, (((1,), (1,)), ((), ())),
                                       preferred_element_type=F32))
        hids = []
        for c in sub:
            gt = gus[c][0:D_EXPERT]
            hids.append((gt * (1.0 / (1.0 + jnp.exp(-gt))) * gus[c][D_EXPERT:]).astype(BF16))
        for c in sub:
            yt_ref[c] = jnp.dot(wd_t[...], hids[c], preferred_element_type=F32)
        for c in sub:
            _store_row_tiles(y_ref, c * MOE_SUB, yt_ref[c].astype(BF16).T.astype(F32))

    @pl.when(jnp.logical_not(used))
    def _():
        y_ref[...] = jnp.zeros_like(y_ref)


def _experts(xb, block_e, n_used, wg, wu, wd):
    n_blocks = xb.shape[0] // (MOE_BLOCK * ROW_SUB)
    last = lambda i, nu: jnp.minimum(i, nu[0] - 1)
    grid_spec = pltpu.PrefetchScalarGridSpec(
        num_scalar_prefetch=2,
        grid=(n_blocks,),
        in_specs=[pl.BlockSpec((MOE_BLOCK * ROW_SUB, LANES), lambda i, be, nu: (last(i, nu), 0)),
                  pl.BlockSpec((None, D_MODEL, D_EXPERT), lambda i, be, nu: (be[last(i, nu)], 0, 0)),
                  pl.BlockSpec((None, D_MODEL, D_EXPERT), lambda i, be, nu: (be[last(i, nu)], 0, 0)),
                  pl.BlockSpec((None, D_EXPERT, D_MODEL), lambda i, be, nu: (be[last(i, nu)], 0, 0))],
        out_specs=pl.BlockSpec((MOE_BLOCK * ROW_SUB, LANES), lambda i, be, nu: (i, 0)),
        scratch_shapes=[pltpu.VMEM((2 * D_EXPERT, D_MODEL), BF16), pltpu.VMEM((D_MODEL, D_EXPERT), BF16),
                        pltpu.VMEM((MOE_BLOCK // MOE_SUB, D_MODEL, MOE_SUB), F32)],
    )
    return pl.pallas_call(
        _expert_kernel,
        grid_spec=grid_spec,
        out_shape=jax.ShapeDtypeStruct(xb.shape, F32),
        compiler_params=pltpu.CompilerParams(dimension_semantics=("arbitrary",),
                                             vmem_limit_bytes=VMEM_LIMIT),
        name="moe_experts",
    )(block_e, n_used, xb, wg, wu, wd)


def _combine_kernel(len_ref, src_ref, dst_ref, long_ref, local_ref, gate_ref, h_ref, g2_ref, b2_ref, yb_hbm,
                    o_ref, buf, sem):
    i = pl.program_id(0)
    n = pl.num_programs(0)
    tb = h_ref.shape[0]
    cur = i % 2

    def fetch(blk, which, bits):
        def run_copy(src, dst, size):
            return pltpu.make_async_copy(yb_hbm.at[_rows(dst, size)], buf.at[which, _rows(src, size)], sem.at[which])
        _for_each_run_piece(blk, len_ref, src_ref, dst_ref, bits, run_copy, lambda c: c.start())

    def drain(which):
        pltpu.make_async_copy(yb_hbm.at[_rows(0, 2 * tb)], buf.at[which], sem.at[which]).wait()

    @pl.when(i == 0)
    def _():
        fetch(i, 0, RUN_BITS)

    drain(cur)
    nxt = jnp.minimum(i + 1, n - 1)

    def fetch_next_and_combine(bits):
        fetch(nxt, 1 - cur, bits)
        y = _load_row_tiles(buf.at[cur], 0, 2 * tb).astype(BF16)
        pos = lax.broadcasted_iota(jnp.int32, (tb, 2 * tb), 1)
        local = local_ref[...]
        gate = gate_ref[...]
        mix = (jnp.where(pos == local[:, 0:1], gate[:, 0:1], 0.0)
               + jnp.where(pos == local[:, 1:2], gate[:, 1:2], 0.0))
        mix_hi = mix.astype(BF16)
        mix_lo = (mix - mix_hi.astype(F32)).astype(BF16)
        moe = (jnp.dot(mix_hi, y, preferred_element_type=F32) + jnp.dot(mix_lo, y, preferred_element_type=F32))
        o_ref[...] = _layer_norm_rows(DEEPNORM_ALPHA * h_ref[...] + moe, g2_ref[...], b2_ref[...])
    _by_run_size(long_ref, nxt, fetch_next_and_combine)

    @pl.when(i == n - 1)
    def _():
        drain(1 - cur)


def _combine(h, yb, local_t, runs, gate, g2_row, b2_row):
    n_tok = h.shape[0]
    tb = TOKEN_BLOCK
    row_spec = lambda w: pl.BlockSpec((tb, w), lambda i, *_: (i, 0))
    full = lambda shape: pl.BlockSpec(shape, lambda i, *_: (0,) * len(shape))
    grid_spec = pltpu.PrefetchScalarGridSpec(
        num_scalar_prefetch=4,
        grid=(n_tok // tb,),
        in_specs=[row_spec(2), row_spec(2), row_spec(D_MODEL), full((1, D_MODEL)), full((1, D_MODEL)),
                  pl.BlockSpec(memory_space=pl.ANY)],
        out_specs=row_spec(D_MODEL),
        scratch_shapes=[pltpu.VMEM((2, 2 * tb * ROW_SUB, LANES), F32), pltpu.SemaphoreType.DMA((2,))],
    )
    return pl.pallas_call(
        _combine_kernel,
        grid_spec=grid_spec,
        out_shape=jax.ShapeDtypeStruct((n_tok, D_MODEL), F32),
        compiler_params=pltpu.CompilerParams(dimension_semantics=("arbitrary",),
                                             vmem_limit_bytes=VMEM_LIMIT),
        name="combine_ln",
    )(*runs, local_t, gate, h, g2_row, b2_row, yb)


def kernel(x, w_in, b_fgate, s5_lambda_re, s5_lambda_im, s5_log_step, s5_b_re, s5_b_im, s5_c_re, s5_c_im, s5_d, w_glu, b_glu, attn_norm_g, ssm_norm_g, w_out, ln1_g, ln1_b, w_router_group, b_router_group, w_router_expert, b_router_expert, w_gate, w_up, w_down, ln2_g, ln2_b):
    bsz, seq, d = x.shape
    n_tok = bsz * seq
    a = ATTN_WIDTH
    for l in range(DEPTH):
        x2 = x.reshape(n_tok, d)
        w = w_in[l]
        f0 = 3 * a
        w_cat = jnp.concatenate(
            [w[:, :a] * (HEAD_DIM ** -0.5 * LOG2E), w[:, a:2 * a], w[:, f0 + ATTN_HEADS:]], axis=1).astype(BF16)
        wvt = jnp.pad(w[:, 2 * a:f0 + ATTN_HEADS].T, ((0, GATE_ROWS - ATTN_HEADS), (0, 0))).astype(BF16)
        q, k, vt, u, eq, ek = _inproj(x2, w_cat, wvt, b_fgate[l].reshape(ATTN_HEADS, 1), seq)
        attn = _attention(q.reshape(bsz, seq, a), k.reshape(bsz, seq, a), vt,
                          eq.reshape(bsz, seq, EXT_W), ek.reshape(bsz, seq, EXT_W))

        bw, cw, ar_s, ai_s = _s5_weights(s5_lambda_re[l], s5_lambda_im[l], s5_log_step[l],
                                         s5_b_re[l], s5_b_im[l], s5_c_re[l], s5_c_im[l])
        u_tb = u.reshape(bsz, seq, SSM_WIDTH).transpose(1, 0, 2)
        ssm_tb = _s5(u_tb, bw, cw, ar_s, ai_s, s5_d[l].reshape(1, SSM_WIDTH), w_glu[l].astype(BF16),
                     b_glu[l].reshape(1, SSM_WIDTH), ssm_norm_g[l].reshape(1, SSM_WIDTH))
        ssm2 = ssm_tb.transpose(1, 0, 2).reshape(n_tok, SSM_WIDTH)

        n_route = N_EXPERT_GROUPS + N_EXPERTS
        wrt = jnp.pad(jnp.concatenate([w_router_group[l], w_router_expert[l]], axis=1).T,
                      ((0, LANES - n_route), (0, 0))).astype(BF16)
        brt = jnp.pad(jnp.concatenate([b_router_group[l], b_router_expert[l]]),
                      (0, ROUTER_ROWS - n_route)).reshape(ROUTER_ROWS, 1)
        h, eid, gate, rank, cnt_col = _outproj(
            attn.reshape(n_tok, a), ssm2, x2, w_out[l].astype(BF16), attn_norm_g[l].reshape(1, a),
            ln1_g[l].reshape(1, d), ln1_b[l].reshape(1, d), wrt, brt)

        local, runs, block_e, n_used, pads, n_blocks = _dispatch_plan(eid, rank, cnt_col, n_tok)
        xb = _dispatch(h, local, runs, pads, n_used, n_blocks)
        yb = _experts(xb, block_e, n_used, w_gate[l], w_up[l], w_down[l])
        x = _combine(h, yb, local.T, runs, gate.T, ln2_g[l].reshape(1, d),
                     ln2_b[l].reshape(1, d)).reshape(bsz, seq, d)
    return x
```

```python
import functools
import math

import jax
import jax.numpy as jnp
import numpy as np
from jax import lax
from jax.experimental import pallas as pl
from jax.experimental.pallas import tpu as pltpu

F32 = jnp.float32
BF16 = jnp.bfloat16

D_MODEL = 1024
HEAD_DIM = 64
ATTN_WIDTH = 512
ATTN_HEADS = 8
SSM_WIDTH = 512
SSM_GROUP = 16
SSM_GROUPS = 32
SSM_STATE = 64
N_EXPERT_GROUPS = 4
EXPERTS_PER_GROUP = 8
N_EXPERTS = 32
D_EXPERT = 512
MOE_BLOCK = 512
MOE_SUB = 256
TOKEN_BLOCK = 512
RUN_BITS = TOKEN_BLOCK.bit_length()
SHORT_RUN_BITS = 6
DEPTH = 1
DEEPNORM_ALPHA = (2.0 * DEPTH) ** 0.25
LN_EPS = 1e-5
RMS_EPS = 1e-6

LANES = 128
NEG_BIG = -1e30
LOG2E = math.log2(math.e)
ONES_ROWS = 16
VT_ROWS = ATTN_HEADS * (HEAD_DIM + ONES_ROWS)
GATE_ROWS = 16
PAIR_BIAS_LANES = 8
EXT_W = ATTN_HEADS // 2 * LANES
VMEM_LIMIT = 48 * 1024 * 1024

ROW_BLOCK = 1024
ATTN_BLOCK = 256
ATTN_HEADS_PER_STEP = 8
S5_TIME_BLOCK = 64
S5_SLABS = 4
ROUTER_ROWS = 40
SLAB_CH = SSM_WIDTH // S5_SLABS
SLAB_STATE = SSM_GROUPS // S5_SLABS * SSM_STATE


def _split3(c):
    hi = c.astype(BF16)
    r1 = c - hi.astype(F32)
    mid = r1.astype(BF16)
    lo = (r1 - mid.astype(F32)).astype(BF16)
    return hi, mid, lo


def _inproj_kernel(x_ref, w_ref, wvt_ref, bf_ref, triu_ref, sel_ref, q_ref, k_ref, vt_ref, u_ref, eq_ref, ek_ref,
                   carry_ref, *, blocks_per_seq):
    i = pl.program_id(0)

    @pl.when(i % blocks_per_seq == 0)
    def _():
        carry_ref[...] = jnp.zeros_like(carry_ref)

    tm = x_ref.shape[0]
    xb = x_ref[...].astype(BF16)
    z = jnp.dot(xb, w_ref[...], preferred_element_type=F32)
    a = ATTN_WIDTH
    q_ref[...] = z[:, :a].astype(BF16)
    k_ref[...] = z[:, a:2 * a].astype(BF16)
    u_ref[...] = z[:, 2 * a:2 * a + SSM_WIDTH].astype(BF16)
    vtf = lax.dot_general(wvt_ref[...], xb, (((1,), (1,)), ((), ())), preferred_element_type=F32)
    vt = vtf[:ATTN_WIDTH].astype(BF16)
    ones_rows = jnp.ones((ONES_ROWS, tm), BF16)
    vt = jnp.concatenate([piece for h in range(ATTN_HEADS)
                          for piece in (vt[HEAD_DIM * h:HEAD_DIM * (h + 1)], ones_rows)], axis=0)
    for jj in range(tm // ATTN_BLOCK):
        vt_ref[jj] = vt[:, jj * ATTN_BLOCK:(jj + 1) * ATTN_BLOCK]

    f = vtf[ATTN_WIDTH:ATTN_WIDTH + ATTN_HEADS] + bf_ref[...]
    logf = jnp.minimum(f, 0.0) - jnp.log1p(jnp.exp(-jnp.abs(f)))
    parts = jnp.dot(jnp.concatenate(_split3(logf), axis=0), triu_ref[...], preferred_element_type=F32)
    nh = ATTN_HEADS
    cum = parts[0:nh] + parts[nh:2 * nh] + parts[2 * nh:3 * nh] + carry_ref[...]
    carry_ref[...] = cum[:, tm - 1:tm]
    hi, mid, lo = (t.astype(F32) for t in _split3(cum * LOG2E))
    one_row = jnp.where(lax.broadcasted_iota(jnp.int32, (nh, tm), 0) == 0, 1.0, 0.0)
    packed_t = jnp.concatenate([hi, mid, lo, one_row, jnp.zeros((LANES - 4 * nh, tm), F32)], axis=0)
    ext = jnp.dot(packed_t.T.astype(BF16), sel_ref[...], preferred_element_type=F32).astype(BF16)
    eq_ref[...] = ext[:, :EXT_W]
    ek_ref[...] = ext[:, EXT_W:]


def _bias_lane_selector():
    sel = np.zeros((LANES, 2, ATTN_HEADS // 2, LANES), np.float32)
    one = 3 * ATTN_HEADS
    for h in range(ATTN_HEADS):
        pair, base = h // 2, PAIR_BIAS_LANES * (h % 2)
        for t in range(3):
            sel[ATTN_HEADS * t + h, 0, pair, base + t] = 1.0
            sel[one, 0, pair, base + 3 + t] = 1.0
            sel[one, 1, pair, base + t] = 1.0
            sel[ATTN_HEADS * t + h, 1, pair, base + 3 + t] = -1.0
    return jnp.asarray(sel.reshape(LANES, 2 * EXT_W), BF16)


def _inproj(x2, w_cat, wvt, bf_col, seq):
    n_tok = x2.shape[0]
    tm = ROW_BLOCK
    assert seq % tm == 0 and n_tok % tm == 0 and tm % ATTN_BLOCK == 0
    bsz = n_tok // seq
    bps = seq // tm
    ncol = w_cat.shape[1]
    vblk = tm // ATTN_BLOCK
    row_spec = lambda w: pl.BlockSpec((tm, w), lambda i: (i, 0))
    return pl.pallas_call(
        functools.partial(_inproj_kernel, blocks_per_seq=bps),
        grid=(n_tok // tm,),
        in_specs=[row_spec(D_MODEL),
                  pl.BlockSpec((D_MODEL, ncol), lambda i: (0, 0)),
                  pl.BlockSpec((ATTN_WIDTH + GATE_ROWS, D_MODEL), lambda i: (0, 0)),
                  pl.BlockSpec((ATTN_HEADS, 1), lambda i: (0, 0)),
                  pl.BlockSpec((tm, tm), lambda i: (0, 0)),
                  pl.BlockSpec((LANES, 2 * EXT_W), lambda i: (0, 0))],
        out_specs=[row_spec(ATTN_WIDTH), row_spec(ATTN_WIDTH),
                   pl.BlockSpec((None, vblk, VT_ROWS, ATTN_BLOCK), lambda i: (i // bps, i % bps, 0, 0)),
                   pl.BlockSpec((None, tm, SSM_WIDTH), lambda i: (i // bps, i % bps, 0)),
                   row_spec(EXT_W), row_spec(EXT_W)],
        out_shape=[jax.ShapeDtypeStruct((n_tok, ATTN_WIDTH), BF16)] * 2
                  + [jax.ShapeDtypeStruct((bsz, seq // ATTN_BLOCK, VT_ROWS, ATTN_BLOCK), BF16),
                     jax.ShapeDtypeStruct((bsz, seq, SSM_WIDTH), BF16),
                     jax.ShapeDtypeStruct((n_tok, EXT_W), BF16),
                     jax.ShapeDtypeStruct((n_tok, EXT_W), BF16)],
        scratch_shapes=[pltpu.VMEM((ATTN_HEADS, 1), F32)],
        compiler_params=pltpu.CompilerParams(dimension_semantics=("arbitrary",),
                                             vmem_limit_bytes=VMEM_LIMIT),
        name="inproj",
    )(x2, w_cat, wvt, bf_col, jnp.triu(jnp.ones((tm, tm), BF16)), _bias_lane_selector())


def _attn_kernel(q_ref, k_ref, vt_ref, eq_ref, ek_ref, o_ref, *, tq, seq, nh):
    qi = pl.program_id(2)
    lane = lax.broadcasted_iota(jnp.int32, (tq, LANES), 1)
    q_augs = []
    for h in range(nh):
        qb = q_ref[:, LANES * (h // 2):LANES * (h // 2 + 1)]
        head_lanes = (lane < HEAD_DIM) if h % 2 == 0 else (lane >= HEAD_DIM)
        qh = jnp.where(head_lanes, qb, jnp.zeros((), BF16))
        own = (lane >= PAIR_BIAS_LANES * (h % 2)) & (lane < PAIR_BIAS_LANES * (h % 2 + 1))
        eqh = jnp.where(own, eq_ref[:, LANES * (h // 2):LANES * (h // 2 + 1)], jnp.zeros((), BF16))
        q_augs.append(jnp.concatenate([qh, eqh], axis=1))
    key_i = lax.broadcasted_iota(jnp.int32, (tq, tq), 0)
    qry_i = lax.broadcasted_iota(jnp.int32, (tq, tq), 1)
    causal = key_i <= qry_i
    vrows = HEAD_DIM + ONES_ROWS

    def scores_of(k0, nk):
        out = []
        for h in range(nh):
            kj = k_ref[pl.ds(k0, nk), LANES * (h // 2):LANES * (h // 2 + 1)]
            k_aug = jnp.concatenate([kj, ek_ref[pl.ds(k0, nk), LANES * (h // 2):LANES * (h // 2 + 1)]], axis=1)
            out.append(lax.dot_general(k_aug, q_augs[h], (((1,), (1,)), ((), ())), preferred_element_type=F32))
        return tuple(out)

    def absorb(j0, scores, carry):
        stats = []
        for h in range(nh):
            m = carry[h][0]
            m_new = jnp.maximum(m, jnp.max(scores[h], axis=0, keepdims=True))
            stats.append((m_new, jnp.exp2(m - m_new), jnp.exp2(scores[h] - m_new).astype(BF16)))
        new = []
        for h in range(nh):
            m_new, corr, p = stats[h]
            res = None
            for c in range(p.shape[0] // tq):
                part = jnp.dot(vt_ref[j0 + c, vrows * h:vrows * (h + 1), :], p[c * tq:(c + 1) * tq],
                               preferred_element_type=F32)
                res = part if res is None else res + part
            acc = corr * carry[h][2] + res[0:HEAD_DIM, :]
            l = corr * carry[h][1] + res[HEAD_DIM:HEAD_DIM + 1, :]
            new.append((m_new, l, acc))
        return tuple(new)

    def pair_step(jj, carry):
        return absorb(2 * jj, scores_of(pl.multiple_of(jj * (2 * tq), 2 * tq), 2 * tq), carry)

    def finish(carry):
        out_t = jnp.concatenate([c[2] / c[1] for c in carry], axis=0)
        o_ref[...] = out_t.T

    init1 = (jnp.full((1, tq), NEG_BIG, F32), jnp.zeros((1, tq), F32), jnp.zeros((HEAD_DIM, tq), F32))
    carry = lax.fori_loop(0, qi // 2, pair_step, (init1,) * nh)

    @pl.when(qi % 2 == 0)
    def _():
        diag = scores_of(pl.multiple_of(qi * tq, tq), tq)
        finish(absorb(qi, tuple(jnp.where(causal, s, NEG_BIG) for s in diag), carry))

    @pl.when(qi % 2 == 1)
    def _():
        last = scores_of(pl.multiple_of((qi - 1) * tq, 2 * tq), 2 * tq)
        causal2 = (lax.broadcasted_iota(jnp.int32, (2 * tq, tq), 0)
                   <= lax.broadcasted_iota(jnp.int32, (2 * tq, tq), 1) + tq)
        finish(absorb(qi - 1, tuple(jnp.where(causal2, s, NEG_BIG) for s in last), carry))


def _attention(q, k, vt, eq, ek):
    bsz, seq, _ = q.shape
    tq = ATTN_BLOCK
    nh = ATTN_HEADS_PER_STEP
    assert seq % tq == 0 and ATTN_HEADS % nh == 0 and nh % 2 == 0
    w = nh * HEAD_DIM
    return pl.pallas_call(
        functools.partial(_attn_kernel, tq=tq, seq=seq, nh=nh),
        grid=(bsz, ATTN_HEADS // nh, seq // tq),
        in_specs=[pl.BlockSpec((None, tq, w), lambda b, g, i: (b, i, g)),
                  pl.BlockSpec((None, seq, w), lambda b, g, i: (b, 0, g)),
                  pl.BlockSpec((None, seq // tq, nh * (HEAD_DIM + ONES_ROWS), tq), lambda b, g, i: (b, 0, g, 0)),
                  pl.BlockSpec((None, tq, nh // 2 * LANES), lambda b, g, i: (b, i, g)),
                  pl.BlockSpec((None, seq, nh // 2 * LANES), lambda b, g, i: (b, 0, g))],
        out_specs=pl.BlockSpec((None, tq, w), lambda b, g, i: (b, i, g)),
        out_shape=jax.ShapeDtypeStruct((bsz, seq, ATTN_WIDTH), F32),
        compiler_params=pltpu.CompilerParams(dimension_semantics=("parallel", "parallel", "arbitrary"),
                                             vmem_limit_bytes=VMEM_LIMIT),
        name="fox_attention",
    )(q, k, vt, eq, ek)


def _s5_kernel(u_ref, bw_ref, cw_ref, ar_ref, ai_ref, d_ref, wglu_ref, bglu_ref, g_ref, o_ref,
               state_ref, bu_ref, *, tc, bsz):
    @pl.when(pl.program_id(0) == 0)
    def _():
        state_ref[...] = jnp.zeros_like(state_ref)

    rows = tc * bsz
    ns = SLAB_STATE
    u = u_ref[...].reshape(rows, SSM_WIDTH)
    ub = u
    for i in range(S5_SLABS):
        bu_ref[i] = jnp.dot(ub[:, SLAB_CH * i:SLAB_CH * (i + 1)], bw_ref[i], preferred_element_type=F32)
    for i in range(S5_SLABS):
        ar = jnp.broadcast_to(ar_ref[i], (bsz, ns))
        ai = jnp.broadcast_to(ai_ref[i], (bsz, ns))
        xr = state_ref[i, :, 0:ns]
        xi = state_ref[i, :, ns:2 * ns]
        for t in range(tc):
            r0 = t * bsz
            nr = ar * xr - ai * xi + bu_ref[i, r0:r0 + bsz, 0:ns]
            ni = ar * xi + ai * xr + bu_ref[i, r0:r0 + bsz, ns:2 * ns]
            bu_ref[i, r0:r0 + bsz, 0:ns] = nr
            bu_ref[i, r0:r0 + bsz, ns:2 * ns] = ni
            xr, xi = nr, ni
        state_ref[i, :, 0:ns] = xr
        state_ref[i, :, ns:2 * ns] = xi
    ys = [jnp.dot(bu_ref[i].astype(BF16), cw_ref[i], preferred_element_type=F32) for i in range(S5_SLABS)]
    y = jnp.concatenate(ys, axis=1) + d_ref[...] * u.astype(F32)
    y = 0.5 * y * (1.0 + jnp.tanh(math.sqrt(2.0 / math.pi) * (y + 0.044715 * (y * y * y))))
    z = jnp.dot(y.astype(BF16), wglu_ref[...], preferred_element_type=F32) + bglu_ref[...]
    y = y * (1.0 / (1.0 + jnp.exp(-z)))
    y = y * lax.rsqrt(jnp.mean(y * y, axis=-1, keepdims=True) + RMS_EPS) * g_ref[...]
    o_ref[...] = y.astype(BF16).reshape(tc, bsz, SSM_WIDTH)


def _s5_weights(lam_re, lam_im, log_step, b_re, b_im, c_re, c_im):
    delta = jnp.exp(log_step)[:, None]
    mag = jnp.exp(lam_re * delta)
    ar = mag * jnp.cos(lam_im * delta)
    ai = mag * jnp.sin(lam_im * delta)
    den = lam_re * lam_re + lam_im * lam_im
    num_re = ar - 1.0
    coef_re = (num_re * lam_re + ai * lam_im) / den
    coef_im = (ai * lam_re - num_re * lam_im) / den
    bbar_re = coef_re[..., None] * b_re - coef_im[..., None] * b_im
    bbar_im = coef_re[..., None] * b_im + coef_im[..., None] * b_re
    gs = SSM_GROUPS // S5_SLABS
    eye = jnp.eye(gs, dtype=F32)

    def b_slab(bb):
        bb = bb.reshape(S5_SLABS, gs, SSM_STATE, SSM_GROUP)
        w = jnp.einsum('ab,sapc->sacbp', eye, bb)
        return w.reshape(S5_SLABS, gs * SSM_GROUP, gs * SSM_STATE)

    def c_slab(cc):
        cc = cc.reshape(S5_SLABS, gs, SSM_GROUP, SSM_STATE)
        w = jnp.einsum('ab,sacp->sbpac', eye, cc)
        return w.reshape(S5_SLABS, gs * SSM_STATE, gs * SSM_GROUP)

    bw = jnp.concatenate([b_slab(bbar_re), b_slab(bbar_im)], axis=2).astype(BF16)
    cw = jnp.concatenate([c_slab(c_re), -c_slab(c_im)], axis=1).astype(BF16)
    ar_s = ar.reshape(S5_SLABS, 1, SLAB_STATE)
    ai_s = ai.reshape(S5_SLABS, 1, SLAB_STATE)
    return bw, cw, ar_s, ai_s


def _s5(u_tb, bw, cw, ar_s, ai_s, d_row, w_glu, b_glu_row, g_row):
    seq, bsz, _ = u_tb.shape
    tc = S5_TIME_BLOCK
    assert seq % tc == 0 and bsz % 16 == 0
    full = lambda shape: pl.BlockSpec(shape, lambda t: (0,) * len(shape))
    return pl.pallas_call(
        functools.partial(_s5_kernel, tc=tc, bsz=bsz),
        grid=(seq // tc,),
        in_specs=[pl.BlockSpec((tc, bsz, SSM_WIDTH), lambda t: (t, 0, 0)),
                  full(bw.shape), full(cw.shape), full(ar_s.shape), full(ai_s.shape),
                  full((1, SSM_WIDTH)), full((SSM_WIDTH, SSM_WIDTH)), full((1, SSM_WIDTH)),
                  full((1, SSM_WIDTH))],
        out_specs=pl.BlockSpec((tc, bsz, SSM_WIDTH), lambda t: (t, 0, 0)),
        out_shape=jax.ShapeDtypeStruct((seq, bsz, SSM_WIDTH), BF16),
        scratch_shapes=[pltpu.VMEM((S5_SLABS, bsz, 2 * SLAB_STATE), F32),
                        pltpu.VMEM((S5_SLABS, tc * bsz, 2 * SLAB_STATE), F32)],
        compiler_params=pltpu.CompilerParams(dimension_semantics=("arbitrary",),
                                             vmem_limit_bytes=VMEM_LIMIT),
        name="s5_branch",
    )(u_tb, bw, cw, ar_s, ai_s, d_row, w_glu, b_glu_row, g_row)


def _layer_norm_rows(v, g, b):
    mu = jnp.mean(v, axis=-1, keepdims=True)
    c = v - mu
    var = jnp.mean(c * c, axis=-1, keepdims=True)
    return c * lax.rsqrt(var + LN_EPS) * g + b


def _outproj_kernel(attn_ref, ssm_ref, x_ref, wo_ref, ag_ref, g1_ref, b1_ref, wrt_ref, brt_ref, triu_ref,
                    h_ref, eid_ref, gate_ref, rank_ref, cnt_ref):
    a = attn_ref[...]
    a = a * lax.rsqrt(jnp.mean(a * a, axis=-1, keepdims=True) + RMS_EPS) * ag_ref[...]
    mix = (jnp.dot(a.astype(BF16), wo_ref[0:ATTN_WIDTH, :], preferred_element_type=F32)
           + jnp.dot(ssm_ref[...], wo_ref[ATTN_WIDTH:, :], preferred_element_type=F32))
    h = _layer_norm_rows(DEEPNORM_ALPHA * x_ref[...] + mix, g1_ref[...], b1_ref[...])
    h_ref[...] = h

    logits = lax.dot_general(wrt_ref[...], h.astype(BF16), (((1,), (1,)), ((), ())),
                             preferred_element_type=F32)[:ROUTER_ROWS] + brt_ref[...]
    tm = logits.shape[1]
    row = lax.broadcasted_iota(jnp.int32, (ROUTER_ROWS, tm), 0).astype(F32)
    no_row = float(LANES)
    is_g = row < N_EXPERT_GROUPS
    gl = jnp.where(is_g, logits, NEG_BIG)
    gmax = jnp.max(gl, axis=0, keepdims=True)
    g_idx = jnp.min(jnp.where(is_g & (gl == gmax), row, no_row), axis=0, keepdims=True)
    g_p = 1.0 / jnp.sum(jnp.where(is_g, jnp.exp(gl - gmax), 0.0), axis=0, keepdims=True)
    first = N_EXPERT_GROUPS + g_idx * EXPERTS_PER_GROUP
    in_grp = (row >= first) & (row < first + EXPERTS_PER_GROUP)
    el = jnp.where(in_grp, logits, NEG_BIG)
    m1 = jnp.max(el, axis=0, keepdims=True)
    i1 = jnp.min(jnp.where(in_grp & (el == m1), row, no_row), axis=0, keepdims=True)
    rest = in_grp & (row != i1)
    el2 = jnp.where(rest, logits, NEG_BIG)
    m2 = jnp.max(el2, axis=0, keepdims=True)
    i2 = jnp.min(jnp.where(rest & (el2 == m2), row, no_row), axis=0, keepdims=True)
    r = jnp.exp(m2 - m1)
    p1 = 1.0 / (1.0 + r)
    p2 = r / (1.0 + r)
    k2 = lax.broadcasted_iota(jnp.int32, (2, tm), 0)
    eid_ref[...] = (jnp.where(k2 == 0, i1, i2) - N_EXPERT_GROUPS).astype(jnp.int32)
    gate_ref[...] = g_p * jnp.where(k2 == 0, p1, p2)

    @pl.when(pl.program_id(0) == 0)
    def _():
        cnt_ref[...] = jnp.zeros_like(cnt_ref)
    hit1 = row == i1
    hit2 = row == i2
    onehot = jnp.where(hit1 | hit2, 1.0, 0.0)
    cnt = cnt_ref[...]
    before = jnp.dot(onehot.astype(BF16), triu_ref[...], preferred_element_type=F32) + cnt
    r1 = jnp.sum(jnp.where(hit1, before, 0.0), axis=0, keepdims=True)
    r2 = jnp.sum(jnp.where(hit2, before, 0.0), axis=0, keepdims=True)
    rank_ref[...] = jnp.where(k2 == 0, r1, r2).astype(jnp.int32)
    cnt_ref[...] = cnt + jnp.sum(onehot, axis=1, keepdims=True)


def _outproj(attn2, ssm2, x2, wo, ag_row, g1_row, b1_row, wrt, brt):
    n_tok = x2.shape[0]
    tm = ROW_BLOCK
    triu = jnp.triu(jnp.ones((tm, tm), BF16), 1)
    row_spec = lambda w: pl.BlockSpec((tm, w), lambda i: (i, 0))
    col_spec = lambda r: pl.BlockSpec((r, tm), lambda i: (0, i))
    full = lambda shape: pl.BlockSpec(shape, lambda i: (0,) * len(shape))
    return pl.pallas_call(
        _outproj_kernel,
        grid=(n_tok // tm,),
        in_specs=[row_spec(ATTN_WIDTH), row_spec(SSM_WIDTH), row_spec(D_MODEL),
                  full((D_MODEL, D_MODEL)), full((1, ATTN_WIDTH)), full((1, D_MODEL)), full((1, D_MODEL)),
                  full((LANES, D_MODEL)), full((ROUTER_ROWS, 1)), full((tm, tm))],
        out_specs=[row_spec(D_MODEL), col_spec(2), col_spec(2), col_spec(2), full((ROUTER_ROWS, 1))],
        out_shape=[jax.ShapeDtypeStruct((n_tok, D_MODEL), F32),
                   jax.ShapeDtypeStruct((2, n_tok), jnp.int32),
                   jax.ShapeDtypeStruct((2, n_tok), F32),
                   jax.ShapeDtypeStruct((2, n_tok), jnp.int32),
                   jax.ShapeDtypeStruct((ROUTER_ROWS, 1), F32)],
        compiler_params=pltpu.CompilerParams(dimension_semantics=("arbitrary",),
                                             vmem_limit_bytes=VMEM_LIMIT),
        name="outproj_ln_router",
    )(attn2, ssm2, x2, wo, ag_row, g1_row, b1_row, wrt, brt, triu)


def _dispatch_plan(eid, rank, cnt_col, n_tok):
    counts = cnt_col[N_EXPERT_GROUPS:N_EXPERT_GROUPS + N_EXPERTS, 0].astype(jnp.int32)
    padded = ((counts + MOE_BLOCK - 1) // MOE_BLOCK) * MOE_BLOCK
    pends = jnp.cumsum(padded)
    pstarts = pends - padded
    n_pad = ((2 * n_tok + N_EXPERTS * (MOE_BLOCK - 1) + MOE_BLOCK - 1) // MOE_BLOCK) * MOE_BLOCK
    n_blocks = n_pad // MOE_BLOCK
    blk0 = jnp.arange(n_blocks, dtype=jnp.int32) * MOE_BLOCK
    block_e = jnp.minimum(jnp.sum((pends[None, :] <= blk0[:, None]).astype(jnp.int32), axis=1), N_EXPERTS - 1)
    n_used = (pends[-1] // MOE_BLOCK).astype(jnp.int32).reshape(1)
    pads = (padded - counts, jnp.zeros_like(counts), pstarts + counts)

    tb = TOKEN_BLOCK
    nb = n_tok // tb
    onehot = eid[:, :, None] == jnp.arange(N_EXPERTS, dtype=jnp.int32)[None, None, :]
    run_len = jnp.sum(onehot.reshape(2, nb, tb, N_EXPERTS).astype(jnp.int32), axis=(0, 2))
    before = jnp.cumsum(run_len, axis=0) - run_len
    run_src = jnp.cumsum(run_len, axis=1) - run_len
    run_dst = pstarts[None, :] + before
    shift = jnp.broadcast_to((run_src - before)[:, None, :], (nb, tb, N_EXPERTS)).reshape(n_tok, N_EXPERTS)
    local = rank + jnp.sum(jnp.where(onehot, shift[None], 0), axis=-1)
    any_long = (jnp.max(run_len, axis=1) >= (1 << SHORT_RUN_BITS)).astype(jnp.int32)
    runs = (run_len.reshape(-1), run_src.reshape(-1), run_dst.reshape(-1), any_long)
    return local, runs, block_e, n_used, pads, n_blocks


ROW_SUB = D_MODEL // LANES


def _rows(start, size):
    return pl.ds(pl.multiple_of(start * ROW_SUB, ROW_SUB), size * ROW_SUB)


def _store_row_tiles(ref, row0, v):
    n = v.shape[0]
    for s in range(ROW_SUB):
        ref[pl.ds(row0 * ROW_SUB + s, n, stride=ROW_SUB), :] = v[:, LANES * s:LANES * (s + 1)]


def _load_row_tiles(ref, row0, n):
    return jnp.concatenate([ref[pl.ds(row0 * ROW_SUB + s, n, stride=ROW_SUB), :] for s in range(ROW_SUB)], axis=1)


def _for_each_run_piece(blk, len_ref, src_ref, dst_ref, bits, make_copy, act):
    for e in range(N_EXPERTS):
        idx = blk * N_EXPERTS + e
        n = len_ref[idx]
        src = src_ref[idx]
        dst = dst_ref[idx]
        for b in reversed(range(bits)):
            @pl.when((n & (1 << b)) != 0)
            def _():
                off = lax.shift_left(lax.shift_right_logical(n, b + 1), b + 1)
                act(make_copy(src + off, dst + off, 1 << b))


def _by_run_size(long_ref, blk, body):
    @pl.when(long_ref[blk] == 0)
    def _():
        body(SHORT_RUN_BITS)

    @pl.when(long_ref[blk] != 0)
    def _():
        body(RUN_BITS)


def _dispatch_kernel(nu_ref, plen_ref, psrc_ref, pdst_ref, len_ref, src_ref, dst_ref, long_ref, local_ref, h_ref,
                     xb_hbm, zbuf, pk_ref, sem_z, sem, *, n_blocks):
    i = pl.program_id(0)
    nb = pl.num_programs(0) - 1
    tb = h_ref.shape[0]
    cur = i % 2

    def each_zero_copy(act):
        def zero_copy(src, dst, size):
            return pltpu.make_async_copy(zbuf.at[_rows(src, size)], xb_hbm.at[_rows(dst, size)], sem_z)
        _for_each_run_piece(0, plen_ref, psrc_ref, pdst_ref, MOE_BLOCK.bit_length() - 1, zero_copy, act)

        def unused(b, carry):
            act(zero_copy(0, b * MOE_BLOCK, MOE_BLOCK))
            return carry
        lax.fori_loop(nu_ref[0], n_blocks, unused, 0)

    @pl.when(i == 0)
    def _():
        zbuf[...] = jnp.zeros_like(zbuf)
        each_zero_copy(lambda c: c.start())

    def drain(which):
        pltpu.make_async_copy(pk_ref.at[which], xb_hbm.at[_rows(0, 2 * tb)], sem.at[which]).wait()

    @pl.when(i >= 2)
    def _():
        drain(cur)

    def group():
        pos = lax.broadcasted_iota(jnp.int32, (2 * tb, tb), 0)
        hit = (pos == local_ref[0:1, :]) | (pos == local_ref[1:2, :])
        perm = jnp.where(hit, 1.0, 0.0).astype(BF16)
        _store_row_tiles(pk_ref.at[cur], 0, jnp.dot(perm, h_ref[...].astype(BF16), preferred_element_type=F32))

    def send(bits):
        def run_copy(src, dst, size):
            return pltpu.make_async_copy(pk_ref.at[1 - cur, _rows(src, size)], xb_hbm.at[_rows(dst, size)],
                                         sem.at[1 - cur])
        _for_each_run_piece(i - 1, len_ref, src_ref, dst_ref, bits, run_copy, lambda c: c.start())

    @pl.when(i == 0)
    def _():
        group()

    @pl.when((i > 0) & (i < nb))
    def _():
        def send_and_group(bits):
            send(bits)
            group()
        _by_run_size(long_ref, i - 1, send_and_group)

    @pl.when(i == nb)
    def _():
        send(RUN_BITS)
        drain(1 - cur)
        each_zero_copy(lambda c: c.wait())


def _dispatch(h, local, runs, pads, n_used, n_blocks):
    n_tok = h.shape[0]
    tb = TOKEN_BLOCK
    nb = n_tok // tb
    grid_spec = pltpu.PrefetchScalarGridSpec(
        num_scalar_prefetch=8,
        grid=(nb + 1,),
        in_specs=[pl.BlockSpec((2, tb), lambda i, *_: (0, jnp.minimum(i, nb - 1))),
                  pl.BlockSpec((tb, D_MODEL), lambda i, *_: (jnp.minimum(i, nb - 1), 0))],
        out_specs=pl.BlockSpec(memory_space=pl.ANY),
        scratch_shapes=[pltpu.VMEM((MOE_BLOCK * ROW_SUB, LANES), F32),
                        pltpu.VMEM((2, 2 * tb * ROW_SUB, LANES), F32),
                        pltpu.SemaphoreType.DMA, pltpu.SemaphoreType.DMA((2,))],
    )
    return pl.pallas_call(
        functools.partial(_dispatch_kernel, n_blocks=n_blocks),
        grid_spec=grid_spec,
        out_shape=jax.ShapeDtypeStruct((n_blocks * MOE_BLOCK * ROW_SUB, LANES), F32),
        compiler_params=pltpu.CompilerParams(dimension_semantics=("arbitrary",),
                                             vmem_limit_bytes=VMEM_LIMIT),
        name="moe_dispatch",
    )(n_used, *pads, *runs, local, h)


def _expert_kernel(be_ref, nu_ref, x_ref, wg_ref, wu_ref, wd_ref, y_ref, wgu_t, wd_t, yt_ref):
    i = pl.program_id(0)
    used = i < nu_ref[0]

    @pl.when(used & ((i == 0) | (be_ref[i] != be_ref[jnp.maximum(i - 1, 0)])))
    def _():
        wgu_t[0:D_EXPERT, :] = wg_ref[...].astype(BF16).T
        wgu_t[D_EXPERT:, :] = wu_ref[...].astype(BF16).T
        wd_t[...] = wd_ref[...].astype(BF16).T

    @pl.when(used)
    def _():
        sub = range(MOE_BLOCK // MOE_SUB)
        gus = []
        for c in sub:
            xb = _load_row_tiles(x_ref, c * MOE_SUB, MOE_SUB).astype(BF16)
            gus.append(lax.dot_general(wgu_t[...], xb, (((1,), (1,)), ((), ())),
                                       preferred_element_type=F32))
        hids = []
        for c in sub:
            gt = gus[c][0:D_EXPERT]
            hids.append((gt * (1.0 / (1.0 + jnp.exp(-gt))) * gus[c][D_EXPERT:]).astype(BF16))
        for c in sub:
            yt_ref[c] = jnp.dot(wd_t[...], hids[c], preferred_element_type=F32)
        for c in sub:
            _store_row_tiles(y_ref, c * MOE_SUB, yt_ref[c].astype(BF16).T.astype(F32))

    @pl.when(jnp.logical_not(used))
    def _():
        y_ref[...] = jnp.zeros_like(y_ref)


def _experts(xb, block_e, n_used, wg, wu, wd):
    n_blocks = xb.shape[0] // (MOE_BLOCK * ROW_SUB)
    last = lambda i, nu: jnp.minimum(i, nu[0] - 1)
    grid_spec = pltpu.PrefetchScalarGridSpec(
        num_scalar_prefetch=2,
        grid=(n_blocks,),
        in_specs=[pl.BlockSpec((MOE_BLOCK * ROW_SUB, LANES), lambda i, be, nu: (last(i, nu), 0)),
                  pl.BlockSpec((None, D_MODEL, D_EXPERT), lambda i, be, nu: (be[last(i, nu)], 0, 0)),
                  pl.BlockSpec((None, D_MODEL, D_EXPERT), lambda i, be, nu: (be[last(i, nu)], 0, 0)),
                  pl.BlockSpec((None, D_EXPERT, D_MODEL), lambda i, be, nu: (be[last(i, nu)], 0, 0))],
        out_specs=pl.BlockSpec((MOE_BLOCK * ROW_SUB, LANES), lambda i, be, nu: (i, 0)),
        scratch_shapes=[pltpu.VMEM((2 * D_EXPERT, D_MODEL), BF16), pltpu.VMEM((D_MODEL, D_EXPERT), BF16),
                        pltpu.VMEM((MOE_BLOCK // MOE_SUB, D_MODEL, MOE_SUB), F32)],
    )
    return pl.pallas_call(
        _expert_kernel,
        grid_spec=grid_spec,
        out_shape=jax.ShapeDtypeStruct(xb.shape, F32),
        compiler_params=pltpu.CompilerParams(dimension_semantics=("arbitrary",),
                                             vmem_limit_bytes=VMEM_LIMIT),
        name="moe_experts",
    )(block_e, n_used, xb, wg, wu, wd)


def _combine_kernel(len_ref, src_ref, dst_ref, long_ref, local_ref, gate_ref, h_ref, g2_ref, b2_ref, yb_hbm,
                    o_ref, buf, sem):
    i = pl.program_id(0)
    n = pl.num_programs(0)
    tb = h_ref.shape[0]
    cur = i % 2

    def fetch(blk, which, bits):
        def run_copy(src, dst, size):
            return pltpu.make_async_copy(yb_hbm.at[_rows(dst, size)], buf.at[which, _rows(src, size)], sem.at[which])
        _for_each_run_piece(blk, len_ref, src_ref, dst_ref, bits, run_copy, lambda c: c.start())

    def drain(which):
        pltpu.make_async_copy(yb_hbm.at[_rows(0, 2 * tb)], buf.at[which], sem.at[which]).wait()

    @pl.when(i == 0)
    def _():
        fetch(i, 0, RUN_BITS)

    drain(cur)
    nxt = jnp.minimum(i + 1, n - 1)

    def fetch_next_and_combine(bits):
        fetch(nxt, 1 - cur, bits)
        y = _load_row_tiles(buf.at[cur], 0, 2 * tb).astype(BF16)
        pos = lax.broadcasted_iota(jnp.int32, (tb, 2 * tb), 1)
        local = local_ref[...]
        gate = gate_ref[...]
        mix = (jnp.where(pos == local[:, 0:1], gate[:, 0:1], 0.0)
               + jnp.where(pos == local[:, 1:2], gate[:, 1:2], 0.0))
        mix_hi = mix.astype(BF16)
        mix_lo = (mix - mix_hi.astype(F32)).astype(BF16)
        moe = (jnp.dot(mix_hi, y, preferred_element_type=F32) + jnp.dot(mix_lo, y, preferred_element_type=F32))
        o_ref[...] = _layer_norm_rows(DEEPNORM_ALPHA * h_ref[...] + moe, g2_ref[...], b2_ref[...])
    _by_run_size(long_ref, nxt, fetch_next_and_combine)

    @pl.when(i == n - 1)
    def _():
        drain(1 - cur)


def _combine(h, yb, local_t, runs, gate, g2_row, b2_row):
    n_tok = h.shape[0]
    tb = TOKEN_BLOCK
    row_spec = lambda w: pl.BlockSpec((tb, w), lambda i, *_: (i, 0))
    full = lambda shape: pl.BlockSpec(shape, lambda i, *_: (0,) * len(shape))
    grid_spec = pltpu.PrefetchScalarGridSpec(
        num_scalar_prefetch=4,
        grid=(n_tok // tb,),
        in_specs=[row_spec(2), row_spec(2), row_spec(D_MODEL), full((1, D_MODEL)), full((1, D_MODEL)),
                  pl.BlockSpec(memory_space=pl.ANY)],
        out_specs=row_spec(D_MODEL),
        scratch_shapes=[pltpu.VMEM((2, 2 * tb * ROW_SUB, LANES), F32), pltpu.SemaphoreType.DMA((2,))],
    )
    return pl.pallas_call(
        _combine_kernel,
        grid_spec=grid_spec,
        out_shape=jax.ShapeDtypeStruct((n_tok, D_MODEL), F32),
        compiler_params=pltpu.CompilerParams(dimension_semantics=("arbitrary",),
                                             vmem_limit_bytes=VMEM_LIMIT),
        name="combine_ln",
    )(*runs, local_t, gate, h, g2_row, b2_row, yb)


def kernel(x, w_in, b_fgate, s5_lambda_re, s5_lambda_im, s5_log_step, s5_b_re, s5_b_im, s5_c_re, s5_c_im, s5_d, w_glu, b_glu, attn_norm_g, ssm_norm_g, w_out, ln1_g, ln1_b, w_router_group, b_router_group, w_router_expert, b_router_expert, w_gate, w_up, w_down, ln2_g, ln2_b):
    bsz, seq, d = x.shape
    n_tok = bsz * seq
    a = ATTN_WIDTH
    for l in range(DEPTH):
        x2 = x.reshape(n_tok, d)
        w = w_in[l]
        f0 = 3 * a
        w_cat = jnp.concatenate(
            [w[:, :a] * (HEAD_DIM ** -0.5 * LOG2E), w[:, a:2 * a], w[:, f0 + ATTN_HEADS:]], axis=1).astype(BF16)
        wvt = jnp.pad(w[:, 2 * a:f0 + ATTN_HEADS].T, ((0, GATE_ROWS - ATTN_HEADS), (0, 0))).astype(BF16)
        q, k, vt, u, eq, ek = _inproj(x2, w_cat, wvt, b_fgate[l].reshape(ATTN_HEADS, 1), seq)
        attn = _attention(q.reshape(bsz, seq, a), k.reshape(bsz, seq, a), vt,
                          eq.reshape(bsz, seq, EXT_W), ek.reshape(bsz, seq, EXT_W))

        bw, cw, ar_s, ai_s = _s5_weights(s5_lambda_re[l], s5_lambda_im[l], s5_log_step[l],
                                         s5_b_re[l], s5_b_im[l], s5_c_re[l], s5_c_im[l])
        u_tb = u.transpose(1, 0, 2)
        ssm_tb = _s5(u_tb, bw, cw, ar_s, ai_s, s5_d[l].reshape(1, SSM_WIDTH), w_glu[l].astype(BF16),
                     b_glu[l].reshape(1, SSM_WIDTH), ssm_norm_g[l].reshape(1, SSM_WIDTH))
        ssm2 = ssm_tb.transpose(1, 0, 2).reshape(n_tok, SSM_WIDTH)

        n_route = N_EXPERT_GROUPS + N_EXPERTS
        wrt = jnp.pad(jnp.concatenate([w_router_group[l], w_router_expert[l]], axis=1).T,
                      ((0, LANES - n_route), (0, 0))).astype(BF16)
        brt = jnp.pad(jnp.concatenate([b_router_group[l], b_router_expert[l]]),
                      (0, ROUTER_ROWS - n_route)).reshape(ROUTER_ROWS, 1)
        h, eid, gate, rank, cnt_col = _outproj(
            attn.reshape(n_tok, a), ssm2, x2, w_out[l].astype(BF16), attn_norm_g[l].reshape(1, a),
            ln1_g[l].reshape(1, d), ln1_b[l].reshape(1, d), wrt, brt)

        local, runs, block_e, n_used, pads, n_blocks = _dispatch_plan(eid, rank, cnt_col, n_tok)
        xb = _dispatch(h, local, runs, pads, n_used, n_blocks)
        yb = _experts(xb, block_e, n_used, w_gate[l], w_up[l], w_down[l])
        x = _combine(h, yb, local.T, runs, gate.T, ln2_g[l].reshape(1, d),
                     ln2_b[l].reshape(1, d)).reshape(bsz, seq, d)
    return x
```

```python
import functools
import math

import jax
import jax.numpy as jnp
import numpy as np
from jax import lax
from jax.experimental import pallas as pl
from jax.experimental.pallas import tpu as pltpu

F32 = jnp.float32
BF16 = jnp.bfloat16

D_MODEL = 1024
HEAD_DIM = 64
ATTN_WIDTH = 512
ATTN_HEADS = 8
SSM_WIDTH = 512
SSM_GROUP = 16
SSM_GROUPS = 32
SSM_STATE = 64
N_EXPERT_GROUPS = 4
EXPERTS_PER_GROUP = 8
N_EXPERTS = 32
D_EXPERT = 512
MOE_BLOCK = 512
MOE_SUB = 256
TOKEN_BLOCK = 512
RUN_BITS = TOKEN_BLOCK.bit_length()
SHORT_RUN_BITS = 6
DEPTH = 1
DEEPNORM_ALPHA = (2.0 * DEPTH) ** 0.25
LN_EPS = 1e-5
RMS_EPS = 1e-6

LANES = 128
NEG_BIG = -1e30
LOG2E = math.log2(math.e)
ONES_ROWS = 16
VT_ROWS = ATTN_HEADS * (HEAD_DIM + ONES_ROWS)
GATE_ROWS = 16
PAIR_BIAS_LANES = 8
EXT_W = ATTN_HEADS // 2 * LANES
VMEM_LIMIT = 48 * 1024 * 1024

ROW_BLOCK = 1024
ATTN_BLOCK = 256
ATTN_HEADS_PER_STEP = 8
S5_TIME_BLOCK = 64
S5_SLABS = 4
ROUTER_ROWS = 40
SLAB_CH = SSM_WIDTH // S5_SLABS
SLAB_STATE = SSM_GROUPS // S5_SLABS * SSM_STATE


def _split3(c):
    hi = c.astype(BF16)
    r1 = c - hi.astype(F32)
    mid = r1.astype(BF16)
    lo = (r1 - mid.astype(F32)).astype(BF16)
    return hi, mid, lo


def _inproj_kernel(x_ref, w_ref, wvt_ref, bf_ref, triu_ref, sel_ref, q_ref, k_ref, vt_ref, u_ref, eq_ref, ek_ref,
                   carry_ref, *, blocks_per_seq):
    i = pl.program_id(0)

    @pl.when(i % blocks_per_seq == 0)
    def _():
        carry_ref[...] = jnp.zeros_like(carry_ref)

    tm = x_ref.shape[0]
    xb = x_ref[...].astype(BF16)
    z = jnp.dot(xb, w_ref[...], preferred_element_type=F32)
    a = ATTN_WIDTH
    q_ref[...] = z[:, :a].astype(BF16)
    k_ref[...] = z[:, a:2 * a].astype(BF16)
    u_ref[...] = z[:, 2 * a:2 * a + SSM_WIDTH].astype(BF16)
    vtf = lax.dot_general(wvt_ref[...], xb, (((1,), (1,)), ((), ())), preferred_element_type=F32)
    vt = vtf[:ATTN_WIDTH].astype(BF16)
    ones_rows = jnp.ones((ONES_ROWS, tm), BF16)
    vt = jnp.concatenate([piece for h in range(ATTN_HEADS)
                          for piece in (vt[HEAD_DIM * h:HEAD_DIM * (h + 1)], ones_rows)], axis=0)
    for jj in range(tm // ATTN_BLOCK):
        vt_ref[jj] = vt[:, jj * ATTN_BLOCK:(jj + 1) * ATTN_BLOCK]

    f = vtf[ATTN_WIDTH:ATTN_WIDTH + ATTN_HEADS] + bf_ref[...]
    logf = jnp.minimum(f, 0.0) - jnp.log1p(jnp.exp(-jnp.abs(f)))
    parts = jnp.dot(jnp.concatenate(_split3(logf), axis=0), triu_ref[...], preferred_element_type=F32)
    nh = ATTN_HEADS
    cum = parts[0:nh] + parts[nh:2 * nh] + parts[2 * nh:3 * nh] + carry_ref[...]
    carry_ref[...] = cum[:, tm - 1:tm]
    hi, mid, lo = (t.astype(F32) for t in _split3(cum * LOG2E))
    one_row = jnp.where(lax.broadcasted_iota(jnp.int32, (nh, tm), 0) == 0, 1.0, 0.0)
    packed_t = jnp.concatenate([hi, mid, lo, one_row, jnp.zeros((LANES - 4 * nh, tm), F32)], axis=0)
    ext = jnp.dot(packed_t.T.astype(BF16), sel_ref[...], preferred_element_type=F32).astype(BF16)
    eq_ref[...] = ext[:, :EXT_W]
    ek_ref[...] = ext[:, EXT_W:]


def _bias_lane_selector():
    sel = np.zeros((LANES, 2, ATTN_HEADS // 2, LANES), np.float32)
    one = 3 * ATTN_HEADS
    for h in range(ATTN_HEADS):
        pair, base = h // 2, PAIR_BIAS_LANES * (h % 2)
        for t in range(3):
            sel[ATTN_HEADS * t + h, 0, pair, base + t] = 1.0
            sel[one, 0, pair, base + 3 + t] = 1.0
            sel[one, 1, pair, base + t] = 1.0
            sel[ATTN_HEADS * t + h, 1, pair, base + 3 + t] = -1.0
    return jnp.asarray(sel.reshape(LANES, 2 * EXT_W), BF16)


def _inproj(x2, w_cat, wvt, bf_col, seq):
    n_tok = x2.shape[0]
    tm = ROW_BLOCK
    assert seq % tm == 0 and n_tok % tm == 0 and tm % ATTN_BLOCK == 0
    bsz = n_tok // seq
    bps = seq // tm
    ncol = w_cat.shape[1]
    vblk = tm // ATTN_BLOCK
    row_spec = lambda w: pl.BlockSpec((tm, w), lambda i: (i, 0))
    return pl.pallas_call(
        functools.partial(_inproj_kernel, blocks_per_seq=bps),
        grid=(n_tok // tm,),
        in_specs=[row_spec(D_MODEL),
                  pl.BlockSpec((D_MODEL, ncol), lambda i: (0, 0)),
                  pl.BlockSpec((ATTN_WIDTH + GATE_ROWS, D_MODEL), lambda i: (0, 0)),
                  pl.BlockSpec((ATTN_HEADS, 1), lambda i: (0, 0)),
                  pl.BlockSpec((tm, tm), lambda i: (0, 0)),
                  pl.BlockSpec((LANES, 2 * EXT_W), lambda i: (0, 0))],
        out_specs=[row_spec(ATTN_WIDTH), row_spec(ATTN_WIDTH),
                   pl.BlockSpec((None, vblk, VT_ROWS, ATTN_BLOCK), lambda i: (i // bps, i % bps, 0, 0)),
                   pl.BlockSpec((tm, SSM_WIDTH), lambda i: (i % bps, i // bps)),
                   row_spec(EXT_W), row_spec(EXT_W)],
        out_shape=[jax.ShapeDtypeStruct((n_tok, ATTN_WIDTH), BF16)] * 2
                  + [jax.ShapeDtypeStruct((bsz, seq // ATTN_BLOCK, VT_ROWS, ATTN_BLOCK), BF16),
                     jax.ShapeDtypeStruct((seq, bsz * SSM_WIDTH), BF16),
                     jax.ShapeDtypeStruct((n_tok, EXT_W), BF16),
                     jax.ShapeDtypeStruct((n_tok, EXT_W), BF16)],
        scratch_shapes=[pltpu.VMEM((ATTN_HEADS, 1), F32)],
        compiler_params=pltpu.CompilerParams(dimension_semantics=("arbitrary",),
                                             vmem_limit_bytes=VMEM_LIMIT),
        name="inproj",
    )(x2, w_cat, wvt, bf_col, jnp.triu(jnp.ones((tm, tm), BF16)), _bias_lane_selector())


def _attn_kernel(q_ref, k_ref, vt_ref, eq_ref, ek_ref, o_ref, *, tq, seq, nh):
    qi = pl.program_id(2)
    lane = lax.broadcasted_iota(jnp.int32, (tq, LANES), 1)
    q_augs = []
    for h in range(nh):
        qb = q_ref[:, LANES * (h // 2):LANES * (h // 2 + 1)]
        head_lanes = (lane < HEAD_DIM) if h % 2 == 0 else (lane >= HEAD_DIM)
        qh = jnp.where(head_lanes, qb, jnp.zeros((), BF16))
        own = (lane >= PAIR_BIAS_LANES * (h % 2)) & (lane < PAIR_BIAS_LANES * (h % 2 + 1))
        eqh = jnp.where(own, eq_ref[:, LANES * (h // 2):LANES * (h // 2 + 1)], jnp.zeros((), BF16))
        q_augs.append(jnp.concatenate([qh, eqh], axis=1))
    key_i = lax.broadcasted_iota(jnp.int32, (tq, tq), 0)
    qry_i = lax.broadcasted_iota(jnp.int32, (tq, tq), 1)
    causal = key_i <= qry_i
    vrows = HEAD_DIM + ONES_ROWS

    def scores_of(k0, nk):
        out = []
        for h in range(nh):
            kj = k_ref[pl.ds(k0, nk), LANES * (h // 2):LANES * (h // 2 + 1)]
            k_aug = jnp.concatenate([kj, ek_ref[pl.ds(k0, nk), LANES * (h // 2):LANES * (h // 2 + 1)]], axis=1)
            out.append(lax.dot_general(k_aug, q_augs[h], (((1,), (1,)), ((), ())), preferred_element_type=F32))
        return tuple(out)

    def absorb(j0, scores, carry):
        stats = []
        for h in range(nh):
            m = carry[h][0]
            m_new = jnp.maximum(m, jnp.max(scores[h], axis=0, keepdims=True))
            stats.append((m_new, jnp.exp2(m - m_new), jnp.exp2(scores[h] - m_new).astype(BF16)))
        new = []
        for h in range(nh):
            m_new, corr, p = stats[h]
            res = None
            for c in range(p.shape[0] // tq):
                part = jnp.dot(vt_ref[j0 + c, vrows * h:vrows * (h + 1), :], p[c * tq:(c + 1) * tq],
                               preferred_element_type=F32)
                res = part if res is None else res + part
            acc = corr * carry[h][2] + res[0:HEAD_DIM, :]
            l = corr * carry[h][1] + res[HEAD_DIM:HEAD_DIM + 1, :]
            new.append((m_new, l, acc))
        return tuple(new)

    def pair_step(jj, carry):
        return absorb(2 * jj, scores_of(pl.multiple_of(jj * (2 * tq), 2 * tq), 2 * tq), carry)

    def finish(carry):
        out_t = jnp.concatenate([c[2] / c[1] for c in carry], axis=0)
        o_ref[...] = out_t.T

    init1 = (jnp.full((1, tq), NEG_BIG, F32), jnp.zeros((1, tq), F32), jnp.zeros((HEAD_DIM, tq), F32))
    carry = lax.fori_loop(0, qi // 2, pair_step, (init1,) * nh)

    @pl.when(qi % 2 == 0)
    def _():
        diag = scores_of(pl.multiple_of(qi * tq, tq), tq)
        finish(absorb(qi, tuple(jnp.where(causal, s, NEG_BIG) for s in diag), carry))

    @pl.when(qi % 2 == 1)
    def _():
        last = scores_of(pl.multiple_of((qi - 1) * tq, 2 * tq), 2 * tq)
        causal2 = (lax.broadcasted_iota(jnp.int32, (2 * tq, tq), 0)
                   <= lax.broadcasted_iota(jnp.int32, (2 * tq, tq), 1) + tq)
        finish(absorb(qi - 1, tuple(jnp.where(causal2, s, NEG_BIG) for s in last), carry))


def _attention(q, k, vt, eq, ek):
    bsz, seq, _ = q.shape
    tq = ATTN_BLOCK
    nh = ATTN_HEADS_PER_STEP
    assert seq % tq == 0 and ATTN_HEADS % nh == 0 and nh % 2 == 0
    w = nh * HEAD_DIM
    return pl.pallas_call(
        functools.partial(_attn_kernel, tq=tq, seq=seq, nh=nh),
        grid=(bsz, ATTN_HEADS // nh, seq // tq),
        in_specs=[pl.BlockSpec((None, tq, w), lambda b, g, i: (b, i, g)),
                  pl.BlockSpec((None, seq, w), lambda b, g, i: (b, 0, g)),
                  pl.BlockSpec((None, seq // tq, nh * (HEAD_DIM + ONES_ROWS), tq), lambda b, g, i: (b, 0, g, 0)),
                  pl.BlockSpec((None, tq, nh // 2 * LANES), lambda b, g, i: (b, i, g)),
                  pl.BlockSpec((None, seq, nh // 2 * LANES), lambda b, g, i: (b, 0, g))],
        out_specs=pl.BlockSpec((None, tq, w), lambda b, g, i: (b, i, g)),
        out_shape=jax.ShapeDtypeStruct((bsz, seq, ATTN_WIDTH), F32),
        compiler_params=pltpu.CompilerParams(dimension_semantics=("parallel", "parallel", "arbitrary"),
                                             vmem_limit_bytes=VMEM_LIMIT),
        name="fox_attention",
    )(q, k, vt, eq, ek)


def _s5_kernel(u_ref, bw_ref, cw_ref, ar_ref, ai_ref, d_ref, wglu_ref, bglu_ref, g_ref, o_ref,
               state_ref, bu_ref, *, tc, bsz):
    @pl.when(pl.program_id(0) == 0)
    def _():
        state_ref[...] = jnp.zeros_like(state_ref)

    rows = tc * bsz
    ns = SLAB_STATE
    u = u_ref[...].reshape(rows, SSM_WIDTH)
    ub = u
    for i in range(S5_SLABS):
        bu_ref[i] = jnp.dot(ub[:, SLAB_CH * i:SLAB_CH * (i + 1)], bw_ref[i], preferred_element_type=F32)
    for i in range(S5_SLABS):
        ar = jnp.broadcast_to(ar_ref[i], (bsz, ns))
        ai = jnp.broadcast_to(ai_ref[i], (bsz, ns))
        xr = state_ref[i, :, 0:ns]
        xi = state_ref[i, :, ns:2 * ns]
        for t in range(tc):
            r0 = t * bsz
            nr = ar * xr - ai * xi + bu_ref[i, r0:r0 + bsz, 0:ns]
            ni = ar * xi + ai * xr + bu_ref[i, r0:r0 + bsz, ns:2 * ns]
            bu_ref[i, r0:r0 + bsz, 0:ns] = nr
            bu_ref[i, r0:r0 + bsz, ns:2 * ns] = ni
            xr, xi = nr, ni
        state_ref[i, :, 0:ns] = xr
        state_ref[i, :, ns:2 * ns] = xi
    ys = [jnp.dot(bu_ref[i].astype(BF16), cw_ref[i], preferred_element_type=F32) for i in range(S5_SLABS)]
    y = jnp.concatenate(ys, axis=1) + d_ref[...] * u.astype(F32)
    y = 0.5 * y * (1.0 + jnp.tanh(math.sqrt(2.0 / math.pi) * (y + 0.044715 * (y * y * y))))
    z = jnp.dot(y.astype(BF16), wglu_ref[...], preferred_element_type=F32) + bglu_ref[...]
    y = y * (1.0 / (1.0 + jnp.exp(-z)))
    y = y * lax.rsqrt(jnp.mean(y * y, axis=-1, keepdims=True) + RMS_EPS) * g_ref[...]
    o_ref[...] = y.astype(BF16).reshape(tc, bsz * SSM_WIDTH)


def _s5_weights(lam_re, lam_im, log_step, b_re, b_im, c_re, c_im):
    delta = jnp.exp(log_step)[:, None]
    mag = jnp.exp(lam_re * delta)
    ar = mag * jnp.cos(lam_im * delta)
    ai = mag * jnp.sin(lam_im * delta)
    den = lam_re * lam_re + lam_im * lam_im
    num_re = ar - 1.0
    coef_re = (num_re * lam_re + ai * lam_im) / den
    coef_im = (ai * lam_re - num_re * lam_im) / den
    bbar_re = coef_re[..., None] * b_re - coef_im[..., None] * b_im
    bbar_im = coef_re[..., None] * b_im + coef_im[..., None] * b_re
    gs = SSM_GROUPS // S5_SLABS
    eye = jnp.eye(gs, dtype=F32)

    def b_slab(bb):
        bb = bb.reshape(S5_SLABS, gs, SSM_STATE, SSM_GROUP)
        w = jnp.einsum('ab,sapc->sacbp', eye, bb)
        return w.reshape(S5_SLABS, gs * SSM_GROUP, gs * SSM_STATE)

    def c_slab(cc):
        cc = cc.reshape(S5_SLABS, gs, SSM_GROUP, SSM_STATE)
        w = jnp.einsum('ab,sacp->sbpac', eye, cc)
        return w.reshape(S5_SLABS, gs * SSM_STATE, gs * SSM_GROUP)

    bw = jnp.concatenate([b_slab(bbar_re), b_slab(bbar_im)], axis=2).astype(BF16)
    cw = jnp.concatenate([c_slab(c_re), -c_slab(c_im)], axis=1).astype(BF16)
    ar_s = ar.reshape(S5_SLABS, 1, SLAB_STATE)
    ai_s = ai.reshape(S5_SLABS, 1, SLAB_STATE)
    return bw, cw, ar_s, ai_s


def _s5(u_tb, bsz, bw, cw, ar_s, ai_s, d_row, w_glu, b_glu_row, g_row):
    seq = u_tb.shape[0]
    tc = S5_TIME_BLOCK
    assert seq % tc == 0 and bsz % 16 == 0
    full = lambda shape: pl.BlockSpec(shape, lambda t: (0,) * len(shape))
    return pl.pallas_call(
        functools.partial(_s5_kernel, tc=tc, bsz=bsz),
        grid=(seq // tc,),
        in_specs=[pl.BlockSpec((tc, bsz * SSM_WIDTH), lambda t: (t, 0)),
                  full(bw.shape), full(cw.shape), full(ar_s.shape), full(ai_s.shape),
                  full((1, SSM_WIDTH)), full((SSM_WIDTH, SSM_WIDTH)), full((1, SSM_WIDTH)),
                  full((1, SSM_WIDTH))],
        out_specs=pl.BlockSpec((tc, bsz * SSM_WIDTH), lambda t: (t, 0)),
        out_shape=jax.ShapeDtypeStruct((seq, bsz * SSM_WIDTH), BF16),
        scratch_shapes=[pltpu.VMEM((S5_SLABS, bsz, 2 * SLAB_STATE), F32),
                        pltpu.VMEM((S5_SLABS, tc * bsz, 2 * SLAB_STATE), F32)],
        compiler_params=pltpu.CompilerParams(dimension_semantics=("arbitrary",),
                                             vmem_limit_bytes=VMEM_LIMIT),
        name="s5_branch",
    )(u_tb, bw, cw, ar_s, ai_s, d_row, w_glu, b_glu_row, g_row)


def _layer_norm_rows(v, g, b):
    mu = jnp.mean(v, axis=-1, keepdims=True)
    c = v - mu
    var = jnp.mean(c * c, axis=-1, keepdims=True)
    return c * lax.rsqrt(var + LN_EPS) * g + b


def _outproj_kernel(attn_ref, ssm_ref, x_ref, wo_ref, ag_ref, g1_ref, b1_ref, wrt_ref, brt_ref, triu_ref,
                    h_ref, eid_ref, gate_ref, rank_ref, cnt_ref):
    a = attn_ref[...]
    a = a * lax.rsqrt(jnp.mean(a * a, axis=-1, keepdims=True) + RMS_EPS) * ag_ref[...]
    mix = (jnp.dot(a.astype(BF16), wo_ref[0:ATTN_WIDTH, :], preferred_element_type=F32)
           + jnp.dot(ssm_ref[...], wo_ref[ATTN_WIDTH:, :], preferred_element_type=F32))
    h = _layer_norm_rows(DEEPNORM_ALPHA * x_ref[...] + mix, g1_ref[...], b1_ref[...])
    h_ref[...] = h

    logits = lax.dot_general(wrt_ref[...], h.astype(BF16), (((1,), (1,)), ((), ())),
                             preferred_element_type=F32)[:ROUTER_ROWS] + brt_ref[...]
    tm = logits.shape[1]
    row = lax.broadcasted_iota(jnp.int32, (ROUTER_ROWS, tm), 0).astype(F32)
    no_row = float(LANES)
    is_g = row < N_EXPERT_GROUPS
    gl = jnp.where(is_g, logits, NEG_BIG)
    gmax = jnp.max(gl, axis=0, keepdims=True)
    g_idx = jnp.min(jnp.where(is_g & (gl == gmax), row, no_row), axis=0, keepdims=True)
    g_p = 1.0 / jnp.sum(jnp.where(is_g, jnp.exp(gl - gmax), 0.0), axis=0, keepdims=True)
    first = N_EXPERT_GROUPS + g_idx * EXPERTS_PER_GROUP
    in_grp = (row >= first) & (row < first + EXPERTS_PER_GROUP)
    el = jnp.where(in_grp, logits, NEG_BIG)
    m1 = jnp.max(el, axis=0, keepdims=True)
    i1 = jnp.min(jnp.where(in_grp & (el == m1), row, no_row), axis=0, keepdims=True)
    rest = in_grp & (row != i1)
    el2 = jnp.where(rest, logits, NEG_BIG)
    m2 = jnp.max(el2, axis=0, keepdims=True)
    i2 = jnp.min(jnp.where(rest & (el2 == m2), row, no_row), axis=0, keepdims=True)
    r = jnp.exp(m2 - m1)
    p1 = 1.0 / (1.0 + r)
    p2 = r / (1.0 + r)
    k2 = lax.broadcasted_iota(jnp.int32, (2, tm), 0)
    eid_ref[...] = (jnp.where(k2 == 0, i1, i2) - N_EXPERT_GROUPS).astype(jnp.int32)
    gate_ref[...] = g_p * jnp.where(k2 == 0, p1, p2)

    @pl.when(pl.program_id(0) == 0)
    def _():
        cnt_ref[...] = jnp.zeros_like(cnt_ref)
    hit1 = row == i1
    hit2 = row == i2
    onehot = jnp.where(hit1 | hit2, 1.0, 0.0)
    cnt = cnt_ref[...]
    before = jnp.dot(onehot.astype(BF16), triu_ref[...], preferred_element_type=F32) + cnt
    r1 = jnp.sum(jnp.where(hit1, before, 0.0), axis=0, keepdims=True)
    r2 = jnp.sum(jnp.where(hit2, before, 0.0), axis=0, keepdims=True)
    rank_ref[...] = jnp.where(k2 == 0, r1, r2).astype(jnp.int32)
    cnt_ref[...] = cnt + jnp.sum(onehot, axis=1, keepdims=True)


def _outproj(attn2, ssm_tb, x2, wo, ag_row, g1_row, b1_row, wrt, brt):
    n_tok = x2.shape[0]
    tm = ROW_BLOCK
    bps = ssm_tb.shape[0] // tm
    triu = jnp.triu(jnp.ones((tm, tm), BF16), 1)
    row_spec = lambda w: pl.BlockSpec((tm, w), lambda i: (i, 0))
    col_spec = lambda r: pl.BlockSpec((r, tm), lambda i: (0, i))
    full = lambda shape: pl.BlockSpec(shape, lambda i: (0,) * len(shape))
    return pl.pallas_call(
        _outproj_kernel,
        grid=(n_tok // tm,),
        in_specs=[row_spec(ATTN_WIDTH), pl.BlockSpec((tm, SSM_WIDTH), lambda i: (i % bps, i // bps)),
                  row_spec(D_MODEL),
                  full((D_MODEL, D_MODEL)), full((1, ATTN_WIDTH)), full((1, D_MODEL)), full((1, D_MODEL)),
                  full((LANES, D_MODEL)), full((ROUTER_ROWS, 1)), full((tm, tm))],
        out_specs=[row_spec(D_MODEL), col_spec(2), col_spec(2), col_spec(2), full((ROUTER_ROWS, 1))],
        out_shape=[jax.ShapeDtypeStruct((n_tok, D_MODEL), F32),
                   jax.ShapeDtypeStruct((2, n_tok), jnp.int32),
                   jax.ShapeDtypeStruct((2, n_tok), F32),
                   jax.ShapeDtypeStruct((2, n_tok), jnp.int32),
                   jax.ShapeDtypeStruct((ROUTER_ROWS, 1), F32)],
        compiler_params=pltpu.CompilerParams(dimension_semantics=("arbitrary",),
                                             vmem_limit_bytes=VMEM_LIMIT),
        name="outproj_ln_router",
    )(attn2, ssm_tb, x2, wo, ag_row, g1_row, b1_row, wrt, brt, triu)


def _dispatch_plan(eid, rank, cnt_col, n_tok):
    counts = cnt_col[N_EXPERT_GROUPS:N_EXPERT_GROUPS + N_EXPERTS, 0].astype(jnp.int32)
    padded = ((counts + MOE_BLOCK - 1) // MOE_BLOCK) * MOE_BLOCK
    pends = jnp.cumsum(padded)
    pstarts = pends - padded
    n_pad = ((2 * n_tok + N_EXPERTS * (MOE_BLOCK - 1) + MOE_BLOCK - 1) // MOE_BLOCK) * MOE_BLOCK
    n_blocks = n_pad // MOE_BLOCK
    blk0 = jnp.arange(n_blocks, dtype=jnp.int32) * MOE_BLOCK
    block_e = jnp.minimum(jnp.sum((pends[None, :] <= blk0[:, None]).astype(jnp.int32), axis=1), N_EXPERTS - 1)
    n_used = (pends[-1] // MOE_BLOCK).astype(jnp.int32).reshape(1)
    pads = (padded - counts, jnp.zeros_like(counts), pstarts + counts)

    tb = TOKEN_BLOCK
    nb = n_tok // tb
    onehot = eid[:, :, None] == jnp.arange(N_EXPERTS, dtype=jnp.int32)[None, None, :]
    run_len = jnp.sum(onehot.reshape(2, nb, tb, N_EXPERTS).astype(jnp.int32), axis=(0, 2))
    before = jnp.cumsum(run_len, axis=0) - run_len
    run_src = jnp.cumsum(run_len, axis=1) - run_len
    run_dst = pstarts[None, :] + before
    shift = jnp.broadcast_to((run_src - before)[:, None, :], (nb, tb, N_EXPERTS)).reshape(n_tok, N_EXPERTS)
    local = rank + jnp.sum(jnp.where(onehot, shift[None], 0), axis=-1)
    any_long = (jnp.max(run_len, axis=1) >= (1 << SHORT_RUN_BITS)).astype(jnp.int32)
    runs = (run_len.reshape(-1), run_src.reshape(-1), run_dst.reshape(-1), any_long)
    return local, runs, block_e, n_used, pads, n_blocks


ROW_SUB = D_MODEL // LANES


def _rows(start, size):
    return pl.ds(pl.multiple_of(start * ROW_SUB, ROW_SUB), size * ROW_SUB)


def _store_row_tiles(ref, row0, v):
    n = v.shape[0]
    for s in range(ROW_SUB):
        ref[pl.ds(row0 * ROW_SUB + s, n, stride=ROW_SUB), :] = v[:, LANES * s:LANES * (s + 1)]


def _load_row_tiles(ref, row0, n):
    return jnp.concatenate([ref[pl.ds(row0 * ROW_SUB + s, n, stride=ROW_SUB), :] for s in range(ROW_SUB)], axis=1)


def _for_each_run_piece(blk, len_ref, src_ref, dst_ref, bits, make_copy, act):
    for e in range(N_EXPERTS):
        idx = blk * N_EXPERTS + e
        n = len_ref[idx]
        src = src_ref[idx]
        dst = dst_ref[idx]
        for b in reversed(range(bits)):
            @pl.when((n & (1 << b)) != 0)
            def _():
                off = lax.shift_left(lax.shift_right_logical(n, b + 1), b + 1)
                act(make_copy(src + off, dst + off, 1 << b))


def _by_run_size(long_ref, blk, body):
    @pl.when(long_ref[blk] == 0)
    def _():
        body(SHORT_RUN_BITS)

    @pl.when(long_ref[blk] != 0)
    def _():
        body(RUN_BITS)


def _dispatch_kernel(nu_ref, plen_ref, psrc_ref, pdst_ref, len_ref, src_ref, dst_ref, long_ref, local_ref, h_ref,
                     xb_hbm, zbuf, pk_ref, sem_z, sem, *, n_blocks):
    i = pl.program_id(0)
    nb = pl.num_programs(0) - 1
    tb = h_ref.shape[0]
    cur = i % 2

    def each_zero_copy(act):
        def zero_copy(src, dst, size):
            return pltpu.make_async_copy(zbuf.at[_rows(src, size)], xb_hbm.at[_rows(dst, size)], sem_z)
        _for_each_run_piece(0, plen_ref, psrc_ref, pdst_ref, MOE_BLOCK.bit_length() - 1, zero_copy, act)

        def unused(b, carry):
            act(zero_copy(0, b * MOE_BLOCK, MOE_BLOCK))
            return carry
        lax.fori_loop(nu_ref[0], n_blocks, unused, 0)

    @pl.when(i == 0)
    def _():
        zbuf[...] = jnp.zeros_like(zbuf)
        each_zero_copy(lambda c: c.start())

    def drain(which):
        pltpu.make_async_copy(pk_ref.at[which], xb_hbm.at[_rows(0, 2 * tb)], sem.at[which]).wait()

    @pl.when(i >= 2)
    def _():
        drain(cur)

    def group():
        pos = lax.broadcasted_iota(jnp.int32, (2 * tb, tb), 0)
        hit = (pos == local_ref[0:1, :]) | (pos == local_ref[1:2, :])
        perm = jnp.where(hit, 1.0, 0.0).astype(BF16)
        _store_row_tiles(pk_ref.at[cur], 0, jnp.dot(perm, h_ref[...].astype(BF16), preferred_element_type=F32))

    def send(bits):
        def run_copy(src, dst, size):
            return pltpu.make_async_copy(pk_ref.at[1 - cur, _rows(src, size)], xb_hbm.at[_rows(dst, size)],
                                         sem.at[1 - cur])
        _for_each_run_piece(i - 1, len_ref, src_ref, dst_ref, bits, run_copy, lambda c: c.start())

    @pl.when(i == 0)
    def _():
        group()

    @pl.when((i > 0) & (i < nb))
    def _():
        def send_and_group(bits):
            send(bits)
            group()
        _by_run_size(long_ref, i - 1, send_and_group)

    @pl.when(i == nb)
    def _():
        send(RUN_BITS)
        drain(1 - cur)
        each_zero_copy(lambda c: c.wait())


def _dispatch(h, local, runs, pads, n_used, n_blocks):
    n_tok = h.shape[0]
    tb = TOKEN_BLOCK
    nb = n_tok // tb
    grid_spec = pltpu.PrefetchScalarGridSpec(
        num_scalar_prefetch=8,
        grid=(nb + 1,),
        in_specs=[pl.BlockSpec((2, tb), lambda i, *_: (0, jnp.minimum(i, nb - 1))),
                  pl.BlockSpec((tb, D_MODEL), lambda i, *_: (jnp.minimum(i, nb - 1), 0))],
        out_specs=pl.BlockSpec(memory_space=pl.ANY),
        scratch_shapes=[pltpu.VMEM((MOE_BLOCK * ROW_SUB, LANES), F32),
                        pltpu.VMEM((2, 2 * tb * ROW_SUB, LANES), F32),
                        pltpu.SemaphoreType.DMA, pltpu.SemaphoreType.DMA((2,))],
    )
    return pl.pallas_call(
        functools.partial(_dispatch_kernel, n_blocks=n_blocks),
        grid_spec=grid_spec,
        out_shape=jax.ShapeDtypeStruct((n_blocks * MOE_BLOCK * ROW_SUB, LANES), F32),
        compiler_params=pltpu.CompilerParams(dimension_semantics=("arbitrary",),
                                             vmem_limit_bytes=VMEM_LIMIT),
        name="moe_dispatch",
    )(n_used, *pads, *runs, local, h)


def _expert_kernel(be_ref, nu_ref, x_ref, wg_ref, wu_ref, wd_ref, y_ref, wgu_t, wd_t, yt_ref):
    i = pl.program_id(0)
    used = i < nu_ref[0]

    @pl.when(used & ((i == 0) | (be_ref[i] != be_ref[jnp.maximum(i - 1, 0)])))
    def _():
        wgu_t[0:D_EXPERT, :] = wg_ref[...].astype(BF16).T
        wgu_t[D_EXPERT:, :] = wu_ref[...].astype(BF16).T
        wd_t[...] = wd_ref[...].astype(BF16).T

    @pl.when(used)
    def _():
        sub = range(MOE_BLOCK // MOE_SUB)
        gus = []
        for c in sub:
            xb = _load_row_tiles(x_ref, c * MOE_SUB, MOE_SUB).astype(BF16)
            gus.append(lax.dot_general(wgu_t[...], xb, (((1,), (1,)), ((), ())),
                                       preferred_element_type=F32))
        hids = []
        for c in sub:
            gt = gus[c][0:D_EXPERT]
            hids.append((gt * (1.0 / (1.0 + jnp.exp(-gt))) * gus[c][D_EXPERT:]).astype(BF16))
        for c in sub:
            yt_ref[c] = jnp.dot(wd_t[...], hids[c], preferred_element_type=F32)
        for c in sub:
            _store_row_tiles(y_ref, c * MOE_SUB, yt_ref[c].astype(BF16).T.astype(F32))

    @pl.when(jnp.logical_not(used))
    def _():
        y_ref[...] = jnp.zeros_like(y_ref)


def _experts(xb, block_e, n_used, wg, wu, wd):
    n_blocks = xb.shape[0] // (MOE_BLOCK * ROW_SUB)
    last = lambda i, nu: jnp.minimum(i, nu[0] - 1)
    grid_spec = pltpu.PrefetchScalarGridSpec(
        num_scalar_prefetch=2,
        grid=(n_blocks,),
        in_specs=[pl.BlockSpec((MOE_BLOCK * ROW_SUB, LANES), lambda i, be, nu: (last(i, nu), 0)),
                  pl.BlockSpec((None, D_MODEL, D_EXPERT), lambda i, be, nu: (be[last(i, nu)], 0, 0)),
                  pl.BlockSpec((None, D_MODEL, D_EXPERT), lambda i, be, nu: (be[last(i, nu)], 0, 0)),
                  pl.BlockSpec((None, D_EXPERT, D_MODEL), lambda i, be, nu: (be[last(i, nu)], 0, 0))],
        out_specs=pl.BlockSpec((MOE_BLOCK * ROW_SUB, LANES), lambda i, be, nu: (i, 0)),
        scratch_shapes=[pltpu.VMEM((2 * D_EXPERT, D_MODEL), BF16), pltpu.VMEM((D_MODEL, D_EXPERT), BF16),
                        pltpu.VMEM((MOE_BLOCK // MOE_SUB, D_MODEL, MOE_SUB), F32)],
    )
    return pl.pallas_call(
        _expert_kernel,
        grid_spec=grid_spec,
        out_shape=jax.ShapeDtypeStruct(xb.shape, F32),
        compiler_params=pltpu.CompilerParams(dimension_semantics=("arbitrary",),
                                             vmem_limit_bytes=VMEM_LIMIT),
        name="moe_experts",
    )(block_e, n_used, xb, wg, wu, wd)


def _combine_kernel(len_ref, src_ref, dst_ref, long_ref, local_ref, gate_ref, h_ref, g2_ref, b2_ref, yb_hbm,
                    o_ref, buf, sem):
    i = pl.program_id(0)
    n = pl.num_programs(0)
    tb = h_ref.shape[0]
    cur = i % 2

    def fetch(blk, which, bits):
        def run_copy(src, dst, size):
            return pltpu.make_async_copy(yb_hbm.at[_rows(dst, size)], buf.at[which, _rows(src, size)], sem.at[which])
        _for_each_run_piece(blk, len_ref, src_ref, dst_ref, bits, run_copy, lambda c: c.start())

    def drain(which):
        pltpu.make_async_copy(yb_hbm.at[_rows(0, 2 * tb)], buf.at[which], sem.at[which]).wait()

    @pl.when(i == 0)
    def _():
        fetch(i, 0, RUN_BITS)

    drain(cur)
    nxt = jnp.minimum(i + 1, n - 1)

    def fetch_next_and_combine(bits):
        fetch(nxt, 1 - cur, bits)
        y = _load_row_tiles(buf.at[cur], 0, 2 * tb).astype(BF16)
        pos = lax.broadcasted_iota(jnp.int32, (tb, 2 * tb), 1)
        local = local_ref[...]
        gate = gate_ref[...]
        mix = (jnp.where(pos == local[:, 0:1], gate[:, 0:1], 0.0)
               + jnp.where(pos == local[:, 1:2], gate[:, 1:2], 0.0))
        mix_hi = mix.astype(BF16)
        mix_lo = (mix - mix_hi.astype(F32)).astype(BF16)
        moe = (jnp.dot(mix_hi, y, preferred_element_type=F32) + jnp.dot(mix_lo, y, preferred_element_type=F32))
        o_ref[...] = _layer_norm_rows(DEEPNORM_ALPHA * h_ref[...] + moe, g2_ref[...], b2_ref[...])
    _by_run_size(long_ref, nxt, fetch_next_and_combine)

    @pl.when(i == n - 1)
    def _():
        drain(1 - cur)


def _combine(h, yb, local_t, runs, gate, g2_row, b2_row):
    n_tok = h.shape[0]
    tb = TOKEN_BLOCK
    row_spec = lambda w: pl.BlockSpec((tb, w), lambda i, *_: (i, 0))
    full = lambda shape: pl.BlockSpec(shape, lambda i, *_: (0,) * len(shape))
    grid_spec = pltpu.PrefetchScalarGridSpec(
        num_scalar_prefetch=4,
        grid=(n_tok // tb,),
        in_specs=[row_spec(2), row_spec(2), row_spec(D_MODEL), full((1, D_MODEL)), full((1, D_MODEL)),
                  pl.BlockSpec(memory_space=pl.ANY)],
        out_specs=row_spec(D_MODEL),
        scratch_shapes=[pltpu.VMEM((2, 2 * tb * ROW_SUB, LANES), F32), pltpu.SemaphoreType.DMA((2,))],
    )
    return pl.pallas_call(
        _combine_kernel,
        grid_spec=grid_spec,
        out_shape=jax.ShapeDtypeStruct((n_tok, D_MODEL), F32),
        compiler_params=pltpu.CompilerParams(dimension_semantics=("arbitrary",),
                                             vmem_limit_bytes=VMEM_LIMIT),
        name="combine_ln",
    )(*runs, local_t, gate, h, g2_row, b2_row, yb)


def kernel(x, w_in, b_fgate, s5_lambda_re, s5_lambda_im, s5_log_step, s5_b_re, s5_b_im, s5_c_re, s5_c_im, s5_d, w_glu, b_glu, attn_norm_g, ssm_norm_g, w_out, ln1_g, ln1_b, w_router_group, b_router_group, w_router_expert, b_router_expert, w_gate, w_up, w_down, ln2_g, ln2_b):
    bsz, seq, d = x.shape
    n_tok = bsz * seq
    a = ATTN_WIDTH
    for l in range(DEPTH):
        x2 = x.reshape(n_tok, d)
        w = w_in[l]
        f0 = 3 * a
        w_cat = jnp.concatenate(
            [w[:, :a] * (HEAD_DIM ** -0.5 * LOG2E), w[:, a:2 * a], w[:, f0 + ATTN_HEADS:]], axis=1).astype(BF16)
        wvt = jnp.pad(w[:, 2 * a:f0 + ATTN_HEADS].T, ((0, GATE_ROWS - ATTN_HEADS), (0, 0))).astype(BF16)
        q, k, vt, u, eq, ek = _inproj(x2, w_cat, wvt, b_fgate[l].reshape(ATTN_HEADS, 1), seq)
        attn = _attention(q.reshape(bsz, seq, a), k.reshape(bsz, seq, a), vt,
                          eq.reshape(bsz, seq, EXT_W), ek.reshape(bsz, seq, EXT_W))

        bw, cw, ar_s, ai_s = _s5_weights(s5_lambda_re[l], s5_lambda_im[l], s5_log_step[l],
                                         s5_b_re[l], s5_b_im[l], s5_c_re[l], s5_c_im[l])
        ssm_tb = _s5(u, bsz, bw, cw, ar_s, ai_s, s5_d[l].reshape(1, SSM_WIDTH), w_glu[l].astype(BF16),
                     b_glu[l].reshape(1, SSM_WIDTH), ssm_norm_g[l].reshape(1, SSM_WIDTH))

        n_route = N_EXPERT_GROUPS + N_EXPERTS
        wrt = jnp.pad(jnp.concatenate([w_router_group[l], w_router_expert[l]], axis=1).T,
                      ((0, LANES - n_route), (0, 0))).astype(BF16)
        brt = jnp.pad(jnp.concatenate([b_router_group[l], b_router_expert[l]]),
                      (0, ROUTER_ROWS - n_route)).reshape(ROUTER_ROWS, 1)
        h, eid, gate, rank, cnt_col = _outproj(
            attn.reshape(n_tok, a), ssm_tb, x2, w_out[l].astype(BF16), attn_norm_g[l].reshape(1, a),
            ln1_g[l].reshape(1, d), ln1_b[l].reshape(1, d), wrt, brt)

        local, runs, block_e, n_used, pads, n_blocks = _dispatch_plan(eid, rank, cnt_col, n_tok)
        xb = _dispatch(h, local, runs, pads, n_used, n_blocks)
        yb = _experts(xb, block_e, n_used, w_gate[l], w_up[l], w_down[l])
        x = _combine(h, yb, local.T, runs, gate.T, ln2_g[l].reshape(1, d),
                     ln2_b[l].reshape(1, d)).reshape(bsz, seq, d)
    return x
```

```python
import functools
import math

import jax
import jax.numpy as jnp
import numpy as np
from jax import lax
from jax.experimental import pallas as pl
from jax.experimental.pallas import tpu as pltpu

F32 = jnp.float32
BF16 = jnp.bfloat16

D_MODEL = 1024
HEAD_DIM = 64
ATTN_WIDTH = 512
ATTN_HEADS = 8
SSM_WIDTH = 512
SSM_GROUP = 16
SSM_GROUPS = 32
SSM_STATE = 64
N_EXPERT_GROUPS = 4
EXPERTS_PER_GROUP = 8
N_EXPERTS = 32
D_EXPERT = 512
MOE_BLOCK = 512
MOE_SUB = 256
TOKEN_BLOCK = 512
RUN_BITS = TOKEN_BLOCK.bit_length()
SHORT_RUN_BITS = 6
DEPTH = 1
DEEPNORM_ALPHA = (2.0 * DEPTH) ** 0.25
LN_EPS = 1e-5
RMS_EPS = 1e-6

LANES = 128
NEG_BIG = -1e30
LOG2E = math.log2(math.e)
ONES_ROWS = 16
VT_ROWS = ATTN_HEADS * (HEAD_DIM + ONES_ROWS)
GATE_ROWS = 16
PAIR_BIAS_LANES = 8
EXT_W = ATTN_HEADS // 2 * LANES
VMEM_LIMIT = 48 * 1024 * 1024

ROW_BLOCK = 1024
ATTN_BLOCK = 256
ATTN_Q_BLOCK = 512
ATTN_HEADS_PER_STEP = 8
S5_TIME_BLOCK = 64
S5_SLABS = 4
ROUTER_ROWS = 40
SLAB_CH = SSM_WIDTH // S5_SLABS
SLAB_STATE = SSM_GROUPS // S5_SLABS * SSM_STATE


def _split3(c):
    hi = c.astype(BF16)
    r1 = c - hi.astype(F32)
    mid = r1.astype(BF16)
    lo = (r1 - mid.astype(F32)).astype(BF16)
    return hi, mid, lo


def _inproj_kernel(x_ref, w_ref, wvt_ref, bf_ref, triu_ref, sel_ref, q_ref, k_ref, vt_ref, u_ref, eq_ref, ek_ref,
                   carry_ref, *, blocks_per_seq):
    i = pl.program_id(0)

    @pl.when(i % blocks_per_seq == 0)
    def _():
        carry_ref[...] = jnp.zeros_like(carry_ref)

    tm = x_ref.shape[0]
    xb = x_ref[...].astype(BF16)
    z = jnp.dot(xb, w_ref[...], preferred_element_type=F32)
    a = ATTN_WIDTH
    q_ref[...] = z[:, :a].astype(BF16)
    k_ref[...] = z[:, a:2 * a].astype(BF16)
    u_ref[...] = z[:, 2 * a:2 * a + SSM_WIDTH].astype(BF16)
    vtf = lax.dot_general(wvt_ref[...], xb, (((1,), (1,)), ((), ())), preferred_element_type=F32)
    vt = vtf[:ATTN_WIDTH].astype(BF16)
    ones_rows = jnp.ones((ONES_ROWS, tm), BF16)
    vt = jnp.concatenate([piece for h in range(ATTN_HEADS)
                          for piece in (vt[HEAD_DIM * h:HEAD_DIM * (h + 1)], ones_rows)], axis=0)
    for jj in range(tm // ATTN_BLOCK):
        vt_ref[jj] = vt[:, jj * ATTN_BLOCK:(jj + 1) * ATTN_BLOCK]

    f = vtf[ATTN_WIDTH:ATTN_WIDTH + ATTN_HEADS] + bf_ref[...]
    logf = jnp.minimum(f, 0.0) - jnp.log1p(jnp.exp(-jnp.abs(f)))
    parts = jnp.dot(jnp.concatenate(_split3(logf), axis=0), triu_ref[...], preferred_element_type=F32)
    nh = ATTN_HEADS
    cum = parts[0:nh] + parts[nh:2 * nh] + parts[2 * nh:3 * nh] + carry_ref[...]
    carry_ref[...] = cum[:, tm - 1:tm]
    hi, mid, lo = (t.astype(F32) for t in _split3(cum * LOG2E))
    one_row = jnp.where(lax.broadcasted_iota(jnp.int32, (nh, tm), 0) == 0, 1.0, 0.0)
    packed_t = jnp.concatenate([hi, mid, lo, one_row, jnp.zeros((LANES - 4 * nh, tm), F32)], axis=0)
    ext = jnp.dot(packed_t.T.astype(BF16), sel_ref[...], preferred_element_type=F32).astype(BF16)
    eq_ref[...] = ext[:, :EXT_W]
    ek_ref[...] = ext[:, EXT_W:]


def _bias_lane_selector():
    sel = np.zeros((LANES, 2, ATTN_HEADS // 2, LANES), np.float32)
    one = 3 * ATTN_HEADS
    for h in range(ATTN_HEADS):
        pair, base = h // 2, PAIR_BIAS_LANES * (h % 2)
        for t in range(3):
            sel[ATTN_HEADS * t + h, 0, pair, base + t] = 1.0
            sel[one, 0, pair, base + 3 + t] = 1.0
            sel[one, 1, pair, base + t] = 1.0
            sel[ATTN_HEADS * t + h, 1, pair, base + 3 + t] = -1.0
    return jnp.asarray(sel.reshape(LANES, 2 * EXT_W), BF16)


def _inproj(x2, w_cat, wvt, bf_col, seq):
    n_tok = x2.shape[0]
    tm = ROW_BLOCK
    assert seq % tm == 0 and n_tok % tm == 0 and tm % ATTN_BLOCK == 0
    bsz = n_tok // seq
    bps = seq // tm
    ncol = w_cat.shape[1]
    vblk = tm // ATTN_BLOCK
    row_spec = lambda w: pl.BlockSpec((tm, w), lambda i: (i, 0))
    return pl.pallas_call(
        functools.partial(_inproj_kernel, blocks_per_seq=bps),
        grid=(n_tok // tm,),
        in_specs=[row_spec(D_MODEL),
                  pl.BlockSpec((D_MODEL, ncol), lambda i: (0, 0)),
                  pl.BlockSpec((ATTN_WIDTH + GATE_ROWS, D_MODEL), lambda i: (0, 0)),
                  pl.BlockSpec((ATTN_HEADS, 1), lambda i: (0, 0)),
                  pl.BlockSpec((tm, tm), lambda i: (0, 0)),
                  pl.BlockSpec((LANES, 2 * EXT_W), lambda i: (0, 0))],
        out_specs=[row_spec(ATTN_WIDTH), row_spec(ATTN_WIDTH),
                   pl.BlockSpec((None, vblk, VT_ROWS, ATTN_BLOCK), lambda i: (i // bps, i % bps, 0, 0)),
                   pl.BlockSpec((tm, SSM_WIDTH), lambda i: (i % bps, i // bps)),
                   row_spec(EXT_W), row_spec(EXT_W)],
        out_shape=[jax.ShapeDtypeStruct((n_tok, ATTN_WIDTH), BF16)] * 2
                  + [jax.ShapeDtypeStruct((bsz, seq // ATTN_BLOCK, VT_ROWS, ATTN_BLOCK), BF16),
                     jax.ShapeDtypeStruct((seq, bsz * SSM_WIDTH), BF16),
                     jax.ShapeDtypeStruct((n_tok, EXT_W), BF16),
                     jax.ShapeDtypeStruct((n_tok, EXT_W), BF16)],
        scratch_shapes=[pltpu.VMEM((ATTN_HEADS, 1), F32)],
        compiler_params=pltpu.CompilerParams(dimension_semantics=("arbitrary",),
                                             vmem_limit_bytes=VMEM_LIMIT),
        name="inproj",
    )(x2, w_cat, wvt, bf_col, jnp.triu(jnp.ones((tm, tm), BF16)), _bias_lane_selector())


def _attn_kernel(q_ref, k_ref, vt_ref, eq_ref, ek_ref, o_ref, *, tq, seq, nh):
    qi = pl.program_id(2)
    lane = lax.broadcasted_iota(jnp.int32, (tq, LANES), 1)
    q_augs = []
    for h in range(nh):
        qb = q_ref[:, LANES * (h // 2):LANES * (h // 2 + 1)]
        head_lanes = (lane < HEAD_DIM) if h % 2 == 0 else (lane >= HEAD_DIM)
        qh = jnp.where(head_lanes, qb, jnp.zeros((), BF16))
        own = (lane >= PAIR_BIAS_LANES * (h % 2)) & (lane < PAIR_BIAS_LANES * (h % 2 + 1))
        eqh = jnp.where(own, eq_ref[:, LANES * (h // 2):LANES * (h // 2 + 1)], jnp.zeros((), BF16))
        q_augs.append(jnp.concatenate([qh, eqh], axis=1))
    key_i = lax.broadcasted_iota(jnp.int32, (tq, tq), 0)
    qry_i = lax.broadcasted_iota(jnp.int32, (tq, tq), 1)
    causal = key_i <= qry_i
    vrows = HEAD_DIM + ONES_ROWS

    def scores_of(k0, nk):
        out = []
        for h in range(nh):
            kj = k_ref[pl.ds(k0, nk), LANES * (h // 2):LANES * (h // 2 + 1)]
            k_aug = jnp.concatenate([kj, ek_ref[pl.ds(k0, nk), LANES * (h // 2):LANES * (h // 2 + 1)]], axis=1)
            out.append(lax.dot_general(k_aug, q_augs[h], (((1,), (1,)), ((), ())), preferred_element_type=F32))
        return tuple(out)

    def absorb(j0, scores, carry):
        kb = ATTN_BLOCK
        stats = []
        for h in range(nh):
            m = carry[h][0]
            m_new = jnp.maximum(m, jnp.max(scores[h], axis=0, keepdims=True))
            stats.append((m_new, jnp.exp2(m - m_new), jnp.exp2(scores[h] - m_new).astype(BF16)))
        new = []
        for h in range(nh):
            m_new, corr, p = stats[h]
            res = None
            for c in range(p.shape[0] // kb):
                part = jnp.dot(vt_ref[j0 + c, vrows * h:vrows * (h + 1), :], p[c * kb:(c + 1) * kb],
                               preferred_element_type=F32)
                res = part if res is None else res + part
            acc = corr * carry[h][2] + res[0:HEAD_DIM, :]
            l = corr * carry[h][1] + res[HEAD_DIM:HEAD_DIM + 1, :]
            new.append((m_new, l, acc))
        return tuple(new)

    per = tq // ATTN_BLOCK

    def full_step(j, carry):
        return absorb(per * j, scores_of(pl.multiple_of(j * tq, tq), tq), carry)

    init1 = (jnp.full((1, tq), NEG_BIG, F32), jnp.zeros((1, tq), F32), jnp.zeros((HEAD_DIM, tq), F32))
    carry = lax.fori_loop(0, qi, full_step, (init1,) * nh)
    diag = scores_of(pl.multiple_of(qi * tq, tq), tq)
    carry = absorb(per * qi, tuple(jnp.where(causal, s, NEG_BIG) for s in diag), carry)
    out_t = jnp.concatenate([c[2] / c[1] for c in carry], axis=0)
    o_ref[...] = out_t.T


def _attention(q, k, vt, eq, ek):
    bsz, seq, _ = q.shape
    tq = ATTN_Q_BLOCK
    nh = ATTN_HEADS_PER_STEP
    assert seq % tq == 0 and tq % ATTN_BLOCK == 0 and ATTN_HEADS % nh == 0 and nh % 2 == 0
    w = nh * HEAD_DIM
    return pl.pallas_call(
        functools.partial(_attn_kernel, tq=tq, seq=seq, nh=nh),
        grid=(bsz, ATTN_HEADS // nh, seq // tq),
        in_specs=[pl.BlockSpec((None, tq, w), lambda b, g, i: (b, i, g)),
                  pl.BlockSpec((None, seq, w), lambda b, g, i: (b, 0, g)),
                  pl.BlockSpec((None, seq // ATTN_BLOCK, nh * (HEAD_DIM + ONES_ROWS), ATTN_BLOCK),
                               lambda b, g, i: (b, 0, g, 0)),
                  pl.BlockSpec((None, tq, nh // 2 * LANES), lambda b, g, i: (b, i, g)),
                  pl.BlockSpec((None, seq, nh // 2 * LANES), lambda b, g, i: (b, 0, g))],
        out_specs=pl.BlockSpec((None, tq, w), lambda b, g, i: (b, i, g)),
        out_shape=jax.ShapeDtypeStruct((bsz, seq, ATTN_WIDTH), F32),
        compiler_params=pltpu.CompilerParams(dimension_semantics=("parallel", "parallel", "arbitrary"),
                                             vmem_limit_bytes=VMEM_LIMIT),
        name="fox_attention",
    )(q, k, vt, eq, ek)


def _s5_kernel(u_ref, bw_ref, cw_ref, ar_ref, ai_ref, d_ref, wglu_ref, bglu_ref, g_ref, o_ref,
               state_ref, bu_ref, *, tc, bsz):
    @pl.when(pl.program_id(0) == 0)
    def _():
        state_ref[...] = jnp.zeros_like(state_ref)

    rows = tc * bsz
    ns = SLAB_STATE
    u = u_ref[...].reshape(rows, SSM_WIDTH)
    ub = u
    for i in range(S5_SLABS):
        bu_ref[i] = jnp.dot(ub[:, SLAB_CH * i:SLAB_CH * (i + 1)], bw_ref[i], preferred_element_type=F32)
    for i in range(S5_SLABS):
        ar = jnp.broadcast_to(ar_ref[i], (bsz, ns))
        ai = jnp.broadcast_to(ai_ref[i], (bsz, ns))
        xr = state_ref[i, :, 0:ns]
        xi = state_ref[i, :, ns:2 * ns]
        for t in range(tc):
            r0 = t * bsz
            nr = ar * xr - ai * xi + bu_ref[i, r0:r0 + bsz, 0:ns]
            ni = ar * xi + ai * xr + bu_ref[i, r0:r0 + bsz, ns:2 * ns]
            bu_ref[i, r0:r0 + bsz, 0:ns] = nr
            bu_ref[i, r0:r0 + bsz, ns:2 * ns] = ni
            xr, xi = nr, ni
        state_ref[i, :, 0:ns] = xr
        state_ref[i, :, ns:2 * ns] = xi
    ys = [jnp.dot(bu_ref[i].astype(BF16), cw_ref[i], preferred_element_type=F32) for i in range(S5_SLABS)]
    y = jnp.concatenate(ys, axis=1) + d_ref[...] * u.astype(F32)
    y = 0.5 * y * (1.0 + jnp.tanh(math.sqrt(2.0 / math.pi) * (y + 0.044715 * (y * y * y))))
    z = jnp.dot(y.astype(BF16), wglu_ref[...], preferred_element_type=F32) + bglu_ref[...]
    y = y * (1.0 / (1.0 + jnp.exp(-z)))
    y = y * lax.rsqrt(jnp.mean(y * y, axis=-1, keepdims=True) + RMS_EPS) * g_ref[...]
    o_ref[...] = y.astype(BF16).reshape(tc, bsz * SSM_WIDTH)


def _s5_weights(lam_re, lam_im, log_step, b_re, b_im, c_re, c_im):
    delta = jnp.exp(log_step)[:, None]
    mag = jnp.exp(lam_re * delta)
    ar = mag * jnp.cos(lam_im * delta)
    ai = mag * jnp.sin(lam_im * delta)
    den = lam_re * lam_re + lam_im * lam_im
    num_re = ar - 1.0
    coef_re = (num_re * lam_re + ai * lam_im) / den
    coef_im = (ai * lam_re - num_re * lam_im) / den
    bbar_re = coef_re[..., None] * b_re - coef_im[..., None] * b_im
    bbar_im = coef_re[..., None] * b_im + coef_im[..., None] * b_re
    gs = SSM_GROUPS // S5_SLABS
    eye = jnp.eye(gs, dtype=F32)

    def b_slab(bb):
        bb = bb.reshape(S5_SLABS, gs, SSM_STATE, SSM_GROUP)
        w = jnp.einsum('ab,sapc->sacbp', eye, bb)
        return w.reshape(S5_SLABS, gs * SSM_GROUP, gs * SSM_STATE)

    def c_slab(cc):
        cc = cc.reshape(S5_SLABS, gs, SSM_GROUP, SSM_STATE)
        w = jnp.einsum('ab,sacp->sbpac', eye, cc)
        return w.reshape(S5_SLABS, gs * SSM_STATE, gs * SSM_GROUP)

    bw = jnp.concatenate([b_slab(bbar_re), b_slab(bbar_im)], axis=2).astype(BF16)
    cw = jnp.concatenate([c_slab(c_re), -c_slab(c_im)], axis=1).astype(BF16)
    ar_s = ar.reshape(S5_SLABS, 1, SLAB_STATE)
    ai_s = ai.reshape(S5_SLABS, 1, SLAB_STATE)
    return bw, cw, ar_s, ai_s


def _s5(u_tb, bsz, bw, cw, ar_s, ai_s, d_row, w_glu, b_glu_row, g_row):
    seq = u_tb.shape[0]
    tc = S5_TIME_BLOCK
    assert seq % tc == 0 and bsz % 16 == 0
    full = lambda shape: pl.BlockSpec(shape, lambda t: (0,) * len(shape))
    return pl.pallas_call(
        functools.partial(_s5_kernel, tc=tc, bsz=bsz),
        grid=(seq // tc,),
        in_specs=[pl.BlockSpec((tc, bsz * SSM_WIDTH), lambda t: (t, 0)),
                  full(bw.shape), full(cw.shape), full(ar_s.shape), full(ai_s.shape),
                  full((1, SSM_WIDTH)), full((SSM_WIDTH, SSM_WIDTH)), full((1, SSM_WIDTH)),
                  full((1, SSM_WIDTH))],
        out_specs=pl.BlockSpec((tc, bsz * SSM_WIDTH), lambda t: (t, 0)),
        out_shape=jax.ShapeDtypeStruct((seq, bsz * SSM_WIDTH), BF16),
        scratch_shapes=[pltpu.VMEM((S5_SLABS, bsz, 2 * SLAB_STATE), F32),
                        pltpu.VMEM((S5_SLABS, tc * bsz, 2 * SLAB_STATE), F32)],
        compiler_params=pltpu.CompilerParams(dimension_semantics=("arbitrary",),
                                             vmem_limit_bytes=VMEM_LIMIT),
        name="s5_branch",
    )(u_tb, bw, cw, ar_s, ai_s, d_row, w_glu, b_glu_row, g_row)


def _layer_norm_rows(v, g, b):
    mu = jnp.mean(v, axis=-1, keepdims=True)
    c = v - mu
    var = jnp.mean(c * c, axis=-1, keepdims=True)
    return c * lax.rsqrt(var + LN_EPS) * g + b


def _outproj_kernel(attn_ref, ssm_ref, x_ref, wo_ref, ag_ref, g1_ref, b1_ref, wrt_ref, brt_ref, triu_ref,
                    h_ref, eid_ref, gate_ref, rank_ref, cnt_ref):
    a = attn_ref[...]
    a = a * lax.rsqrt(jnp.mean(a * a, axis=-1, keepdims=True) + RMS_EPS) * ag_ref[...]
    mix = (jnp.dot(a.astype(BF16), wo_ref[0:ATTN_WIDTH, :], preferred_element_type=F32)
           + jnp.dot(ssm_ref[...], wo_ref[ATTN_WIDTH:, :], preferred_element_type=F32))
    h = _layer_norm_rows(DEEPNORM_ALPHA * x_ref[...] + mix, g1_ref[...], b1_ref[...])
    h_ref[...] = h

    logits = lax.dot_general(wrt_ref[...], h.astype(BF16), (((1,), (1,)), ((), ())),
                             preferred_element_type=F32)[:ROUTER_ROWS] + brt_ref[...]
    tm = logits.shape[1]
    row = lax.broadcasted_iota(jnp.int32, (ROUTER_ROWS, tm), 0).astype(F32)
    no_row = float(LANES)
    is_g = row < N_EXPERT_GROUPS
    gl = jnp.where(is_g, logits, NEG_BIG)
    gmax = jnp.max(gl, axis=0, keepdims=True)
    g_idx = jnp.min(jnp.where(is_g & (gl == gmax), row, no_row), axis=0, keepdims=True)
    g_p = 1.0 / jnp.sum(jnp.where(is_g, jnp.exp(gl - gmax), 0.0), axis=0, keepdims=True)
    first = N_EXPERT_GROUPS + g_idx * EXPERTS_PER_GROUP
    in_grp = (row >= first) & (row < first + EXPERTS_PER_GROUP)
    el = jnp.where(in_grp, logits, NEG_BIG)
    m1 = jnp.max(el, axis=0, keepdims=True)
    i1 = jnp.min(jnp.where(in_grp & (el == m1), row, no_row), axis=0, keepdims=True)
    rest = in_grp & (row != i1)
    el2 = jnp.where(rest, logits, NEG_BIG)
    m2 = jnp.max(el2, axis=0, keepdims=True)
    i2 = jnp.min(jnp.where(rest & (el2 == m2), row, no_row), axis=0, keepdims=True)
    r = jnp.exp(m2 - m1)
    p1 = 1.0 / (1.0 + r)
    p2 = r / (1.0 + r)
    k2 = lax.broadcasted_iota(jnp.int32, (2, tm), 0)
    eid_ref[...] = (jnp.where(k2 == 0, i1, i2) - N_EXPERT_GROUPS).astype(jnp.int32)
    gate_ref[...] = g_p * jnp.where(k2 == 0, p1, p2)

    @pl.when(pl.program_id(0) == 0)
    def _():
        cnt_ref[...] = jnp.zeros_like(cnt_ref)
    hit1 = row == i1
    hit2 = row == i2
    onehot = jnp.where(hit1 | hit2, 1.0, 0.0)
    cnt = cnt_ref[...]
    before = jnp.dot(onehot.astype(BF16), triu_ref[...], preferred_element_type=F32) + cnt
    r1 = jnp.sum(jnp.where(hit1, before, 0.0), axis=0, keepdims=True)
    r2 = jnp.sum(jnp.where(hit2, before, 0.0), axis=0, keepdims=True)
    rank_ref[...] = jnp.where(k2 == 0, r1, r2).astype(jnp.int32)
    cnt_ref[...] = cnt + jnp.sum(onehot, axis=1, keepdims=True)


def _outproj(attn2, ssm_tb, x2, wo, ag_row, g1_row, b1_row, wrt, brt):
    n_tok = x2.shape[0]
    tm = ROW_BLOCK
    bps = ssm_tb.shape[0] // tm
    triu = jnp.triu(jnp.ones((tm, tm), BF16), 1)
    row_spec = lambda w: pl.BlockSpec((tm, w), lambda i: (i, 0))
    col_spec = lambda r: pl.BlockSpec((r, tm), lambda i: (0, i))
    full = lambda shape: pl.BlockSpec(shape, lambda i: (0,) * len(shape))
    return pl.pallas_call(
        _outproj_kernel,
        grid=(n_tok // tm,),
        in_specs=[row_spec(ATTN_WIDTH), pl.BlockSpec((tm, SSM_WIDTH), lambda i: (i % bps, i // bps)),
                  row_spec(D_MODEL),
                  full((D_MODEL, D_MODEL)), full((1, ATTN_WIDTH)), full((1, D_MODEL)), full((1, D_MODEL)),
                  full((LANES, D_MODEL)), full((ROUTER_ROWS, 1)), full((tm, tm))],
        out_specs=[row_spec(D_MODEL), col_spec(2), col_spec(2), col_spec(2), full((ROUTER_ROWS, 1))],
        out_shape=[jax.ShapeDtypeStruct((n_tok, D_MODEL), F32),
                   jax.ShapeDtypeStruct((2, n_tok), jnp.int32),
                   jax.ShapeDtypeStruct((2, n_tok), F32),
                   jax.ShapeDtypeStruct((2, n_tok), jnp.int32),
                   jax.ShapeDtypeStruct((ROUTER_ROWS, 1), F32)],
        compiler_params=pltpu.CompilerParams(dimension_semantics=("arbitrary",),
                                             vmem_limit_bytes=VMEM_LIMIT),
        name="outproj_ln_router",
    )(attn2, ssm_tb, x2, wo, ag_row, g1_row, b1_row, wrt, brt, triu)


def _dispatch_plan(eid, rank, cnt_col, n_tok):
    counts = cnt_col[N_EXPERT_GROUPS:N_EXPERT_GROUPS + N_EXPERTS, 0].astype(jnp.int32)
    padded = ((counts + MOE_BLOCK - 1) // MOE_BLOCK) * MOE_BLOCK
    pends = jnp.cumsum(padded)
    pstarts = pends - padded
    n_pad = ((2 * n_tok + N_EXPERTS * (MOE_BLOCK - 1) + MOE_BLOCK - 1) // MOE_BLOCK) * MOE_BLOCK
    n_blocks = n_pad // MOE_BLOCK
    blk0 = jnp.arange(n_blocks, dtype=jnp.int32) * MOE_BLOCK
    block_e = jnp.minimum(jnp.sum((pends[None, :] <= blk0[:, None]).astype(jnp.int32), axis=1), N_EXPERTS - 1)
    n_used = (pends[-1] // MOE_BLOCK).astype(jnp.int32).reshape(1)
    pads = (padded - counts, jnp.zeros_like(counts), pstarts + counts)

    tb = TOKEN_BLOCK
    nb = n_tok // tb
    onehot = eid[:, :, None] == jnp.arange(N_EXPERTS, dtype=jnp.int32)[None, None, :]
    run_len = jnp.sum(onehot.reshape(2, nb, tb, N_EXPERTS).astype(jnp.int32), axis=(0, 2))
    before = jnp.cumsum(run_len, axis=0) - run_len
    run_src = jnp.cumsum(run_len, axis=1) - run_len
    run_dst = pstarts[None, :] + before
    shift = jnp.broadcast_to((run_src - before)[:, None, :], (nb, tb, N_EXPERTS)).reshape(n_tok, N_EXPERTS)
    local = rank + jnp.sum(jnp.where(onehot, shift[None], 0), axis=-1)
    any_long = (jnp.max(run_len, axis=1) >= (1 << SHORT_RUN_BITS)).astype(jnp.int32)
    runs = (run_len.reshape(-1), run_src.reshape(-1), run_dst.reshape(-1), any_long)
    return local, runs, block_e, n_used, pads, n_blocks


ROW_SUB = D_MODEL // LANES


def _rows(start, size):
    return pl.ds(pl.multiple_of(start * ROW_SUB, ROW_SUB), size * ROW_SUB)


def _store_row_tiles(ref, row0, v):
    n = v.shape[0]
    for s in range(ROW_SUB):
        ref[pl.ds(row0 * ROW_SUB + s, n, stride=ROW_SUB), :] = v[:, LANES * s:LANES * (s + 1)]


def _load_row_tiles(ref, row0, n):
    return jnp.concatenate([ref[pl.ds(row0 * ROW_SUB + s, n, stride=ROW_SUB), :] for s in range(ROW_SUB)], axis=1)


def _for_each_run_piece(blk, len_ref, src_ref, dst_ref, bits, make_copy, act):
    for e in range(N_EXPERTS):
        idx = blk * N_EXPERTS + e
        n = len_ref[idx]
        src = src_ref[idx]
        dst = dst_ref[idx]
        for b in reversed(range(bits)):
            @pl.when((n & (1 << b)) != 0)
            def _():
                off = lax.shift_left(lax.shift_right_logical(n, b + 1), b + 1)
                act(make_copy(src + off, dst + off, 1 << b))


def _by_run_size(long_ref, blk, body):
    @pl.when(long_ref[blk] == 0)
    def _():
        body(SHORT_RUN_BITS)

    @pl.when(long_ref[blk] != 0)
    def _():
        body(RUN_BITS)


def _dispatch_kernel(nu_ref, plen_ref, psrc_ref, pdst_ref, len_ref, src_ref, dst_ref, long_ref, local_ref, h_ref,
                     xb_hbm, zbuf, pk_ref, sem_z, sem, *, n_blocks):
    i = pl.program_id(0)
    nb = pl.num_programs(0) - 1
    tb = h_ref.shape[0]
    cur = i % 2

    def each_zero_copy(act):
        def zero_copy(src, dst, size):
            return pltpu.make_async_copy(zbuf.at[_rows(src, size)], xb_hbm.at[_rows(dst, size)], sem_z)
        _for_each_run_piece(0, plen_ref, psrc_ref, pdst_ref, MOE_BLOCK.bit_length() - 1, zero_copy, act)

        def unused(b, carry):
            act(zero_copy(0, b * MOE_BLOCK, MOE_BLOCK))
            return carry
        lax.fori_loop(nu_ref[0], n_blocks, unused, 0)

    @pl.when(i == 0)
    def _():
        zbuf[...] = jnp.zeros_like(zbuf)
        each_zero_copy(lambda c: c.start())

    def drain(which):
        pltpu.make_async_copy(pk_ref.at[which], xb_hbm.at[_rows(0, 2 * tb)], sem.at[which]).wait()

    @pl.when(i >= 2)
    def _():
        drain(cur)

    def group():
        pos = lax.broadcasted_iota(jnp.int32, (2 * tb, tb), 0)
        hit = (pos == local_ref[0:1, :]) | (pos == local_ref[1:2, :])
        perm = jnp.where(hit, 1.0, 0.0).astype(BF16)
        _store_row_tiles(pk_ref.at[cur], 0, jnp.dot(perm, h_ref[...].astype(BF16), preferred_element_type=F32))

    def send(bits):
        def run_copy(src, dst, size):
            return pltpu.make_async_copy(pk_ref.at[1 - cur, _rows(src, size)], xb_hbm.at[_rows(dst, size)],
                                         sem.at[1 - cur])
        _for_each_run_piece(i - 1, len_ref, src_ref, dst_ref, bits, run_copy, lambda c: c.start())

    @pl.when(i == 0)
    def _():
        group()

    @pl.when((i > 0) & (i < nb))
    def _():
        def send_and_group(bits):
            send(bits)
            group()
        _by_run_size(long_ref, i - 1, send_and_group)

    @pl.when(i == nb)
    def _():
        send(RUN_BITS)
        drain(1 - cur)
        each_zero_copy(lambda c: c.wait())


def _dispatch(h, local, runs, pads, n_used, n_blocks):
    n_tok = h.shape[0]
    tb = TOKEN_BLOCK
    nb = n_tok // tb
    grid_spec = pltpu.PrefetchScalarGridSpec(
        num_scalar_prefetch=8,
        grid=(nb + 1,),
        in_specs=[pl.BlockSpec((2, tb), lambda i, *_: (0, jnp.minimum(i, nb - 1))),
                  pl.BlockSpec((tb, D_MODEL), lambda i, *_: (jnp.minimum(i, nb - 1), 0))],
        out_specs=pl.BlockSpec(memory_space=pl.ANY),
        scratch_shapes=[pltpu.VMEM((MOE_BLOCK * ROW_SUB, LANES), F32),
                        pltpu.VMEM((2, 2 * tb * ROW_SUB, LANES), F32),
                        pltpu.SemaphoreType.DMA, pltpu.SemaphoreType.DMA((2,))],
    )
    return pl.pallas_call(
        functools.partial(_dispatch_kernel, n_blocks=n_blocks),
        grid_spec=grid_spec,
        out_shape=jax.ShapeDtypeStruct((n_blocks * MOE_BLOCK * ROW_SUB, LANES), F32),
        compiler_params=pltpu.CompilerParams(dimension_semantics=("arbitrary",),
                                             vmem_limit_bytes=VMEM_LIMIT),
        name="moe_dispatch",
    )(n_used, *pads, *runs, local, h)


def _expert_kernel(be_ref, nu_ref, x_ref, wg_ref, wu_ref, wd_ref, y_ref, wgu_t, wd_t, yt_ref):
    i = pl.program_id(0)
    used = i < nu_ref[0]

    @pl.when(used & ((i == 0) | (be_ref[i] != be_ref[jnp.maximum(i - 1, 0)])))
    def _():
        wgu_t[0:D_EXPERT, :] = wg_ref[...].astype(BF16).T
        wgu_t[D_EXPERT:, :] = wu_ref[...].astype(BF16).T
        wd_t[...] = wd_ref[...].astype(BF16).T

    @pl.when(used)
    def _():
        sub = range(MOE_BLOCK // MOE_SUB)
        gus = []
        for c in sub:
            xb = _load_row_tiles(x_ref, c * MOE_SUB, MOE_SUB).astype(BF16)
            gus.append(lax.dot_general(wgu_t[...], xb, (((1,), (1,)), ((), ())),
                                       preferred_element_type=F32))
        hids = []
        for c in sub:
            gt = gus[c][0:D_EXPERT]
            hids.append((gt * (1.0 / (1.0 + jnp.exp(-gt))) * gus[c][D_EXPERT:]).astype(BF16))
        for c in sub:
            yt_ref[c] = jnp.dot(wd_t[...], hids[c], preferred_element_type=F32)
        for c in sub:
            _store_row_tiles(y_ref, c * MOE_SUB, yt_ref[c].astype(BF16).T.astype(F32))

    @pl.when(jnp.logical_not(used))
    def _():
        y_ref[...] = jnp.zeros_like(y_ref)


def _experts(xb, block_e, n_used, wg, wu, wd):
    n_blocks = xb.shape[0] // (MOE_BLOCK * ROW_SUB)
    last = lambda i, nu: jnp.minimum(i, nu[0] - 1)
    grid_spec = pltpu.PrefetchScalarGridSpec(
        num_scalar_prefetch=2,
        grid=(n_blocks,),
        in_specs=[pl.BlockSpec((MOE_BLOCK * ROW_SUB, LANES), lambda i, be, nu: (last(i, nu), 0)),
                  pl.BlockSpec((None, D_MODEL, D_EXPERT), lambda i, be, nu: (be[last(i, nu)], 0, 0)),
                  pl.BlockSpec((None, D_MODEL, D_EXPERT), lambda i, be, nu: (be[last(i, nu)], 0, 0)),
                  pl.BlockSpec((None, D_EXPERT, D_MODEL), lambda i, be, nu: (be[last(i, nu)], 0, 0))],
        out_specs=pl.BlockSpec((MOE_BLOCK * ROW_SUB, LANES), lambda i, be, nu: (i, 0)),
        scratch_shapes=[pltpu.VMEM((2 * D_EXPERT, D_MODEL), BF16), pltpu.VMEM((D_MODEL, D_EXPERT), BF16),
                        pltpu.VMEM((MOE_BLOCK // MOE_SUB, D_MODEL, MOE_SUB), F32)],
    )
    return pl.pallas_call(
        _expert_kernel,
        grid_spec=grid_spec,
        out_shape=jax.ShapeDtypeStruct(xb.shape, F32),
        compiler_params=pltpu.CompilerParams(dimension_semantics=("arbitrary",),
                                             vmem_limit_bytes=VMEM_LIMIT),
        name="moe_experts",
    )(block_e, n_used, xb, wg, wu, wd)


def _combine_kernel(len_ref, src_ref, dst_ref, long_ref, local_ref, gate_ref, h_ref, g2_ref, b2_ref, yb_hbm,
                    o_ref, buf, sem):
    i = pl.program_id(0)
    n = pl.num_programs(0)
    tb = h_ref.shape[0]
    cur = i % 2

    def fetch(blk, which, bits):
        def run_copy(src, dst, size):
            return pltpu.make_async_copy(yb_hbm.at[_rows(dst, size)], buf.at[which, _rows(src, size)], sem.at[which])
        _for_each_run_piece(blk, len_ref, src_ref, dst_ref, bits, run_copy, lambda c: c.start())

    def drain(which):
        pltpu.make_async_copy(yb_hbm.at[_rows(0, 2 * tb)], buf.at[which], sem.at[which]).wait()

    @pl.when(i == 0)
    def _():
        fetch(i, 0, RUN_BITS)

    drain(cur)
    nxt = jnp.minimum(i + 1, n - 1)

    def fetch_next_and_combine(bits):
        fetch(nxt, 1 - cur, bits)
        y = _load_row_tiles(buf.at[cur], 0, 2 * tb).astype(BF16)
        pos = lax.broadcasted_iota(jnp.int32, (tb, 2 * tb), 1)
        local = local_ref[...]
        gate = gate_ref[...]
        mix = (jnp.where(pos == local[:, 0:1], gate[:, 0:1], 0.0)
               + jnp.where(pos == local[:, 1:2], gate[:, 1:2], 0.0))
        mix_hi = mix.astype(BF16)
        mix_lo = (mix - mix_hi.astype(F32)).astype(BF16)
        moe = (jnp.dot(mix_hi, y, preferred_element_type=F32) + jnp.dot(mix_lo, y, preferred_element_type=F32))
        o_ref[...] = _layer_norm_rows(DEEPNORM_ALPHA * h_ref[...] + moe, g2_ref[...], b2_ref[...])
    _by_run_size(long_ref, nxt, fetch_next_and_combine)

    @pl.when(i == n - 1)
    def _():
        drain(1 - cur)


def _combine(h, yb, local_t, runs, gate, g2_row, b2_row):
    n_tok = h.shape[0]
    tb = TOKEN_BLOCK
    row_spec = lambda w: pl.BlockSpec((tb, w), lambda i, *_: (i, 0))
    full = lambda shape: pl.BlockSpec(shape, lambda i, *_: (0,) * len(shape))
    grid_spec = pltpu.PrefetchScalarGridSpec(
        num_scalar_prefetch=4,
        grid=(n_tok // tb,),
        in_specs=[row_spec(2), row_spec(2), row_spec(D_MODEL), full((1, D_MODEL)), full((1, D_MODEL)),
                  pl.BlockSpec(memory_space=pl.ANY)],
        out_specs=row_spec(D_MODEL),
        scratch_shapes=[pltpu.VMEM((2, 2 * tb * ROW_SUB, LANES), F32), pltpu.SemaphoreType.DMA((2,))],
    )
    return pl.pallas_call(
        _combine_kernel,
        grid_spec=grid_spec,
        out_shape=jax.ShapeDtypeStruct((n_tok, D_MODEL), F32),
        compiler_params=pltpu.CompilerParams(dimension_semantics=("arbitrary",),
                                             vmem_limit_bytes=VMEM_LIMIT),
        name="combine_ln",
    )(*runs, local_t, gate, h, g2_row, b2_row, yb)


def kernel(x, w_in, b_fgate, s5_lambda_re, s5_lambda_im, s5_log_step, s5_b_re, s5_b_im, s5_c_re, s5_c_im, s5_d, w_glu, b_glu, attn_norm_g, ssm_norm_g, w_out, ln1_g, ln1_b, w_router_group, b_router_group, w_router_expert, b_router_expert, w_gate, w_up, w_down, ln2_g, ln2_b):
    bsz, seq, d = x.shape
    n_tok = bsz * seq
    a = ATTN_WIDTH
    for l in range(DEPTH):
        x2 = x.reshape(n_tok, d)
        w = w_in[l]
        f0 = 3 * a
        w_cat = jnp.concatenate(
            [w[:, :a] * (HEAD_DIM ** -0.5 * LOG2E), w[:, a:2 * a], w[:, f0 + ATTN_HEADS:]], axis=1).astype(BF16)
        wvt = jnp.pad(w[:, 2 * a:f0 + ATTN_HEADS].T, ((0, GATE_ROWS - ATTN_HEADS), (0, 0))).astype(BF16)
        q, k, vt, u, eq, ek = _inproj(x2, w_cat, wvt, b_fgate[l].reshape(ATTN_HEADS, 1), seq)
        attn = _attention(q.reshape(bsz, seq, a), k.reshape(bsz, seq, a), vt,
                          eq.reshape(bsz, seq, EXT_W), ek.reshape(bsz, seq, EXT_W))

        bw, cw, ar_s, ai_s = _s5_weights(s5_lambda_re[l], s5_lambda_im[l], s5_log_step[l],
                                         s5_b_re[l], s5_b_im[l], s5_c_re[l], s5_c_im[l])
        ssm_tb = _s5(u, bsz, bw, cw, ar_s, ai_s, s5_d[l].reshape(1, SSM_WIDTH), w_glu[l].astype(BF16),
                     b_glu[l].reshape(1, SSM_WIDTH), ssm_norm_g[l].reshape(1, SSM_WIDTH))

        n_route = N_EXPERT_GROUPS + N_EXPERTS
        wrt = jnp.pad(jnp.concatenate([w_router_group[l], w_router_expert[l]], axis=1).T,
                      ((0, LANES - n_route), (0, 0))).astype(BF16)
        brt = jnp.pad(jnp.concatenate([b_router_group[l], b_router_expert[l]]),
                      (0, ROUTER_ROWS - n_route)).reshape(ROUTER_ROWS, 1)
        h, eid, gate, rank, cnt_col = _outproj(
            attn.reshape(n_tok, a), ssm_tb, x2, w_out[l].astype(BF16), attn_norm_g[l].reshape(1, a),
            ln1_g[l].reshape(1, d), ln1_b[l].reshape(1, d), wrt, brt)

        local, runs, block_e, n_used, pads, n_blocks = _dispatch_plan(eid, rank, cnt_col, n_tok)
        xb = _dispatch(h, local, runs, pads, n_used, n_blocks)
        yb = _experts(xb, block_e, n_used, w_gate[l], w_up[l], w_down[l])
        x = _combine(h, yb, local.T, runs, gate.T, ln2_g[l].reshape(1, d),
                     ln2_b[l].reshape(1, d)).reshape(bsz, seq, d)
    return x
```

```python
import functools
import math

import jax
import jax.numpy as jnp
import numpy as np
from jax import lax
from jax.experimental import pallas as pl
from jax.experimental.pallas import tpu as pltpu

F32 = jnp.float32
BF16 = jnp.bfloat16

D_MODEL = 1024
HEAD_DIM = 64
ATTN_WIDTH = 512
ATTN_HEADS = 8
SSM_WIDTH = 512
SSM_GROUP = 16
SSM_GROUPS = 32
SSM_STATE = 64
N_EXPERT_GROUPS = 4
EXPERTS_PER_GROUP = 8
N_EXPERTS = 32
D_EXPERT = 512
MOE_BLOCK = 512
MOE_SUB = 256
TOKEN_BLOCK = 512
RUN_BITS = TOKEN_BLOCK.bit_length()
SHORT_RUN_BITS = 6
DEPTH = 1
DEEPNORM_ALPHA = (2.0 * DEPTH) ** 0.25
LN_EPS = 1e-5
RMS_EPS = 1e-6

LANES = 128
NEG_BIG = -1e30
LOG2E = math.log2(math.e)
ONES_ROWS = 16
VT_ROWS = ATTN_HEADS * (HEAD_DIM + ONES_ROWS)
GATE_ROWS = 16
PAIR_BIAS_LANES = 8
EXT_W = ATTN_HEADS // 2 * LANES
VMEM_LIMIT = 48 * 1024 * 1024

ROW_BLOCK = 1024
ATTN_BLOCK = 256
ATTN_Q_BLOCK = 512
ATTN_HEADS_PER_STEP = 8
S5_TIME_BLOCK = 64
S5_SLABS = 4
ROUTER_ROWS = 40
SLAB_CH = SSM_WIDTH // S5_SLABS
SLAB_STATE = SSM_GROUPS // S5_SLABS * SSM_STATE


def _split3(c):
    hi = c.astype(BF16)
    r1 = c - hi.astype(F32)
    mid = r1.astype(BF16)
    lo = (r1 - mid.astype(F32)).astype(BF16)
    return hi, mid, lo


def _inproj_kernel(x_ref, w_ref, wvt_ref, bf_ref, triu_ref, sel_ref, q_ref, k_ref, vt_ref, u_ref, eq_ref, ek_ref,
                   carry_ref, *, blocks_per_seq):
    i = pl.program_id(0)

    @pl.when(i % blocks_per_seq == 0)
    def _():
        carry_ref[...] = jnp.zeros_like(carry_ref)

    tm = x_ref.shape[0]
    xb = x_ref[...].astype(BF16)
    z = jnp.dot(xb, w_ref[...], preferred_element_type=F32)
    a = ATTN_WIDTH
    q_ref[...] = z[:, :a].astype(BF16)
    k_ref[...] = z[:, a:2 * a].astype(BF16)
    u_ref[...] = z[:, 2 * a:2 * a + SSM_WIDTH].astype(BF16)
    vtf = lax.dot_general(wvt_ref[...], xb, (((1,), (1,)), ((), ())), preferred_element_type=F32)
    vt = vtf[:ATTN_WIDTH].astype(BF16)
    ones_rows = jnp.ones((ONES_ROWS, tm), BF16)
    vt = jnp.concatenate([piece for h in range(ATTN_HEADS)
                          for piece in (vt[HEAD_DIM * h:HEAD_DIM * (h + 1)], ones_rows)], axis=0)
    for jj in range(tm // ATTN_BLOCK):
        vt_ref[jj] = vt[:, jj * ATTN_BLOCK:(jj + 1) * ATTN_BLOCK]

    f = vtf[ATTN_WIDTH:ATTN_WIDTH + ATTN_HEADS] + bf_ref[...]
    logf = jnp.minimum(f, 0.0) - jnp.log1p(jnp.exp(-jnp.abs(f)))
    parts = jnp.dot(jnp.concatenate(_split3(logf), axis=0), triu_ref[...], preferred_element_type=F32)
    nh = ATTN_HEADS
    cum = parts[0:nh] + parts[nh:2 * nh] + parts[2 * nh:3 * nh] + carry_ref[...]
    carry_ref[...] = cum[:, tm - 1:tm]
    hi, mid, lo = (t.astype(F32) for t in _split3(cum * LOG2E))
    one_row = jnp.where(lax.broadcasted_iota(jnp.int32, (nh, tm), 0) == 0, 1.0, 0.0)
    packed_t = jnp.concatenate([hi, mid, lo, one_row, jnp.zeros((LANES - 4 * nh, tm), F32)], axis=0)
    ext = jnp.dot(packed_t.T.astype(BF16), sel_ref[...], preferred_element_type=F32).astype(BF16)
    eq_ref[...] = ext[:, :EXT_W]
    ek_ref[...] = ext[:, EXT_W:]


def _bias_lane_selector():
    sel = np.zeros((LANES, 2, ATTN_HEADS // 2, LANES), np.float32)
    one = 3 * ATTN_HEADS
    for h in range(ATTN_HEADS):
        pair, base = h // 2, PAIR_BIAS_LANES * (h % 2)
        for t in range(3):
            sel[ATTN_HEADS * t + h, 0, pair, base + t] = 1.0
            sel[one, 0, pair, base + 3 + t] = 1.0
            sel[one, 1, pair, base + t] = 1.0
            sel[ATTN_HEADS * t + h, 1, pair, base + 3 + t] = -1.0
    return jnp.asarray(sel.reshape(LANES, 2 * EXT_W), BF16)


def _inproj(x2, w_cat, wvt, bf_col, seq):
    n_tok = x2.shape[0]
    tm = ROW_BLOCK
    assert seq % tm == 0 and n_tok % tm == 0 and tm % ATTN_BLOCK == 0
    bsz = n_tok // seq
    bps = seq // tm
    ncol = w_cat.shape[1]
    vblk = tm // ATTN_BLOCK
    row_spec = lambda w: pl.BlockSpec((tm, w), lambda i: (i, 0))
    return pl.pallas_call(
        functools.partial(_inproj_kernel, blocks_per_seq=bps),
        grid=(n_tok // tm,),
        in_specs=[row_spec(D_MODEL),
                  pl.BlockSpec((D_MODEL, ncol), lambda i: (0, 0)),
                  pl.BlockSpec((ATTN_WIDTH + GATE_ROWS, D_MODEL), lambda i: (0, 0)),
                  pl.BlockSpec((ATTN_HEADS, 1), lambda i: (0, 0)),
                  pl.BlockSpec((tm, tm), lambda i: (0, 0)),
                  pl.BlockSpec((LANES, 2 * EXT_W), lambda i: (0, 0))],
        out_specs=[row_spec(ATTN_WIDTH), row_spec(ATTN_WIDTH),
                   pl.BlockSpec((None, vblk, VT_ROWS, ATTN_BLOCK), lambda i: (i // bps, i % bps, 0, 0)),
                   pl.BlockSpec((tm, SSM_WIDTH), lambda i: (i % bps, i // bps)),
                   row_spec(EXT_W), row_spec(EXT_W)],
        out_shape=[jax.ShapeDtypeStruct((n_tok, ATTN_WIDTH), BF16)] * 2
                  + [jax.ShapeDtypeStruct((bsz, seq // ATTN_BLOCK, VT_ROWS, ATTN_BLOCK), BF16),
                     jax.ShapeDtypeStruct((seq, bsz * SSM_WIDTH), BF16),
                     jax.ShapeDtypeStruct((n_tok, EXT_W), BF16),
                     jax.ShapeDtypeStruct((n_tok, EXT_W), BF16)],
        scratch_shapes=[pltpu.VMEM((ATTN_HEADS, 1), F32)],
        compiler_params=pltpu.CompilerParams(dimension_semantics=("arbitrary",),
                                             vmem_limit_bytes=VMEM_LIMIT),
        name="inproj",
    )(x2, w_cat, wvt, bf_col, jnp.triu(jnp.ones((tm, tm), BF16)), _bias_lane_selector())


def _attn_kernel(q_ref, k_ref, vt_ref, eq_ref, ek_ref, o_ref, *, tq, seq, nh):
    qi = pl.program_id(2)
    lane = lax.broadcasted_iota(jnp.int32, (tq, LANES), 1)
    q_augs = []
    for h in range(nh):
        qb = q_ref[:, LANES * (h // 2):LANES * (h // 2 + 1)]
        head_lanes = (lane < HEAD_DIM) if h % 2 == 0 else (lane >= HEAD_DIM)
        qh = jnp.where(head_lanes, qb, jnp.zeros((), BF16))
        own = (lane >= PAIR_BIAS_LANES * (h % 2)) & (lane < PAIR_BIAS_LANES * (h % 2 + 1))
        eqh = jnp.where(own, eq_ref[:, LANES * (h // 2):LANES * (h // 2 + 1)], jnp.zeros((), BF16))
        q_augs.append(jnp.concatenate([qh, eqh], axis=1))
    key_i = lax.broadcasted_iota(jnp.int32, (tq, tq), 0)
    qry_i = lax.broadcasted_iota(jnp.int32, (tq, tq), 1)
    causal = key_i <= qry_i
    vrows = HEAD_DIM + ONES_ROWS

    def scores_of(k0, nk):
        out = []
        for h in range(nh):
            kj = k_ref[pl.ds(k0, nk), LANES * (h // 2):LANES * (h // 2 + 1)]
            k_aug = jnp.concatenate([kj, ek_ref[pl.ds(k0, nk), LANES * (h // 2):LANES * (h // 2 + 1)]], axis=1)
            out.append(lax.dot_general(k_aug, q_augs[h], (((1,), (1,)), ((), ())), preferred_element_type=F32))
        return tuple(out)

    def absorb(j0, scores, carry):
        kb = ATTN_BLOCK
        stats = []
        for h in range(nh):
            m = carry[h][0]
            m_new = jnp.maximum(m, jnp.max(scores[h], axis=0, keepdims=True))
            stats.append((m_new, jnp.exp2(m - m_new), jnp.exp2(scores[h] - m_new).astype(BF16)))
        new = []
        for h in range(nh):
            m_new, corr, p = stats[h]
            res = None
            for c in range(p.shape[0] // kb):
                part = jnp.dot(vt_ref[j0 + c, vrows * h:vrows * (h + 1), :], p[c * kb:(c + 1) * kb],
                               preferred_element_type=F32)
                res = part if res is None else res + part
            acc = corr * carry[h][2] + res[0:HEAD_DIM, :]
            l = corr * carry[h][1] + res[HEAD_DIM:HEAD_DIM + 1, :]
            new.append((m_new, l, acc))
        return tuple(new)

    per = tq // ATTN_BLOCK

    def full_step(j, carry):
        return absorb(per * j, scores_of(pl.multiple_of(j * tq, tq), tq), carry)

    init1 = (jnp.full((1, tq), NEG_BIG, F32), jnp.zeros((1, tq), F32), jnp.zeros((HEAD_DIM, tq), F32))
    carry = lax.fori_loop(0, qi, full_step, (init1,) * nh)
    diag = scores_of(pl.multiple_of(qi * tq, tq), tq)
    carry = absorb(per * qi, tuple(jnp.where(causal, s, NEG_BIG) for s in diag), carry)
    out_t = jnp.concatenate([c[2] / c[1] for c in carry], axis=0)
    o_ref[...] = out_t.T


def _attention(q, k, vt, eq, ek):
    bsz, seq, _ = q.shape
    tq = ATTN_Q_BLOCK
    nh = ATTN_HEADS_PER_STEP
    assert seq % tq == 0 and tq % ATTN_BLOCK == 0 and ATTN_HEADS % nh == 0 and nh % 2 == 0
    w = nh * HEAD_DIM
    return pl.pallas_call(
        functools.partial(_attn_kernel, tq=tq, seq=seq, nh=nh),
        grid=(bsz, ATTN_HEADS // nh, seq // tq),
        in_specs=[pl.BlockSpec((None, tq, w), lambda b, g, i: (b, i, g)),
                  pl.BlockSpec((None, seq, w), lambda b, g, i: (b, 0, g)),
                  pl.BlockSpec((None, seq // ATTN_BLOCK, nh * (HEAD_DIM + ONES_ROWS), ATTN_BLOCK),
                               lambda b, g, i: (b, 0, g, 0)),
                  pl.BlockSpec((None, tq, nh // 2 * LANES), lambda b, g, i: (b, i, g)),
                  pl.BlockSpec((None, seq, nh // 2 * LANES), lambda b, g, i: (b, 0, g))],
        out_specs=pl.BlockSpec((None, tq, w), lambda b, g, i: (b, i, g)),
        out_shape=jax.ShapeDtypeStruct((bsz, seq, ATTN_WIDTH), F32),
        compiler_params=pltpu.CompilerParams(dimension_semantics=("parallel", "parallel", "arbitrary"),
                                             vmem_limit_bytes=VMEM_LIMIT),
        name="fox_attention",
    )(q, k, vt, eq, ek)


def _s5_kernel(u_ref, bw_ref, cw_ref, ar_ref, ai_ref, d_ref, wglu_ref, bglu_ref, g_ref, o_ref,
               state_ref, bu_ref, *, tc, bsz):
    @pl.when(pl.program_id(0) == 0)
    def _():
        state_ref[...] = jnp.zeros_like(state_ref)

    rows = tc * bsz
    ns = SLAB_STATE
    u = u_ref[...].reshape(rows, SSM_WIDTH)
    ub = u
    for i in range(S5_SLABS):
        bu_ref[i] = jnp.dot(ub[:, SLAB_CH * i:SLAB_CH * (i + 1)], bw_ref[i], preferred_element_type=F32)
    for i in range(S5_SLABS):
        ar = jnp.broadcast_to(ar_ref[i], (bsz, ns))
        ai = jnp.broadcast_to(ai_ref[i], (bsz, ns))
        xr = state_ref[i, :, 0:ns]
        xi = state_ref[i, :, ns:2 * ns]
        for t in range(tc):
            r0 = t * bsz
            nr = ar * xr - ai * xi + bu_ref[i, r0:r0 + bsz, 0:ns]
            ni = ar * xi + ai * xr + bu_ref[i, r0:r0 + bsz, ns:2 * ns]
            bu_ref[i, r0:r0 + bsz, 0:ns] = nr
            bu_ref[i, r0:r0 + bsz, ns:2 * ns] = ni
            xr, xi = nr, ni
        state_ref[i, :, 0:ns] = xr
        state_ref[i, :, ns:2 * ns] = xi
    ys = [jnp.dot(bu_ref[i].astype(BF16), cw_ref[i], preferred_element_type=F32) for i in range(S5_SLABS)]
    y = jnp.concatenate(ys, axis=1) + d_ref[...] * u.astype(F32)
    y = 0.5 * y * (1.0 + jnp.tanh(math.sqrt(2.0 / math.pi) * (y + 0.044715 * (y * y * y))))
    z = jnp.dot(y.astype(BF16), wglu_ref[...], preferred_element_type=F32) + bglu_ref[...]
    y = y * (1.0 / (1.0 + jnp.exp(-z)))
    y = y * lax.rsqrt(jnp.mean(y * y, axis=-1, keepdims=True) + RMS_EPS) * g_ref[...]
    o_ref[...] = y.astype(BF16).reshape(tc, bsz * SSM_WIDTH)


def _s5_weights(lam_re, lam_im, log_step, b_re, b_im, c_re, c_im):
    delta = jnp.exp(log_step)[:, None]
    mag = jnp.exp(lam_re * delta)
    ar = mag * jnp.cos(lam_im * delta)
    ai = mag * jnp.sin(lam_im * delta)
    den = lam_re * lam_re + lam_im * lam_im
    num_re = ar - 1.0
    coef_re = (num_re * lam_re + ai * lam_im) / den
    coef_im = (ai * lam_re - num_re * lam_im) / den
    bbar_re = coef_re[..., None] * b_re - coef_im[..., None] * b_im
    bbar_im = coef_re[..., None] * b_im + coef_im[..., None] * b_re
    gs = SSM_GROUPS // S5_SLABS
    eye = jnp.eye(gs, dtype=F32)

    def b_slab(bb):
        bb = bb.reshape(S5_SLABS, gs, SSM_STATE, SSM_GROUP)
        w = jnp.einsum('ab,sapc->sacbp', eye, bb)
        return w.reshape(S5_SLABS, gs * SSM_GROUP, gs * SSM_STATE)

    def c_slab(cc):
        cc = cc.reshape(S5_SLABS, gs, SSM_GROUP, SSM_STATE)
        w = jnp.einsum('ab,sacp->sbpac', eye, cc)
        return w.reshape(S5_SLABS, gs * SSM_STATE, gs * SSM_GROUP)

    bw = jnp.concatenate([b_slab(bbar_re), b_slab(bbar_im)], axis=2).astype(BF16)
    cw = jnp.concatenate([c_slab(c_re), -c_slab(c_im)], axis=1).astype(BF16)
    ar_s = ar.reshape(S5_SLABS, 1, SLAB_STATE)
    ai_s = ai.reshape(S5_SLABS, 1, SLAB_STATE)
    return bw, cw, ar_s, ai_s


def _s5(u_tb, bsz, bw, cw, ar_s, ai_s, d_row, w_glu, b_glu_row, g_row):
    seq = u_tb.shape[0]
    tc = S5_TIME_BLOCK
    assert seq % tc == 0 and bsz % 16 == 0
    full = lambda shape: pl.BlockSpec(shape, lambda t: (0,) * len(shape))
    return pl.pallas_call(
        functools.partial(_s5_kernel, tc=tc, bsz=bsz),
        grid=(seq // tc,),
        in_specs=[pl.BlockSpec((tc, bsz * SSM_WIDTH), lambda t: (t, 0)),
                  full(bw.shape), full(cw.shape), full(ar_s.shape), full(ai_s.shape),
                  full((1, SSM_WIDTH)), full((SSM_WIDTH, SSM_WIDTH)), full((1, SSM_WIDTH)),
                  full((1, SSM_WIDTH))],
        out_specs=pl.BlockSpec((tc, bsz * SSM_WIDTH), lambda t: (t, 0)),
        out_shape=jax.ShapeDtypeStruct((seq, bsz * SSM_WIDTH), BF16),
        scratch_shapes=[pltpu.VMEM((S5_SLABS, bsz, 2 * SLAB_STATE), F32),
                        pltpu.VMEM((S5_SLABS, tc * bsz, 2 * SLAB_STATE), F32)],
        compiler_params=pltpu.CompilerParams(dimension_semantics=("arbitrary",),
                                             vmem_limit_bytes=VMEM_LIMIT),
        name="s5_branch",
    )(u_tb, bw, cw, ar_s, ai_s, d_row, w_glu, b_glu_row, g_row)


def _layer_norm_rows(v, g, b):
    mu = jnp.mean(v, axis=-1, keepdims=True)
    c = v - mu
    var = jnp.mean(c * c, axis=-1, keepdims=True)
    return c * lax.rsqrt(var + LN_EPS) * g + b


def _outproj_kernel(attn_ref, ssm_ref, x_ref, wo_ref, ag_ref, g1_ref, b1_ref, wrt_ref, brt_ref, triu_ref,
                    h_ref, eid_ref, gate_ref, rank_ref, cnt_ref):
    a = attn_ref[...]
    a = a * lax.rsqrt(jnp.mean(a * a, axis=-1, keepdims=True) + RMS_EPS) * ag_ref[...]
    mix = (jnp.dot(a.astype(BF16), wo_ref[0:ATTN_WIDTH, :], preferred_element_type=F32)
           + jnp.dot(ssm_ref[...], wo_ref[ATTN_WIDTH:, :], preferred_element_type=F32))
    h = _layer_norm_rows(DEEPNORM_ALPHA * x_ref[...] + mix, g1_ref[...], b1_ref[...])
    h_ref[...] = h

    logits = lax.dot_general(wrt_ref[...], h.astype(BF16), (((1,), (1,)), ((), ())),
                             preferred_element_type=F32)[:ROUTER_ROWS] + brt_ref[...]
    tm = logits.shape[1]
    row = lax.broadcasted_iota(jnp.int32, (ROUTER_ROWS, tm), 0).astype(F32)
    no_row = float(LANES)
    is_g = row < N_EXPERT_GROUPS
    gl = jnp.where(is_g, logits, NEG_BIG)
    gmax = jnp.max(gl, axis=0, keepdims=True)
    g_idx = jnp.min(jnp.where(is_g & (gl == gmax), row, no_row), axis=0, keepdims=True)
    g_p = 1.0 / jnp.sum(jnp.where(is_g, jnp.exp(gl - gmax), 0.0), axis=0, keepdims=True)
    first = N_EXPERT_GROUPS + g_idx * EXPERTS_PER_GROUP
    in_grp = (row >= first) & (row < first + EXPERTS_PER_GROUP)
    el = jnp.where(in_grp, logits, NEG_BIG)
    m1 = jnp.max(el, axis=0, keepdims=True)
    i1 = jnp.min(jnp.where(in_grp & (el == m1), row, no_row), axis=0, keepdims=True)
    rest = in_grp & (row != i1)
    el2 = jnp.where(rest, logits, NEG_BIG)
    m2 = jnp.max(el2, axis=0, keepdims=True)
    i2 = jnp.min(jnp.where(rest & (el2 == m2), row, no_row), axis=0, keepdims=True)
    r = jnp.exp(m2 - m1)
    p1 = 1.0 / (1.0 + r)
    p2 = r / (1.0 + r)
    k2 = lax.broadcasted_iota(jnp.int32, (2, tm), 0)
    eid_ref[...] = (jnp.where(k2 == 0, i1, i2) - N_EXPERT_GROUPS).astype(jnp.int32)
    gate_ref[...] = g_p * jnp.where(k2 == 0, p1, p2)

    @pl.when(pl.program_id(0) == 0)
    def _():
        cnt_ref[...] = jnp.zeros_like(cnt_ref)
    hit1 = row == i1
    hit2 = row == i2
    onehot = jnp.where(hit1 | hit2, 1.0, 0.0)
    cnt = cnt_ref[...]
    before = jnp.dot(onehot.astype(BF16), triu_ref[...], preferred_element_type=F32) + cnt
    r1 = jnp.sum(jnp.where(hit1, before, 0.0), axis=0, keepdims=True)
    r2 = jnp.sum(jnp.where(hit2, before, 0.0), axis=0, keepdims=True)
    rank_ref[...] = jnp.where(k2 == 0, r1, r2).astype(jnp.int32)
    cnt_ref[...] = cnt + jnp.sum(onehot, axis=1, keepdims=True)


def _outproj(attn2, ssm_tb, x2, wo, ag_row, g1_row, b1_row, wrt, brt):
    n_tok = x2.shape[0]
    tm = ROW_BLOCK
    bps = ssm_tb.shape[0] // tm
    triu = jnp.triu(jnp.ones((tm, tm), BF16), 1)
    row_spec = lambda w: pl.BlockSpec((tm, w), lambda i: (i, 0))
    col_spec = lambda r: pl.BlockSpec((r, tm), lambda i: (0, i))
    full = lambda shape: pl.BlockSpec(shape, lambda i: (0,) * len(shape))
    return pl.pallas_call(
        _outproj_kernel,
        grid=(n_tok // tm,),
        in_specs=[row_spec(ATTN_WIDTH), pl.BlockSpec((tm, SSM_WIDTH), lambda i: (i % bps, i // bps)),
                  row_spec(D_MODEL),
                  full((D_MODEL, D_MODEL)), full((1, ATTN_WIDTH)), full((1, D_MODEL)), full((1, D_MODEL)),
                  full((LANES, D_MODEL)), full((ROUTER_ROWS, 1)), full((tm, tm))],
        out_specs=[row_spec(D_MODEL), col_spec(2), col_spec(2), col_spec(2), full((ROUTER_ROWS, 1))],
        out_shape=[jax.ShapeDtypeStruct((n_tok, D_MODEL), F32),
                   jax.ShapeDtypeStruct((2, n_tok), jnp.int32),
                   jax.ShapeDtypeStruct((2, n_tok), F32),
                   jax.ShapeDtypeStruct((2, n_tok), jnp.int32),
                   jax.ShapeDtypeStruct((ROUTER_ROWS, 1), F32)],
        compiler_params=pltpu.CompilerParams(dimension_semantics=("arbitrary",),
                                             vmem_limit_bytes=VMEM_LIMIT),
        name="outproj_ln_router",
    )(attn2, ssm_tb, x2, wo, ag_row, g1_row, b1_row, wrt, brt, triu)


def _dispatch_plan(eid, rank, cnt_col, n_tok):
    counts = cnt_col[N_EXPERT_GROUPS:N_EXPERT_GROUPS + N_EXPERTS, 0].astype(jnp.int32)
    padded = ((counts + MOE_BLOCK - 1) // MOE_BLOCK) * MOE_BLOCK
    pends = jnp.cumsum(padded)
    pstarts = pends - padded
    n_pad = ((2 * n_tok + N_EXPERTS * (MOE_BLOCK - 1) + MOE_BLOCK - 1) // MOE_BLOCK) * MOE_BLOCK
    n_blocks = n_pad // MOE_BLOCK
    blk0 = jnp.arange(n_blocks, dtype=jnp.int32) * MOE_BLOCK
    block_e = jnp.minimum(jnp.sum((pends[None, :] <= blk0[:, None]).astype(jnp.int32), axis=1), N_EXPERTS - 1)
    n_used = (pends[-1] // MOE_BLOCK).astype(jnp.int32).reshape(1)
    pads = (padded - counts, jnp.zeros_like(counts), pstarts + counts)

    tb = TOKEN_BLOCK
    nb = n_tok // tb
    onehot = eid[:, :, None] == jnp.arange(N_EXPERTS, dtype=jnp.int32)[None, None, :]
    run_len = jnp.sum(onehot.reshape(2, nb, tb, N_EXPERTS).astype(jnp.int32), axis=(0, 2))
    before = jnp.cumsum(run_len, axis=0) - run_len
    run_src = jnp.cumsum(run_len, axis=1) - run_len
    run_dst = pstarts[None, :] + before
    shift = jnp.broadcast_to((run_src - before)[:, None, :], (nb, tb, N_EXPERTS)).reshape(n_tok, N_EXPERTS)
    local = rank + jnp.sum(jnp.where(onehot, shift[None], 0), axis=-1)
    any_long = (jnp.max(run_len, axis=1) >= (1 << SHORT_RUN_BITS)).astype(jnp.int32)
    runs = (run_len.reshape(-1), run_src.reshape(-1), run_dst.reshape(-1), any_long)
    return local, runs, block_e, n_used, pads, n_blocks


ROW_SUB = D_MODEL // LANES


def _rows(start, size):
    return pl.ds(pl.multiple_of(start * ROW_SUB, ROW_SUB), size * ROW_SUB)


def _store_row_tiles(ref, row0, v):
    n = v.shape[0]
    for s in range(ROW_SUB):
        ref[pl.ds(row0 * ROW_SUB + s, n, stride=ROW_SUB), :] = v[:, LANES * s:LANES * (s + 1)]


def _load_row_tiles(ref, row0, n):
    return jnp.concatenate([ref[pl.ds(row0 * ROW_SUB + s, n, stride=ROW_SUB), :] for s in range(ROW_SUB)], axis=1)


def _for_each_run_piece(blk, len_ref, src_ref, dst_ref, bits, make_copy, act):
    for e in range(N_EXPERTS):
        idx = blk * N_EXPERTS + e
        n = len_ref[idx]
        src = src_ref[idx]
        dst = dst_ref[idx]
        for b in reversed(range(bits)):
            @pl.when((n & (1 << b)) != 0)
            def _():
                off = lax.shift_left(lax.shift_right_logical(n, b + 1), b + 1)
                act(make_copy(src + off, dst + off, 1 << b), (e + b) % 2)


def _by_run_size(long_ref, blk, body):
    @pl.when(long_ref[blk] == 0)
    def _():
        body(SHORT_RUN_BITS)

    @pl.when(long_ref[blk] != 0)
    def _():
        body(RUN_BITS)


def _dispatch_kernel(nu_ref, plen_ref, psrc_ref, pdst_ref, len_ref, src_ref, dst_ref, long_ref, local_ref, h_ref,
                     xb_hbm, zbuf, pk_ref, sem_z, sem, *, n_blocks):
    i = pl.program_id(0)
    nb = pl.num_programs(0) - 1
    tb = h_ref.shape[0]
    cur = i % 2

    def each_zero_copy(act):
        def zero_copy(src, dst, size):
            return pltpu.make_async_copy(zbuf.at[_rows(src, size)], xb_hbm.at[_rows(dst, size)], sem_z)
        _for_each_run_piece(0, plen_ref, psrc_ref, pdst_ref, MOE_BLOCK.bit_length() - 1, zero_copy, act)

        def unused(b, carry):
            act(zero_copy(0, b * MOE_BLOCK, MOE_BLOCK), 0)
            return carry
        lax.fori_loop(nu_ref[0], n_blocks, unused, 0)

    @pl.when(i == 0)
    def _():
        zbuf[...] = jnp.zeros_like(zbuf)
        each_zero_copy(lambda c, prio: c.start(priority=prio))

    def drain(which):
        pltpu.make_async_copy(pk_ref.at[which], xb_hbm.at[_rows(0, 2 * tb)], sem.at[which]).wait()

    @pl.when(i >= 2)
    def _():
        drain(cur)

    def group():
        pos = lax.broadcasted_iota(jnp.int32, (2 * tb, tb), 0)
        hit = (pos == local_ref[0:1, :]) | (pos == local_ref[1:2, :])
        perm = jnp.where(hit, 1.0, 0.0).astype(BF16)
        _store_row_tiles(pk_ref.at[cur], 0, jnp.dot(perm, h_ref[...].astype(BF16), preferred_element_type=F32))

    def send(bits):
        def run_copy(src, dst, size):
            return pltpu.make_async_copy(pk_ref.at[1 - cur, _rows(src, size)], xb_hbm.at[_rows(dst, size)],
                                         sem.at[1 - cur])
        _for_each_run_piece(i - 1, len_ref, src_ref, dst_ref, bits, run_copy,
                            lambda c, prio: c.start(priority=prio))

    @pl.when(i == 0)
    def _():
        group()

    @pl.when((i > 0) & (i < nb))
    def _():
        def send_and_group(bits):
            send(bits)
            group()
        _by_run_size(long_ref, i - 1, send_and_group)

    @pl.when(i == nb)
    def _():
        send(RUN_BITS)
        drain(1 - cur)
        each_zero_copy(lambda c, prio: c.wait())


def _dispatch(h, local, runs, pads, n_used, n_blocks):
    n_tok = h.shape[0]
    tb = TOKEN_BLOCK
    nb = n_tok // tb
    grid_spec = pltpu.PrefetchScalarGridSpec(
        num_scalar_prefetch=8,
        grid=(nb + 1,),
        in_specs=[pl.BlockSpec((2, tb), lambda i, *_: (0, jnp.minimum(i, nb - 1))),
                  pl.BlockSpec((tb, D_MODEL), lambda i, *_: (jnp.minimum(i, nb - 1), 0))],
        out_specs=pl.BlockSpec(memory_space=pl.ANY),
        scratch_shapes=[pltpu.VMEM((MOE_BLOCK * ROW_SUB, LANES), F32),
                        pltpu.VMEM((2, 2 * tb * ROW_SUB, LANES), F32),
                        pltpu.SemaphoreType.DMA, pltpu.SemaphoreType.DMA((2,))],
    )
    return pl.pallas_call(
        functools.partial(_dispatch_kernel, n_blocks=n_blocks),
        grid_spec=grid_spec,
        out_shape=jax.ShapeDtypeStruct((n_blocks * MOE_BLOCK * ROW_SUB, LANES), F32),
        compiler_params=pltpu.CompilerParams(dimension_semantics=("arbitrary",),
                                             vmem_limit_bytes=VMEM_LIMIT),
        name="moe_dispatch",
    )(n_used, *pads, *runs, local, h)


def _expert_kernel(be_ref, nu_ref, x_ref, wg_ref, wu_ref, wd_ref, y_ref, wgu_t, wd_t, yt_ref):
    i = pl.program_id(0)
    used = i < nu_ref[0]

    @pl.when(used & ((i == 0) | (be_ref[i] != be_ref[jnp.maximum(i - 1, 0)])))
    def _():
        wgu_t[0:D_EXPERT, :] = wg_ref[...].astype(BF16).T
        wgu_t[D_EXPERT:, :] = wu_ref[...].astype(BF16).T
        wd_t[...] = wd_ref[...].astype(BF16).T

    @pl.when(used)
    def _():
        sub = range(MOE_BLOCK // MOE_SUB)
        gus = []
        for c in sub:
            xb = _load_row_tiles(x_ref, c * MOE_SUB, MOE_SUB).astype(BF16)
            gus.append(lax.dot_general(wgu_t[...], xb, (((1,), (1,)), ((), ())),
                                       preferred_element_type=F32))
        hids = []
        for c in sub:
            gt = gus[c][0:D_EXPERT]
            hids.append((gt * (1.0 / (1.0 + jnp.exp(-gt))) * gus[c][D_EXPERT:]).astype(BF16))
        for c in sub:
            yt_ref[c] = jnp.dot(wd_t[...], hids[c], preferred_element_type=F32)
        for c in sub:
            _store_row_tiles(y_ref, c * MOE_SUB, yt_ref[c].astype(BF16).T.astype(F32))

    @pl.when(jnp.logical_not(used))
    def _():
        y_ref[...] = jnp.zeros_like(y_ref)


def _experts(xb, block_e, n_used, wg, wu, wd):
    n_blocks = xb.shape[0] // (MOE_BLOCK * ROW_SUB)
    last = lambda i, nu: jnp.minimum(i, nu[0] - 1)
    grid_spec = pltpu.PrefetchScalarGridSpec(
        num_scalar_prefetch=2,
        grid=(n_blocks,),
        in_specs=[pl.BlockSpec((MOE_BLOCK * ROW_SUB, LANES), lambda i, be, nu: (last(i, nu), 0)),
                  pl.BlockSpec((None, D_MODEL, D_EXPERT), lambda i, be, nu: (be[last(i, nu)], 0, 0)),
                  pl.BlockSpec((None, D_MODEL, D_EXPERT), lambda i, be, nu: (be[last(i, nu)], 0, 0)),
                  pl.BlockSpec((None, D_EXPERT, D_MODEL), lambda i, be, nu: (be[last(i, nu)], 0, 0))],
        out_specs=pl.BlockSpec((MOE_BLOCK * ROW_SUB, LANES), lambda i, be, nu: (i, 0)),
        scratch_shapes=[pltpu.VMEM((2 * D_EXPERT, D_MODEL), BF16), pltpu.VMEM((D_MODEL, D_EXPERT), BF16),
                        pltpu.VMEM((MOE_BLOCK // MOE_SUB, D_MODEL, MOE_SUB), F32)],
    )
    return pl.pallas_call(
        _expert_kernel,
        grid_spec=grid_spec,
        out_shape=jax.ShapeDtypeStruct(xb.shape, F32),
        compiler_params=pltpu.CompilerParams(dimension_semantics=("arbitrary",),
                                             vmem_limit_bytes=VMEM_LIMIT),
        name="moe_experts",
    )(block_e, n_used, xb, wg, wu, wd)


def _combine_kernel(len_ref, src_ref, dst_ref, long_ref, local_ref, gate_ref, h_ref, g2_ref, b2_ref, yb_hbm,
                    o_ref, buf, sem):
    i = pl.program_id(0)
    n = pl.num_programs(0)
    tb = h_ref.shape[0]
    cur = i % 2

    def fetch(blk, which, bits):
        def run_copy(src, dst, size):
            return pltpu.make_async_copy(yb_hbm.at[_rows(dst, size)], buf.at[which, _rows(src, size)], sem.at[which])
        _for_each_run_piece(blk, len_ref, src_ref, dst_ref, bits, run_copy,
                            lambda c, prio: c.start(priority=prio))

    def drain(which):
        pltpu.make_async_copy(yb_hbm.at[_rows(0, 2 * tb)], buf.at[which], sem.at[which]).wait()

    @pl.when(i == 0)
    def _():
        fetch(i, 0, RUN_BITS)

    drain(cur)
    nxt = jnp.minimum(i + 1, n - 1)

    def fetch_next_and_combine(bits):
        fetch(nxt, 1 - cur, bits)
        y = _load_row_tiles(buf.at[cur], 0, 2 * tb).astype(BF16)
        pos = lax.broadcasted_iota(jnp.int32, (tb, 2 * tb), 1)
        local = local_ref[...]
        gate = gate_ref[...]
        mix = (jnp.where(pos == local[:, 0:1], gate[:, 0:1], 0.0)
               + jnp.where(pos == local[:, 1:2], gate[:, 1:2], 0.0))
        mix_hi = mix.astype(BF16)
        mix_lo = (mix - mix_hi.astype(F32)).astype(BF16)
        moe = (jnp.dot(mix_hi, y, preferred_element_type=F32) + jnp.dot(mix_lo, y, preferred_element_type=F32))
        o_ref[...] = _layer_norm_rows(DEEPNORM_ALPHA * h_ref[...] + moe, g2_ref[...], b2_ref[...])
    _by_run_size(long_ref, nxt, fetch_next_and_combine)

    @pl.when(i == n - 1)
    def _():
        drain(1 - cur)


def _combine(h, yb, local_t, runs, gate, g2_row, b2_row):
    n_tok = h.shape[0]
    tb = TOKEN_BLOCK
    row_spec = lambda w: pl.BlockSpec((tb, w), lambda i, *_: (i, 0))
    full = lambda shape: pl.BlockSpec(shape, lambda i, *_: (0,) * len(shape))
    grid_spec = pltpu.PrefetchScalarGridSpec(
        num_scalar_prefetch=4,
        grid=(n_tok // tb,),
        in_specs=[row_spec(2), row_spec(2), row_spec(D_MODEL), full((1, D_MODEL)), full((1, D_MODEL)),
                  pl.BlockSpec(memory_space=pl.ANY)],
        out_specs=row_spec(D_MODEL),
        scratch_shapes=[pltpu.VMEM((2, 2 * tb * ROW_SUB, LANES), F32), pltpu.SemaphoreType.DMA((2,))],
    )
    return pl.pallas_call(
        _combine_kernel,
        grid_spec=grid_spec,
        out_shape=jax.ShapeDtypeStruct((n_tok, D_MODEL), F32),
        compiler_params=pltpu.CompilerParams(dimension_semantics=("arbitrary",),
                                             vmem_limit_bytes=VMEM_LIMIT),
        name="combine_ln",
    )(*runs, local_t, gate, h, g2_row, b2_row, yb)


def kernel(x, w_in, b_fgate, s5_lambda_re, s5_lambda_im, s5_log_step, s5_b_re, s5_b_im, s5_c_re, s5_c_im, s5_d, w_glu, b_glu, attn_norm_g, ssm_norm_g, w_out, ln1_g, ln1_b, w_router_group, b_router_group, w_router_expert, b_router_expert, w_gate, w_up, w_down, ln2_g, ln2_b):
    bsz, seq, d = x.shape
    n_tok = bsz * seq
    a = ATTN_WIDTH
    for l in range(DEPTH):
        x2 = x.reshape(n_tok, d)
        w = w_in[l]
        f0 = 3 * a
        w_cat = jnp.concatenate(
            [w[:, :a] * (HEAD_DIM ** -0.5 * LOG2E), w[:, a:2 * a], w[:, f0 + ATTN_HEADS:]], axis=1).astype(BF16)
        wvt = jnp.pad(w[:, 2 * a:f0 + ATTN_HEADS].T, ((0, GATE_ROWS - ATTN_HEADS), (0, 0))).astype(BF16)
        q, k, vt, u, eq, ek = _inproj(x2, w_cat, wvt, b_fgate[l].reshape(ATTN_HEADS, 1), seq)
        attn = _attention(q.reshape(bsz, seq, a), k.reshape(bsz, seq, a), vt,
                          eq.reshape(bsz, seq, EXT_W), ek.reshape(bsz, seq, EXT_W))

        bw, cw, ar_s, ai_s = _s5_weights(s5_lambda_re[l], s5_lambda_im[l], s5_log_step[l],
                                         s5_b_re[l], s5_b_im[l], s5_c_re[l], s5_c_im[l])
        ssm_tb = _s5(u, bsz, bw, cw, ar_s, ai_s, s5_d[l].reshape(1, SSM_WIDTH), w_glu[l].astype(BF16),
                     b_glu[l].reshape(1, SSM_WIDTH), ssm_norm_g[l].reshape(1, SSM_WIDTH))

        n_route = N_EXPERT_GROUPS + N_EXPERTS
        wrt = jnp.pad(jnp.concatenate([w_router_group[l], w_router_expert[l]], axis=1).T,
                      ((0, LANES - n_route), (0, 0))).astype(BF16)
        brt = jnp.pad(jnp.concatenate([b_router_group[l], b_router_expert[l]]),
                      (0, ROUTER_ROWS - n_route)).reshape(ROUTER_ROWS, 1)
        h, eid, gate, rank, cnt_col = _outproj(
            attn.reshape(n_tok, a), ssm_tb, x2, w_out[l].astype(BF16), attn_norm_g[l].reshape(1, a),
            ln1_g[l].reshape(1, d), ln1_b[l].reshape(1, d), wrt, brt)

        local, runs, block_e, n_used, pads, n_blocks = _dispatch_plan(eid, rank, cnt_col, n_tok)
        xb = _dispatch(h, local, runs, pads, n_used, n_blocks)
        yb = _experts(xb, block_e, n_used, w_gate[l], w_up[l], w_down[l])
        x = _combine(h, yb, local.T, runs, gate.T, ln2_g[l].reshape(1, d),
                     ln2_b[l].reshape(1, d)).reshape(bsz, seq, d)
    return x
```
